```python
import jax, jax.numpy as jnp
from jax import lax
import numpy as np


D_MODEL = 1024
BATCH = 4
SEQ = 8192
DEPTH = 4

N_MIXERS = 4
N_HEADS = 16
HEAD_DIM = D_MODEL // N_HEADS
HD = N_HEADS * HEAD_DIM
FFN_DIM = 2816
ROPE_THETA = 10000.0
Q_BLOCK = 128
RMS_EPS = 1e-6
NEG_INF = -1e30
NSA_KV_HEADS = 4
NSA_CMP_LEN = 32
NSA_CMP_STRIDE = 16
NSA_SEL_LEN = 64
NSA_N_SEL = 16
NSA_WINDOW = 512
NSA_FORCED_SCORE = 1e9
GLA_HEADS = 4
GLA_KEY_DIM = D_MODEL // 2
GLA_VALUE_DIM = D_MODEL
GLA_GATE_RANK = 16
GLA_TAU = 16.0
GLA_CHUNK = 64

kernel_name = 'hybrid_fox_nsa_gla_stickbreak_macaron'


def rms_norm(x, g):
    xf = x.astype(jnp.float32)
    y = xf * lax.rsqrt(jnp.mean(xf * xf, axis=-1, keepdims=True) + RMS_EPS)
    return (y * g.astype(jnp.float32)).astype(x.dtype)


def rope_tables(seq, dim):
    inv = ROPE_THETA ** (-jnp.arange(0, dim, 2, dtype=jnp.float32) / dim)
    ang = jnp.arange(seq, dtype=jnp.float32)[:, None] * inv[None, :]
    return jnp.cos(ang), jnp.sin(ang)


def apply_rope(x, cos, sin):
    x1, x2 = jnp.split(x, 2, axis=-1)
    c, s = cos[:, None, :], sin[:, None, :]
    return jnp.concatenate([x1 * c - x2 * s, x2 * c + x1 * s], axis=-1).astype(x.dtype)


def swiglu(x, w_gate, w_up, w_down):
    return (jax.nn.silu(x @ w_gate) * (x @ w_up)) @ w_down


def fox_mixer(h, w_in, b_f, g_q, g_k, w_out):
    B, S, _ = h.shape
    H, dh = N_HEADS, HEAD_DIM
    q, k, v, f, og = jnp.split(h @ w_in, np.cumsum([HD, HD, HD, H]).tolist(), axis=-1)
    q = rms_norm(q.reshape(B, S, H, dh), g_q).transpose(0, 2, 1, 3)
    k = rms_norm(k.reshape(B, S, H, dh), g_k).transpose(0, 2, 1, 3)
    v = v.reshape(B, S, H, dh).transpose(0, 2, 1, 3)
    c = jnp.cumsum(jax.nn.log_sigmoid((f + b_f).astype(jnp.float32)), axis=1).transpose(0, 2, 1)
    nb = S // Q_BLOCK
    qb = q.reshape(B, H, nb, Q_BLOCK, dh).transpose(2, 0, 1, 3, 4)
    cb = c.reshape(B, H, nb, Q_BLOCK).transpose(2, 0, 1, 3)
    k_pos = jnp.arange(S)
    scale = dh ** -0.5

    def block(args):
        i, q_i, c_i = args
        q_pos = i * Q_BLOCK + jnp.arange(Q_BLOCK)
        s = jnp.einsum('bhqd,bhkd->bhqk', q_i, k).astype(jnp.float32) * scale + (c_i[..., None] - c[:, :, None, :])
        p = jax.nn.softmax(jnp.where(k_pos[None, :] <= q_pos[:, None], s, NEG_INF), axis=-1)
        return jnp.einsum('bhqk,bhkd->bhqd', p.astype(v.dtype), v)

    o = lax.map(block, (jnp.arange(nb), qb, cb))
    o = o.transpose(1, 0, 3, 2, 4).reshape(B, S, HD)
    return (o * jax.nn.sigmoid(og)) @ w_out


def nsa_mixer(h, w_in, cmp_pos_k, cmp_pos_v, w_cmp_k, w_cmp_v, g_q, g_k, w_out, cos, sin):
    B, S, _ = h.shape
    H, G, dh = N_HEADS, NSA_KV_HEADS, HEAD_DIM
    P = H // G
    kvw = G * dh
    q, kc, vc, ks, vs, kw, vw, gate = jnp.split(h @ w_in, np.cumsum([HD] + [kvw] * 6).tolist(), axis=-1)
    q = apply_rope(rms_norm(q.reshape(B, S, H, dh), g_q), cos, sin)
    q = q.reshape(B, S, G, P, dh).transpose(0, 2, 3, 1, 4)

    def keys(t):
        return apply_rope(rms_norm(t.reshape(B, S, G, dh), g_k), cos, sin).transpose(0, 2, 1, 3)

    def vals(t):
        return t.reshape(B, S, G, dh).transpose(0, 2, 1, 3)

    kc, ks, kw = keys(kc), keys(ks), keys(kw)
    vc, vs, vw = vals(vc), vals(vs), vals(vw)
    n_cmp = (S - NSA_CMP_LEN) // NSA_CMP_STRIDE + 1
    cmp_start = np.arange(n_cmp) * NSA_CMP_STRIDE
    cmp_idx = cmp_start[:, None] + np.arange(NSA_CMP_LEN)[None, :]
    kc = rms_norm(jnp.einsum('bgnld,lde->bgne', kc[:, :, cmp_idx] + cmp_pos_k, w_cmp_k), g_k)
    vc = jnp.einsum('bgnld,lde->bgne', vc[:, :, cmp_idx] + cmp_pos_v, w_cmp_v)
    cmp_end = jnp.asarray(cmp_start + NSA_CMP_LEN - 1)
    n_sel = S // NSA_SEL_LEN
    sel_start = np.arange(n_sel) * NSA_SEL_LEN
    cover = jnp.asarray(((cmp_start[:, None] < sel_start[None, :] + NSA_SEL_LEN)
                         & (cmp_start[:, None] + NSA_CMP_LEN > sel_start[None, :])).astype(np.float32))
    k_top = min(NSA_N_SEL, n_sel)
    ks_blk = ks.reshape(B, G, n_sel, NSA_SEL_LEN, dh)
    vs_blk = vs.reshape(B, G, n_sel, NSA_SEL_LEN, dh)
    pad = ((0, 0), (0, 0), (NSA_WINDOW, 0), (0, 0))
    kw_pad, vw_pad = jnp.pad(kw, pad), jnp.pad(vw, pad)
    nb = S // Q_BLOCK
    qb = q.reshape(B, G, P, nb, Q_BLOCK, dh).transpose(3, 0, 1, 2, 4, 5)
    b_idx = jnp.arange(B)[:, None, None, None]
    g_idx = jnp.arange(G)[None, :, None, None]
    blk = jnp.arange(n_sel)
    scale = dh ** -0.5

    def block(args):
        i, q_i = args
        q_pos = i * Q_BLOCK + jnp.arange(Q_BLOCK)
        s = jnp.einsum('bgpqd,bgnd->bgpqn', q_i, kc).astype(jnp.float32) * scale
        valid = cmp_end[None, :] <= q_pos[:, None]
        p_c = jax.nn.softmax(jnp.where(valid, s, NEG_INF), axis=-1) * valid
        o_c = jnp.einsum('bgpqn,bgnd->bgpqd', p_c.astype(vc.dtype), vc)
        imp = jnp.einsum('bgpqn,nj->bgqj', p_c, cover)
        cur = q_pos[:, None] // NSA_SEL_LEN
        forced = (blk[None, :] == 0) | (blk[None, :] == cur) | (blk[None, :] == cur - 1)
        imp = jnp.where(forced, NSA_FORCED_SCORE, jnp.where(blk[None, :] <= cur, imp, -1.0))
        _, sel = lax.top_k(imp, k_top)
        k_g = ks_blk[b_idx, g_idx, sel]
        v_g = vs_blk[b_idx, g_idx, sel]
        s = jnp.einsum('bgpqd,bgqkld->bgpqkl', q_i, k_g).astype(jnp.float32) * scale
        k_pos = sel[..., None] * NSA_SEL_LEN + jnp.arange(NSA_SEL_LEN)
        ok = (k_pos <= q_pos[:, None, None])[:, :, None]
        s = jnp.where(ok, s, NEG_INF).reshape(B, G, P, Q_BLOCK, k_top * NSA_SEL_LEN)
        p_s = jax.nn.softmax(s, axis=-1).reshape(B, G, P, Q_BLOCK, k_top, NSA_SEL_LEN)
        o_s = jnp.einsum('bgpqkl,bgqkld->bgpqd', p_s.astype(v_g.dtype), v_g)
        k_w = lax.dynamic_slice_in_dim(kw_pad, i * Q_BLOCK, NSA_WINDOW + Q_BLOCK, axis=2)
        v_w = lax.dynamic_slice_in_dim(vw_pad, i * Q_BLOCK, NSA_WINDOW + Q_BLOCK, axis=2)
        w_pos = i * Q_BLOCK - NSA_WINDOW + jnp.arange(NSA_WINDOW + Q_BLOCK)
        diff = q_pos[:, None] - w_pos[None, :]
        ok_w = (diff >= 0) & (diff < NSA_WINDOW) & (w_pos[None, :] >= 0)
        s = jnp.einsum('bgpqd,bgkd->bgpqk', q_i, k_w).astype(jnp.float32) * scale
        p_w = jax.nn.softmax(jnp.where(ok_w, s, NEG_INF), axis=-1)
        o_w = jnp.einsum('bgpqk,bgkd->bgpqd', p_w.astype(v_w.dtype), v_w)
        return jnp.stack([o_c, o_s, o_w], axis=-2)

    o = lax.map(block, (jnp.arange(nb), qb))
    o = o.transpose(1, 0, 4, 2, 3, 5, 6).reshape(B, S, H, 3, dh)
    g = jax.nn.sigmoid(gate.reshape(B, S, H, 3))
    o = jnp.einsum('bshc,bshcd->bshd', g.astype(o.dtype), o)
    return o.reshape(B, S, HD) @ w_out


def gla_mixer(h, w_in, w_gate_up, b_gate, g_out, w_out):
    B, S, _ = h.shape
    Hg = GLA_HEADS
    dk, dv = GLA_KEY_DIM // Hg, GLA_VALUE_DIM // Hg
    C = GLA_CHUNK
    nc = S // C
    q, k, v, g_low, r = jnp.split(h @ w_in, np.cumsum([GLA_KEY_DIM, GLA_KEY_DIM, GLA_VALUE_DIM, GLA_GATE_RANK]).tolist(), axis=-1)
    log_a = jax.nn.log_sigmoid((g_low @ w_gate_up + b_gate).astype(jnp.float32)) / GLA_TAU

    def chunked(t, d):
        return t.reshape(B, nc, C, Hg, d).transpose(0, 3, 1, 2, 4).astype(jnp.float32)

    q = chunked(q, dk) * (dk ** -0.5)
    k = chunked(k, dk)
    v = chunked(v, dv)
    b = jnp.cumsum(chunked(log_a, dk), axis=3)
    b_last = b[:, :, :, -1:, :]
    q_dec = q * jnp.exp(b)
    causal = jnp.tril(jnp.ones((C, C), dtype=bool))
    attn = jnp.where(causal, jnp.einsum('bhncd,bhnsd->bhncs', q_dec, k * jnp.exp(-b)), 0.0)
    o_intra = jnp.einsum('bhncs,bhnse->bhnce', attn, v)
    u = jnp.einsum('bhncd,bhnce->bhnde', k * jnp.exp(b_last - b), v)
    decay = jnp.exp(b_last[:, :, :, 0, :])

    def step(state, inp):
        q_c, dec_c, u_c = inp
        o_c = jnp.einsum('bhcd,bhde->bhce', q_c, state)
        return dec_c[..., None] * state + u_c, o_c

    state0 = jnp.zeros((B, Hg, dk, dv), jnp.float32)
    _, o_inter = lax.scan(step, state0, (jnp.moveaxis(q_dec, 2, 0), jnp.moveaxis(decay, 2, 0), jnp.moveaxis(u, 2, 0)))
    o = o_intra + jnp.moveaxis(o_inter, 0, 2)
    o = o.transpose(0, 2, 3, 1, 4).reshape(B, S, Hg, dv)
    o = rms_norm(o, g_out) * jax.nn.silu(r.reshape(B, S, Hg, dv).astype(jnp.float32))
    return o.reshape(B, S, GLA_VALUE_DIM).astype(h.dtype) @ w_out


def sb_mixer(h, w_in, w_out):
    B, S, _ = h.shape
    H, dh = N_HEADS, HEAD_DIM
    q, k, v = [t.reshape(B, S, H, dh).transpose(0, 2, 1, 3) for t in jnp.split(h @ w_in, 3, axis=-1)]
    nb = S // Q_BLOCK
    qb = q.reshape(B, H, nb, Q_BLOCK, dh).transpose(2, 0, 1, 3, 4)
    k_pos = jnp.arange(S)
    scale = dh ** -0.5

    def block(args):
        i, q_i = args
        q_pos = i * Q_BLOCK + jnp.arange(Q_BLOCK)
        z = jnp.einsum('bhqd,bhkd->bhqk', q_i, k).astype(jnp.float32) * scale
        strict = k_pos[None, :] < q_pos[:, None]
        log_1m = jnp.where(strict, jax.nn.log_sigmoid(-z), 0.0)
        later = lax.cumsum(log_1m, axis=3, reverse=True) - log_1m
        a = jnp.where(strict, jnp.exp(jax.nn.log_sigmoid(z) + later), 0.0)
        return jnp.einsum('bhqk,bhkd->bhqd', a.astype(v.dtype), v)

    o = lax.map(block, (jnp.arange(nb), qb))
    return o.transpose(1, 0, 3, 2, 4).reshape(B, S, HD) @ w_out


def _layers_with_mixer(m):
    return len(range(m, DEPTH, N_MIXERS))


def setup_inputs(seed: int = 0) -> dict:
    keys = iter(jax.random.split(jax.random.key(seed), 40))

    def normal(shape, scale):
        return jax.random.normal(next(keys), shape, jnp.float32) * scale

    def dense(shape):
        return normal(shape, shape[-2] ** -0.5)

    def gain(shape):
        return 1.0 + normal(shape, 0.02)

    nA, nB, nC, nD = (_layers_with_mixer(m) for m in range(N_MIXERS))
    D, F, dh = D_MODEL, FFN_DIM, HEAD_DIM
    fox_cols = 3 * HD + N_HEADS + D
    nsa_cols = HD + 6 * NSA_KV_HEADS * dh + 3 * N_HEADS
    gla_cols = 2 * GLA_KEY_DIM + GLA_VALUE_DIM + GLA_GATE_RANK + GLA_VALUE_DIM
    cmp_scale = (NSA_CMP_LEN * dh) ** -0.5
    return {
        'x': normal((BATCH, SEQ, D), 1.0),
        'norm_g': gain((DEPTH, 3, D)),
        'ffn1_w_gate': dense((DEPTH, D, F)),
        'ffn1_w_up': dense((DEPTH, D, F)),
        'ffn1_w_down': dense((DEPTH, F, D)),
        'ffn2_w_gate': dense((DEPTH, D, F)),
        'ffn2_w_up': dense((DEPTH, D, F)),
        'ffn2_w_down': dense((DEPTH, F, D)),
        'fox_w_in': dense((nA, D, fox_cols)),
        'fox_b_f': normal((nA, N_HEADS), 0.1),
        'fox_g_q': gain((nA, dh)),
        'fox_g_k': gain((nA, dh)),
        'fox_w_out': dense((nA, HD, D)),
        'nsa_w_in': dense((nB, D, nsa_cols)),
        'nsa_cmp_pos_k': normal((nB, NSA_CMP_LEN, dh), 0.02),
        'nsa_cmp_pos_v': normal((nB, NSA_CMP_LEN, dh), 0.02),
        'nsa_w_cmp_k': normal((nB, NSA_CMP_LEN, dh, dh), cmp_scale),
        'nsa_w_cmp_v': normal((nB, NSA_CMP_LEN, dh, dh), cmp_scale),
        'nsa_g_q': gain((nB, dh)),
        'nsa_g_k': gain((nB, dh)),
        'nsa_w_out': dense((nB, HD, D)),
        'gla_w_in': dense((nC, D, gla_cols)),
        'gla_w_gate_up': dense((nC, GLA_GATE_RANK, GLA_KEY_DIM)),
        'gla_b_gate': normal((nC, GLA_KEY_DIM), 0.1),
        'gla_g_out': gain((nC, GLA_VALUE_DIM // GLA_HEADS)),
        'gla_w_out': dense((nC, GLA_VALUE_DIM, D)),
        'sb_w_in': dense((nD, D, 3 * HD)),
        'sb_w_out': dense((nD, HD, D)),
    }


def reference(x, norm_g, ffn1_w_gate, ffn1_w_up, ffn1_w_down, ffn2_w_gate, ffn2_w_up, ffn2_w_down,
              fox_w_in, fox_b_f, fox_g_q, fox_g_k, fox_w_out,
              nsa_w_in, nsa_cmp_pos_k, nsa_cmp_pos_v, nsa_w_cmp_k, nsa_w_cmp_v, nsa_g_q, nsa_g_k, nsa_w_out,
              gla_w_in, gla_w_gate_up, gla_b_gate, gla_g_out, gla_w_out,
              sb_w_in, sb_w_out):
    cos, sin = rope_tables(x.shape[1], HEAD_DIM)
    for i in range(DEPTH):
        m, j = i % N_MIXERS, i // N_MIXERS
        x = x + 0.5 * swiglu(rms_norm(x, norm_g[i, 0]), ffn1_w_gate[i], ffn1_w_up[i], ffn1_w_down[i])
        h = rms_norm(x, norm_g[i, 1])
        if m == 0:
            y = fox_mixer(h, fox_w_in[j], fox_b_f[j], fox_g_q[j], fox_g_k[j], fox_w_out[j])
        elif m == 1:
            y = nsa_mixer(h, nsa_w_in[j], nsa_cmp_pos_k[j], nsa_cmp_pos_v[j], nsa_w_cmp_k[j], nsa_w_cmp_v[j],
                          nsa_g_q[j], nsa_g_k[j], nsa_w_out[j], cos, sin)
        elif m == 2:
            y = gla_mixer(h, gla_w_in[j], gla_w_gate_up[j], gla_b_gate[j], gla_g_out[j], gla_w_out[j])
        else:
            y = sb_mixer(h, sb_w_in[j], sb_w_out[j])
        x = x + y
        x = x + 0.5 * swiglu(rms_norm(x, norm_g[i, 2]), ffn2_w_gate[i], ffn2_w_up[i], ffn2_w_down[i])
    return x
```

```python
import functools

import numpy as np
import jax
import jax.numpy as jnp
from jax import lax
from jax.experimental import pallas as pl
from jax.experimental.pallas import tpu as pltpu

F32 = jnp.float32
BF16 = jnp.bfloat16

N_HEADS = 16
HEAD_DIM = 64
ROPE_THETA = 10000.0
RMS_EPS = 1e-6
NEG_INF = -1e30
NSA_KV_HEADS = 4
NSA_CMP_LEN = 32
NSA_CMP_STRIDE = 16
NSA_SEL_LEN = 64
NSA_N_SEL = 16
NSA_WINDOW = 512
NSA_FORCED_SCORE = 1e9
GLA_HEADS = 4
GLA_GATE_RANK = 16
GLA_TAU = 16.0
GLA_CHUNK = 64

LANES = 128
V7X_VMEM_BYTES = 64 * 1024 * 1024
VMEM_LIMIT = V7X_VMEM_BYTES - 8 * 1024 * 1024

_NT = (((1,), (1,)), ((), ()))
_TN = (((0,), (0,)), ((), ()))


def _cparams(*sem):
    return pltpu.CompilerParams(dimension_semantics=sem, vmem_limit_bytes=VMEM_LIMIT)


def _rms_rows(x, g):
    return x * lax.rsqrt(jnp.mean(x * x, axis=-1, keepdims=True) + RMS_EPS) * g


def _softplus(z):
    return jnp.maximum(z, 0.0) + jnp.log(1.0 + jnp.exp(-jnp.abs(z)))


def _sigmoid(z):
    return 1.0 / (1.0 + jnp.exp(-z))


def _pick_tile(n, target):
    t = min(n, target)
    while n % t:
        t //= 2
    return t


def _ffn_kernel(x_ref, g_ref, wg_ref, wu_ref, wd_ref, o_ref, h_ref, acc_ref):
    j = pl.program_id(1)

    @pl.when(j == 0)
    def _():
        h_ref[...] = _rms_rows(x_ref[...], g_ref[...]).astype(BF16)
        acc_ref[...] = jnp.zeros_like(acc_ref)

    h = h_ref[...]
    gate = jnp.dot(h, wg_ref[...], preferred_element_type=F32)
    up = jnp.dot(h, wu_ref[...], preferred_element_type=F32)
    a = (gate * _sigmoid(gate) * up).astype(BF16)
    acc_ref[...] += jnp.dot(a, wd_ref[...], preferred_element_type=F32)

    @pl.when(j == pl.num_programs(1) - 1)
    def _():
        o_ref[...] = x_ref[...] + 0.5 * acc_ref[...]


def _ffn(x, g, wg, wu, wd):
    T, D = x.shape
    F = wg.shape[1]
    tm = _pick_tile(T, 1024)
    tf = 256 if F % 256 == 0 else F
    return pl.pallas_call(
        _ffn_kernel,
        grid=(T // tm, F // tf),
        in_specs=[
            pl.BlockSpec((tm, D), lambda i, j: (i, 0)),
            pl.BlockSpec((1, D), lambda i, j: (0, 0)),
            pl.BlockSpec((D, tf), lambda i, j: (0, j)),
            pl.BlockSpec((D, tf), lambda i, j: (0, j)),
            pl.BlockSpec((tf, D), lambda i, j: (j, 0)),
        ],
        out_specs=pl.BlockSpec((tm, D), lambda i, j: (i, 0)),
        out_shape=jax.ShapeDtypeStruct((T, D), F32),
        scratch_shapes=[pltpu.VMEM((tm, D), BF16), pltpu.VMEM((tm, D), F32)],
        compiler_params=_cparams("parallel", "arbitrary"),
        name="ffn",
    )(x, g.reshape(1, D), wg, wu, wd)


def _proj_kernel(*refs, mode, tn):
    if mode == "plain":
        x_ref, g_ref, w_ref, o_ref, h_ref = refs
    elif mode == "headnorm":
        x_ref, g_ref, w_ref, hg_ref, gsum_ref, o_ref, h_ref = refs
    else:
        x_ref, g_ref, w_ref, hg_ref, gsum_ref, cos_ref, sin_ref, o_ref, h_ref = refs

    @pl.when(pl.program_id(1) == 0)
    def _():
        h_ref[...] = _rms_rows(x_ref[...], g_ref[...]).astype(BF16)

    y = jnp.dot(h_ref[...], w_ref[...], preferred_element_type=F32)
    if mode != "plain":
        ss = jnp.dot((y * y).astype(BF16), gsum_ref[...], preferred_element_type=F32)
        y = y * lax.rsqrt(ss * (1.0 / HEAD_DIM) + RMS_EPS) * hg_ref[...]
    if mode == "headnorm_rope":
        lane = lax.broadcasted_iota(jnp.int32, (1, tn), 1)
        first_half = (lane % HEAD_DIM) < (HEAD_DIM // 2)
        partner = jnp.where(first_half,
                            pltpu.roll(y, tn - HEAD_DIM // 2, 1),
                            pltpu.roll(y, HEAD_DIM // 2, 1))
        y = y * cos_ref[...] + partner * sin_ref[...]
    o_ref[...] = y.astype(o_ref.dtype)


def _proj(x, g, w, *, seq, out_dtype=BF16, head_gain=None, rope=None):
    T, D = x.shape
    N = w.shape[1]
    tm = _pick_tile(seq, 1024)
    mode = "plain" if head_gain is None else ("headnorm" if rope is None else "headnorm_rope")
    tn = 256 if mode != "plain" else _pick_tile(N, 512)
    assert N % tn == 0 and T % tm == 0
    args = [x, g.reshape(1, D), w]
    in_specs = [
        pl.BlockSpec((tm, D), lambda i, j: (i, 0)),
        pl.BlockSpec((1, D), lambda i, j: (0, 0)),
        pl.BlockSpec((D, tn), lambda i, j: (0, j)),
    ]
    if mode != "plain":
        gidx = np.arange(tn) // HEAD_DIM
        gsum = jnp.asarray((gidx[:, None] == gidx[None, :]).astype(np.float32), BF16)
        args += [head_gain.reshape(1, N), gsum]
        in_specs += [pl.BlockSpec((1, tn), lambda i, j: (0, j)),
                     pl.BlockSpec((tn, tn), lambda i, j: (0, 0))]
    if mode == "headnorm_rope":
        cos_t, sin_t = rope
        nblk = seq // tm
        args += [cos_t, sin_t]
        in_specs += [pl.BlockSpec((tm, tn), lambda i, j: (i % nblk, 0)),
                     pl.BlockSpec((tm, tn), lambda i, j: (i % nblk, 0))]
    return pl.pallas_call(
        functools.partial(_proj_kernel, mode=mode, tn=tn),
        grid=(T // tm, N // tn),
        in_specs=in_specs,
        out_specs=pl.BlockSpec((tm, tn), lambda i, j: (i, j)),
        out_shape=jax.ShapeDtypeStruct((T, N), out_dtype),
        scratch_shapes=[pltpu.VMEM((tm, D), BF16)],
        compiler_params=_cparams("parallel", "arbitrary"),
        name="proj_" + mode,
    )(*args)


def _outproj_kernel(*refs, mode):
    if mode == "plain":
        x_ref, a_ref, w_ref, o_ref = refs
        a = a_ref[...]
    elif mode == "sigmoid_gate":
        x_ref, a_ref, og_ref, w_ref, o_ref = refs
        a = (a_ref[...].astype(F32) * _sigmoid(og_ref[...].astype(F32))).astype(BF16)
    else:
        x_ref, oc_ref, os_ref, ow_ref, gate_ref, exp_ref, w_ref, o_ref = refs
        gs = _sigmoid(gate_ref[...]).astype(BF16)
        a = None
        for c, o_c in enumerate((oc_ref, os_ref, ow_ref)):
            ge = jnp.dot(gs, exp_ref[c], preferred_element_type=F32)
            t = ge * o_c[...].astype(F32)
            a = t if a is None else a + t
        a = a.astype(BF16)
    o_ref[...] = x_ref[...] + jnp.dot(a, w_ref[...], preferred_element_type=F32)


def _outproj(x, w, *, mode, acts):
    T, D = x.shape
    K = w.shape[0]
    tm = _pick_tile(T, 512)
    row = lambda i: (i, 0)
    args = [x]
    in_specs = [pl.BlockSpec((tm, D), row)]
    for a in acts:
        if a.ndim == 3:
            args.append(a)
            in_specs.append(pl.BlockSpec(a.shape, lambda i: (0, 0, 0)))
        else:
            args.append(a)
            in_specs.append(pl.BlockSpec((tm, a.shape[1]), row))
    args.append(w)
    in_specs.append(pl.BlockSpec((K, D), lambda i: (0, 0)))
    return pl.pallas_call(
        functools.partial(_outproj_kernel, mode=mode),
        grid=(T // tm,),
        in_specs=in_specs,
        out_specs=pl.BlockSpec((tm, D), row),
        out_shape=jax.ShapeDtypeStruct((T, D), F32),
        compiler_params=_cparams("parallel"),
        name="outproj_" + mode,
    )(*args)


def _fox_c_kernel(x_ref, g_ref, wf_ref, bf_ref, tri_ref, c_ref, carry_ref):
    @pl.when(pl.program_id(1) == 0)
    def _():
        carry_ref[...] = jnp.zeros_like(carry_ref)

    h = _rms_rows(x_ref[...], g_ref[...])
    f = lax.dot_general(wf_ref[...], h, _NT, precision=lax.Precision.HIGHEST,
                        preferred_element_type=F32) + bf_ref[...]
    ls = jnp.minimum(f, 0.0) - jnp.log(1.0 + jnp.exp(-jnp.abs(f)))
    cs = jnp.dot(ls, tri_ref[...], precision=lax.Precision.HIGHEST,
                 preferred_element_type=F32) + carry_ref[...]
    c_ref[...] = cs
    carry_ref[...] = cs[:, cs.shape[1] - 1:]


def _fox_c(x, g, wf_t, b_f, *, batch, seq):
    T, D = x.shape
    H = wf_t.shape[0]
    tm = _pick_tile(seq, 512)
    ns = seq // tm
    tri = jnp.asarray(np.triu(np.ones((tm, tm), np.float32)))
    return pl.pallas_call(
        _fox_c_kernel,
        grid=(batch, ns),
        in_specs=[
            pl.BlockSpec((tm, D), lambda b, s: (b * ns + s, 0)),
            pl.BlockSpec((1, D), lambda b, s: (0, 0)),
            pl.BlockSpec((H, D), lambda b, s: (0, 0)),
            pl.BlockSpec((H, 1), lambda b, s: (0, 0)),
            pl.BlockSpec((tm, tm), lambda b, s: (0, 0)),
        ],
        out_specs=pl.BlockSpec((None, H, tm), lambda b, s: (b, 0, s)),
        out_shape=jax.ShapeDtypeStruct((batch, H, seq), F32),
        scratch_shapes=[pltpu.VMEM((H, 1), F32)],
        compiler_params=_cparams("parallel", "arbitrary"),
        name="fox_c",
    )(x, g.reshape(1, D), wf_t, b_f.reshape(H, 1), tri)


def _fox_attn_kernel(q_ref, k_ref, v_ref, c_ref, o_ref, vaug_ref, *, t):
    qi = pl.program_id(2)
    lane = lax.broadcasted_iota(jnp.int32, (1, LANES), 1)
    upper = lane >= HEAD_DIM

    @pl.when(qi == 0)
    def _():
        v = v_ref[...]
        one = jnp.ones_like(v)
        vaug_ref[0] = jnp.where(upper, one, v)
        vaug_ref[1] = jnp.where(upper, v, one)

    q0 = pl.multiple_of(qi * t, t)
    qslab = q_ref[...]
    zero = jnp.zeros_like(qslab)
    qpos = lax.broadcasted_iota(jnp.int32, (t, t), 0)
    kpos = lax.broadcasted_iota(jnp.int32, (t, t), 1)
    causal = kpos <= qpos

    accs = []
    for i in range(2):
        qh = jnp.where(upper, qslab, zero) if i else jnp.where(upper, zero, qslab)
        c_q0 = c_ref[i, :, pl.ds(q0, LANES)][:, 0:1]

        def tile(k0, m, acc, masked, qh=qh, c_q0=c_q0, i=i):
            s = lax.dot_general(qh, k_ref[pl.ds(k0, t), :], _NT, preferred_element_type=F32)
            s = s + (c_q0 - c_ref[i, :, pl.ds(k0, t)])
            if masked:
                s = jnp.where(causal, s, NEG_INF)
            m_new = jnp.maximum(m, jnp.max(s, axis=1, keepdims=True))
            p = jnp.exp(s - m_new).astype(BF16)
            acc = jnp.exp(m - m_new) * acc + jnp.dot(p, vaug_ref[i, pl.ds(k0, t), :],
                                                     preferred_element_type=F32)
            return m_new, acc

        def body(j, carry):
            return tile(pl.multiple_of(j * t, t), *carry, masked=False)

        m0 = jnp.full((t, 1), NEG_INF, F32)
        acc0 = jnp.zeros((t, LANES), F32)
        m, acc = lax.fori_loop(0, qi, body, (m0, acc0))
        m, acc = tile(q0, m, acc, masked=True)
        accs.append(acc)

    num = jnp.where(upper, accs[1], accs[0])
    den = jnp.where(upper, accs[0], accs[1])
    den = pltpu.roll(den, HEAD_DIM, 1)
    o_ref[...] = (num / den).astype(o_ref.dtype)


def _fox_attn(qk, vg, c, *, batch, seq):
    T = qk.shape[0]
    HD = N_HEADS * HEAD_DIM
    t = _pick_tile(seq, 512)
    nq = seq // t
    npair = HD // LANES
    return pl.pallas_call(
        functools.partial(_fox_attn_kernel, t=t),
        grid=(batch, npair, nq),
        in_specs=[
            pl.BlockSpec((t, LANES), lambda b, h, q: (b * nq + q, h)),
            pl.BlockSpec((seq, LANES), lambda b, h, q: (b, npair + h)),
            pl.BlockSpec((seq, LANES), lambda b, h, q: (b, h)),
            pl.BlockSpec((None, 2, 1, seq), lambda b, h, q: (b, h, 0, 0)),
        ],
        out_specs=pl.BlockSpec((t, LANES), lambda b, h, q: (b * nq + q, h)),
        out_shape=jax.ShapeDtypeStruct((T, HD), BF16),
        scratch_shapes=[pltpu.VMEM((2, seq, LANES), BF16)],
        compiler_params=_cparams("parallel", "arbitrary", "arbitrary"),
        name="fox_attn",
    )(qk, qk, vg, c.reshape(batch, N_HEADS, 1, seq))


def _fox_mixer(x, g, w_in, b_f, g_q, g_k, w_out, *, batch, seq):
    HD = N_HEADS * HEAD_DIM
    scale = HEAD_DIM ** -0.5
    w_qk = w_in[:, :2 * HD].astype(BF16)
    w_vg = jnp.concatenate([w_in[:, 2 * HD:3 * HD], w_in[:, 3 * HD + N_HEADS:]], axis=1).astype(BF16)
    wf_t = w_in[:, 3 * HD:3 * HD + N_HEADS].T
    head_gain = jnp.concatenate([jnp.tile(g_q * scale, N_HEADS), jnp.tile(g_k, N_HEADS)])
    qk = _proj(x, g, w_qk, seq=seq, head_gain=head_gain)
    vg = _proj(x, g, w_vg, seq=seq)
    c = _fox_c(x, g, wf_t, b_f, batch=batch, seq=seq)
    o = _fox_attn(qk, vg, c, batch=batch, seq=seq)
    og = vg
    return _outproj_fox(x, o, og, w_out.astype(BF16))


def _outproj_fox(x, o, vg, w):
    T, D = x.shape
    HD = o.shape[1]
    tm = _pick_tile(T, 512)
    return pl.pallas_call(
        functools.partial(_outproj_kernel, mode="sigmoid_gate"),
        grid=(T // tm,),
        in_specs=[
            pl.BlockSpec((tm, D), lambda i: (i, 0)),
            pl.BlockSpec((tm, HD), lambda i: (i, 0)),
            pl.BlockSpec((tm, HD), lambda i: (i, 1)),
            pl.BlockSpec((HD, D), lambda i: (0, 0)),
        ],
        out_specs=pl.BlockSpec((tm, D), lambda i: (i, 0)),
        out_shape=jax.ShapeDtypeStruct((T, D), F32),
        compiler_params=_cparams("parallel"),
        name="outproj_fox",
    )(x, o, vg, w)


def _sb_attn_kernel(q_ref, k_ref, v_ref, tri_ref, o_ref, *, tq, tk):
    qi = pl.program_id(2)
    lane = lax.broadcasted_iota(jnp.int32, (1, LANES), 1)
    upper = lane >= HEAD_DIM
    q0 = pl.multiple_of(qi * tq, tq)
    qslab = q_ref[...]
    zero = jnp.zeros_like(qslab)
    qpos = lax.broadcasted_iota(jnp.int32, (tq, tk), 0)
    kpos = lax.broadcasted_iota(jnp.int32, (tq, tk), 1)
    tri = tri_ref[...]

    accs = []
    for i in range(2):
        qh = jnp.where(upper, qslab, zero) if i else jnp.where(upper, zero, qslab)

        def tile(k0, rest, acc, diag_offset, qh=qh):
            z = lax.dot_general(qh, k_ref[pl.ds(k0, tk), :], _NT, preferred_element_type=F32)
            sp = _softplus(z)
            log_1m = -sp
            if diag_offset is not None:
                strict = kpos + diag_offset < qpos
                log_1m = jnp.where(strict, log_1m, 0.0)
            hi = log_1m.astype(BF16)
            lo = (log_1m - hi.astype(F32)).astype(BF16)
            later = (jnp.dot(hi, tri, preferred_element_type=F32)
                     + jnp.dot(lo, tri, preferred_element_type=F32)) + rest
            a = jnp.exp((z - sp) + later)
            if diag_offset is not None:
                a = jnp.where(strict, a, 0.0)
            acc = acc + jnp.dot(a.astype(BF16), v_ref[pl.ds(k0, tk), :], preferred_element_type=F32)
            rest = later[:, 0:1] + log_1m[:, 0:1]
            return rest, acc

        carry = (jnp.zeros((tq, 1), F32), jnp.zeros((tq, LANES), F32))
        for d in reversed(range(tq // tk)):
            carry = tile(q0 + d * tk, *carry, diag_offset=d * tk)
        n_full = qi * (tq // tk)

        def body(jj, c):
            return tile(pl.multiple_of((n_full - 1 - jj) * tk, tk), *c, diag_offset=None)

        _, acc = lax.fori_loop(0, n_full, body, carry)
        accs.append(acc)
    o_ref[...] = jnp.where(upper, accs[1], accs[0]).astype(o_ref.dtype)


def _sb_attn(qkv, *, batch, seq):
    T = qkv.shape[0]
    HD = N_HEADS * HEAD_DIM
    tq = _pick_tile(seq, 512)
    tk = 256
    nq = seq // tq
    npair = HD // LANES
    tri = jnp.asarray(np.tril(np.ones((tk, tk), np.float32), -1), BF16)
    return pl.pallas_call(
        functools.partial(_sb_attn_kernel, tq=tq, tk=tk),
        grid=(batch, npair, nq),
        in_specs=[
            pl.BlockSpec((tq, LANES), lambda b, h, q: (b * nq + q, h)),
            pl.BlockSpec((seq, LANES), lambda b, h, q: (b, npair + h)),
            pl.BlockSpec((seq, LANES), lambda b, h, q: (b, 2 * npair + h)),
            pl.BlockSpec((tk, tk), lambda b, h, q: (0, 0)),
        ],
        out_specs=pl.BlockSpec((tq, LANES), lambda b, h, q: (b * nq + q, h)),
        out_shape=jax.ShapeDtypeStruct((T, HD), BF16),
        compiler_params=_cparams("parallel", "parallel", "arbitrary"),
        name="sb_attn",
    )(qkv, qkv, qkv, tri)


def _sb_mixer(x, g, w_in, w_out, *, batch, seq):
    HD = N_HEADS * HEAD_DIM
    scale = HEAD_DIM ** -0.5
    w = jnp.concatenate([w_in[:, :HD] * scale, w_in[:, HD:]], axis=1).astype(BF16)
    qkv = _proj(x, g, w, seq=seq)
    o = _sb_attn(qkv, batch=batch, seq=seq)
    return _outproj(x, w_out.astype(BF16), mode="plain", acts=[o])


def _gla_kernel(q_ref, k_ref, v_ref, r_ref, glow_ref, wgu_ref, bg_ref, gout_ref, tri_ref,
                o_ref, state_ref, *, tm, dk):
    C = GLA_CHUNK

    @pl.when(pl.program_id(2) == 0)
    def _():
        state_ref[...] = jnp.zeros_like(state_ref)

    gate = jnp.dot(glow_ref[...], wgu_ref[...], precision=lax.Precision.HIGHEST,
                   preferred_element_type=F32) + bg_ref[...]
    log_a = (jnp.minimum(gate, 0.0) - jnp.log(1.0 + jnp.exp(-jnp.abs(gate)))) * (1.0 / GLA_TAU)
    row = lax.broadcasted_iota(jnp.int32, (C, C), 0)
    col = lax.broadcasted_iota(jnp.int32, (C, C), 1)
    causal = col <= row
    tri = tri_ref[...]
    q_scale = dk ** -0.5

    for c in range(tm // C):
        sl = slice(c * C, (c + 1) * C)
        b = jnp.dot(tri, log_a[sl], precision=lax.Precision.HIGHEST, preferred_element_type=F32)
        b_last = b[C - 1:C]
        q = q_ref[sl, :].astype(F32) * q_scale
        k = k_ref[sl, :].astype(F32)
        v = v_ref[sl, :]
        q_dec = (q * jnp.exp(b)).astype(BF16)
        k_neg = (k * jnp.exp(-b)).astype(BF16)
        k_rem = (k * jnp.exp(b_last - b)).astype(BF16)
        attn = lax.dot_general(q_dec, k_neg, _NT, preferred_element_type=F32)
        attn = jnp.where(causal, attn, 0.0).astype(BF16)
        state_t = state_ref[...]
        o = (jnp.dot(attn, v, preferred_element_type=F32)
             + lax.dot_general(q_dec, state_t.astype(BF16), _NT, preferred_element_type=F32))
        u_t = lax.dot_general(v, k_rem, _TN, preferred_element_type=F32)
        state_ref[...] = state_t * jnp.exp(b_last) + u_t
        y = o * lax.rsqrt(jnp.mean(o * o, axis=-1, keepdims=True) + RMS_EPS) * gout_ref[...]
        r = r_ref[sl, :].astype(F32)
        o_ref[sl, :] = (y * (r * _sigmoid(r))).astype(o_ref.dtype)


def _gla(qkvr, glow, wgu, b_gate, g_out, *, batch, seq):
    T = qkvr.shape[0]
    Hg = GLA_HEADS
    dk_total = wgu.shape[1]
    dk = dk_total // Hg
    dv = (qkvr.shape[1] - 2 * dk_total) // 2 // Hg
    tm = _pick_tile(seq, 512)
    ns = seq // tm
    kq, kv = dk_total // dk, dk_total // dv
    tri = jnp.asarray(np.tril(np.ones((GLA_CHUNK, GLA_CHUNK), np.float32)))
    rowmap = lambda off: (lambda b, h, s: (b * ns + s, off + h))
    return pl.pallas_call(
        functools.partial(_gla_kernel, tm=tm, dk=dk),
        grid=(batch, Hg, ns),
        in_specs=[
            pl.BlockSpec((tm, dk), rowmap(0)),
            pl.BlockSpec((tm, dk), rowmap(kq)),
            pl.BlockSpec((tm, dv), rowmap(2 * kv)),
            pl.BlockSpec((tm, dv), rowmap(2 * kv + Hg)),
            pl.BlockSpec((tm, LANES), lambda b, h, s: (b * ns + s, 0)),
            pl.BlockSpec((LANES, dk), lambda b, h, s: (0, h)),
            pl.BlockSpec((1, dk), lambda b, h, s: (0, h)),
            pl.BlockSpec((1, dv), lambda b, h, s: (0, 0)),
            pl.BlockSpec((GLA_CHUNK, GLA_CHUNK), lambda b, h, s: (0, 0)),
        ],
        out_specs=pl.BlockSpec((tm, dv), rowmap(0)),
        out_shape=jax.ShapeDtypeStruct((T, Hg * dv), BF16),
        scratch_shapes=[pltpu.VMEM((dv, dk), F32)],
        compiler_params=_cparams("parallel", "parallel", "arbitrary"),
        name="gla",
    )(qkvr, qkvr, qkvr, qkvr, glow, wgu, b_gate.reshape(1, dk_total), g_out.reshape(1, dv), tri)


def _gla_mixer(x, g, w_in, w_gate_up, b_gate, g_out, w_out, *, batch, seq):
    dk_total = w_gate_up.shape[1]
    dv_total = w_out.shape[0]
    lo = 2 * dk_total + dv_total
    w_main = jnp.concatenate([w_in[:, :lo], w_in[:, lo + GLA_GATE_RANK:]], axis=1).astype(BF16)
    w_low = jnp.pad(w_in[:, lo:lo + GLA_GATE_RANK], ((0, 0), (0, LANES - GLA_GATE_RANK))).astype(BF16)
    wgu = jnp.pad(w_gate_up, ((0, LANES - GLA_GATE_RANK), (0, 0)))
    qkvr = _proj(x, g, w_main, seq=seq)
    glow = _proj(x, g, w_low, seq=seq, out_dtype=F32)
    o = _gla(qkvr, glow, wgu, b_gate, g_out, batch=batch, seq=seq)
    return _outproj(x, w_out.astype(BF16), mode="plain", acts=[o])


def _nsa_compress_kernel(kc_ref, vc_ref, pk_ref, pv_ref, wk_ref, wv_ref, gk_ref, ko_ref, vo_ref, *, n16):
    def compress(r_ref, pos_ref, w_ref):
        r = r_ref[...].astype(F32)
        first = jnp.dot((r + pos_ref[0:1, :]).astype(BF16), w_ref[0], preferred_element_type=F32)
        second = jnp.dot((r + pos_ref[1:2, :]).astype(BF16), w_ref[1], preferred_element_type=F32)
        return first + pltpu.roll(second, n16 - 1, 0)

    kc = compress(kc_ref, pk_ref, wk_ref)
    ko_ref[...] = _rms_rows(kc, gk_ref[...]).astype(ko_ref.dtype)
    vo_ref[...] = compress(vc_ref, pv_ref, wv_ref).astype(vo_ref.dtype)


def _nsa_compress(kc_r, vc_r, pos_k, pos_v, w_k, w_v, g_k):
    B, G, n16, width = kc_r.shape
    dh = HEAD_DIM
    half = NSA_CMP_LEN // 2
    dup = lambda w: jnp.concatenate([w, w], axis=-1).reshape(2, half * dh, 2 * dh).astype(BF16)
    spec_r = pl.BlockSpec((None, None, n16, width), lambda b, g: (b, g, 0, 0))
    spec_o = pl.BlockSpec((None, None, n16, 2 * dh), lambda b, g: (b, g, 0, 0))
    const2 = lambda shape: pl.BlockSpec(shape, lambda b, g: (0,) * len(shape))
    out = jax.ShapeDtypeStruct((B, G, n16, 2 * dh), BF16)
    return pl.pallas_call(
        functools.partial(_nsa_compress_kernel, n16=n16),
        grid=(B, G),
        in_specs=[spec_r, spec_r, const2((2, width)), const2((2, width)),
                  const2((2, width, 2 * dh)), const2((2, width, 2 * dh)), const2((1, 2 * dh))],
        out_specs=[spec_o, spec_o],
        out_shape=[out, out],
        compiler_params=_cparams("parallel", "parallel"),
        name="nsa_compress",
    )(kc_r, vc_r, pos_k.reshape(2, width), pos_v.reshape(2, width), dup(w_k), dup(w_v),
      jnp.tile(g_k, 2).reshape(1, 2 * dh))


def _stack_heads(q_ref, extra=None):
    lane = lax.broadcasted_iota(jnp.int32, (1, LANES), 1)
    upper = lane >= HEAD_DIM
    rows = []
    for j in range(2):
        slab = q_ref[:, j * LANES:(j + 1) * LANES]
        zero = jnp.zeros_like(slab)
        for i in range(2):
            qh = jnp.where(upper, slab, zero) if i else jnp.where(upper, zero, slab)
            rows.append(qh if extra is None else jnp.concatenate([qh, extra], axis=1))
    return jnp.concatenate(rows, axis=0)


def _unstack_heads(o, tq):
    lane = lax.broadcasted_iota(jnp.int32, (1, LANES), 1)
    upper = lane >= HEAD_DIM
    slabs = []
    for j in range(2):
        lo = o[(2 * j) * tq:(2 * j + 1) * tq]
        hi = pltpu.roll(o[(2 * j + 1) * tq:(2 * j + 2) * tq], HEAD_DIM, 1)
        slabs.append(jnp.where(upper, hi, lo))
    return jnp.concatenate(slabs, axis=1)


def _group_lower(slab_ref, g):
    x = slab_ref[...].astype(F32)
    return jnp.where(g % 2 == 1, pltpu.roll(x, HEAD_DIM, 1), x)


def _both_halves(x):
    lane = lax.broadcasted_iota(jnp.int32, (1, LANES), 1)
    return jnp.where(lane >= HEAD_DIM, pltpu.roll(x, HEAD_DIM, 1), x)


def _with_ones(x):
    lane = lax.broadcasted_iota(jnp.int32, (1, LANES), 1)
    return jnp.where(lane >= HEAD_DIM, 1.0, x)


def _normalise(acc):
    return acc / pltpu.roll(acc, HEAD_DIM, 1)


def _nsa_cmp_attn_kernel(q_ref, kc_ref, vc_ref, cover_ref, oc_ref, sel_ref, *, tq, n16):
    qi = pl.program_id(2)
    q4 = _stack_heads(q_ref)
    s = lax.dot_general(q4, kc_ref[...], _NT, preferred_element_type=F32)
    qpos = qi * tq + lax.broadcasted_iota(jnp.int32, (tq, n16), 0)
    cmp_end = lax.broadcasted_iota(jnp.int32, (tq, n16), 1) * NSA_CMP_STRIDE + (NSA_CMP_LEN - 1)
    valid = jnp.concatenate([cmp_end <= qpos] * 4, axis=0)
    s = jnp.where(valid, s, NEG_INF)
    e = jnp.exp(s - jnp.max(s, axis=1, keepdims=True))
    p = jnp.where(valid, e / jnp.sum(e, axis=1, keepdims=True), 0.0)
    o = jnp.dot(p.astype(BF16), vc_ref[...], preferred_element_type=F32)
    oc_ref[...] = _unstack_heads(o, tq).astype(oc_ref.dtype)

    psum = p[0:tq] + p[tq:2 * tq] + p[2 * tq:3 * tq] + p[3 * tq:4 * tq]
    cover = cover_ref[...]
    hi = psum.astype(BF16)
    r1 = psum - hi.astype(F32)
    mid = r1.astype(BF16)
    lo = (r1 - mid.astype(F32)).astype(BF16)
    imp = (jnp.dot(hi, cover, preferred_element_type=F32) + jnp.dot(mid, cover, preferred_element_type=F32)
           + jnp.dot(lo, cover, preferred_element_type=F32))

    blk = lax.broadcasted_iota(jnp.int32, (tq, LANES), 1)
    cur = (qi * tq + lax.broadcasted_iota(jnp.int32, (tq, LANES), 0)) // NSA_SEL_LEN
    forced = (blk == 0) | (blk == cur) | (blk == cur - 1)
    vals = jnp.where(forced, NSA_FORCED_SCORE, jnp.where(blk <= cur, imp, -1.0))
    blk_f = blk.astype(F32)
    chosen = jnp.zeros((tq, LANES), jnp.bool_)
    for _ in range(NSA_N_SEL):
        top = jnp.max(vals, axis=1, keepdims=True)
        first = jnp.min(jnp.where(vals == top, blk_f, float(LANES)), axis=1, keepdims=True)
        pick = blk_f == first
        chosen = chosen | pick
        vals = jnp.where(pick, -3e38, vals)
    sel_ref[...] = jnp.where(chosen, 0.0, NEG_INF).astype(sel_ref.dtype)


def _nsa_cmp_attn(qk, kcmp, vcmp, *, batch, seq):
    T = qk.shape[0]
    G = NSA_KV_HEADS
    tq = 128
    nq = seq // tq
    n16 = kcmp.shape[2]
    n_sel = seq // NSA_SEL_LEN
    assert n_sel <= LANES and seq % tq == 0
    n = np.arange(n16)[:, None] * NSA_CMP_STRIDE
    j = np.arange(LANES)[None, :] * NSA_SEL_LEN
    cover = (n < j + NSA_SEL_LEN) & (n + NSA_CMP_LEN > j) & (np.arange(LANES)[None, :] < n_sel)
    cover = jnp.asarray(cover.astype(np.float32), BF16)
    spec_c = pl.BlockSpec((None, None, n16, LANES), lambda b, g, q: (b, g, 0, 0))
    return pl.pallas_call(
        functools.partial(_nsa_cmp_attn_kernel, tq=tq, n16=n16),
        grid=(batch, G, nq),
        in_specs=[pl.BlockSpec((tq, 2 * LANES), lambda b, g, q: (b * nq + q, g)), spec_c, spec_c,
                  pl.BlockSpec((n16, LANES), lambda b, g, q: (0, 0))],
        out_specs=[pl.BlockSpec((tq, 2 * LANES), lambda b, g, q: (b * nq + q, g)),
                   pl.BlockSpec((tq, LANES), lambda b, g, q: (b * nq + q, g))],
        out_shape=[jax.ShapeDtypeStruct((T, N_HEADS * HEAD_DIM), BF16),
                   jax.ShapeDtypeStruct((T, G * LANES), BF16)],
        compiler_params=_cparams("parallel", "parallel", "parallel"),
        name="nsa_cmp_attn",
    )(qk, kcmp, vcmp, cover)


def _nsa_sel_attn_kernel(q_ref, k_ref, v_ref, sel_ref, o_ref, kaug_ref, vaug_ref, *, tq, tk, seq):
    g = pl.program_id(1)
    qi = pl.program_id(2)

    @pl.when(qi == 0)
    def _():
        kaug_ref[:, 0:LANES] = _both_halves(_group_lower(k_ref, g)).astype(BF16)
        blk_of_key = lax.broadcasted_iota(jnp.int32, (seq, LANES), 0) // NSA_SEL_LEN
        blk = lax.broadcasted_iota(jnp.int32, (seq, LANES), 1)
        kaug_ref[:, LANES:2 * LANES] = jnp.where(blk_of_key == blk, 1.0, 0.0).astype(BF16)
        vaug_ref[...] = _with_ones(_group_lower(v_ref, g)).astype(BF16)

    q4 = _stack_heads(q_ref, extra=sel_ref[...])
    q0 = qi * tq
    k_diag = pl.multiple_of((q0 // tk) * tk, tk)

    def tile(k0, m, acc, masked):
        s = lax.dot_general(q4, kaug_ref[pl.ds(k0, tk), :], _NT, preferred_element_type=F32)
        if masked:
            qpos = q0 + lax.broadcasted_iota(jnp.int32, (tq, tk), 0)
            kpos = k0 + lax.broadcasted_iota(jnp.int32, (tq, tk), 1)
            s = jnp.where(jnp.concatenate([kpos <= qpos] * 4, axis=0), s, NEG_INF)
        m_new = jnp.maximum(m, jnp.max(s, axis=1, keepdims=True))
        p = jnp.exp(s - m_new).astype(BF16)
        acc = jnp.exp(m - m_new) * acc + jnp.dot(p, vaug_ref[pl.ds(k0, tk), :], preferred_element_type=F32)
        return m_new, acc

    def body(j, carry):
        return tile(pl.multiple_of(j * tk, tk), *carry, masked=False)

    carry = (jnp.full((4 * tq, 1), NEG_INF, F32), jnp.zeros((4 * tq, LANES), F32))
    carry = lax.fori_loop(0, q0 // tk, body, carry)
    _, acc = tile(k_diag, *carry, masked=True)
    o_ref[...] = _unstack_heads(_normalise(acc), tq).astype(o_ref.dtype)


def _nsa_sel_attn(qk, vv, selbias, *, batch, seq, k_col, v_col):
    T = qk.shape[0]
    G = NSA_KV_HEADS
    tq = 128
    tk = _pick_tile(seq, 512)
    nq = seq // tq
    return pl.pallas_call(
        functools.partial(_nsa_sel_attn_kernel, tq=tq, tk=tk, seq=seq),
        grid=(batch, G, nq),
        in_specs=[pl.BlockSpec((tq, 2 * LANES), lambda b, g, q: (b * nq + q, g)),
                  pl.BlockSpec((seq, LANES), lambda b, g, q: (b, k_col + g // 2)),
                  pl.BlockSpec((seq, LANES), lambda b, g, q: (b, v_col + g // 2)),
                  pl.BlockSpec((tq, LANES), lambda b, g, q: (b * nq + q, g))],
        out_specs=pl.BlockSpec((tq, 2 * LANES), lambda b, g, q: (b * nq + q, g)),
        out_shape=jax.ShapeDtypeStruct((T, N_HEADS * HEAD_DIM), BF16),
        scratch_shapes=[pltpu.VMEM((seq, 2 * LANES), BF16), pltpu.VMEM((seq, LANES), BF16)],
        compiler_params=_cparams("parallel", "arbitrary", "arbitrary"),
        name="nsa_sel_attn",
    )(qk, qk, vv, selbias)


def _nsa_win_attn_kernel(q_ref, k_ref, v_ref, o_ref, kb_ref, vaug_ref, *, tq):
    g = pl.program_id(1)
    qi = pl.program_id(2)

    @pl.when(qi == 0)
    def _():
        kb_ref[...] = _both_halves(_group_lower(k_ref, g)).astype(BF16)
        vaug_ref[...] = _with_ones(_group_lower(v_ref, g)).astype(BF16)

    q4 = _stack_heads(q_ref)
    q0 = qi * tq
    row = lax.broadcasted_iota(jnp.int32, (tq, tq), 0)
    col = lax.broadcasted_iota(jnp.int32, (tq, tq), 1)
    n_tiles = NSA_WINDOW // tq + 1
    m = jnp.full((4 * tq, 1), NEG_INF, F32)
    acc = jnp.zeros((4 * tq, LANES), F32)
    for d in reversed(range(n_tiles)):
        k0 = q0 - NSA_WINDOW + d * tq
        in_range = k0 >= 0
        k0c = pl.multiple_of(jnp.maximum(k0, 0), tq)
        s = lax.dot_general(q4, kb_ref[pl.ds(k0c, tq), :], _NT, preferred_element_type=F32)
        if d == n_tiles - 1:
            s = jnp.where(jnp.concatenate([col <= row] * 4, axis=0), s, NEG_INF)
        else:
            s = jnp.where(in_range, s, NEG_INF)
            if d == 0:
                s = jnp.where(jnp.concatenate([col > row] * 4, axis=0), s, NEG_INF)
        m_new = jnp.maximum(m, jnp.max(s, axis=1, keepdims=True))
        p = jnp.exp(s - m_new).astype(BF16)
        acc = jnp.exp(m - m_new) * acc + jnp.dot(p, vaug_ref[pl.ds(k0c, tq), :], preferred_element_type=F32)
        m = m_new
    o_ref[...] = _unstack_heads(_normalise(acc), tq).astype(o_ref.dtype)


def _nsa_win_attn(qk, vv, *, batch, seq, k_col, v_col):
    T = qk.shape[0]
    G = NSA_KV_HEADS
    tq = 128
    nq = seq // tq
    return pl.pallas_call(
        functools.partial(_nsa_win_attn_kernel, tq=tq),
        grid=(batch, G, nq),
        in_specs=[pl.BlockSpec((tq, 2 * LANES), lambda b, g, q: (b * nq + q, g)),
                  pl.BlockSpec((seq, LANES), lambda b, g, q: (b, k_col + g // 2)),
                  pl.BlockSpec((seq, LANES), lambda b, g, q: (b, v_col + g // 2))],
        out_specs=pl.BlockSpec((tq, 2 * LANES), lambda b, g, q: (b * nq + q, g)),
        out_shape=jax.ShapeDtypeStruct((T, N_HEADS * HEAD_DIM), BF16),
        scratch_shapes=[pltpu.VMEM((seq, LANES), BF16), pltpu.VMEM((seq, LANES), BF16)],
        compiler_params=_cparams("parallel", "arbitrary", "arbitrary"),
        name="nsa_win_attn",
    )(qk, qk, vv)


def _nsa_mixer(x, g, w_in, pos_k, pos_v, w_cmp_k, w_cmp_v, g_q, g_k, w_out, *, batch, seq):
    HD = N_HEADS * HEAD_DIM
    G, dh = NSA_KV_HEADS, HEAD_DIM
    kvw = G * dh
    scale = dh ** -0.5
    col = lambda n: slice(HD + n * kvw, HD + (n + 1) * kvw)
    w_qk = jnp.concatenate([w_in[:, :HD], w_in[:, col(0)], w_in[:, col(2)], w_in[:, col(4)]], axis=1).astype(BF16)
    w_v = jnp.concatenate([w_in[:, col(1)], w_in[:, col(3)], w_in[:, col(5)]], axis=1).astype(BF16)
    n_gate = 3 * N_HEADS
    w_gate = jnp.pad(w_in[:, HD + 6 * kvw:], ((0, 0), (0, LANES - n_gate))).astype(BF16)
    head_gain = jnp.concatenate([jnp.tile(g_q * scale, N_HEADS), jnp.tile(g_k, 3 * G)])

    inv = ROPE_THETA ** (-jnp.arange(0, dh, 2, dtype=F32) / dh)
    ang = jnp.arange(seq, dtype=F32)[:, None] * inv[None, :]
    cos, sin = jnp.cos(ang), jnp.sin(ang)
    tn = 256
    cos_t = jnp.tile(cos, (1, tn // (dh // 2)))
    sin_t = jnp.tile(jnp.concatenate([-sin, sin], axis=1), (1, tn // dh))

    qk = _proj(x, g, w_qk, seq=seq, head_gain=head_gain, rope=(cos_t, sin_t))
    vv = _proj(x, g, w_v, seq=seq)
    gate = _proj(x, g, w_gate, seq=seq, out_dtype=F32)

    def blocks16(t):
        half = NSA_CMP_LEN // 2
        t = t.reshape(batch, seq // half, half, G, dh).transpose(0, 3, 1, 2, 4)
        return t.reshape(batch, G, seq // half, half * dh)

    kcmp, vcmp = _nsa_compress(blocks16(qk[:, HD:HD + kvw]), blocks16(vv[:, :kvw]),
                               pos_k, pos_v, w_cmp_k, w_cmp_v, g_k)
    o_c, selbias = _nsa_cmp_attn(qk, kcmp, vcmp, batch=batch, seq=seq)
    nq_cols = HD // LANES
    o_s = _nsa_sel_attn(qk, vv, selbias, batch=batch, seq=seq,
                        k_col=nq_cols + kvw // LANES, v_col=kvw // LANES)
    o_w = _nsa_win_attn(qk, vv, batch=batch, seq=seq,
                        k_col=nq_cols + 2 * kvw // LANES, v_col=2 * kvw // LANES)
    r = np.arange(LANES)[:, None]
    c = np.arange(HD)[None, :] // dh
    expand = np.stack([(r == 3 * c + b) for b in range(3)]).astype(np.float32)
    return _outproj(x, w_out.astype(BF16), mode="gated3",
                    acts=[o_c, o_s, o_w, gate, jnp.asarray(expand, BF16)])


def kernel(x, norm_g, ffn1_w_gate, ffn1_w_up, ffn1_w_down, ffn2_w_gate, ffn2_w_up, ffn2_w_down, fox_w_in, fox_b_f, fox_g_q, fox_g_k, fox_w_out, nsa_w_in, nsa_cmp_pos_k, nsa_cmp_pos_v, nsa_w_cmp_k, nsa_w_cmp_v, nsa_g_q, nsa_g_k, nsa_w_out, gla_w_in, gla_w_gate_up, gla_b_gate, gla_g_out, gla_w_out, sb_w_in, sb_w_out):
    B, S, D = x.shape
    depth = norm_g.shape[0]
    n_mixers = 4
    x2 = x.reshape(B * S, D)
    for i in range(depth):
        m, j = i % n_mixers, i // n_mixers
        x2 = _ffn(x2, norm_g[i, 0], ffn1_w_gate[i].astype(BF16), ffn1_w_up[i].astype(BF16),
                  ffn1_w_down[i].astype(BF16))
        g = norm_g[i, 1]
        if m == 0:
            x2 = _fox_mixer(x2, g, fox_w_in[j], fox_b_f[j], fox_g_q[j], fox_g_k[j], fox_w_out[j], batch=B, seq=S)
        elif m == 1:
            x2 = _nsa_mixer(x2, g, nsa_w_in[j], nsa_cmp_pos_k[j], nsa_cmp_pos_v[j], nsa_w_cmp_k[j],
                            nsa_w_cmp_v[j], nsa_g_q[j], nsa_g_k[j], nsa_w_out[j], batch=B, seq=S)
        elif m == 2:
            x2 = _gla_mixer(x2, g, gla_w_in[j], gla_w_gate_up[j], gla_b_gate[j], gla_g_out[j], gla_w_out[j],
                            batch=B, seq=S)
        else:
            x2 = _sb_mixer(x2, g, sb_w_in[j], sb_w_out[j], batch=B, seq=S)
        x2 = _ffn(x2, norm_g[i, 2], ffn2_w_gate[i].astype(BF16), ffn2_w_up[i].astype(BF16),
                  ffn2_w_down[i].astype(BF16))
    return x2.reshape(B, S, D)
```

```python
import functools

import numpy as np
import jax
import jax.numpy as jnp
from jax import lax
from jax.experimental import pallas as pl
from jax.experimental.pallas import tpu as pltpu

F32 = jnp.float32
BF16 = jnp.bfloat16

N_HEADS = 16
HEAD_DIM = 64
ROPE_THETA = 10000.0
RMS_EPS = 1e-6
NEG_INF = -1e30
NSA_KV_HEADS = 4
NSA_CMP_LEN = 32
NSA_CMP_STRIDE = 16
NSA_SEL_LEN = 64
NSA_N_SEL = 16
NSA_WINDOW = 512
NSA_FORCED_SCORE = 1e9
GLA_HEADS = 4
GLA_GATE_RANK = 16
GLA_TAU = 16.0
GLA_CHUNK = 64

LANES = 128
V7X_VMEM_BYTES = 64 * 1024 * 1024
VMEM_LIMIT = V7X_VMEM_BYTES - 8 * 1024 * 1024

_NT = (((1,), (1,)), ((), ()))
_TN = (((0,), (0,)), ((), ()))


def _cparams(*sem):
    return pltpu.CompilerParams(dimension_semantics=sem, vmem_limit_bytes=VMEM_LIMIT)


def _rms_rows(x, g):
    return x * lax.rsqrt(jnp.mean(x * x, axis=-1, keepdims=True) + RMS_EPS) * g


def _softplus(z):
    return jnp.maximum(z, 0.0) + jnp.log(1.0 + jnp.exp(-jnp.abs(z)))


def _neg_abs(z):
    bits = lax.bitcast_convert_type(z, jnp.int32) | jnp.int32(-2 ** 31)
    return lax.bitcast_convert_type(bits, F32)


def _sigmoid(z):
    return 1.0 / (1.0 + jnp.exp(-z))


def _pick_tile(n, target):
    t = min(n, target)
    while n % t:
        t //= 2
    return t


def _ffn_kernel(x_ref, g_ref, wg_ref, wu_ref, wd_ref, o_ref, h_ref, acc_ref):
    j = pl.program_id(1)

    @pl.when(j == 0)
    def _():
        h_ref[...] = _rms_rows(x_ref[...], g_ref[...]).astype(BF16)
        acc_ref[...] = jnp.zeros_like(acc_ref)

    h = h_ref[...]
    gate = jnp.dot(h, wg_ref[...], preferred_element_type=F32)
    up = jnp.dot(h, wu_ref[...], preferred_element_type=F32)
    a = (gate * _sigmoid(gate) * up).astype(BF16)
    acc_ref[...] += jnp.dot(a, wd_ref[...], preferred_element_type=F32)

    @pl.when(j == pl.num_programs(1) - 1)
    def _():
        o_ref[...] = x_ref[...] + 0.5 * acc_ref[...]


def _ffn(x, g, wg, wu, wd):
    T, D = x.shape
    F = wg.shape[1]
    tm = _pick_tile(T, 1024)
    tf = 256 if F % 256 == 0 else F
    return pl.pallas_call(
        _ffn_kernel,
        grid=(T // tm, F // tf),
        in_specs=[
            pl.BlockSpec((tm, D), lambda i, j: (i, 0)),
            pl.BlockSpec((1, D), lambda i, j: (0, 0)),
            pl.BlockSpec((D, tf), lambda i, j: (0, j)),
            pl.BlockSpec((D, tf), lambda i, j: (0, j)),
            pl.BlockSpec((tf, D), lambda i, j: (j, 0)),
        ],
        out_specs=pl.BlockSpec((tm, D), lambda i, j: (i, 0)),
        out_shape=jax.ShapeDtypeStruct((T, D), F32),
        scratch_shapes=[pltpu.VMEM((tm, D), BF16), pltpu.VMEM((tm, D), F32)],
        compiler_params=_cparams("parallel", "arbitrary"),
        name="ffn",
    )(x, g.reshape(1, D), wg, wu, wd)


def _proj_kernel(*refs, mode, tn):
    if mode == "plain":
        x_ref, g_ref, w_ref, o_ref, h_ref = refs
    elif mode == "headnorm":
        x_ref, g_ref, w_ref, hg_ref, gsum_ref, o_ref, h_ref = refs
    else:
        x_ref, g_ref, w_ref, hg_ref, gsum_ref, cos_ref, sin_ref, o_ref, h_ref = refs

    @pl.when(pl.program_id(1) == 0)
    def _():
        h_ref[...] = _rms_rows(x_ref[...], g_ref[...]).astype(BF16)

    y = jnp.dot(h_ref[...], w_ref[...], preferred_element_type=F32)
    if mode != "plain":
        ss = jnp.dot((y * y).astype(BF16), gsum_ref[...], preferred_element_type=F32)
        y = y * lax.rsqrt(ss * (1.0 / HEAD_DIM) + RMS_EPS) * hg_ref[...]
    if mode == "headnorm_rope":
        lane = lax.broadcasted_iota(jnp.int32, (1, tn), 1)
        first_half = (lane % HEAD_DIM) < (HEAD_DIM // 2)
        partner = jnp.where(first_half,
                            pltpu.roll(y, tn - HEAD_DIM // 2, 1),
                            pltpu.roll(y, HEAD_DIM // 2, 1))
        y = y * cos_ref[...] + partner * sin_ref[...]
    o_ref[...] = y.astype(o_ref.dtype)


def _proj(x, g, w, *, seq, out_dtype=BF16, head_gain=None, rope=None):
    T, D = x.shape
    N = w.shape[1]
    tm = _pick_tile(seq, 1024)
    mode = "plain" if head_gain is None else ("headnorm" if rope is None else "headnorm_rope")
    tn = 256 if mode != "plain" else _pick_tile(N, 512)
    assert N % tn == 0 and T % tm == 0
    args = [x, g.reshape(1, D), w]
    in_specs = [
        pl.BlockSpec((tm, D), lambda i, j: (i, 0)),
        pl.BlockSpec((1, D), lambda i, j: (0, 0)),
        pl.BlockSpec((D, tn), lambda i, j: (0, j)),
    ]
    if mode != "plain":
        gidx = np.arange(tn) // HEAD_DIM
        gsum = jnp.asarray((gidx[:, None] == gidx[None, :]).astype(np.float32), BF16)
        args += [head_gain.reshape(1, N), gsum]
        in_specs += [pl.BlockSpec((1, tn), lambda i, j: (0, j)),
                     pl.BlockSpec((tn, tn), lambda i, j: (0, 0))]
    if mode == "headnorm_rope":
        cos_t, sin_t = rope
        nblk = seq // tm
        args += [cos_t, sin_t]
        in_specs += [pl.BlockSpec((tm, tn), lambda i, j: (i % nblk, 0)),
                     pl.BlockSpec((tm, tn), lambda i, j: (i % nblk, 0))]
    return pl.pallas_call(
        functools.partial(_proj_kernel, mode=mode, tn=tn),
        grid=(T // tm, N // tn),
        in_specs=in_specs,
        out_specs=pl.BlockSpec((tm, tn), lambda i, j: (i, j)),
        out_shape=jax.ShapeDtypeStruct((T, N), out_dtype),
        scratch_shapes=[pltpu.VMEM((tm, D), BF16)],
        compiler_params=_cparams("parallel", "arbitrary"),
        name="proj_" + mode,
    )(*args)


def _outproj_kernel(*refs, mode):
    if mode == "plain":
        x_ref, a_ref, w_ref, o_ref = refs
        a = a_ref[...]
    elif mode == "sigmoid_gate":
        x_ref, a_ref, og_ref, w_ref, o_ref = refs
        a = (a_ref[...].astype(F32) * _sigmoid(og_ref[...].astype(F32))).astype(BF16)
    else:
        x_ref, oc_ref, os_ref, ow_ref, gate_ref, exp_ref, w_ref, o_ref = refs
        gs = _sigmoid(gate_ref[...]).astype(BF16)
        a = None
        for c, o_c in enumerate((oc_ref, os_ref, ow_ref)):
            ge = jnp.dot(gs, exp_ref[c], preferred_element_type=F32)
            t = ge * o_c[...].astype(F32)
            a = t if a is None else a + t
        a = a.astype(BF16)
    o_ref[...] = x_ref[...] + jnp.dot(a, w_ref[...], preferred_element_type=F32)


def _outproj(x, w, *, mode, acts):
    T, D = x.shape
    K = w.shape[0]
    tm = _pick_tile(T, 512)
    row = lambda i: (i, 0)
    args = [x]
    in_specs = [pl.BlockSpec((tm, D), row)]
    for a in acts:
        if a.ndim == 3:
            args.append(a)
            in_specs.append(pl.BlockSpec(a.shape, lambda i: (0, 0, 0)))
        else:
            args.append(a)
            in_specs.append(pl.BlockSpec((tm, a.shape[1]), row))
    args.append(w)
    in_specs.append(pl.BlockSpec((K, D), lambda i: (0, 0)))
    return pl.pallas_call(
        functools.partial(_outproj_kernel, mode=mode),
        grid=(T // tm,),
        in_specs=in_specs,
        out_specs=pl.BlockSpec((tm, D), row),
        out_shape=jax.ShapeDtypeStruct((T, D), F32),
        compiler_params=_cparams("parallel"),
        name="outproj_" + mode,
    )(*args)


def _fox_c_kernel(x_ref, g_ref, wf_ref, bf_ref, tri_ref, c_ref, carry_ref):
    @pl.when(pl.program_id(1) == 0)
    def _():
        carry_ref[...] = jnp.zeros_like(carry_ref)

    h = _rms_rows(x_ref[...], g_ref[...])
    f = lax.dot_general(wf_ref[...], h, _NT, precision=lax.Precision.HIGHEST,
                        preferred_element_type=F32) + bf_ref[...]
    ls = jnp.minimum(f, 0.0) - jnp.log(1.0 + jnp.exp(-jnp.abs(f)))
    cs = jnp.dot(ls, tri_ref[...], precision=lax.Precision.HIGHEST,
                 preferred_element_type=F32) + carry_ref[...]
    c_ref[...] = cs
    carry_ref[...] = cs[:, cs.shape[1] - 1:]


def _fox_c(x, g, wf_t, b_f, *, batch, seq):
    T, D = x.shape
    H = wf_t.shape[0]
    tm = _pick_tile(seq, 512)
    ns = seq // tm
    tri = jnp.asarray(np.triu(np.ones((tm, tm), np.float32)))
    return pl.pallas_call(
        _fox_c_kernel,
        grid=(batch, ns),
        in_specs=[
            pl.BlockSpec((tm, D), lambda b, s: (b * ns + s, 0)),
            pl.BlockSpec((1, D), lambda b, s: (0, 0)),
            pl.BlockSpec((H, D), lambda b, s: (0, 0)),
            pl.BlockSpec((H, 1), lambda b, s: (0, 0)),
            pl.BlockSpec((tm, tm), lambda b, s: (0, 0)),
        ],
        out_specs=pl.BlockSpec((None, H, tm), lambda b, s: (b, 0, s)),
        out_shape=jax.ShapeDtypeStruct((batch, H, seq), F32),
        scratch_shapes=[pltpu.VMEM((H, 1), F32)],
        compiler_params=_cparams("parallel", "arbitrary"),
        name="fox_c",
    )(x, g.reshape(1, D), wf_t, b_f.reshape(H, 1), tri)


def _fox_attn_kernel(q_ref, k_ref, v_ref, c_ref, o_ref, vaug_ref, *, t):
    qi = pl.program_id(2)
    lane = lax.broadcasted_iota(jnp.int32, (1, LANES), 1)
    upper = lane >= HEAD_DIM

    @pl.when(qi == 0)
    def _():
        v = v_ref[...]
        one = jnp.ones_like(v)
        vaug_ref[0] = jnp.where(upper, one, v)
        vaug_ref[1] = jnp.where(upper, v, one)

    q0 = pl.multiple_of(qi * t, t)
    qslab = q_ref[...]
    zero = jnp.zeros_like(qslab)
    qpos = lax.broadcasted_iota(jnp.int32, (t, t), 0)
    kpos = lax.broadcasted_iota(jnp.int32, (t, t), 1)
    causal = kpos <= qpos

    heads = range(2)
    qh = (jnp.where(upper, zero, qslab), jnp.where(upper, qslab, zero))
    c_q0 = [c_ref[i, :, pl.ds(q0, LANES)][:, 0:1] for i in heads]

    def tile(k0, m, acc, masked):
        kt = k_ref[pl.ds(k0, t), :]
        s = [lax.dot_general(qh[i], kt, _NT, preferred_element_type=F32) for i in heads]
        s = [s[i] + (c_q0[i] - c_ref[i, :, pl.ds(k0, t)]) for i in heads]
        if masked:
            s = [jnp.where(causal, s[i], NEG_INF) for i in heads]
        m_new = [jnp.maximum(m[i], jnp.max(s[i], axis=1, keepdims=True)) for i in heads]
        p = [jnp.exp(s[i] - m_new[i]).astype(BF16) for i in heads]
        acc = [jnp.exp(m[i] - m_new[i]) * acc[i]
               + jnp.dot(p[i], vaug_ref[i, pl.ds(k0, t), :], preferred_element_type=F32) for i in heads]
        return tuple(m_new), tuple(acc)

    def body(j, carry):
        return tile(pl.multiple_of(j * t, t), *carry, masked=False)

    m0 = (jnp.full((t, 1), NEG_INF, F32),) * 2
    acc0 = (jnp.zeros((t, LANES), F32),) * 2
    m, acc = lax.fori_loop(0, qi, body, (m0, acc0))
    m, accs = tile(q0, m, acc, masked=True)

    num = jnp.where(upper, accs[1], accs[0])
    den = jnp.where(upper, accs[0], accs[1])
    den = pltpu.roll(den, HEAD_DIM, 1)
    o_ref[...] = (num / den).astype(o_ref.dtype)


def _fox_attn(qk, vg, c, *, batch, seq):
    T = qk.shape[0]
    HD = N_HEADS * HEAD_DIM
    t = _pick_tile(seq, 512)
    nq = seq // t
    npair = HD // LANES
    return pl.pallas_call(
        functools.partial(_fox_attn_kernel, t=t),
        grid=(batch, npair, nq),
        in_specs=[
            pl.BlockSpec((t, LANES), lambda b, h, q: (b * nq + q, h)),
            pl.BlockSpec((seq, LANES), lambda b, h, q: (b, npair + h)),
            pl.BlockSpec((seq, LANES), lambda b, h, q: (b, h)),
            pl.BlockSpec((None, 2, 1, seq), lambda b, h, q: (b, h, 0, 0)),
        ],
        out_specs=pl.BlockSpec((t, LANES), lambda b, h, q: (b * nq + q, h)),
        out_shape=jax.ShapeDtypeStruct((T, HD), BF16),
        scratch_shapes=[pltpu.VMEM((2, seq, LANES), BF16)],
        compiler_params=_cparams("parallel", "arbitrary", "arbitrary"),
        name="fox_attn",
    )(qk, qk, vg, c.reshape(batch, N_HEADS, 1, seq))


def _fox_mixer(x, g, w_in, b_f, g_q, g_k, w_out, *, batch, seq):
    HD = N_HEADS * HEAD_DIM
    scale = HEAD_DIM ** -0.5
    w_qk = w_in[:, :2 * HD].astype(BF16)
    w_vg = jnp.concatenate([w_in[:, 2 * HD:3 * HD], w_in[:, 3 * HD + N_HEADS:]], axis=1).astype(BF16)
    wf_t = w_in[:, 3 * HD:3 * HD + N_HEADS].T
    head_gain = jnp.concatenate([jnp.tile(g_q * scale, N_HEADS), jnp.tile(g_k, N_HEADS)])
    qk = _proj(x, g, w_qk, seq=seq, head_gain=head_gain)
    vg = _proj(x, g, w_vg, seq=seq)
    c = _fox_c(x, g, wf_t, b_f, batch=batch, seq=seq)
    o = _fox_attn(qk, vg, c, batch=batch, seq=seq)
    og = vg
    return _outproj_fox(x, o, og, w_out.astype(BF16))


def _outproj_fox(x, o, vg, w):
    T, D = x.shape
    HD = o.shape[1]
    tm = _pick_tile(T, 512)
    return pl.pallas_call(
        functools.partial(_outproj_kernel, mode="sigmoid_gate"),
        grid=(T // tm,),
        in_specs=[
            pl.BlockSpec((tm, D), lambda i: (i, 0)),
            pl.BlockSpec((tm, HD), lambda i: (i, 0)),
            pl.BlockSpec((tm, HD), lambda i: (i, 1)),
            pl.BlockSpec((HD, D), lambda i: (0, 0)),
        ],
        out_specs=pl.BlockSpec((tm, D), lambda i: (i, 0)),
        out_shape=jax.ShapeDtypeStruct((T, D), F32),
        compiler_params=_cparams("parallel"),
        name="outproj_fox",
    )(x, o, vg, w)


def _sb_attn_kernel(q_ref, k_ref, v_ref, tri_ref, o_ref, *, tq, tk):
    qi = pl.program_id(2)
    lane = lax.broadcasted_iota(jnp.int32, (1, LANES), 1)
    upper = lane >= HEAD_DIM
    q0 = pl.multiple_of(qi * tq, tq)
    qslab = q_ref[...]
    zero = jnp.zeros_like(qslab)
    qh = (jnp.where(upper, zero, qslab), jnp.where(upper, qslab, zero))
    qpos = lax.broadcasted_iota(jnp.int32, (tq, tk), 0)
    kpos = lax.broadcasted_iota(jnp.int32, (tq, tk), 1)

    heads = range(2)

    def scores(j, diag_offset):
        kt = k_ref[pl.ds(pl.multiple_of(j * tk, tk), tk), :]
        z = [lax.dot_general(qh[i], kt, _NT, preferred_element_type=F32) for i in heads]
        sp = [jnp.maximum(z[i], 0.0) + jnp.log2(1.0 + jnp.exp2(_neg_abs(z[i]))) for i in heads]
        if diag_offset is not None:
            strict = kpos + diag_offset < qpos
            spm = [jnp.where(strict, sp[i], 0.0) for i in heads]
        else:
            spm = sp
        cum = [jnp.dot(spm[i].astype(BF16), tri_ref[...], preferred_element_type=F32) for i in heads]
        w = [(z[i] - sp[i]) - cum[i] for i in heads]
        if diag_offset is not None:
            w = [jnp.where(strict, w[i], NEG_INF) for i in heads]
        total = [cum[i][:, 0:1] + spm[i][:, 0:1] for i in heads]
        return tuple(w), tuple(total)

    def accumulate(j, w, total, rest, acc):
        vt = v_ref[pl.ds(pl.multiple_of(j * tk, tk), tk), :]
        a = [jnp.exp2(w[i] - rest[i]).astype(BF16) for i in heads]
        acc = tuple(acc[i] + jnp.dot(a[i], vt, preferred_element_type=F32) for i in heads)
        return tuple(rest[i] + total[i] for i in heads), acc

    n_diag = tq // tk
    assert n_diag == 2
    rest = (jnp.zeros((tq, 1), F32),) * 2
    acc = (jnp.zeros((tq, LANES), F32),) * 2

    def pair(j_hi, rest, acc, diag):
        s_hi = scores(j_hi, tk if diag else None)
        s_lo = scores(j_hi - 1, 0 if diag else None)
        rest, acc = accumulate(j_hi, *s_hi, rest, acc)
        return accumulate(j_hi - 1, *s_lo, rest, acc)

    rest, acc = pair(qi * n_diag + 1, rest, acc, True)

    def body(jj, c):
        return pair((qi - jj) * n_diag - 1, *c, False)

    rest, acc = lax.fori_loop(0, qi, body, (rest, acc))
    o_ref[...] = jnp.where(upper, acc[1], acc[0]).astype(o_ref.dtype)


def _sb_attn(qkv, *, batch, seq):
    T = qkv.shape[0]
    HD = N_HEADS * HEAD_DIM
    tq = _pick_tile(seq, 512)
    tk = 256
    nq = seq // tq
    npair = HD // LANES
    tri = jnp.asarray(np.tril(np.ones((tk, tk), np.float32), -1), BF16)
    return pl.pallas_call(
        functools.partial(_sb_attn_kernel, tq=tq, tk=tk),
        grid=(batch, npair, nq),
        in_specs=[
            pl.BlockSpec((tq, LANES), lambda b, h, q: (b * nq + q, h)),
            pl.BlockSpec((seq, LANES), lambda b, h, q: (b, npair + h)),
            pl.BlockSpec((seq, LANES), lambda b, h, q: (b, 2 * npair + h)),
            pl.BlockSpec((tk, tk), lambda b, h, q: (0, 0)),
        ],
        out_specs=pl.BlockSpec((tq, LANES), lambda b, h, q: (b * nq + q, h)),
        out_shape=jax.ShapeDtypeStruct((T, HD), BF16),
        compiler_params=_cparams("parallel", "parallel", "arbitrary"),
        name="sb_attn",
    )(qkv, qkv, qkv, tri)


def _sb_mixer(x, g, w_in, w_out, *, batch, seq):
    HD = N_HEADS * HEAD_DIM
    scale = HEAD_DIM ** -0.5 * float(np.log2(np.e))
    w = jnp.concatenate([w_in[:, :HD] * scale, w_in[:, HD:]], axis=1).astype(BF16)
    qkv = _proj(x, g, w, seq=seq)
    o = _sb_attn(qkv, batch=batch, seq=seq)
    return _outproj(x, w_out.astype(BF16), mode="plain", acts=[o])


def _gla_kernel(q_ref, k_ref, v_ref, r_ref, glow_ref, wgu_ref, bg_ref, gout_ref, tri_ref,
                o_ref, state_ref, *, tm, dk):
    C = GLA_CHUNK

    @pl.when(pl.program_id(2) == 0)
    def _():
        state_ref[...] = jnp.zeros_like(state_ref)

    gate = jnp.dot(glow_ref[...], wgu_ref[...], precision=lax.Precision.HIGHEST,
                   preferred_element_type=F32) + bg_ref[...]
    log_a = (jnp.minimum(gate, 0.0) - jnp.log(1.0 + jnp.exp(-jnp.abs(gate)))) * (1.0 / GLA_TAU)
    row = lax.broadcasted_iota(jnp.int32, (C, C), 0)
    col = lax.broadcasted_iota(jnp.int32, (C, C), 1)
    causal = col <= row
    tri = tri_ref[...]
    q_scale = dk ** -0.5

    for c in range(tm // C):
        sl = slice(c * C, (c + 1) * C)
        b = jnp.dot(tri, log_a[sl], precision=lax.Precision.HIGHEST, preferred_element_type=F32)
        b_last = b[C - 1:C]
        q = q_ref[sl, :].astype(F32) * q_scale
        k = k_ref[sl, :].astype(F32)
        v = v_ref[sl, :]
        q_dec = (q * jnp.exp(b)).astype(BF16)
        k_neg = (k * jnp.exp(-b)).astype(BF16)
        k_rem = (k * jnp.exp(b_last - b)).astype(BF16)
        attn = lax.dot_general(q_dec, k_neg, _NT, preferred_element_type=F32)
        attn = jnp.where(causal, attn, 0.0).astype(BF16)
        state_t = state_ref[...]
        o = (jnp.dot(attn, v, preferred_element_type=F32)
             + lax.dot_general(q_dec, state_t.astype(BF16), _NT, preferred_element_type=F32))
        u_t = lax.dot_general(v, k_rem, _TN, preferred_element_type=F32)
        state_ref[...] = state_t * jnp.exp(b_last) + u_t
        y = o * lax.rsqrt(jnp.mean(o * o, axis=-1, keepdims=True) + RMS_EPS) * gout_ref[...]
        r = r_ref[sl, :].astype(F32)
        o_ref[sl, :] = (y * (r * _sigmoid(r))).astype(o_ref.dtype)


def _gla(qkvr, glow, wgu, b_gate, g_out, *, batch, seq):
    T = qkvr.shape[0]
    Hg = GLA_HEADS
    dk_total = wgu.shape[1]
    dk = dk_total // Hg
    dv = (qkvr.shape[1] - 2 * dk_total) // 2 // Hg
    tm = _pick_tile(seq, 512)
    ns = seq // tm
    kq, kv = dk_total // dk, dk_total // dv
    tri = jnp.asarray(np.tril(np.ones((GLA_CHUNK, GLA_CHUNK), np.float32)))
    rowmap = lambda off: (lambda b, h, s: (b * ns + s, off + h))
    return pl.pallas_call(
        functools.partial(_gla_kernel, tm=tm, dk=dk),
        grid=(batch, Hg, ns),
        in_specs=[
            pl.BlockSpec((tm, dk), rowmap(0)),
            pl.BlockSpec((tm, dk), rowmap(kq)),
            pl.BlockSpec((tm, dv), rowmap(2 * kv)),
            pl.BlockSpec((tm, dv), rowmap(2 * kv + Hg)),
            pl.BlockSpec((tm, LANES), lambda b, h, s: (b * ns + s, 0)),
            pl.BlockSpec((LANES, dk), lambda b, h, s: (0, h)),
            pl.BlockSpec((1, dk), lambda b, h, s: (0, h)),
            pl.BlockSpec((1, dv), lambda b, h, s: (0, 0)),
            pl.BlockSpec((GLA_CHUNK, GLA_CHUNK), lambda b, h, s: (0, 0)),
        ],
        out_specs=pl.BlockSpec((tm, dv), rowmap(0)),
        out_shape=jax.ShapeDtypeStruct((T, Hg * dv), BF16),
        scratch_shapes=[pltpu.VMEM((dv, dk), F32)],
        compiler_params=_cparams("parallel", "parallel", "arbitrary"),
        name="gla",
    )(qkvr, qkvr, qkvr, qkvr, glow, wgu, b_gate.reshape(1, dk_total), g_out.reshape(1, dv), tri)


def _gla_mixer(x, g, w_in, w_gate_up, b_gate, g_out, w_out, *, batch, seq):
    dk_total = w_gate_up.shape[1]
    dv_total = w_out.shape[0]
    lo = 2 * dk_total + dv_total
    w_main = jnp.concatenate([w_in[:, :lo], w_in[:, lo + GLA_GATE_RANK:]], axis=1).astype(BF16)
    w_low = jnp.pad(w_in[:, lo:lo + GLA_GATE_RANK], ((0, 0), (0, LANES - GLA_GATE_RANK))).astype(BF16)
    wgu = jnp.pad(w_gate_up, ((0, LANES - GLA_GATE_RANK), (0, 0)))
    qkvr = _proj(x, g, w_main, seq=seq)
    glow = _proj(x, g, w_low, seq=seq, out_dtype=F32)
    o = _gla(qkvr, glow, wgu, b_gate, g_out, batch=batch, seq=seq)
    return _outproj(x, w_out.astype(BF16), mode="plain", acts=[o])


def _nsa_compress_kernel(kc_ref, vc_ref, pk_ref, pv_ref, wk_ref, wv_ref, gk_ref, ko_ref, vo_ref, *, n16):
    def compress(r_ref, pos_ref, w_ref):
        r = r_ref[...].astype(F32)
        first = jnp.dot((r + pos_ref[0:1, :]).astype(BF16), w_ref[0], preferred_element_type=F32)
        second = jnp.dot((r + pos_ref[1:2, :]).astype(BF16), w_ref[1], preferred_element_type=F32)
        return first + pltpu.roll(second, n16 - 1, 0)

    kc = compress(kc_ref, pk_ref, wk_ref)
    ko_ref[...] = _rms_rows(kc, gk_ref[...]).astype(ko_ref.dtype)
    vo_ref[...] = compress(vc_ref, pv_ref, wv_ref).astype(vo_ref.dtype)


def _nsa_compress(kc_r, vc_r, pos_k, pos_v, w_k, w_v, g_k):
    B, G, n16, width = kc_r.shape
    dh = HEAD_DIM
    half = NSA_CMP_LEN // 2
    dup = lambda w: jnp.concatenate([w, w], axis=-1).reshape(2, half * dh, 2 * dh).astype(BF16)
    spec_r = pl.BlockSpec((None, None, n16, width), lambda b, g: (b, g, 0, 0))
    spec_o = pl.BlockSpec((None, None, n16, 2 * dh), lambda b, g: (b, g, 0, 0))
    const2 = lambda shape: pl.BlockSpec(shape, lambda b, g: (0,) * len(shape))
    out = jax.ShapeDtypeStruct((B, G, n16, 2 * dh), BF16)
    return pl.pallas_call(
        functools.partial(_nsa_compress_kernel, n16=n16),
        grid=(B, G),
        in_specs=[spec_r, spec_r, const2((2, width)), const2((2, width)),
                  const2((2, width, 2 * dh)), const2((2, width, 2 * dh)), const2((1, 2 * dh))],
        out_specs=[spec_o, spec_o],
        out_shape=[out, out],
        compiler_params=_cparams("parallel", "parallel"),
        name="nsa_compress",
    )(kc_r, vc_r, pos_k.reshape(2, width), pos_v.reshape(2, width), dup(w_k), dup(w_v),
      jnp.tile(g_k, 2).reshape(1, 2 * dh))


def _stack_heads(q_ref, extra=None, rows_in=slice(None)):
    lane = lax.broadcasted_iota(jnp.int32, (1, LANES), 1)
    upper = lane >= HEAD_DIM
    rows = []
    for j in range(2):
        slab = q_ref[rows_in, j * LANES:(j + 1) * LANES]
        zero = jnp.zeros_like(slab)
        for i in range(2):
            qh = jnp.where(upper, slab, zero) if i else jnp.where(upper, zero, slab)
            rows.append(qh if extra is None else jnp.concatenate([qh, extra], axis=1))
    return jnp.concatenate(rows, axis=0)


def _unstack_heads(o, tq):
    lane = lax.broadcasted_iota(jnp.int32, (1, LANES), 1)
    upper = lane >= HEAD_DIM
    slabs = []
    for j in range(2):
        lo = o[(2 * j) * tq:(2 * j + 1) * tq]
        hi = pltpu.roll(o[(2 * j + 1) * tq:(2 * j + 2) * tq], HEAD_DIM, 1)
        slabs.append(jnp.where(upper, hi, lo))
    return jnp.concatenate(slabs, axis=1)


def _group_lower(slab_ref, g):
    x = slab_ref[...].astype(F32)
    return jnp.where(g % 2 == 1, pltpu.roll(x, HEAD_DIM, 1), x)


def _both_halves(x):
    lane = lax.broadcasted_iota(jnp.int32, (1, LANES), 1)
    return jnp.where(lane >= HEAD_DIM, pltpu.roll(x, HEAD_DIM, 1), x)


def _with_ones(x):
    lane = lax.broadcasted_iota(jnp.int32, (1, LANES), 1)
    return jnp.where(lane >= HEAD_DIM, 1.0, x)


def _normalise(acc):
    return acc / pltpu.roll(acc, HEAD_DIM, 1)


def _nsa_cmp_attn_kernel(q_ref, kc_ref, vc_ref, cover_ref, oc_ref, sel_ref, *, tq, n16):
    qi = pl.program_id(2)
    q4 = _stack_heads(q_ref)
    s = lax.dot_general(q4, kc_ref[...], _NT, preferred_element_type=F32)
    qpos = qi * tq + lax.broadcasted_iota(jnp.int32, (tq, n16), 0)
    cmp_end = lax.broadcasted_iota(jnp.int32, (tq, n16), 1) * NSA_CMP_STRIDE + (NSA_CMP_LEN - 1)
    valid = jnp.concatenate([cmp_end <= qpos] * 4, axis=0)
    s = jnp.where(valid, s, NEG_INF)
    e = jnp.exp(s - jnp.max(s, axis=1, keepdims=True))
    p = jnp.where(valid, e / jnp.sum(e, axis=1, keepdims=True), 0.0)
    o = jnp.dot(p.astype(BF16), vc_ref[...], preferred_element_type=F32)
    oc_ref[...] = _unstack_heads(o, tq).astype(oc_ref.dtype)

    psum = p[0:tq] + p[tq:2 * tq] + p[2 * tq:3 * tq] + p[3 * tq:4 * tq]
    cover = cover_ref[...]
    hi = psum.astype(BF16)
    r1 = psum - hi.astype(F32)
    mid = r1.astype(BF16)
    lo = (r1 - mid.astype(F32)).astype(BF16)
    imp = (jnp.dot(hi, cover, preferred_element_type=F32) + jnp.dot(mid, cover, preferred_element_type=F32)
           + jnp.dot(lo, cover, preferred_element_type=F32))

    blk = lax.broadcasted_iota(jnp.int32, (tq, LANES), 1)
    cur = (qi * tq + lax.broadcasted_iota(jnp.int32, (tq, LANES), 0)) // NSA_SEL_LEN
    forced = (blk == 0) | (blk == cur) | (blk == cur - 1)
    vals = jnp.where(forced, NSA_FORCED_SCORE, jnp.where(blk <= cur, imp, -1.0))
    blk_f = blk.astype(F32)
    chosen = jnp.zeros((tq, LANES), jnp.bool_)
    for _ in range(NSA_N_SEL):
        top = jnp.max(vals, axis=1, keepdims=True)
        first = jnp.min(jnp.where(vals == top, blk_f, float(LANES)), axis=1, keepdims=True)
        pick = blk_f == first
        chosen = chosen | pick
        vals = jnp.where(pick, -3e38, vals)
    sel_ref[...] = jnp.where(chosen, 0.0, NEG_INF).astype(sel_ref.dtype)


def _nsa_cmp_attn(qk, kcmp, vcmp, *, batch, seq):
    T = qk.shape[0]
    G = NSA_KV_HEADS
    tq = _pick_tile(seq, 512)
    nq = seq // tq
    n16 = kcmp.shape[2]
    n_sel = seq // NSA_SEL_LEN
    assert n_sel <= LANES and seq % tq == 0
    n = np.arange(n16)[:, None] * NSA_CMP_STRIDE
    j = np.arange(LANES)[None, :] * NSA_SEL_LEN
    cover = (n < j + NSA_SEL_LEN) & (n + NSA_CMP_LEN > j) & (np.arange(LANES)[None, :] < n_sel)
    cover = jnp.asarray(cover.astype(np.float32), BF16)
    spec_c = pl.BlockSpec((None, None, n16, LANES), lambda b, g, q: (b, g, 0, 0))
    return pl.pallas_call(
        functools.partial(_nsa_cmp_attn_kernel, tq=tq, n16=n16),
        grid=(batch, G, nq),
        in_specs=[pl.BlockSpec((tq, 2 * LANES), lambda b, g, q: (b * nq + q, g)), spec_c, spec_c,
                  pl.BlockSpec((n16, LANES), lambda b, g, q: (0, 0))],
        out_specs=[pl.BlockSpec((tq, 2 * LANES), lambda b, g, q: (b * nq + q, g)),
                   pl.BlockSpec((tq, LANES), lambda b, g, q: (b * nq + q, g))],
        out_shape=[jax.ShapeDtypeStruct((T, N_HEADS * HEAD_DIM), BF16),
                   jax.ShapeDtypeStruct((T, G * LANES), BF16)],
        compiler_params=_cparams("parallel", "parallel", "parallel"),
        name="nsa_cmp_attn",
    )(qk, kcmp, vcmp, cover)


def _nsa_sel_attn_kernel(q_ref, k_ref, v_ref, sel_ref, o_ref, kaug_ref, vaug_ref, *, tq, sub, tk, seq):
    g = pl.program_id(1)
    qi = pl.program_id(2)

    @pl.when(qi == 0)
    def _():
        kaug_ref[:, 0:LANES] = _both_halves(_group_lower(k_ref, g)).astype(BF16)
        blk_of_key = lax.broadcasted_iota(jnp.int32, (seq, LANES), 0) // NSA_SEL_LEN
        blk = lax.broadcasted_iota(jnp.int32, (seq, LANES), 1)
        kaug_ref[:, LANES:2 * LANES] = jnp.where(blk_of_key == blk, 1.0, 0.0).astype(BF16)
        vaug_ref[...] = _with_ones(_group_lower(v_ref, g)).astype(BF16)

    chains = range(tq // sub)
    rows = [slice(c * sub, (c + 1) * sub) for c in chains]
    q4 = [_stack_heads(q_ref, extra=sel_ref[rows[c], :], rows_in=rows[c]) for c in chains]
    q0 = qi * tq
    k_diag = pl.multiple_of((q0 // tk) * tk, tk)

    def tile(k0, m, acc, masked):
        kt = kaug_ref[pl.ds(k0, tk), :]
        vt = vaug_ref[pl.ds(k0, tk), :]
        s = [lax.dot_general(q4[c], kt, _NT, preferred_element_type=F32) for c in chains]
        if masked:
            kpos = k0 + lax.broadcasted_iota(jnp.int32, (sub, tk), 1)
            for c in chains:
                qpos = q0 + c * sub + lax.broadcasted_iota(jnp.int32, (sub, tk), 0)
                s[c] = jnp.where(jnp.concatenate([kpos <= qpos] * 4, axis=0), s[c], NEG_INF)
        m_new = [jnp.maximum(m[c], jnp.max(s[c], axis=1, keepdims=True)) for c in chains]
        p = [jnp.exp(s[c] - m_new[c]).astype(BF16) for c in chains]
        acc = [jnp.exp(m[c] - m_new[c]) * acc[c] + jnp.dot(p[c], vt, preferred_element_type=F32)
               for c in chains]
        return tuple(m_new), tuple(acc)

    def body(j, carry):
        return tile(pl.multiple_of(j * tk, tk), *carry, masked=False)

    carry = ((jnp.full((4 * sub, 1), NEG_INF, F32),) * len(chains),
             (jnp.zeros((4 * sub, LANES), F32),) * len(chains))
    carry = lax.fori_loop(0, q0 // tk, body, carry)
    _, acc = tile(k_diag, *carry, masked=True)
    for c in chains:
        o_ref[rows[c], :] = _unstack_heads(_normalise(acc[c]), sub).astype(o_ref.dtype)


def _nsa_sel_attn(qk, vv, selbias, *, batch, seq, k_col, v_col):
    T = qk.shape[0]
    G = NSA_KV_HEADS
    sub = 128
    tq = _pick_tile(seq, 256)
    tk = _pick_tile(seq, 512)
    nq = seq // tq
    assert tk % tq == 0
    return pl.pallas_call(
        functools.partial(_nsa_sel_attn_kernel, tq=tq, sub=sub, tk=tk, seq=seq),
        grid=(batch, G, nq),
        in_specs=[pl.BlockSpec((tq, 2 * LANES), lambda b, g, q: (b * nq + q, g)),
                  pl.BlockSpec((seq, LANES), lambda b, g, q: (b, k_col + g // 2)),
                  pl.BlockSpec((seq, LANES), lambda b, g, q: (b, v_col + g // 2)),
                  pl.BlockSpec((tq, LANES), lambda b, g, q: (b * nq + q, g))],
        out_specs=pl.BlockSpec((tq, 2 * LANES), lambda b, g, q: (b * nq + q, g)),
        out_shape=jax.ShapeDtypeStruct((T, N_HEADS * HEAD_DIM), BF16),
        scratch_shapes=[pltpu.VMEM((seq, 2 * LANES), BF16), pltpu.VMEM((seq, LANES), BF16)],
        compiler_params=_cparams("parallel", "arbitrary", "arbitrary"),
        name="nsa_sel_attn",
    )(qk, qk, vv, selbias)


def _nsa_win_attn_kernel(q_ref, k_ref, v_ref, o_ref, kb_ref, vaug_ref, *, tq):
    g = pl.program_id(1)
    qi = pl.program_id(2)

    @pl.when(qi == 0)
    def _():
        kb_ref[...] = _both_halves(_group_lower(k_ref, g)).astype(BF16)
        vaug_ref[...] = _with_ones(_group_lower(v_ref, g)).astype(BF16)

    q4 = _stack_heads(q_ref)
    q0 = qi * tq
    row = lax.broadcasted_iota(jnp.int32, (tq, tq), 0)
    col = lax.broadcasted_iota(jnp.int32, (tq, tq), 1)
    n_tiles = NSA_WINDOW // tq + 1
    m = jnp.full((4 * tq, 1), NEG_INF, F32)
    acc = jnp.zeros((4 * tq, LANES), F32)
    for d in reversed(range(n_tiles)):
        k0 = q0 - NSA_WINDOW + d * tq
        in_range = k0 >= 0
        k0c = pl.multiple_of(jnp.maximum(k0, 0), tq)
        s = lax.dot_general(q4, kb_ref[pl.ds(k0c, tq), :], _NT, preferred_element_type=F32)
        if d == n_tiles - 1:
            s = jnp.where(jnp.concatenate([col <= row] * 4, axis=0), s, NEG_INF)
        else:
            s = jnp.where(in_range, s, NEG_INF)
            if d == 0:
                s = jnp.where(jnp.concatenate([col > row] * 4, axis=0), s, NEG_INF)
        m_new = jnp.maximum(m, jnp.max(s, axis=1, keepdims=True))
        p = jnp.exp(s - m_new).astype(BF16)
        acc = jnp.exp(m - m_new) * acc + jnp.dot(p, vaug_ref[pl.ds(k0c, tq), :], preferred_element_type=F32)
        m = m_new
    o_ref[...] = _unstack_heads(_normalise(acc), tq).astype(o_ref.dtype)


def _nsa_win_attn(qk, vv, *, batch, seq, k_col, v_col):
    T = qk.shape[0]
    G = NSA_KV_HEADS
    tq = 128
    nq = seq // tq
    return pl.pallas_call(
        functools.partial(_nsa_win_attn_kernel, tq=tq),
        grid=(batch, G, nq),
        in_specs=[pl.BlockSpec((tq, 2 * LANES), lambda b, g, q: (b * nq + q, g)),
                  pl.BlockSpec((seq, LANES), lambda b, g, q: (b, k_col + g // 2)),
                  pl.BlockSpec((seq, LANES), lambda b, g, q: (b, v_col + g // 2))],
        out_specs=pl.BlockSpec((tq, 2 * LANES), lambda b, g, q: (b * nq + q, g)),
        out_shape=jax.ShapeDtypeStruct((T, N_HEADS * HEAD_DIM), BF16),
        scratch_shapes=[pltpu.VMEM((seq, LANES), BF16), pltpu.VMEM((seq, LANES), BF16)],
        compiler_params=_cparams("parallel", "arbitrary", "arbitrary"),
        name="nsa_win_attn",
    )(qk, qk, vv)


def _nsa_mixer(x, g, w_in, pos_k, pos_v, w_cmp_k, w_cmp_v, g_q, g_k, w_out, *, batch, seq):
    HD = N_HEADS * HEAD_DIM
    G, dh = NSA_KV_HEADS, HEAD_DIM
    kvw = G * dh
    scale = dh ** -0.5
    col = lambda n: slice(HD + n * kvw, HD + (n + 1) * kvw)
    w_qk = jnp.concatenate([w_in[:, :HD], w_in[:, col(0)], w_in[:, col(2)], w_in[:, col(4)]], axis=1).astype(BF16)
    w_v = jnp.concatenate([w_in[:, col(1)], w_in[:, col(3)], w_in[:, col(5)]], axis=1).astype(BF16)
    n_gate = 3 * N_HEADS
    w_gate = jnp.pad(w_in[:, HD + 6 * kvw:], ((0, 0), (0, LANES - n_gate))).astype(BF16)
    head_gain = jnp.concatenate([jnp.tile(g_q * scale, N_HEADS), jnp.tile(g_k, 3 * G)])

    inv = ROPE_THETA ** (-jnp.arange(0, dh, 2, dtype=F32) / dh)
    ang = jnp.arange(seq, dtype=F32)[:, None] * inv[None, :]
    cos, sin = jnp.cos(ang), jnp.sin(ang)
    tn = 256
    cos_t = jnp.tile(cos, (1, tn // (dh // 2)))
    sin_t = jnp.tile(jnp.concatenate([-sin, sin], axis=1), (1, tn // dh))

    qk = _proj(x, g, w_qk, seq=seq, head_gain=head_gain, rope=(cos_t, sin_t))
    vv = _proj(x, g, w_v, seq=seq)
    gate = _proj(x, g, w_gate, seq=seq, out_dtype=F32)

    def blocks16(t):
        half = NSA_CMP_LEN // 2
        t = t.reshape(batch, seq // half, half, G, dh).transpose(0, 3, 1, 2, 4)
        return t.reshape(batch, G, seq // half, half * dh)

    kcmp, vcmp = _nsa_compress(blocks16(qk[:, HD:HD + kvw]), blocks16(vv[:, :kvw]),
                               pos_k, pos_v, w_cmp_k, w_cmp_v, g_k)
    o_c, selbias = _nsa_cmp_attn(qk, kcmp, vcmp, batch=batch, seq=seq)
    nq_cols = HD // LANES
    o_s = _nsa_sel_attn(qk, vv, selbias, batch=batch, seq=seq,
                        k_col=nq_cols + kvw // LANES, v_col=kvw // LANES)
    o_w = _nsa_win_attn(qk, vv, batch=batch, seq=seq,
                        k_col=nq_cols + 2 * kvw // LANES, v_col=2 * kvw // LANES)
    r = np.arange(LANES)[:, None]
    c = np.arange(HD)[None, :] // dh
    expand = np.stack([(r == 3 * c + b) for b in range(3)]).astype(np.float32)
    return _outproj(x, w_out.astype(BF16), mode="gated3",
                    acts=[o_c, o_s, o_w, gate, jnp.asarray(expand, BF16)])


def kernel(x, norm_g, ffn1_w_gate, ffn1_w_up, ffn1_w_down, ffn2_w_gate, ffn2_w_up, ffn2_w_down, fox_w_in, fox_b_f, fox_g_q, fox_g_k, fox_w_out, nsa_w_in, nsa_cmp_pos_k, nsa_cmp_pos_v, nsa_w_cmp_k, nsa_w_cmp_v, nsa_g_q, nsa_g_k, nsa_w_out, gla_w_in, gla_w_gate_up, gla_b_gate, gla_g_out, gla_w_out, sb_w_in, sb_w_out):
    B, S, D = x.shape
    depth = norm_g.shape[0]
    n_mixers = 4
    x2 = x.reshape(B * S, D)
    for i in range(depth):
        m, j = i % n_mixers, i // n_mixers
        x2 = _ffn(x2, norm_g[i, 0], ffn1_w_gate[i].astype(BF16), ffn1_w_up[i].astype(BF16),
                  ffn1_w_down[i].astype(BF16))
        g = norm_g[i, 1]
        if m == 0:
            x2 = _fox_mixer(x2, g, fox_w_in[j], fox_b_f[j], fox_g_q[j], fox_g_k[j], fox_w_out[j], batch=B, seq=S)
        elif m == 1:
            x2 = _nsa_mixer(x2, g, nsa_w_in[j], nsa_cmp_pos_k[j], nsa_cmp_pos_v[j], nsa_w_cmp_k[j],
                            nsa_w_cmp_v[j], nsa_g_q[j], nsa_g_k[j], nsa_w_out[j], batch=B, seq=S)
        elif m == 2:
            x2 = _gla_mixer(x2, g, gla_w_in[j], gla_w_gate_up[j], gla_b_gate[j], gla_g_out[j], gla_w_out[j],
                            batch=B, seq=S)
        else:
            x2 = _sb_mixer(x2, g, sb_w_in[j], sb_w_out[j], batch=B, seq=S)
        x2 = _ffn(x2, norm_g[i, 2], ffn2_w_gate[i].astype(BF16), ffn2_w_up[i].astype(BF16),
                  ffn2_w_down[i].astype(BF16))
    return x2.reshape(B, S, D)
```

```python
import functools

import numpy as np
import jax
import jax.numpy as jnp
from jax import lax
from jax.experimental import pallas as pl
from jax.experimental.pallas import tpu as pltpu

F32 = jnp.float32
BF16 = jnp.bfloat16

N_HEADS = 16
HEAD_DIM = 64
ROPE_THETA = 10000.0
RMS_EPS = 1e-6
NEG_INF = -1e30
NSA_KV_HEADS = 4
NSA_CMP_LEN = 32
NSA_CMP_STRIDE = 16
NSA_SEL_LEN = 64
NSA_N_SEL = 16
NSA_WINDOW = 512
NSA_FORCED_SCORE = 1e9
GLA_HEADS = 4
GLA_GATE_RANK = 16
GLA_TAU = 16.0
GLA_CHUNK = 64

LANES = 128
V7X_VMEM_BYTES = 64 * 1024 * 1024
VMEM_LIMIT = V7X_VMEM_BYTES - 8 * 1024 * 1024

_NT = (((1,), (1,)), ((), ()))
_TN = (((0,), (0,)), ((), ()))


def _cparams(*sem):
    return pltpu.CompilerParams(dimension_semantics=sem, vmem_limit_bytes=VMEM_LIMIT)


def _rms_rows(x, g):
    return x * lax.rsqrt(jnp.mean(x * x, axis=-1, keepdims=True) + RMS_EPS) * g


def _softplus(z):
    return jnp.maximum(z, 0.0) + jnp.log(1.0 + jnp.exp(-jnp.abs(z)))


def _neg_abs(z):
    bits = lax.bitcast_convert_type(z, jnp.int32) | jnp.int32(-2 ** 31)
    return lax.bitcast_convert_type(bits, F32)


def _sigmoid(z):
    return 1.0 / (1.0 + jnp.exp(-z))


def _pick_tile(n, target):
    t = min(n, target)
    while n % t:
        t //= 2
    return t


def _ffn_kernel(x_ref, g_ref, wg_ref, wu_ref, wd_ref, o_ref, a_ref, *, tf):
    x = x_ref[...]
    h = _rms_rows(x, g_ref[...]).astype(BF16)
    for c in range(a_ref.shape[1] // tf):
        cols = slice(c * tf, (c + 1) * tf)
        gate = jnp.dot(h, wg_ref[:, cols], preferred_element_type=F32)
        up = jnp.dot(h, wu_ref[:, cols], preferred_element_type=F32)
        a_ref[:, cols] = (gate * _sigmoid(gate) * up).astype(BF16)
    o_ref[...] = x + 0.5 * jnp.dot(a_ref[...], wd_ref[...], preferred_element_type=F32)


def _resident(shape):
    return pl.BlockSpec(shape, lambda *_: (0,) * len(shape), pipeline_mode=pl.Buffered(1))


def _ffn(x, g, wg, wu, wd):
    T, D = x.shape
    F = wg.shape[1]
    tm = _pick_tile(T, 1024)
    tf = 256 if F % 256 == 0 else F
    return pl.pallas_call(
        functools.partial(_ffn_kernel, tf=tf),
        grid=(T // tm,),
        in_specs=[
            pl.BlockSpec((tm, D), lambda i: (i, 0)),
            _resident((1, D)),
            _resident((D, F)),
            _resident((D, F)),
            _resident((F, D)),
        ],
        out_specs=pl.BlockSpec((tm, D), lambda i: (i, 0)),
        out_shape=jax.ShapeDtypeStruct((T, D), F32),
        scratch_shapes=[pltpu.VMEM((tm, F), BF16)],
        compiler_params=_cparams("parallel"),
        name="ffn",
    )(x, g.reshape(1, D), wg, wu, wd)


PROJ_CHUNK = 256


def _proj_kernel(*refs, n_norm, rope, has_aux):
    it = iter(refs)
    x_ref, g_ref, w_ref = next(it), next(it), next(it)
    hg_ref, gsum_ref = (next(it), next(it)) if n_norm else (None, None)
    cos_ref, sin_ref = (next(it), next(it)) if rope else (None, None)
    waux_ref = next(it) if has_aux else None
    o_ref = next(it)
    ch = PROJ_CHUNK
    h = _rms_rows(x_ref[...], g_ref[...]).astype(BF16)
    for c in range(o_ref.shape[1] // ch):
        cols = slice(c * ch, (c + 1) * ch)
        y = jnp.dot(h, w_ref[:, cols], preferred_element_type=F32)
        if c * ch < n_norm:
            ss = jnp.dot((y * y).astype(BF16), gsum_ref[...], preferred_element_type=F32)
            y = y * lax.rsqrt(ss * (1.0 / HEAD_DIM) + RMS_EPS) * hg_ref[:, cols]
            if rope:
                lane = lax.broadcasted_iota(jnp.int32, (1, ch), 1)
                first_half = (lane % HEAD_DIM) < (HEAD_DIM // 2)
                partner = jnp.where(first_half,
                                    pltpu.roll(y, ch - HEAD_DIM // 2, 1),
                                    pltpu.roll(y, HEAD_DIM // 2, 1))
                y = y * cos_ref[...] + partner * sin_ref[...]
        o_ref[:, cols] = y.astype(o_ref.dtype)
    if has_aux:
        next(it)[...] = jnp.dot(h, waux_ref[...], preferred_element_type=F32)


def _proj(x, g, w, *, seq, head_gain=None, rope=None, w_aux=None):
    T, D = x.shape
    N = w.shape[1]
    tm = _pick_tile(seq, 1024)
    ch = PROJ_CHUNK
    n_norm = 0 if head_gain is None else head_gain.shape[0]
    assert N % ch == 0 and n_norm % ch == 0 and T % tm == 0
    args = [x, g.reshape(1, D), w]
    in_specs = [pl.BlockSpec((tm, D), lambda i: (i, 0)), _resident((1, D)), _resident((D, N))]
    if n_norm:
        gidx = np.arange(ch) // HEAD_DIM
        gsum = jnp.asarray((gidx[:, None] == gidx[None, :]).astype(np.float32), BF16)
        args += [head_gain.reshape(1, n_norm), gsum]
        in_specs += [_resident((1, n_norm)), _resident((ch, ch))]
    if rope is not None:
        nblk = seq // tm
        args += list(rope)
        in_specs += [pl.BlockSpec((tm, ch), lambda i: (i % nblk, 0))] * 2
    out_specs = [pl.BlockSpec((tm, N), lambda i: (i, 0))]
    out_shape = [jax.ShapeDtypeStruct((T, N), BF16)]
    if w_aux is not None:
        args.append(w_aux)
        in_specs.append(_resident(w_aux.shape))
        out_specs.append(pl.BlockSpec((tm, w_aux.shape[1]), lambda i: (i, 0)))
        out_shape.append(jax.ShapeDtypeStruct((T, w_aux.shape[1]), F32))
    out = pl.pallas_call(
        functools.partial(_proj_kernel, n_norm=n_norm, rope=rope is not None, has_aux=w_aux is not None),
        grid=(T // tm,),
        in_specs=in_specs,
        out_specs=out_specs,
        out_shape=out_shape,
        compiler_params=_cparams("parallel"),
        name="proj",
    )(*args)
    return out if w_aux is not None else out[0]


def _outproj_kernel(*refs, mode):
    if mode == "plain":
        x_ref, a_ref, w_ref, o_ref = refs
        a = a_ref[...]
    elif mode == "sigmoid_gate":
        x_ref, a_ref, og_ref, w_ref, o_ref = refs
        a = (a_ref[...].astype(F32) * _sigmoid(og_ref[...].astype(F32))).astype(BF16)
    else:
        x_ref, oc_ref, os_ref, ow_ref, gate_ref, exp_ref, w_ref, o_ref = refs
        gs = _sigmoid(gate_ref[...]).astype(BF16)
        a = None
        for c, o_c in enumerate((oc_ref, os_ref, ow_ref)):
            ge = jnp.dot(gs, exp_ref[c], preferred_element_type=F32)
            t = ge * o_c[...].astype(F32)
            a = t if a is None else a + t
        a = a.astype(BF16)
    o_ref[...] = x_ref[...] + jnp.dot(a, w_ref[...], preferred_element_type=F32)


def _outproj(x, w, *, mode, acts):
    T, D = x.shape
    K = w.shape[0]
    tm = _pick_tile(T, 512)
    row = lambda i: (i, 0)
    args = [x]
    in_specs = [pl.BlockSpec((tm, D), row)]
    for a in acts:
        if a.ndim == 3:
            args.append(a)
            in_specs.append(pl.BlockSpec(a.shape, lambda i: (0, 0, 0)))
        else:
            args.append(a)
            in_specs.append(pl.BlockSpec((tm, a.shape[1]), row))
    args.append(w)
    in_specs.append(pl.BlockSpec((K, D), lambda i: (0, 0)))
    return pl.pallas_call(
        functools.partial(_outproj_kernel, mode=mode),
        grid=(T // tm,),
        in_specs=in_specs,
        out_specs=pl.BlockSpec((tm, D), row),
        out_shape=jax.ShapeDtypeStruct((T, D), F32),
        compiler_params=_cparams("parallel"),
        name="outproj_" + mode,
    )(*args)


def _fox_c_kernel(x_ref, g_ref, wf_ref, bf_ref, tri_ref, c_ref, carry_ref):
    @pl.when(pl.program_id(1) == 0)
    def _():
        carry_ref[...] = jnp.zeros_like(carry_ref)

    h = _rms_rows(x_ref[...], g_ref[...])
    f = lax.dot_general(wf_ref[...], h, _NT, precision=lax.Precision.HIGHEST,
                        preferred_element_type=F32) + bf_ref[...]
    ls = jnp.minimum(f, 0.0) - jnp.log(1.0 + jnp.exp(-jnp.abs(f)))
    cs = jnp.dot(ls, tri_ref[...], precision=lax.Precision.HIGHEST,
                 preferred_element_type=F32) + carry_ref[...]
    c_ref[...] = cs
    carry_ref[...] = cs[:, cs.shape[1] - 1:]


def _fox_c(x, g, wf_t, b_f, *, batch, seq):
    T, D = x.shape
    H = wf_t.shape[0]
    tm = _pick_tile(seq, 512)
    ns = seq // tm
    tri = jnp.asarray(np.triu(np.ones((tm, tm), np.float32)))
    return pl.pallas_call(
        _fox_c_kernel,
        grid=(batch, ns),
        in_specs=[
            pl.BlockSpec((tm, D), lambda b, s: (b * ns + s, 0)),
            pl.BlockSpec((1, D), lambda b, s: (0, 0)),
            pl.BlockSpec((H, D), lambda b, s: (0, 0)),
            pl.BlockSpec((H, 1), lambda b, s: (0, 0)),
            pl.BlockSpec((tm, tm), lambda b, s: (0, 0)),
        ],
        out_specs=pl.BlockSpec((None, H, tm), lambda b, s: (b, 0, s)),
        out_shape=jax.ShapeDtypeStruct((batch, H, seq), F32),
        scratch_shapes=[pltpu.VMEM((H, 1), F32)],
        compiler_params=_cparams("parallel", "arbitrary"),
        name="fox_c",
    )(x, g.reshape(1, D), wf_t, b_f.reshape(H, 1), tri)


def _fox_attn_kernel(q_ref, k_ref, v_ref, c_ref, o_ref, vaug_ref, *, t):
    qi = pl.program_id(2)
    lane = lax.broadcasted_iota(jnp.int32, (1, LANES), 1)
    upper = lane >= HEAD_DIM

    @pl.when(qi == 0)
    def _():
        v = v_ref[...]
        one = jnp.ones_like(v)
        vaug_ref[0] = jnp.where(upper, one, v)
        vaug_ref[1] = jnp.where(upper, v, one)

    q0 = pl.multiple_of(qi * t, t)
    qslab = q_ref[...]
    zero = jnp.zeros_like(qslab)
    qpos = lax.broadcasted_iota(jnp.int32, (t, t), 0)
    kpos = lax.broadcasted_iota(jnp.int32, (t, t), 1)
    causal = kpos <= qpos

    heads = range(2)
    qh = (jnp.where(upper, zero, qslab), jnp.where(upper, qslab, zero))
    c_q0 = [c_ref[i, :, pl.ds(q0, LANES)][:, 0:1] for i in heads]

    def tile(k0, m, acc, masked):
        kt = k_ref[pl.ds(k0, t), :]
        s = [lax.dot_general(qh[i], kt, _NT, preferred_element_type=F32) for i in heads]
        s = [s[i] + (c_q0[i] - c_ref[i, :, pl.ds(k0, t)]) for i in heads]
        if masked:
            s = [jnp.where(causal, s[i], NEG_INF) for i in heads]
        m_new = [jnp.maximum(m[i], jnp.max(s[i], axis=1, keepdims=True)) for i in heads]
        p = [jnp.exp(s[i] - m_new[i]).astype(BF16) for i in heads]
        acc = [jnp.exp(m[i] - m_new[i]) * acc[i]
               + jnp.dot(p[i], vaug_ref[i, pl.ds(k0, t), :], preferred_element_type=F32) for i in heads]
        return tuple(m_new), tuple(acc)

    def body(j, carry):
        return tile(pl.multiple_of(j * t, t), *carry, masked=False)

    m0 = (jnp.full((t, 1), NEG_INF, F32),) * 2
    acc0 = (jnp.zeros((t, LANES), F32),) * 2
    m, acc = lax.fori_loop(0, qi, body, (m0, acc0))
    m, accs = tile(q0, m, acc, masked=True)

    num = jnp.where(upper, accs[1], accs[0])
    den = jnp.where(upper, accs[0], accs[1])
    den = pltpu.roll(den, HEAD_DIM, 1)
    o_ref[...] = (num / den).astype(o_ref.dtype)


def _fox_attn(qkvg, c, *, batch, seq):
    T = qkvg.shape[0]
    HD = N_HEADS * HEAD_DIM
    t = _pick_tile(seq, 512)
    nq = seq // t
    npair = HD // LANES
    return pl.pallas_call(
        functools.partial(_fox_attn_kernel, t=t),
        grid=(batch, npair, nq),
        in_specs=[
            pl.BlockSpec((t, LANES), lambda b, h, q: (b * nq + q, h)),
            pl.BlockSpec((seq, LANES), lambda b, h, q: (b, npair + h)),
            pl.BlockSpec((seq, LANES), lambda b, h, q: (b, 2 * npair + h)),
            pl.BlockSpec((None, 2, 1, seq), lambda b, h, q: (b, h, 0, 0)),
        ],
        out_specs=pl.BlockSpec((t, LANES), lambda b, h, q: (b * nq + q, h)),
        out_shape=jax.ShapeDtypeStruct((T, HD), BF16),
        scratch_shapes=[pltpu.VMEM((2, seq, LANES), BF16)],
        compiler_params=_cparams("parallel", "arbitrary", "arbitrary"),
        name="fox_attn",
    )(qkvg, qkvg, qkvg, c.reshape(batch, N_HEADS, 1, seq))


def _fox_mixer(x, g, w_in, b_f, g_q, g_k, w_out, *, batch, seq):
    HD = N_HEADS * HEAD_DIM
    scale = HEAD_DIM ** -0.5
    w = jnp.concatenate([w_in[:, :3 * HD], w_in[:, 3 * HD + N_HEADS:]], axis=1).astype(BF16)
    wf_t = w_in[:, 3 * HD:3 * HD + N_HEADS].T
    head_gain = jnp.concatenate([jnp.tile(g_q * scale, N_HEADS), jnp.tile(g_k, N_HEADS)])
    qkvg = _proj(x, g, w, seq=seq, head_gain=head_gain)
    c = _fox_c(x, g, wf_t, b_f, batch=batch, seq=seq)
    o = _fox_attn(qkvg, c, batch=batch, seq=seq)
    return _outproj_fox(x, o, qkvg, w_out.astype(BF16))


def _outproj_fox(x, o, qkvg, w):
    T, D = x.shape
    HD = o.shape[1]
    tm = _pick_tile(T, 512)
    return pl.pallas_call(
        functools.partial(_outproj_kernel, mode="sigmoid_gate"),
        grid=(T // tm,),
        in_specs=[
            pl.BlockSpec((tm, D), lambda i: (i, 0)),
            pl.BlockSpec((tm, HD), lambda i: (i, 0)),
            pl.BlockSpec((tm, HD), lambda i: (i, 3)),
            pl.BlockSpec((HD, D), lambda i: (0, 0)),
        ],
        out_specs=pl.BlockSpec((tm, D), lambda i: (i, 0)),
        out_shape=jax.ShapeDtypeStruct((T, D), F32),
        compiler_params=_cparams("parallel"),
        name="outproj_fox",
    )(x, o, qkvg, w)


def _sb_attn_kernel(q_ref, k_ref, v_ref, tri_ref, o_ref, *, tq, tk):
    qi = pl.program_id(2)
    lane = lax.broadcasted_iota(jnp.int32, (1, LANES), 1)
    upper = lane >= HEAD_DIM
    q0 = pl.multiple_of(qi * tq, tq)
    qslab = q_ref[...]
    zero = jnp.zeros_like(qslab)
    qh = (jnp.where(upper, zero, qslab), jnp.where(upper, qslab, zero))
    qpos = lax.broadcasted_iota(jnp.int32, (tq, tk), 0)
    kpos = lax.broadcasted_iota(jnp.int32, (tq, tk), 1)

    heads = range(2)

    def scores(j, diag_offset):
        kt = k_ref[pl.ds(pl.multiple_of(j * tk, tk), tk), :]
        z = [lax.dot_general(qh[i], kt, _NT, preferred_element_type=F32) for i in heads]
        sp = [jnp.maximum(z[i], 0.0) + jnp.log2(1.0 + jnp.exp2(_neg_abs(z[i]))) for i in heads]
        if diag_offset is not None:
            strict = kpos + diag_offset < qpos
            spm = [jnp.where(strict, sp[i], 0.0) for i in heads]
        else:
            spm = sp
        cum = [jnp.dot(spm[i].astype(BF16), tri_ref[...], preferred_element_type=F32) for i in heads]
        w = [(z[i] - sp[i]) - cum[i] for i in heads]
        if diag_offset is not None:
            w = [jnp.where(strict, w[i], NEG_INF) for i in heads]
        total = [cum[i][:, 0:1] + spm[i][:, 0:1] for i in heads]
        return tuple(w), tuple(total)

    def accumulate(j, w, total, rest, acc):
        vt = v_ref[pl.ds(pl.multiple_of(j * tk, tk), tk), :]
        a = [jnp.exp2(w[i] - rest[i]).astype(BF16) for i in heads]
        acc = tuple(acc[i] + jnp.dot(a[i], vt, preferred_element_type=F32) for i in heads)
        return tuple(rest[i] + total[i] for i in heads), acc

    n_diag = tq // tk
    assert n_diag == 2
    rest = (jnp.zeros((tq, 1), F32),) * 2
    acc = (jnp.zeros((tq, LANES), F32),) * 2

    def pair(j_hi, rest, acc, diag):
        s_hi = scores(j_hi, tk if diag else None)
        s_lo = scores(j_hi - 1, 0 if diag else None)
        rest, acc = accumulate(j_hi, *s_hi, rest, acc)
        return accumulate(j_hi - 1, *s_lo, rest, acc)

    rest, acc = pair(qi * n_diag + 1, rest, acc, True)

    def body(jj, c):
        return pair((qi - jj) * n_diag - 1, *c, False)

    rest, acc = lax.fori_loop(0, qi, body, (rest, acc))
    o_ref[...] = jnp.where(upper, acc[1], acc[0]).astype(o_ref.dtype)


def _sb_attn(qkv, *, batch, seq):
    T = qkv.shape[0]
    HD = N_HEADS * HEAD_DIM
    tq = _pick_tile(seq, 512)
    tk = 256
    nq = seq // tq
    npair = HD // LANES
    tri = jnp.asarray(np.tril(np.ones((tk, tk), np.float32), -1), BF16)
    return pl.pallas_call(
        functools.partial(_sb_attn_kernel, tq=tq, tk=tk),
        grid=(batch, npair, nq),
        in_specs=[
            pl.BlockSpec((tq, LANES), lambda b, h, q: (b * nq + q, h)),
            pl.BlockSpec((seq, LANES), lambda b, h, q: (b, npair + h)),
            pl.BlockSpec((seq, LANES), lambda b, h, q: (b, 2 * npair + h)),
            pl.BlockSpec((tk, tk), lambda b, h, q: (0, 0)),
        ],
        out_specs=pl.BlockSpec((tq, LANES), lambda b, h, q: (b * nq + q, h)),
        out_shape=jax.ShapeDtypeStruct((T, HD), BF16),
        compiler_params=_cparams("parallel", "parallel", "arbitrary"),
        name="sb_attn",
    )(qkv, qkv, qkv, tri)


def _sb_mixer(x, g, w_in, w_out, *, batch, seq):
    HD = N_HEADS * HEAD_DIM
    scale = HEAD_DIM ** -0.5 * float(np.log2(np.e))
    w = jnp.concatenate([w_in[:, :HD] * scale, w_in[:, HD:]], axis=1).astype(BF16)
    qkv = _proj(x, g, w, seq=seq)
    o = _sb_attn(qkv, batch=batch, seq=seq)
    return _outproj(x, w_out.astype(BF16), mode="plain", acts=[o])


def _gla_kernel(q_ref, k_ref, v_ref, r_ref, glow_ref, wgu_ref, bg_ref, gout_ref, tri_ref,
                o_ref, state_ref, *, tm, dk):
    C = GLA_CHUNK

    @pl.when(pl.program_id(2) == 0)
    def _():
        state_ref[...] = jnp.zeros_like(state_ref)

    gate = jnp.dot(glow_ref[...], wgu_ref[...], precision=lax.Precision.HIGHEST,
                   preferred_element_type=F32) + bg_ref[...]
    log_a = (jnp.minimum(gate, 0.0) - jnp.log(1.0 + jnp.exp(-jnp.abs(gate)))) * (1.0 / GLA_TAU)
    row = lax.broadcasted_iota(jnp.int32, (C, C), 0)
    col = lax.broadcasted_iota(jnp.int32, (C, C), 1)
    causal = col <= row
    q_scale = dk ** -0.5
    chunks = [slice(c * C, (c + 1) * C) for c in range(tm // C)]

    tri = tri_ref[...]
    b = [jnp.dot(tri, log_a[sl], precision=lax.Precision.HIGHEST, preferred_element_type=F32)
         for sl in chunks]
    b_last = jnp.concatenate([jnp.broadcast_to(bc[C - 1:C], bc.shape) for bc in b], axis=0)
    b = jnp.concatenate(b, axis=0)
    q = q_ref[...].astype(F32) * q_scale
    k = k_ref[...].astype(F32)
    q_dec = (q * jnp.exp(b)).astype(BF16)
    k_neg = (k * jnp.exp(-b)).astype(BF16)
    k_rem = (k * jnp.exp(b_last - b)).astype(BF16)
    decay = jnp.exp(b_last)
    attn = [lax.dot_general(q_dec[sl], k_neg[sl], _NT, preferred_element_type=F32) for sl in chunks]
    attn = [jnp.where(causal, a, 0.0).astype(BF16) for a in attn]
    o_intra = [jnp.dot(attn[c], v_ref[sl, :], preferred_element_type=F32) for c, sl in enumerate(chunks)]
    u_t = [lax.dot_general(v_ref[sl, :], k_rem[sl], _TN, preferred_element_type=F32) for sl in chunks]

    state_t = state_ref[...]
    outs = []
    for c, sl in enumerate(chunks):
        outs.append(o_intra[c] + lax.dot_general(q_dec[sl], state_t.astype(BF16), _NT,
                                                 preferred_element_type=F32))
        state_t = state_t * decay[c * C:c * C + 1] + u_t[c]
    state_ref[...] = state_t

    o = jnp.concatenate(outs, axis=0)
    y = o * lax.rsqrt(jnp.mean(o * o, axis=-1, keepdims=True) + RMS_EPS) * gout_ref[...]
    r = r_ref[...].astype(F32)
    o_ref[...] = (y * (r * _sigmoid(r))).astype(o_ref.dtype)


def _gla(qkvr, glow, wgu, b_gate, g_out, *, batch, seq):
    T = qkvr.shape[0]
    Hg = GLA_HEADS
    dk_total = wgu.shape[1]
    dk = dk_total // Hg
    dv = (qkvr.shape[1] - 2 * dk_total) // 2 // Hg
    tm = _pick_tile(seq, 512)
    ns = seq // tm
    kq, kv = dk_total // dk, dk_total // dv
    tri = jnp.asarray(np.tril(np.ones((GLA_CHUNK, GLA_CHUNK), np.float32)))
    rowmap = lambda off: (lambda b, h, s: (b * ns + s, off + h))
    return pl.pallas_call(
        functools.partial(_gla_kernel, tm=tm, dk=dk),
        grid=(batch, Hg, ns),
        in_specs=[
            pl.BlockSpec((tm, dk), rowmap(0)),
            pl.BlockSpec((tm, dk), rowmap(kq)),
            pl.BlockSpec((tm, dv), rowmap(2 * kv)),
            pl.BlockSpec((tm, dv), rowmap(2 * kv + Hg)),
            pl.BlockSpec((tm, LANES), lambda b, h, s: (b * ns + s, 0)),
            pl.BlockSpec((LANES, dk), lambda b, h, s: (0, h)),
            pl.BlockSpec((1, dk), lambda b, h, s: (0, h)),
            pl.BlockSpec((1, dv), lambda b, h, s: (0, 0)),
            pl.BlockSpec((GLA_CHUNK, GLA_CHUNK), lambda b, h, s: (0, 0)),
        ],
        out_specs=pl.BlockSpec((tm, dv), rowmap(0)),
        out_shape=jax.ShapeDtypeStruct((T, Hg * dv), BF16),
        scratch_shapes=[pltpu.VMEM((dv, dk), F32)],
        compiler_params=_cparams("parallel", "parallel", "arbitrary"),
        name="gla",
    )(qkvr, qkvr, qkvr, qkvr, glow, wgu, b_gate.reshape(1, dk_total), g_out.reshape(1, dv), tri)


def _gla_mixer(x, g, w_in, w_gate_up, b_gate, g_out, w_out, *, batch, seq):
    dk_total = w_gate_up.shape[1]
    dv_total = w_out.shape[0]
    lo = 2 * dk_total + dv_total
    w_main = jnp.concatenate([w_in[:, :lo], w_in[:, lo + GLA_GATE_RANK:]], axis=1).astype(BF16)
    w_low = jnp.pad(w_in[:, lo:lo + GLA_GATE_RANK], ((0, 0), (0, LANES - GLA_GATE_RANK))).astype(BF16)
    wgu = jnp.pad(w_gate_up, ((0, LANES - GLA_GATE_RANK), (0, 0)))
    qkvr, glow = _proj(x, g, w_main, seq=seq, w_aux=w_low)
    o = _gla(qkvr, glow, wgu, b_gate, g_out, batch=batch, seq=seq)
    return _outproj(x, w_out.astype(BF16), mode="plain", acts=[o])


def _nsa_compress_kernel(kc_ref, vc_ref, pk_ref, pv_ref, wk_ref, wv_ref, gk_ref, ko_ref, vo_ref, *, n16):
    def compress(r_ref, pos_ref, w_ref):
        r = r_ref[...].astype(F32)
        first = jnp.dot((r + pos_ref[0:1, :]).astype(BF16), w_ref[0], preferred_element_type=F32)
        second = jnp.dot((r + pos_ref[1:2, :]).astype(BF16), w_ref[1], preferred_element_type=F32)
        return first + pltpu.roll(second, n16 - 1, 0)

    kc = compress(kc_ref, pk_ref, wk_ref)
    ko_ref[...] = _rms_rows(kc, gk_ref[...]).astype(ko_ref.dtype)
    vo_ref[...] = compress(vc_ref, pv_ref, wv_ref).astype(vo_ref.dtype)


def _nsa_compress(kc_r, vc_r, pos_k, pos_v, w_k, w_v, g_k):
    B, G, n16, width = kc_r.shape
    dh = HEAD_DIM
    half = NSA_CMP_LEN // 2
    dup = lambda w: jnp.concatenate([w, w], axis=-1).reshape(2, half * dh, 2 * dh).astype(BF16)
    spec_r = pl.BlockSpec((None, None, n16, width), lambda b, g: (b, g, 0, 0))
    spec_o = pl.BlockSpec((None, None, n16, 2 * dh), lambda b, g: (b, g, 0, 0))
    const2 = lambda shape: pl.BlockSpec(shape, lambda b, g: (0,) * len(shape))
    out = jax.ShapeDtypeStruct((B, G, n16, 2 * dh), BF16)
    return pl.pallas_call(
        functools.partial(_nsa_compress_kernel, n16=n16),
        grid=(B, G),
        in_specs=[spec_r, spec_r, const2((2, width)), const2((2, width)),
                  const2((2, width, 2 * dh)), const2((2, width, 2 * dh)), const2((1, 2 * dh))],
        out_specs=[spec_o, spec_o],
        out_shape=[out, out],
        compiler_params=_cparams("parallel", "parallel"),
        name="nsa_compress",
    )(kc_r, vc_r, pos_k.reshape(2, width), pos_v.reshape(2, width), dup(w_k), dup(w_v),
      jnp.tile(g_k, 2).reshape(1, 2 * dh))


def _stack_heads(q_ref, extra=None, rows_in=slice(None)):
    lane = lax.broadcasted_iota(jnp.int32, (1, LANES), 1)
    upper = lane >= HEAD_DIM
    rows = []
    for j in range(2):
        slab = q_ref[rows_in, j * LANES:(j + 1) * LANES]
        zero = jnp.zeros_like(slab)
        for i in range(2):
            qh = jnp.where(upper, slab, zero) if i else jnp.where(upper, zero, slab)
            rows.append(qh if extra is None else jnp.concatenate([qh, extra], axis=1))
    return jnp.concatenate(rows, axis=0)


def _unstack_heads(o, tq):
    lane = lax.broadcasted_iota(jnp.int32, (1, LANES), 1)
    upper = lane >= HEAD_DIM
    slabs = []
    for j in range(2):
        lo = o[(2 * j) * tq:(2 * j + 1) * tq]
        hi = pltpu.roll(o[(2 * j + 1) * tq:(2 * j + 2) * tq], HEAD_DIM, 1)
        slabs.append(jnp.where(upper, hi, lo))
    return jnp.concatenate(slabs, axis=1)


def _group_lower(slab_ref, g):
    x = slab_ref[...].astype(F32)
    return jnp.where(g % 2 == 1, pltpu.roll(x, HEAD_DIM, 1), x)


def _both_halves(x):
    lane = lax.broadcasted_iota(jnp.int32, (1, LANES), 1)
    return jnp.where(lane >= HEAD_DIM, pltpu.roll(x, HEAD_DIM, 1), x)


def _with_ones(x):
    lane = lax.broadcasted_iota(jnp.int32, (1, LANES), 1)
    return jnp.where(lane >= HEAD_DIM, 1.0, x)


def _normalise(acc):
    return acc / pltpu.roll(acc, HEAD_DIM, 1)


def _nsa_cmp_attn_kernel(q_ref, kc_ref, vc_ref, cover_ref, oc_ref, sel_ref, *, tq, n16):
    qi = pl.program_id(2)
    q4 = _stack_heads(q_ref)
    s = lax.dot_general(q4, kc_ref[...], _NT, preferred_element_type=F32)
    qpos = qi * tq + lax.broadcasted_iota(jnp.int32, (tq, n16), 0)
    cmp_end = lax.broadcasted_iota(jnp.int32, (tq, n16), 1) * NSA_CMP_STRIDE + (NSA_CMP_LEN - 1)
    valid = jnp.concatenate([cmp_end <= qpos] * 4, axis=0)
    s = jnp.where(valid, s, NEG_INF)
    e = jnp.exp(s - jnp.max(s, axis=1, keepdims=True))
    p = jnp.where(valid, e / jnp.sum(e, axis=1, keepdims=True), 0.0)
    o = jnp.dot(p.astype(BF16), vc_ref[...], preferred_element_type=F32)
    oc_ref[...] = _unstack_heads(o, tq).astype(oc_ref.dtype)

    psum = p[0:tq] + p[tq:2 * tq] + p[2 * tq:3 * tq] + p[3 * tq:4 * tq]
    cover = cover_ref[...]
    hi = psum.astype(BF16)
    r1 = psum - hi.astype(F32)
    mid = r1.astype(BF16)
    lo = (r1 - mid.astype(F32)).astype(BF16)
    imp = (jnp.dot(hi, cover, preferred_element_type=F32) + jnp.dot(mid, cover, preferred_element_type=F32)
           + jnp.dot(lo, cover, preferred_element_type=F32))

    blk = lax.broadcasted_iota(jnp.int32, (tq, LANES), 1)
    cur = (qi * tq + lax.broadcasted_iota(jnp.int32, (tq, LANES), 0)) // NSA_SEL_LEN
    forced = (blk == 0) | (blk == cur) | (blk == cur - 1)
    vals = jnp.where(forced, NSA_FORCED_SCORE, jnp.where(blk <= cur, imp, -1.0))
    blk_f = blk.astype(F32)
    chosen = jnp.zeros((tq, LANES), jnp.bool_)
    for _ in range(NSA_N_SEL):
        top = jnp.max(vals, axis=1, keepdims=True)
        first = jnp.min(jnp.where(vals == top, blk_f, float(LANES)), axis=1, keepdims=True)
        pick = blk_f == first
        chosen = chosen | pick
        vals = jnp.where(pick, -3e38, vals)
    sel_ref[...] = jnp.where(chosen, 0.0, NEG_INF).astype(sel_ref.dtype)


def _nsa_cmp_attn(qk, kcmp, vcmp, *, batch, seq):
    T = qk.shape[0]
    G = NSA_KV_HEADS
    tq = _pick_tile(seq, 512)
    nq = seq // tq
    n16 = kcmp.shape[2]
    n_sel = seq // NSA_SEL_LEN
    assert n_sel <= LANES and seq % tq == 0
    n = np.arange(n16)[:, None] * NSA_CMP_STRIDE
    j = np.arange(LANES)[None, :] * NSA_SEL_LEN
    cover = (n < j + NSA_SEL_LEN) & (n + NSA_CMP_LEN > j) & (np.arange(LANES)[None, :] < n_sel)
    cover = jnp.asarray(cover.astype(np.float32), BF16)
    spec_c = pl.BlockSpec((None, None, n16, LANES), lambda b, g, q: (b, g, 0, 0))
    return pl.pallas_call(
        functools.partial(_nsa_cmp_attn_kernel, tq=tq, n16=n16),
        grid=(batch, G, nq),
        in_specs=[pl.BlockSpec((tq, 2 * LANES), lambda b, g, q: (b * nq + q, g)), spec_c, spec_c,
                  pl.BlockSpec((n16, LANES), lambda b, g, q: (0, 0))],
        out_specs=[pl.BlockSpec((tq, 2 * LANES), lambda b, g, q: (b * nq + q, g)),
                   pl.BlockSpec((tq, LANES), lambda b, g, q: (b * nq + q, g))],
        out_shape=[jax.ShapeDtypeStruct((T, N_HEADS * HEAD_DIM), BF16),
                   jax.ShapeDtypeStruct((T, G * LANES), BF16)],
        compiler_params=_cparams("parallel", "parallel", "parallel"),
        name="nsa_cmp_attn",
    )(qk, kcmp, vcmp, cover)


def _nsa_sel_attn_kernel(q_ref, k_ref, v_ref, sel_ref, o_ref, kaug_ref, vaug_ref, *, tq, sub, tk, seq):
    g = pl.program_id(1)
    qi = pl.program_id(2)

    @pl.when(qi == 0)
    def _():
        kaug_ref[:, 0:LANES] = _both_halves(_group_lower(k_ref, g)).astype(BF16)
        blk_of_key = lax.broadcasted_iota(jnp.int32, (seq, LANES), 0) // NSA_SEL_LEN
        blk = lax.broadcasted_iota(jnp.int32, (seq, LANES), 1)
        kaug_ref[:, LANES:2 * LANES] = jnp.where(blk_of_key == blk, 1.0, 0.0).astype(BF16)
        vaug_ref[...] = _with_ones(_group_lower(v_ref, g)).astype(BF16)

    chains = range(tq // sub)
    rows = [slice(c * sub, (c + 1) * sub) for c in chains]
    q4 = [_stack_heads(q_ref, extra=sel_ref[rows[c], :], rows_in=rows[c]) for c in chains]
    q0 = qi * tq
    k_diag = pl.multiple_of((q0 // tk) * tk, tk)

    def tile(k0, m, acc, masked):
        kt = kaug_ref[pl.ds(k0, tk), :]
        vt = vaug_ref[pl.ds(k0, tk), :]
        s = [lax.dot_general(q4[c], kt, _NT, preferred_element_type=F32) for c in chains]
        if masked:
            kpos = k0 + lax.broadcasted_iota(jnp.int32, (sub, tk), 1)
            for c in chains:
                qpos = q0 + c * sub + lax.broadcasted_iota(jnp.int32, (sub, tk), 0)
                s[c] = jnp.where(jnp.concatenate([kpos <= qpos] * 4, axis=0), s[c], NEG_INF)
        m_new = [jnp.maximum(m[c], jnp.max(s[c], axis=1, keepdims=True)) for c in chains]
        p = [jnp.exp(s[c] - m_new[c]).astype(BF16) for c in chains]
        acc = [jnp.exp(m[c] - m_new[c]) * acc[c] + jnp.dot(p[c], vt, preferred_element_type=F32)
               for c in chains]
        return tuple(m_new), tuple(acc)

    def body(j, carry):
        return tile(pl.multiple_of(j * tk, tk), *carry, masked=False)

    carry = ((jnp.full((4 * sub, 1), NEG_INF, F32),) * len(chains),
             (jnp.zeros((4 * sub, LANES), F32),) * len(chains))
    carry = lax.fori_loop(0, q0 // tk, body, carry)
    _, acc = tile(k_diag, *carry, masked=True)
    for c in chains:
        o_ref[rows[c], :] = _unstack_heads(_normalise(acc[c]), sub).astype(o_ref.dtype)


def _nsa_sel_attn(qk, vv, selbias, *, batch, seq, k_col, v_col):
    T = qk.shape[0]
    G = NSA_KV_HEADS
    sub = 128
    tq = _pick_tile(seq, 256)
    tk = _pick_tile(seq, 512)
    nq = seq // tq
    assert tk % tq == 0
    return pl.pallas_call(
        functools.partial(_nsa_sel_attn_kernel, tq=tq, sub=sub, tk=tk, seq=seq),
        grid=(batch, G, nq),
        in_specs=[pl.BlockSpec((tq, 2 * LANES), lambda b, g, q: (b * nq + q, g)),
                  pl.BlockSpec((seq, LANES), lambda b, g, q: (b, k_col + g // 2)),
                  pl.BlockSpec((seq, LANES), lambda b, g, q: (b, v_col + g // 2)),
                  pl.BlockSpec((tq, LANES), lambda b, g, q: (b * nq + q, g))],
        out_specs=pl.BlockSpec((tq, 2 * LANES), lambda b, g, q: (b * nq + q, g)),
        out_shape=jax.ShapeDtypeStruct((T, N_HEADS * HEAD_DIM), BF16),
        scratch_shapes=[pltpu.VMEM((seq, 2 * LANES), BF16), pltpu.VMEM((seq, LANES), BF16)],
        compiler_params=_cparams("parallel", "arbitrary", "arbitrary"),
        name="nsa_sel_attn",
    )(qk, qk, vv, selbias)


def _nsa_win_attn_kernel(q_ref, k_ref, v_ref, o_ref, kb_ref, vaug_ref, *, tq, span):
    g = pl.program_id(1)
    qi = pl.program_id(2)

    @pl.when(qi == 0)
    def _():
        kb_ref[...] = _both_halves(_group_lower(k_ref, g)).astype(BF16)
        vaug_ref[...] = _with_ones(_group_lower(v_ref, g)).astype(BF16)

    q4 = _stack_heads(q_ref)
    q0 = qi * tq
    k_start = pl.multiple_of(jnp.maximum(q0 - NSA_WINDOW, 0), tq)
    keys = pl.ds(k_start, span)
    dist = (q0 - k_start) + (lax.broadcasted_iota(jnp.int32, (tq, span), 0)
                             - lax.broadcasted_iota(jnp.int32, (tq, span), 1))
    ok = (dist >= 0) & (dist < NSA_WINDOW)
    ok2 = jnp.concatenate([ok, ok], axis=0)
    pairs = [q4[0:2 * tq], q4[2 * tq:4 * tq]]
    s = [lax.dot_general(qp, kb_ref[keys, :], _NT, preferred_element_type=F32) for qp in pairs]
    s = [jnp.where(ok2, sc, NEG_INF) for sc in s]
    p = [jnp.exp(sc - jnp.max(sc, axis=1, keepdims=True)).astype(BF16) for sc in s]
    acc = [jnp.dot(pc, vaug_ref[keys, :], preferred_element_type=F32) for pc in p]
    acc = jnp.concatenate(acc, axis=0)
    o_ref[...] = _unstack_heads(_normalise(acc), tq).astype(o_ref.dtype)


def _nsa_win_attn(qk, vv, *, batch, seq, k_col, v_col):
    T = qk.shape[0]
    G = NSA_KV_HEADS
    tq = _pick_tile(seq, 256)
    nq = seq // tq
    span = NSA_WINDOW + tq
    assert NSA_WINDOW % tq == 0 and seq >= span
    return pl.pallas_call(
        functools.partial(_nsa_win_attn_kernel, tq=tq, span=span),
        grid=(batch, G, nq),
        in_specs=[pl.BlockSpec((tq, 2 * LANES), lambda b, g, q: (b * nq + q, g)),
                  pl.BlockSpec((seq, LANES), lambda b, g, q: (b, k_col + g // 2)),
                  pl.BlockSpec((seq, LANES), lambda b, g, q: (b, v_col + g // 2))],
        out_specs=pl.BlockSpec((tq, 2 * LANES), lambda b, g, q: (b * nq + q, g)),
        out_shape=jax.ShapeDtypeStruct((T, N_HEADS * HEAD_DIM), BF16),
        scratch_shapes=[pltpu.VMEM((seq, LANES), BF16), pltpu.VMEM((seq, LANES), BF16)],
        compiler_params=_cparams("parallel", "arbitrary", "arbitrary"),
        name="nsa_win_attn",
    )(qk, qk, vv)


def _nsa_mixer(x, g, w_in, pos_k, pos_v, w_cmp_k, w_cmp_v, g_q, g_k, w_out, *, batch, seq):
    HD = N_HEADS * HEAD_DIM
    G, dh = NSA_KV_HEADS, HEAD_DIM
    kvw = G * dh
    scale = dh ** -0.5
    col = lambda n: slice(HD + n * kvw, HD + (n + 1) * kvw)
    order = [slice(0, HD), col(0), col(2), col(4), col(1), col(3), col(5)]
    w = jnp.concatenate([w_in[:, c] for c in order], axis=1).astype(BF16)
    n_gate = 3 * N_HEADS
    w_gate = jnp.pad(w_in[:, HD + 6 * kvw:], ((0, 0), (0, LANES - n_gate))).astype(BF16)
    head_gain = jnp.concatenate([jnp.tile(g_q * scale, N_HEADS), jnp.tile(g_k, 3 * G)])

    inv = ROPE_THETA ** (-jnp.arange(0, dh, 2, dtype=F32) / dh)
    ang = jnp.arange(seq, dtype=F32)[:, None] * inv[None, :]
    cos, sin = jnp.cos(ang), jnp.sin(ang)
    cos_t = jnp.tile(cos, (1, PROJ_CHUNK // (dh // 2)))
    sin_t = jnp.tile(jnp.concatenate([-sin, sin], axis=1), (1, PROJ_CHUNK // dh))

    qkv, gate = _proj(x, g, w, seq=seq, head_gain=head_gain, rope=(cos_t, sin_t), w_aux=w_gate)

    def blocks16(t):
        half = NSA_CMP_LEN // 2
        t = t.reshape(batch, seq // half, half, G, dh).transpose(0, 3, 1, 2, 4)
        return t.reshape(batch, G, seq // half, half * dh)

    kc0, vc0 = HD, HD + 3 * kvw
    kcmp, vcmp = _nsa_compress(blocks16(qkv[:, kc0:kc0 + kvw]), blocks16(qkv[:, vc0:vc0 + kvw]),
                               pos_k, pos_v, w_cmp_k, w_cmp_v, g_k)
    o_c, selbias = _nsa_cmp_attn(qkv, kcmp, vcmp, batch=batch, seq=seq)
    o_s = _nsa_sel_attn(qkv, qkv, selbias, batch=batch, seq=seq,
                        k_col=(HD + kvw) // LANES, v_col=(HD + 4 * kvw) // LANES)
    o_w = _nsa_win_attn(qkv, qkv, batch=batch, seq=seq,
                        k_col=(HD + 2 * kvw) // LANES, v_col=(HD + 5 * kvw) // LANES)
    r = np.arange(LANES)[:, None]
    c = np.arange(HD)[None, :] // dh
    expand = np.stack([(r == 3 * c + b) for b in range(3)]).astype(np.float32)
    return _outproj(x, w_out.astype(BF16), mode="gated3",
                    acts=[o_c, o_s, o_w, gate, jnp.asarray(expand, BF16)])


def kernel(x, norm_g, ffn1_w_gate, ffn1_w_up, ffn1_w_down, ffn2_w_gate, ffn2_w_up, ffn2_w_down, fox_w_in, fox_b_f, fox_g_q, fox_g_k, fox_w_out, nsa_w_in, nsa_cmp_pos_k, nsa_cmp_pos_v, nsa_w_cmp_k, nsa_w_cmp_v, nsa_g_q, nsa_g_k, nsa_w_out, gla_w_in, gla_w_gate_up, gla_b_gate, gla_g_out, gla_w_out, sb_w_in, sb_w_out):
    B, S, D = x.shape
    depth = norm_g.shape[0]
    n_mixers = 4
    x2 = x.reshape(B * S, D)
    for i in range(depth):
        m, j = i % n_mixers, i // n_mixers
        x2 = _ffn(x2, norm_g[i, 0], ffn1_w_gate[i].astype(BF16), ffn1_w_up[i].astype(BF16),
                  ffn1_w_down[i].astype(BF16))
        g = norm_g[i, 1]
        if m == 0:
            x2 = _fox_mixer(x2, g, fox_w_in[j], fox_b_f[j], fox_g_q[j], fox_g_k[j], fox_w_out[j], batch=B, seq=S)
        elif m == 1:
            x2 = _nsa_mixer(x2, g, nsa_w_in[j], nsa_cmp_pos_k[j], nsa_cmp_pos_v[j], nsa_w_cmp_k[j],
                            nsa_w_cmp_v[j], nsa_g_q[j], nsa_g_k[j], nsa_w_out[j], batch=B, seq=S)
        elif m == 2:
            x2 = _gla_mixer(x2, g, gla_w_in[j], gla_w_gate_up[j], gla_b_gate[j], gla_g_out[j], gla_w_out[j],
                            batch=B, seq=S)
        else:
            x2 = _sb_mixer(x2, g, sb_w_in[j], sb_w_out[j], batch=B, seq=S)
        x2 = _ffn(x2, norm_g[i, 2], ffn2_w_gate[i].astype(BF16), ffn2_w_up[i].astype(BF16),
                  ffn2_w_down[i].astype(BF16))
    return x2.reshape(B, S, D)
```

```python
import functools

import numpy as np
import jax
import jax.numpy as jnp
from jax import lax
from jax.experimental import pallas as pl
from jax.experimental.pallas import tpu as pltpu

F32 = jnp.float32
BF16 = jnp.bfloat16

N_HEADS = 16
HEAD_DIM = 64
ROPE_THETA = 10000.0
RMS_EPS = 1e-6
NEG_INF = -1e30
NSA_KV_HEADS = 4
NSA_CMP_LEN = 32
NSA_CMP_STRIDE = 16
NSA_SEL_LEN = 64
NSA_N_SEL = 16
NSA_WINDOW = 512
NSA_FORCED_SCORE = 1e9
GLA_HEADS = 4
GLA_GATE_RANK = 16
GLA_TAU = 16.0
GLA_CHUNK = 64

LANES = 128
V7X_VMEM_BYTES = 64 * 1024 * 1024
VMEM_LIMIT = V7X_VMEM_BYTES - 8 * 1024 * 1024

_NT = (((1,), (1,)), ((), ()))
_TN = (((0,), (0,)), ((), ()))


def _cparams(*sem):
    return pltpu.CompilerParams(dimension_semantics=sem, vmem_limit_bytes=VMEM_LIMIT)


def _rms_rows(x, g):
    return x * lax.rsqrt(jnp.mean(x * x, axis=-1, keepdims=True) + RMS_EPS) * g


def _softplus(z):
    return jnp.maximum(z, 0.0) + jnp.log(1.0 + jnp.exp(-jnp.abs(z)))


def _neg_abs(z):
    bits = lax.bitcast_convert_type(z, jnp.int32) | jnp.int32(-2 ** 31)
    return lax.bitcast_convert_type(bits, F32)


def _sigmoid(z):
    return 1.0 / (1.0 + jnp.exp(-z))


def _pick_tile(n, target):
    t = min(n, target)
    while n % t:
        t //= 2
    return t


def _ffn_kernel(x_ref, g_ref, wg_ref, wu_ref, wd_ref, o_ref, a_ref, *, tf):
    x = x_ref[...]
    h = _rms_rows(x, g_ref[...]).astype(BF16)
    for c in range(a_ref.shape[1] // tf):
        cols = slice(c * tf, (c + 1) * tf)
        gate = jnp.dot(h, wg_ref[:, cols], preferred_element_type=F32)
        up = jnp.dot(h, wu_ref[:, cols], preferred_element_type=F32)
        a_ref[:, cols] = (gate * _sigmoid(gate) * up).astype(BF16)
    o_ref[...] = x + 0.5 * jnp.dot(a_ref[...], wd_ref[...], preferred_element_type=F32)


def _resident(shape):
    return pl.BlockSpec(shape, lambda *_: (0,) * len(shape), pipeline_mode=pl.Buffered(1))


def _ffn(x, g, wg, wu, wd):
    T, D = x.shape
    F = wg.shape[1]
    tm = _pick_tile(T, 1024)
    tf = 256 if F % 256 == 0 else F
    return pl.pallas_call(
        functools.partial(_ffn_kernel, tf=tf),
        grid=(T // tm,),
        in_specs=[
            pl.BlockSpec((tm, D), lambda i: (i, 0)),
            _resident((1, D)),
            _resident((D, F)),
            _resident((D, F)),
            _resident((F, D)),
        ],
        out_specs=pl.BlockSpec((tm, D), lambda i: (i, 0)),
        out_shape=jax.ShapeDtypeStruct((T, D), F32),
        scratch_shapes=[pltpu.VMEM((tm, F), BF16)],
        compiler_params=_cparams("parallel"),
        name="ffn",
    )(x, g.reshape(1, D), wg, wu, wd)


PROJ_CHUNK = 256


def _proj_kernel(*refs, n_norm, rope, has_aux):
    it = iter(refs)
    x_ref, g_ref, w_ref = next(it), next(it), next(it)
    hg_ref, gsum_ref = (next(it), next(it)) if n_norm else (None, None)
    cos_ref, sin_ref = (next(it), next(it)) if rope else (None, None)
    waux_ref = next(it) if has_aux else None
    o_ref = next(it)
    ch = PROJ_CHUNK
    h = _rms_rows(x_ref[...], g_ref[...]).astype(BF16)
    for c in range(o_ref.shape[1] // ch):
        cols = slice(c * ch, (c + 1) * ch)
        y = jnp.dot(h, w_ref[:, cols], preferred_element_type=F32)
        if c * ch < n_norm:
            ss = jnp.dot((y * y).astype(BF16), gsum_ref[...], preferred_element_type=F32)
            y = y * lax.rsqrt(ss * (1.0 / HEAD_DIM) + RMS_EPS) * hg_ref[:, cols]
            if rope:
                lane = lax.broadcasted_iota(jnp.int32, (1, ch), 1)
                first_half = (lane % HEAD_DIM) < (HEAD_DIM // 2)
                partner = jnp.where(first_half,
                                    pltpu.roll(y, ch - HEAD_DIM // 2, 1),
                                    pltpu.roll(y, HEAD_DIM // 2, 1))
                y = y * cos_ref[...] + partner * sin_ref[...]
        o_ref[:, cols] = y.astype(o_ref.dtype)
    if has_aux:
        next(it)[...] = jnp.dot(h, waux_ref[...], preferred_element_type=F32)


def _proj(x, g, w, *, seq, head_gain=None, rope=None, w_aux=None):
    T, D = x.shape
    N = w.shape[1]
    tm = _pick_tile(seq, 1024)
    ch = PROJ_CHUNK
    n_norm = 0 if head_gain is None else head_gain.shape[0]
    assert N % ch == 0 and n_norm % ch == 0 and T % tm == 0
    args = [x, g.reshape(1, D), w]
    in_specs = [pl.BlockSpec((tm, D), lambda i: (i, 0)), _resident((1, D)), _resident((D, N))]
    if n_norm:
        gidx = np.arange(ch) // HEAD_DIM
        gsum = jnp.asarray((gidx[:, None] == gidx[None, :]).astype(np.float32), BF16)
        args += [head_gain.reshape(1, n_norm), gsum]
        in_specs += [_resident((1, n_norm)), _resident((ch, ch))]
    if rope is not None:
        nblk = seq // tm
        args += list(rope)
        in_specs += [pl.BlockSpec((tm, ch), lambda i: (i % nblk, 0))] * 2
    out_specs = [pl.BlockSpec((tm, N), lambda i: (i, 0))]
    out_shape = [jax.ShapeDtypeStruct((T, N), BF16)]
    if w_aux is not None:
        args.append(w_aux)
        in_specs.append(_resident(w_aux.shape))
        out_specs.append(pl.BlockSpec((tm, w_aux.shape[1]), lambda i: (i, 0)))
        out_shape.append(jax.ShapeDtypeStruct((T, w_aux.shape[1]), F32))
    out = pl.pallas_call(
        functools.partial(_proj_kernel, n_norm=n_norm, rope=rope is not None, has_aux=w_aux is not None),
        grid=(T // tm,),
        in_specs=in_specs,
        out_specs=out_specs,
        out_shape=out_shape,
        compiler_params=_cparams("parallel"),
        name="proj",
    )(*args)
    return out if w_aux is not None else out[0]


def _outproj_kernel(*refs, mode):
    if mode == "plain":
        x_ref, a_ref, w_ref, o_ref = refs
        a = a_ref[...]
    elif mode == "sigmoid_gate":
        x_ref, a_ref, og_ref, w_ref, o_ref = refs
        a = (a_ref[...].astype(F32) * _sigmoid(og_ref[...].astype(F32))).astype(BF16)
    else:
        x_ref, oc_ref, os_ref, ow_ref, gate_ref, exp_ref, w_ref, o_ref = refs
        gs = _sigmoid(gate_ref[...]).astype(BF16)
        a = None
        for c, o_c in enumerate((oc_ref, os_ref, ow_ref)):
            ge = jnp.dot(gs, exp_ref[c], preferred_element_type=F32)
            t = ge * o_c[...].astype(F32)
            a = t if a is None else a + t
        a = a.astype(BF16)
    o_ref[...] = x_ref[...] + jnp.dot(a, w_ref[...], preferred_element_type=F32)


def _outproj(x, w, *, mode, acts):
    T, D = x.shape
    K = w.shape[0]
    tm = _pick_tile(T, 512)
    row = lambda i: (i, 0)
    args = [x]
    in_specs = [pl.BlockSpec((tm, D), row)]
    for a in acts:
        if a.ndim == 3:
            args.append(a)
            in_specs.append(pl.BlockSpec(a.shape, lambda i: (0, 0, 0)))
        else:
            args.append(a)
            in_specs.append(pl.BlockSpec((tm, a.shape[1]), row))
    args.append(w)
    in_specs.append(pl.BlockSpec((K, D), lambda i: (0, 0)))
    return pl.pallas_call(
        functools.partial(_outproj_kernel, mode=mode),
        grid=(T // tm,),
        in_specs=in_specs,
        out_specs=pl.BlockSpec((tm, D), row),
        out_shape=jax.ShapeDtypeStruct((T, D), F32),
        compiler_params=_cparams("parallel"),
        name="outproj_" + mode,
    )(*args)


def _fox_c_kernel(x_ref, g_ref, wf_ref, bf_ref, tri_ref, c_ref, carry_ref):
    @pl.when(pl.program_id(1) == 0)
    def _():
        carry_ref[...] = jnp.zeros_like(carry_ref)

    h = _rms_rows(x_ref[...], g_ref[...])
    f = lax.dot_general(wf_ref[...], h, _NT, precision=lax.Precision.HIGHEST,
                        preferred_element_type=F32) + bf_ref[...]
    ls = jnp.minimum(f, 0.0) - jnp.log(1.0 + jnp.exp(-jnp.abs(f)))
    cs = jnp.dot(ls, tri_ref[...], precision=lax.Precision.HIGHEST,
                 preferred_element_type=F32) + carry_ref[...]
    c_ref[...] = cs
    carry_ref[...] = cs[:, cs.shape[1] - 1:]


def _fox_c(x, g, wf_t, b_f, *, batch, seq):
    T, D = x.shape
    H = wf_t.shape[0]
    tm = _pick_tile(seq, 512)
    ns = seq // tm
    tri = jnp.asarray(np.triu(np.ones((tm, tm), np.float32)))
    return pl.pallas_call(
        _fox_c_kernel,
        grid=(batch, ns),
        in_specs=[
            pl.BlockSpec((tm, D), lambda b, s: (b * ns + s, 0)),
            pl.BlockSpec((1, D), lambda b, s: (0, 0)),
            pl.BlockSpec((H, D), lambda b, s: (0, 0)),
            pl.BlockSpec((H, 1), lambda b, s: (0, 0)),
            pl.BlockSpec((tm, tm), lambda b, s: (0, 0)),
        ],
        out_specs=pl.BlockSpec((None, H, tm), lambda b, s: (b, 0, s)),
        out_shape=jax.ShapeDtypeStruct((batch, H, seq), F32),
        scratch_shapes=[pltpu.VMEM((H, 1), F32)],
        compiler_params=_cparams("parallel", "arbitrary"),
        name="fox_c",
    )(x, g.reshape(1, D), wf_t, b_f.reshape(H, 1), tri)


def _fox_attn_kernel(q_ref, k_ref, v_ref, c_ref, o_ref, vaug_ref, sa_ref, sb_ref, m_ref, acc_ref, *, t):
    qi = pl.program_id(2)
    lane = lax.broadcasted_iota(jnp.int32, (1, LANES), 1)
    upper = lane >= HEAD_DIM

    @pl.when(qi == 0)
    def _():
        v = v_ref[...]
        one = jnp.ones_like(v)
        vaug_ref[0] = jnp.where(upper, one, v)
        vaug_ref[1] = jnp.where(upper, v, one)

    q0 = pl.multiple_of(qi * t, t)
    qslab = q_ref[...]
    zero = jnp.zeros_like(qslab)
    qpos = lax.broadcasted_iota(jnp.int32, (t, t), 0)
    kpos = lax.broadcasted_iota(jnp.int32, (t, t), 1)
    causal = kpos <= qpos

    heads = range(2)
    qh = (jnp.where(upper, zero, qslab), jnp.where(upper, qslab, zero))
    c_q0 = [c_ref[i, :, pl.ds(q0, LANES)][:, 0:1] for i in heads]

    def produce(s_ref, j):
        keys = pl.ds(pl.multiple_of(j * t, t), t)
        kt = k_ref[keys, :]
        for i in heads:
            s_ref[i] = (lax.dot_general(qh[i], kt, _NT, preferred_element_type=F32)
                        + (c_q0[i] - c_ref[i, :, keys]))

    def consume(s_ref, j, masked):
        keys = pl.ds(pl.multiple_of(j * t, t), t)
        score = (lambda i: jnp.where(causal, s_ref[i], NEG_INF)) if masked else (lambda i: s_ref[i])
        m_old = [m_ref[i] for i in heads]
        m_new = [jnp.maximum(m_old[i], jnp.max(score(i), axis=1, keepdims=True)) for i in heads]
        p = [jnp.exp(score(i) - jnp.tile(m_new[i], (1, t // LANES))).astype(BF16) for i in heads]
        for i in heads:
            m_ref[i] = m_new[i]
            acc_ref[i] = (jnp.exp(m_old[i] - m_new[i]) * acc_ref[i]
                          + jnp.dot(p[i], vaug_ref[i, keys, :], preferred_element_type=F32))

    m_ref[...] = jnp.full(m_ref.shape, NEG_INF, F32)
    acc_ref[...] = jnp.zeros(acc_ref.shape, F32)
    produce(sa_ref, 0)

    def body(jj, _):
        produce(sb_ref, 2 * jj + 1)
        consume(sa_ref, 2 * jj, False)
        produce(sa_ref, 2 * jj + 2)
        consume(sb_ref, 2 * jj + 1, False)
        return 0

    lax.fori_loop(0, qi // 2, body, 0)

    @pl.when(qi % 2 == 0)
    def _():
        consume(sa_ref, qi, True)

    @pl.when(qi % 2 == 1)
    def _():
        produce(sb_ref, qi)
        consume(sa_ref, qi - 1, False)
        consume(sb_ref, qi, True)

    accs = (acc_ref[0], acc_ref[1])
    num = jnp.where(upper, accs[1], accs[0])
    den = jnp.where(upper, accs[0], accs[1])
    den = pltpu.roll(den, HEAD_DIM, 1)
    o_ref[...] = (num / den).astype(o_ref.dtype)


def _fox_attn(qkvg, c, *, batch, seq):
    T = qkvg.shape[0]
    HD = N_HEADS * HEAD_DIM
    t = _pick_tile(seq, 512)
    nq = seq // t
    npair = HD // LANES
    return pl.pallas_call(
        functools.partial(_fox_attn_kernel, t=t),
        grid=(batch, npair, nq),
        in_specs=[
            pl.BlockSpec((t, LANES), lambda b, h, q: (b * nq + q, h)),
            pl.BlockSpec((seq, LANES), lambda b, h, q: (b, npair + h)),
            pl.BlockSpec((seq, LANES), lambda b, h, q: (b, 2 * npair + h)),
            pl.BlockSpec((None, 2, 1, seq), lambda b, h, q: (b, h, 0, 0)),
        ],
        out_specs=pl.BlockSpec((t, LANES), lambda b, h, q: (b * nq + q, h)),
        out_shape=jax.ShapeDtypeStruct((T, HD), BF16),
        scratch_shapes=[pltpu.VMEM((2, seq, LANES), BF16),
                        pltpu.VMEM((2, t, t), F32), pltpu.VMEM((2, t, t), F32),
                        pltpu.VMEM((2, t, LANES), F32), pltpu.VMEM((2, t, LANES), F32)],
        compiler_params=_cparams("parallel", "arbitrary", "arbitrary"),
        name="fox_attn",
    )(qkvg, qkvg, qkvg, c.reshape(batch, N_HEADS, 1, seq))


def _fox_mixer(x, g, w_in, b_f, g_q, g_k, w_out, *, batch, seq):
    HD = N_HEADS * HEAD_DIM
    scale = HEAD_DIM ** -0.5
    w = jnp.concatenate([w_in[:, :3 * HD], w_in[:, 3 * HD + N_HEADS:]], axis=1).astype(BF16)
    wf_t = w_in[:, 3 * HD:3 * HD + N_HEADS].T
    head_gain = jnp.concatenate([jnp.tile(g_q * scale, N_HEADS), jnp.tile(g_k, N_HEADS)])
    qkvg = _proj(x, g, w, seq=seq, head_gain=head_gain)
    c = _fox_c(x, g, wf_t, b_f, batch=batch, seq=seq)
    o = _fox_attn(qkvg, c, batch=batch, seq=seq)
    return _outproj_fox(x, o, qkvg, w_out.astype(BF16))


def _outproj_fox(x, o, qkvg, w):
    T, D = x.shape
    HD = o.shape[1]
    tm = _pick_tile(T, 512)
    return pl.pallas_call(
        functools.partial(_outproj_kernel, mode="sigmoid_gate"),
        grid=(T // tm,),
        in_specs=[
            pl.BlockSpec((tm, D), lambda i: (i, 0)),
            pl.BlockSpec((tm, HD), lambda i: (i, 0)),
            pl.BlockSpec((tm, HD), lambda i: (i, 3)),
            pl.BlockSpec((HD, D), lambda i: (0, 0)),
        ],
        out_specs=pl.BlockSpec((tm, D), lambda i: (i, 0)),
        out_shape=jax.ShapeDtypeStruct((T, D), F32),
        compiler_params=_cparams("parallel"),
        name="outproj_fox",
    )(x, o, qkvg, w)


def _sb_attn_kernel(q_ref, k_ref, v_ref, tri_ref, o_ref, wa_ref, wb_ref, ta_ref, tb_ref, rest_ref, acc_ref,
                    *, tq, tk):
    qi = pl.program_id(2)
    lane = lax.broadcasted_iota(jnp.int32, (1, LANES), 1)
    upper = lane >= HEAD_DIM
    q0 = pl.multiple_of(qi * tq, tq)
    qslab = q_ref[...]
    zero = jnp.zeros_like(qslab)
    qh = (jnp.where(upper, zero, qslab), jnp.where(upper, qslab, zero))
    qpos = lax.broadcasted_iota(jnp.int32, (tq, tk), 0)
    kpos = lax.broadcasted_iota(jnp.int32, (tq, tk), 1)

    heads = range(2)

    def produce(w_ref, tot_ref, j, diag_offset):
        kt = k_ref[pl.ds(pl.multiple_of(j * tk, tk), tk), :]
        z = [lax.dot_general(qh[i], kt, _NT, preferred_element_type=F32) for i in heads]
        sp = [jnp.maximum(z[i], 0.0) + jnp.log2(1.0 + jnp.exp2(_neg_abs(z[i]))) for i in heads]
        if diag_offset is not None:
            strict = kpos + diag_offset < qpos
            spm = [jnp.where(strict, sp[i], 0.0) for i in heads]
        else:
            spm = sp
        cum = [jnp.dot(spm[i].astype(BF16), tri_ref[...], preferred_element_type=F32) for i in heads]
        w = [(z[i] - sp[i]) - cum[i] for i in heads]
        if diag_offset is not None:
            w = [jnp.where(strict, w[i], NEG_INF) for i in heads]
        for i in heads:
            w_ref[i] = w[i]
            tot_ref[i] = jnp.broadcast_to(cum[i][:, 0:1] + spm[i][:, 0:1], (tq, LANES))

    def consume(w_ref, tot_ref, j):
        vt = v_ref[pl.ds(pl.multiple_of(j * tk, tk), tk), :]
        a = [jnp.exp2(w_ref[i] - jnp.tile(rest_ref[i], (1, tk // LANES))).astype(BF16) for i in heads]
        for i in heads:
            acc_ref[i] += jnp.dot(a[i], vt, preferred_element_type=F32)
            rest_ref[i] += tot_ref[i]

    assert tq == 2 * tk
    last = 2 * qi + 1
    buf_a, buf_b = (wa_ref, ta_ref), (wb_ref, tb_ref)
    rest_ref[...] = jnp.zeros(rest_ref.shape, F32)
    acc_ref[...] = jnp.zeros(acc_ref.shape, F32)
    produce(*buf_a, last, tk)
    produce(*buf_b, last - 1, 0)
    consume(*buf_a, last)

    def body(m, _):
        produce(*buf_a, last - 2 * m, None)
        consume(*buf_b, last - 2 * m + 1)
        produce(*buf_b, last - 2 * m - 1, None)
        consume(*buf_a, last - 2 * m)
        return 0

    lax.fori_loop(1, qi + 1, body, 0)
    consume(*buf_b, 0)
    o_ref[...] = jnp.where(upper, acc_ref[1], acc_ref[0]).astype(o_ref.dtype)


def _sb_attn(qkv, *, batch, seq):
    T = qkv.shape[0]
    HD = N_HEADS * HEAD_DIM
    tq = _pick_tile(seq, 512)
    tk = 256
    nq = seq // tq
    npair = HD // LANES
    tri = jnp.asarray(np.tril(np.ones((tk, tk), np.float32), -1), BF16)
    return pl.pallas_call(
        functools.partial(_sb_attn_kernel, tq=tq, tk=tk),
        grid=(batch, npair, nq),
        in_specs=[
            pl.BlockSpec((tq, LANES), lambda b, h, q: (b * nq + q, h)),
            pl.BlockSpec((seq, LANES), lambda b, h, q: (b, npair + h)),
            pl.BlockSpec((seq, LANES), lambda b, h, q: (b, 2 * npair + h)),
            pl.BlockSpec((tk, tk), lambda b, h, q: (0, 0)),
        ],
        out_specs=pl.BlockSpec((tq, LANES), lambda b, h, q: (b * nq + q, h)),
        out_shape=jax.ShapeDtypeStruct((T, HD), BF16),
        scratch_shapes=[pltpu.VMEM((2, tq, tk), F32), pltpu.VMEM((2, tq, tk), F32),
                        pltpu.VMEM((2, tq, LANES), F32), pltpu.VMEM((2, tq, LANES), F32),
                        pltpu.VMEM((2, tq, LANES), F32), pltpu.VMEM((2, tq, LANES), F32)],
        compiler_params=_cparams("parallel", "parallel", "arbitrary"),
        name="sb_attn",
    )(qkv, qkv, qkv, tri)


def _sb_mixer(x, g, w_in, w_out, *, batch, seq):
    HD = N_HEADS * HEAD_DIM
    scale = HEAD_DIM ** -0.5 * float(np.log2(np.e))
    w = jnp.concatenate([w_in[:, :HD] * scale, w_in[:, HD:]], axis=1).astype(BF16)
    qkv = _proj(x, g, w, seq=seq)
    o = _sb_attn(qkv, batch=batch, seq=seq)
    return _outproj(x, w_out.astype(BF16), mode="plain", acts=[o])


def _gla_kernel(q_ref, k_ref, v_ref, r_ref, glow_ref, wgu_ref, bg_ref, gout_ref, tri_ref,
                o_ref, state_ref, *, tm, dk):
    C = GLA_CHUNK

    @pl.when(pl.program_id(2) == 0)
    def _():
        state_ref[...] = jnp.zeros_like(state_ref)

    gate = jnp.dot(glow_ref[...], wgu_ref[...], precision=lax.Precision.HIGHEST,
                   preferred_element_type=F32) + bg_ref[...]
    log_a = (jnp.minimum(gate, 0.0) - jnp.log(1.0 + jnp.exp(-jnp.abs(gate)))) * (1.0 / GLA_TAU)
    row = lax.broadcasted_iota(jnp.int32, (C, C), 0)
    col = lax.broadcasted_iota(jnp.int32, (C, C), 1)
    causal = col <= row
    q_scale = dk ** -0.5
    chunks = [slice(c * C, (c + 1) * C) for c in range(tm // C)]

    tri = tri_ref[...]
    b = [jnp.dot(tri, log_a[sl], precision=lax.Precision.HIGHEST, preferred_element_type=F32)
         for sl in chunks]
    b_last = jnp.concatenate([jnp.broadcast_to(bc[C - 1:C], bc.shape) for bc in b], axis=0)
    b = jnp.concatenate(b, axis=0)
    q = q_ref[...].astype(F32) * q_scale
    k = k_ref[...].astype(F32)
    q_dec = (q * jnp.exp(b)).astype(BF16)
    k_neg = (k * jnp.exp(-b)).astype(BF16)
    k_rem = (k * jnp.exp(b_last - b)).astype(BF16)
    decay = jnp.exp(b_last)
    attn = [lax.dot_general(q_dec[sl], k_neg[sl], _NT, preferred_element_type=F32) for sl in chunks]
    attn = [jnp.where(causal, a, 0.0).astype(BF16) for a in attn]
    o_intra = [jnp.dot(attn[c], v_ref[sl, :], preferred_element_type=F32) for c, sl in enumerate(chunks)]
    u_t = [lax.dot_general(v_ref[sl, :], k_rem[sl], _TN, preferred_element_type=F32) for sl in chunks]

    state_t = state_ref[...]
    outs = []
    for c, sl in enumerate(chunks):
        outs.append(o_intra[c] + lax.dot_general(q_dec[sl], state_t.astype(BF16), _NT,
                                                 preferred_element_type=F32))
        state_t = state_t * decay[c * C:c * C + 1] + u_t[c]
    state_ref[...] = state_t

    o = jnp.concatenate(outs, axis=0)
    y = o * lax.rsqrt(jnp.mean(o * o, axis=-1, keepdims=True) + RMS_EPS) * gout_ref[...]
    r = r_ref[...].astype(F32)
    o_ref[...] = (y * (r * _sigmoid(r))).astype(o_ref.dtype)


def _gla(qkvr, glow, wgu, b_gate, g_out, *, batch, seq):
    T = qkvr.shape[0]
    Hg = GLA_HEADS
    dk_total = wgu.shape[1]
    dk = dk_total // Hg
    dv = (qkvr.shape[1] - 2 * dk_total) // 2 // Hg
    tm = _pick_tile(seq, 512)
    ns = seq // tm
    kq, kv = dk_total // dk, dk_total // dv
    tri = jnp.asarray(np.tril(np.ones((GLA_CHUNK, GLA_CHUNK), np.float32)))
    rowmap = lambda off: (lambda b, h, s: (b * ns + s, off + h))
    return pl.pallas_call(
        functools.partial(_gla_kernel, tm=tm, dk=dk),
        grid=(batch, Hg, ns),
        in_specs=[
            pl.BlockSpec((tm, dk), rowmap(0)),
            pl.BlockSpec((tm, dk), rowmap(kq)),
            pl.BlockSpec((tm, dv), rowmap(2 * kv)),
            pl.BlockSpec((tm, dv), rowmap(2 * kv + Hg)),
            pl.BlockSpec((tm, LANES), lambda b, h, s: (b * ns + s, 0)),
            pl.BlockSpec((LANES, dk), lambda b, h, s: (0, h)),
            pl.BlockSpec((1, dk), lambda b, h, s: (0, h)),
            pl.BlockSpec((1, dv), lambda b, h, s: (0, 0)),
            pl.BlockSpec((GLA_CHUNK, GLA_CHUNK), lambda b, h, s: (0, 0)),
        ],
        out_specs=pl.BlockSpec((tm, dv), rowmap(0)),
        out_shape=jax.ShapeDtypeStruct((T, Hg * dv), BF16),
        scratch_shapes=[pltpu.VMEM((dv, dk), F32)],
        compiler_params=_cparams("parallel", "parallel", "arbitrary"),
        name="gla",
    )(qkvr, qkvr, qkvr, qkvr, glow, wgu, b_gate.reshape(1, dk_total), g_out.reshape(1, dv), tri)


def _gla_mixer(x, g, w_in, w_gate_up, b_gate, g_out, w_out, *, batch, seq):
    dk_total = w_gate_up.shape[1]
    dv_total = w_out.shape[0]
    lo = 2 * dk_total + dv_total
    w_main = jnp.concatenate([w_in[:, :lo], w_in[:, lo + GLA_GATE_RANK:]], axis=1).astype(BF16)
    w_low = jnp.pad(w_in[:, lo:lo + GLA_GATE_RANK], ((0, 0), (0, LANES - GLA_GATE_RANK))).astype(BF16)
    wgu = jnp.pad(w_gate_up, ((0, LANES - GLA_GATE_RANK), (0, 0)))
    qkvr, glow = _proj(x, g, w_main, seq=seq, w_aux=w_low)
    o = _gla(qkvr, glow, wgu, b_gate, g_out, batch=batch, seq=seq)
    return _outproj(x, w_out.astype(BF16), mode="plain", acts=[o])


def _nsa_compress_kernel(kc_ref, vc_ref, pk_ref, pv_ref, wk_ref, wv_ref, gk_ref, ko_ref, vo_ref, *, n16):
    def compress(r_ref, pos_ref, w_ref):
        r = r_ref[...].astype(F32)
        first = jnp.dot((r + pos_ref[0:1, :]).astype(BF16), w_ref[0], preferred_element_type=F32)
        second = jnp.dot((r + pos_ref[1:2, :]).astype(BF16), w_ref[1], preferred_element_type=F32)
        return first + pltpu.roll(second, n16 - 1, 0)

    kc = compress(kc_ref, pk_ref, wk_ref)
    ko_ref[...] = _rms_rows(kc, gk_ref[...]).astype(ko_ref.dtype)
    vo_ref[...] = compress(vc_ref, pv_ref, wv_ref).astype(vo_ref.dtype)


def _nsa_compress(kc_r, vc_r, pos_k, pos_v, w_k, w_v, g_k):
    B, G, n16, width = kc_r.shape
    dh = HEAD_DIM
    half = NSA_CMP_LEN // 2
    dup = lambda w: jnp.concatenate([w, w], axis=-1).reshape(2, half * dh, 2 * dh).astype(BF16)
    spec_r = pl.BlockSpec((None, None, n16, width), lambda b, g: (b, g, 0, 0))
    spec_o = pl.BlockSpec((None, None, n16, 2 * dh), lambda b, g: (b, g, 0, 0))
    const2 = lambda shape: pl.BlockSpec(shape, lambda b, g: (0,) * len(shape))
    out = jax.ShapeDtypeStruct((B, G, n16, 2 * dh), BF16)
    return pl.pallas_call(
        functools.partial(_nsa_compress_kernel, n16=n16),
        grid=(B, G),
        in_specs=[spec_r, spec_r, const2((2, width)), const2((2, width)),
                  const2((2, width, 2 * dh)), const2((2, width, 2 * dh)), const2((1, 2 * dh))],
        out_specs=[spec_o, spec_o],
        out_shape=[out, out],
        compiler_params=_cparams("parallel", "parallel"),
        name="nsa_compress",
    )(kc_r, vc_r, pos_k.reshape(2, width), pos_v.reshape(2, width), dup(w_k), dup(w_v),
      jnp.tile(g_k, 2).reshape(1, 2 * dh))


def _stack_heads(q_ref, extra=None, rows_in=slice(None)):
    lane = lax.broadcasted_iota(jnp.int32, (1, LANES), 1)
    upper = lane >= HEAD_DIM
    rows = []
    for j in range(2):
        slab = q_ref[rows_in, j * LANES:(j + 1) * LANES]
        zero = jnp.zeros_like(slab)
        for i in range(2):
            qh = jnp.where(upper, slab, zero) if i else jnp.where(upper, zero, slab)
            rows.append(qh if extra is None else jnp.concatenate([qh, extra], axis=1))
    return jnp.concatenate(rows, axis=0)


def _unstack_heads(o, tq):
    lane = lax.broadcasted_iota(jnp.int32, (1, LANES), 1)
    upper = lane >= HEAD_DIM
    slabs = []
    for j in range(2):
        lo = o[(2 * j) * tq:(2 * j + 1) * tq]
        hi = pltpu.roll(o[(2 * j + 1) * tq:(2 * j + 2) * tq], HEAD_DIM, 1)
        slabs.append(jnp.where(upper, hi, lo))
    return jnp.concatenate(slabs, axis=1)


def _group_lower(slab_ref, g):
    x = slab_ref[...].astype(F32)
    return jnp.where(g % 2 == 1, pltpu.roll(x, HEAD_DIM, 1), x)


def _both_halves(x):
    lane = lax.broadcasted_iota(jnp.int32, (1, LANES), 1)
    return jnp.where(lane >= HEAD_DIM, pltpu.roll(x, HEAD_DIM, 1), x)


def _with_ones(x):
    lane = lax.broadcasted_iota(jnp.int32, (1, LANES), 1)
    return jnp.where(lane >= HEAD_DIM, 1.0, x)


def _normalise(acc):
    return acc / pltpu.roll(acc, HEAD_DIM, 1)


def _nsa_cmp_attn_kernel(q_ref, kc_ref, vc_ref, cover_ref, oc_ref, sel_ref, *, tq, n16):
    qi = pl.program_id(2)
    q4 = _stack_heads(q_ref)
    s = lax.dot_general(q4, kc_ref[...], _NT, preferred_element_type=F32)
    qpos = qi * tq + lax.broadcasted_iota(jnp.int32, (tq, n16), 0)
    cmp_end = lax.broadcasted_iota(jnp.int32, (tq, n16), 1) * NSA_CMP_STRIDE + (NSA_CMP_LEN - 1)
    valid = jnp.concatenate([cmp_end <= qpos] * 4, axis=0)
    s = jnp.where(valid, s, NEG_INF)
    e = jnp.exp(s - jnp.max(s, axis=1, keepdims=True))
    p = jnp.where(valid, e / jnp.sum(e, axis=1, keepdims=True), 0.0)
    o = jnp.dot(p.astype(BF16), vc_ref[...], preferred_element_type=F32)
    oc_ref[...] = _unstack_heads(o, tq).astype(oc_ref.dtype)

    psum = p[0:tq] + p[tq:2 * tq] + p[2 * tq:3 * tq] + p[3 * tq:4 * tq]
    cover = cover_ref[...]
    hi = psum.astype(BF16)
    r1 = psum - hi.astype(F32)
    mid = r1.astype(BF16)
    lo = (r1 - mid.astype(F32)).astype(BF16)
    imp = (jnp.dot(hi, cover, preferred_element_type=F32) + jnp.dot(mid, cover, preferred_element_type=F32)
           + jnp.dot(lo, cover, preferred_element_type=F32))

    blk = lax.broadcasted_iota(jnp.int32, (tq, LANES), 1)
    cur = (qi * tq + lax.broadcasted_iota(jnp.int32, (tq, LANES), 0)) // NSA_SEL_LEN
    forced = (blk == 0) | (blk == cur) | (blk == cur - 1)
    vals = jnp.where(forced, NSA_FORCED_SCORE, jnp.where(blk <= cur, imp, -1.0))
    blk_f = blk.astype(F32)
    chosen = jnp.zeros((tq, LANES), jnp.bool_)
    for _ in range(NSA_N_SEL):
        top = jnp.max(vals, axis=1, keepdims=True)
        first = jnp.min(jnp.where(vals == top, blk_f, float(LANES)), axis=1, keepdims=True)
        pick = blk_f == first
        chosen = chosen | pick
        vals = jnp.where(pick, -3e38, vals)
    sel_ref[...] = jnp.where(chosen, 0.0, NEG_INF).astype(sel_ref.dtype)


def _nsa_cmp_attn(qk, kcmp, vcmp, *, batch, seq):
    T = qk.shape[0]
    G = NSA_KV_HEADS
    tq = _pick_tile(seq, 512)
    nq = seq // tq
    n16 = kcmp.shape[2]
    n_sel = seq // NSA_SEL_LEN
    assert n_sel <= LANES and seq % tq == 0
    n = np.arange(n16)[:, None] * NSA_CMP_STRIDE
    j = np.arange(LANES)[None, :] * NSA_SEL_LEN
    cover = (n < j + NSA_SEL_LEN) & (n + NSA_CMP_LEN > j) & (np.arange(LANES)[None, :] < n_sel)
    cover = jnp.asarray(cover.astype(np.float32), BF16)
    spec_c = pl.BlockSpec((None, None, n16, LANES), lambda b, g, q: (b, g, 0, 0))
    return pl.pallas_call(
        functools.partial(_nsa_cmp_attn_kernel, tq=tq, n16=n16),
        grid=(batch, G, nq),
        in_specs=[pl.BlockSpec((tq, 2 * LANES), lambda b, g, q: (b * nq + q, g)), spec_c, spec_c,
                  pl.BlockSpec((n16, LANES), lambda b, g, q: (0, 0))],
        out_specs=[pl.BlockSpec((tq, 2 * LANES), lambda b, g, q: (b * nq + q, g)),
                   pl.BlockSpec((tq, LANES), lambda b, g, q: (b * nq + q, g))],
        out_shape=[jax.ShapeDtypeStruct((T, N_HEADS * HEAD_DIM), BF16),
                   jax.ShapeDtypeStruct((T, G * LANES), BF16)],
        compiler_params=_cparams("parallel", "parallel", "parallel"),
        name="nsa_cmp_attn",
    )(qk, kcmp, vcmp, cover)


def _nsa_sel_attn_kernel(q_ref, k_ref, v_ref, sel_ref, o_ref, kaug_ref, vaug_ref, sa_ref, sb_ref, m_ref, acc_ref,
                         *, tq, sub, tk, seq):
    g = pl.program_id(1)
    qi = pl.program_id(2)

    @pl.when(qi == 0)
    def _():
        kaug_ref[:, 0:LANES] = _both_halves(_group_lower(k_ref, g)).astype(BF16)
        blk_of_key = lax.broadcasted_iota(jnp.int32, (seq, LANES), 0) // NSA_SEL_LEN
        blk = lax.broadcasted_iota(jnp.int32, (seq, LANES), 1)
        kaug_ref[:, LANES:2 * LANES] = jnp.where(blk_of_key == blk, 1.0, 0.0).astype(BF16)
        vaug_ref[...] = _with_ones(_group_lower(v_ref, g)).astype(BF16)

    chains = range(tq // sub)
    rows = [slice(c * sub, (c + 1) * sub) for c in chains]
    q4 = [_stack_heads(q_ref, extra=sel_ref[rows[c], :], rows_in=rows[c]) for c in chains]
    q0 = qi * tq
    n_full = q0 // tk

    def produce(s_ref, j):
        kt = kaug_ref[pl.ds(pl.multiple_of(j * tk, tk), tk), :]
        for c in chains:
            s_ref[c] = lax.dot_general(q4[c], kt, _NT, preferred_element_type=F32)

    def consume(s_ref, j, masked):
        keys = pl.ds(pl.multiple_of(j * tk, tk), tk)
        if masked:
            kpos = j * tk + lax.broadcasted_iota(jnp.int32, (sub, tk), 1)
            qpos = q0 + lax.broadcasted_iota(jnp.int32, (sub, tk), 0)
            ok = [jnp.concatenate([kpos <= qpos + c * sub] * 4, axis=0) for c in chains]
            score = lambda c: jnp.where(ok[c], s_ref[c], NEG_INF)
        else:
            score = lambda c: s_ref[c]
        vt = vaug_ref[keys, :]
        m_old = [m_ref[c] for c in chains]
        m_new = [jnp.maximum(m_old[c], jnp.max(score(c), axis=1, keepdims=True)) for c in chains]
        p = [jnp.exp(score(c) - jnp.tile(m_new[c], (1, tk // LANES))).astype(BF16) for c in chains]
        for c in chains:
            m_ref[c] = m_new[c]
            acc_ref[c] = (jnp.exp(m_old[c] - m_new[c]) * acc_ref[c]
                          + jnp.dot(p[c], vt, preferred_element_type=F32))

    m_ref[...] = jnp.full(m_ref.shape, NEG_INF, F32)
    acc_ref[...] = jnp.zeros(acc_ref.shape, F32)
    produce(sa_ref, 0)

    def body(jj, _):
        produce(sb_ref, 2 * jj + 1)
        consume(sa_ref, 2 * jj, False)
        produce(sa_ref, 2 * jj + 2)
        consume(sb_ref, 2 * jj + 1, False)
        return 0

    lax.fori_loop(0, n_full // 2, body, 0)

    @pl.when(n_full % 2 == 0)
    def _():
        consume(sa_ref, n_full, True)

    @pl.when(n_full % 2 == 1)
    def _():
        produce(sb_ref, n_full)
        consume(sa_ref, n_full - 1, False)
        consume(sb_ref, n_full, True)

    for c in chains:
        o_ref[rows[c], :] = _unstack_heads(_normalise(acc_ref[c]), sub).astype(o_ref.dtype)


def _nsa_sel_attn(qk, vv, selbias, *, batch, seq, k_col, v_col):
    T = qk.shape[0]
    G = NSA_KV_HEADS
    sub = 128
    tq = _pick_tile(seq, 256)
    tk = _pick_tile(seq, 512)
    nq = seq // tq
    assert tk % tq == 0
    return pl.pallas_call(
        functools.partial(_nsa_sel_attn_kernel, tq=tq, sub=sub, tk=tk, seq=seq),
        grid=(batch, G, nq),
        in_specs=[pl.BlockSpec((tq, 2 * LANES), lambda b, g, q: (b * nq + q, g)),
                  pl.BlockSpec((seq, LANES), lambda b, g, q: (b, k_col + g // 2)),
                  pl.BlockSpec((seq, LANES), lambda b, g, q: (b, v_col + g // 2)),
                  pl.BlockSpec((tq, LANES), lambda b, g, q: (b * nq + q, g))],
        out_specs=pl.BlockSpec((tq, 2 * LANES), lambda b, g, q: (b * nq + q, g)),
        out_shape=jax.ShapeDtypeStruct((T, N_HEADS * HEAD_DIM), BF16),
        scratch_shapes=[pltpu.VMEM((seq, 2 * LANES), BF16), pltpu.VMEM((seq, LANES), BF16),
                        pltpu.VMEM((tq // sub, 4 * sub, tk), F32), pltpu.VMEM((tq // sub, 4 * sub, tk), F32),
                        pltpu.VMEM((tq // sub, 4 * sub, LANES), F32),
                        pltpu.VMEM((tq // sub, 4 * sub, LANES), F32)],
        compiler_params=_cparams("parallel", "arbitrary", "arbitrary"),
        name="nsa_sel_attn",
    )(qk, qk, vv, selbias)


def _nsa_win_attn_kernel(q_ref, k_ref, v_ref, o_ref, kb_ref, vaug_ref, *, tq, span):
    g = pl.program_id(1)
    qi = pl.program_id(2)

    @pl.when(qi == 0)
    def _():
        kb_ref[...] = _both_halves(_group_lower(k_ref, g)).astype(BF16)
        vaug_ref[...] = _with_ones(_group_lower(v_ref, g)).astype(BF16)

    q4 = _stack_heads(q_ref)
    q0 = qi * tq
    k_start = pl.multiple_of(jnp.maximum(q0 - NSA_WINDOW, 0), tq)
    keys = pl.ds(k_start, span)
    dist = (q0 - k_start) + (lax.broadcasted_iota(jnp.int32, (tq, span), 0)
                             - lax.broadcasted_iota(jnp.int32, (tq, span), 1))
    ok = (dist >= 0) & (dist < NSA_WINDOW)
    ok2 = jnp.concatenate([ok, ok], axis=0)
    pairs = [q4[0:2 * tq], q4[2 * tq:4 * tq]]
    s = [lax.dot_general(qp, kb_ref[keys, :], _NT, preferred_element_type=F32) for qp in pairs]
    s = [jnp.where(ok2, sc, NEG_INF) for sc in s]
    p = [jnp.exp(sc - jnp.max(sc, axis=1, keepdims=True)).astype(BF16) for sc in s]
    acc = [jnp.dot(pc, vaug_ref[keys, :], preferred_element_type=F32) for pc in p]
    acc = jnp.concatenate(acc, axis=0)
    o_ref[...] = _unstack_heads(_normalise(acc), tq).astype(o_ref.dtype)


def _nsa_win_attn(qk, vv, *, batch, seq, k_col, v_col):
    T = qk.shape[0]
    G = NSA_KV_HEADS
    tq = _pick_tile(seq, 256)
    nq = seq // tq
    span = NSA_WINDOW + tq
    assert NSA_WINDOW % tq == 0 and seq >= span
    return pl.pallas_call(
        functools.partial(_nsa_win_attn_kernel, tq=tq, span=span),
        grid=(batch, G, nq),
        in_specs=[pl.BlockSpec((tq, 2 * LANES), lambda b, g, q: (b * nq + q, g)),
                  pl.BlockSpec((seq, LANES), lambda b, g, q: (b, k_col + g // 2)),
                  pl.BlockSpec((seq, LANES), lambda b, g, q: (b, v_col + g // 2))],
        out_specs=pl.BlockSpec((tq, 2 * LANES), lambda b, g, q: (b * nq + q, g)),
        out_shape=jax.ShapeDtypeStruct((T, N_HEADS * HEAD_DIM), BF16),
        scratch_shapes=[pltpu.VMEM((seq, LANES), BF16), pltpu.VMEM((seq, LANES), BF16)],
        compiler_params=_cparams("parallel", "arbitrary", "arbitrary"),
        name="nsa_win_attn",
    )(qk, qk, vv)


def _nsa_mixer(x, g, w_in, pos_k, pos_v, w_cmp_k, w_cmp_v, g_q, g_k, w_out, *, batch, seq):
    HD = N_HEADS * HEAD_DIM
    G, dh = NSA_KV_HEADS, HEAD_DIM
    kvw = G * dh
    scale = dh ** -0.5
    col = lambda n: slice(HD + n * kvw, HD + (n + 1) * kvw)
    order = [slice(0, HD), col(0), col(2), col(4), col(1), col(3), col(5)]
    w = jnp.concatenate([w_in[:, c] for c in order], axis=1).astype(BF16)
    n_gate = 3 * N_HEADS
    w_gate = jnp.pad(w_in[:, HD + 6 * kvw:], ((0, 0), (0, LANES - n_gate))).astype(BF16)
    head_gain = jnp.concatenate([jnp.tile(g_q * scale, N_HEADS), jnp.tile(g_k, 3 * G)])

    inv = ROPE_THETA ** (-jnp.arange(0, dh, 2, dtype=F32) / dh)
    ang = jnp.arange(seq, dtype=F32)[:, None] * inv[None, :]
    cos, sin = jnp.cos(ang), jnp.sin(ang)
    cos_t = jnp.tile(cos, (1, PROJ_CHUNK // (dh // 2)))
    sin_t = jnp.tile(jnp.concatenate([-sin, sin], axis=1), (1, PROJ_CHUNK // dh))

    qkv, gate = _proj(x, g, w, seq=seq, head_gain=head_gain, rope=(cos_t, sin_t), w_aux=w_gate)

    def blocks16(t):
        half = NSA_CMP_LEN // 2
        t = t.reshape(batch, seq // half, half, G, dh).transpose(0, 3, 1, 2, 4)
        return t.reshape(batch, G, seq // half, half * dh)

    kc0, vc0 = HD, HD + 3 * kvw
    kcmp, vcmp = _nsa_compress(blocks16(qkv[:, kc0:kc0 + kvw]), blocks16(qkv[:, vc0:vc0 + kvw]),
                               pos_k, pos_v, w_cmp_k, w_cmp_v, g_k)
    o_c, selbias = _nsa_cmp_attn(qkv, kcmp, vcmp, batch=batch, seq=seq)
    o_s = _nsa_sel_attn(qkv, qkv, selbias, batch=batch, seq=seq,
                        k_col=(HD + kvw) // LANES, v_col=(HD + 4 * kvw) // LANES)
    o_w = _nsa_win_attn(qkv, qkv, batch=batch, seq=seq,
                        k_col=(HD + 2 * kvw) // LANES, v_col=(HD + 5 * kvw) // LANES)
    r = np.arange(LANES)[:, None]
    c = np.arange(HD)[None, :] // dh
    expand = np.stack([(r == 3 * c + b) for b in range(3)]).astype(np.float32)
    return _outproj(x, w_out.astype(BF16), mode="gated3",
                    acts=[o_c, o_s, o_w, gate, jnp.asarray(expand, BF16)])


def kernel(x, norm_g, ffn1_w_gate, ffn1_w_up, ffn1_w_down, ffn2_w_gate, ffn2_w_up, ffn2_w_down, fox_w_in, fox_b_f, fox_g_q, fox_g_k, fox_w_out, nsa_w_in, nsa_cmp_pos_k, nsa_cmp_pos_v, nsa_w_cmp_k, nsa_w_cmp_v, nsa_g_q, nsa_g_k, nsa_w_out, gla_w_in, gla_w_gate_up, gla_b_gate, gla_g_out, gla_w_out, sb_w_in, sb_w_out):
    B, S, D = x.shape
    depth = norm_g.shape[0]
    n_mixers = 4
    x2 = x.reshape(B * S, D)
    for i in range(depth):
        m, j = i % n_mixers, i // n_mixers
        x2 = _ffn(x2, norm_g[i, 0], ffn1_w_gate[i].astype(BF16), ffn1_w_up[i].astype(BF16),
                  ffn1_w_down[i].astype(BF16))
        g = norm_g[i, 1]
        if m == 0:
            x2 = _fox_mixer(x2, g, fox_w_in[j], fox_b_f[j], fox_g_q[j], fox_g_k[j], fox_w_out[j], batch=B, seq=S)
        elif m == 1:
            x2 = _nsa_mixer(x2, g, nsa_w_in[j], nsa_cmp_pos_k[j], nsa_cmp_pos_v[j], nsa_w_cmp_k[j],
                            nsa_w_cmp_v[j], nsa_g_q[j], nsa_g_k[j], nsa_w_out[j], batch=B, seq=S)
        elif m == 2:
            x2 = _gla_mixer(x2, g, gla_w_in[j], gla_w_gate_up[j], gla_b_gate[j], gla_g_out[j], gla_w_out[j],
                            batch=B, seq=S)
        else:
            x2 = _sb_mixer(x2, g, sb_w_in[j], sb_w_out[j], batch=B, seq=S)
        x2 = _ffn(x2, norm_g[i, 2], ffn2_w_gate[i].astype(BF16), ffn2_w_up[i].astype(BF16),
                  ffn2_w_down[i].astype(BF16))
    return x2.reshape(B, S, D)
```

```python
import functools

import numpy as np
import jax
import jax.numpy as jnp
from jax import lax
from jax.experimental import pallas as pl
from jax.experimental.pallas import tpu as pltpu

F32 = jnp.float32
BF16 = jnp.bfloat16

N_HEADS = 16
HEAD_DIM = 64
ROPE_THETA = 10000.0
RMS_EPS = 1e-6
NEG_INF = -1e30
NSA_KV_HEADS = 4
NSA_CMP_LEN = 32
NSA_CMP_STRIDE = 16
NSA_SEL_LEN = 64
NSA_N_SEL = 16
NSA_WINDOW = 512
NSA_FORCED_SCORE = 1e9
GLA_HEADS = 4
GLA_GATE_RANK = 16
GLA_TAU = 16.0
GLA_CHUNK = 64

LANES = 128
V7X_VMEM_BYTES = 64 * 1024 * 1024
VMEM_LIMIT = V7X_VMEM_BYTES - 8 * 1024 * 1024

_NT = (((1,), (1,)), ((), ()))
_TN = (((0,), (0,)), ((), ()))


def _cparams(*sem):
    return pltpu.CompilerParams(dimension_semantics=sem, vmem_limit_bytes=VMEM_LIMIT)


def _rms_rows(x, g):
    return x * lax.rsqrt(jnp.mean(x * x, axis=-1, keepdims=True) + RMS_EPS) * g


def _softplus(z):
    return jnp.maximum(z, 0.0) + jnp.log(1.0 + jnp.exp(-jnp.abs(z)))


def _neg_abs(z):
    bits = lax.bitcast_convert_type(z, jnp.int32) | jnp.int32(-2 ** 31)
    return lax.bitcast_convert_type(bits, F32)


def _sigmoid(z):
    return 1.0 / (1.0 + jnp.exp(-z))


def _pick_tile(n, target):
    t = min(n, target)
    while n % t:
        t //= 2
    return t


def _ffn_kernel(x_ref, g_ref, wg_ref, wu_ref, wd_ref, o_ref, a_ref, *, tf):
    x = x_ref[...]
    h = _rms_rows(x, g_ref[...]).astype(BF16)
    for c in range(a_ref.shape[1] // tf):
        cols = slice(c * tf, (c + 1) * tf)
        gate = jnp.dot(h, wg_ref[:, cols], preferred_element_type=F32)
        up = jnp.dot(h, wu_ref[:, cols], preferred_element_type=F32)
        a_ref[:, cols] = (gate * _sigmoid(gate) * up).astype(BF16)
    o_ref[...] = x + 0.5 * jnp.dot(a_ref[...], wd_ref[...], preferred_element_type=F32)


def _resident(shape):
    return pl.BlockSpec(shape, lambda *_: (0,) * len(shape), pipeline_mode=pl.Buffered(1))


def _ffn(x, g, wg, wu, wd):
    T, D = x.shape
    F = wg.shape[1]
    tm = _pick_tile(T, 1024)
    tf = 256 if F % 256 == 0 else F
    return pl.pallas_call(
        functools.partial(_ffn_kernel, tf=tf),
        grid=(T // tm,),
        in_specs=[
            pl.BlockSpec((tm, D), lambda i: (i, 0)),
            _resident((1, D)),
            _resident((D, F)),
            _resident((D, F)),
            _resident((F, D)),
        ],
        out_specs=pl.BlockSpec((tm, D), lambda i: (i, 0)),
        out_shape=jax.ShapeDtypeStruct((T, D), F32),
        scratch_shapes=[pltpu.VMEM((tm, F), BF16)],
        compiler_params=_cparams("parallel"),
        name="ffn",
    )(x, g.reshape(1, D), wg, wu, wd)


PROJ_CHUNK = 256


def _proj_kernel(*refs, n_norm, rope, has_aux):
    it = iter(refs)
    x_ref, g_ref, w_ref = next(it), next(it), next(it)
    hg_ref, gsum_ref = (next(it), next(it)) if n_norm else (None, None)
    cos_ref, sin_ref = (next(it), next(it)) if rope else (None, None)
    waux_ref = next(it) if has_aux else None
    o_ref = next(it)
    ch = PROJ_CHUNK
    h = _rms_rows(x_ref[...], g_ref[...]).astype(BF16)
    for c in range(o_ref.shape[1] // ch):
        cols = slice(c * ch, (c + 1) * ch)
        y = jnp.dot(h, w_ref[:, cols], preferred_element_type=F32)
        if c * ch < n_norm:
            ss = jnp.dot((y * y).astype(BF16), gsum_ref[...], preferred_element_type=F32)
            y = y * lax.rsqrt(ss * (1.0 / HEAD_DIM) + RMS_EPS) * hg_ref[:, cols]
            if rope:
                lane = lax.broadcasted_iota(jnp.int32, (1, ch), 1)
                first_half = (lane % HEAD_DIM) < (HEAD_DIM // 2)
                partner = jnp.where(first_half,
                                    pltpu.roll(y, ch - HEAD_DIM // 2, 1),
                                    pltpu.roll(y, HEAD_DIM // 2, 1))
                y = y * cos_ref[...] + partner * sin_ref[...]
        o_ref[:, cols] = y.astype(o_ref.dtype)
    if has_aux:
        next(it)[...] = jnp.dot(h, waux_ref[...], preferred_element_type=F32)


def _proj(x, g, w, *, seq, head_gain=None, rope=None, w_aux=None):
    T, D = x.shape
    N = w.shape[1]
    tm = _pick_tile(seq, 1024)
    ch = PROJ_CHUNK
    n_norm = 0 if head_gain is None else head_gain.shape[0]
    assert N % ch == 0 and n_norm % ch == 0 and T % tm == 0
    args = [x, g.reshape(1, D), w]
    in_specs = [pl.BlockSpec((tm, D), lambda i: (i, 0)), _resident((1, D)), _resident((D, N))]
    if n_norm:
        gidx = np.arange(ch) // HEAD_DIM
        gsum = jnp.asarray((gidx[:, None] == gidx[None, :]).astype(np.float32), BF16)
        args += [head_gain.reshape(1, n_norm), gsum]
        in_specs += [_resident((1, n_norm)), _resident((ch, ch))]
    if rope is not None:
        nblk = seq // tm
        args += list(rope)
        in_specs += [pl.BlockSpec((tm, ch), lambda i: (i % nblk, 0))] * 2
    out_specs = [pl.BlockSpec((tm, N), lambda i: (i, 0))]
    out_shape = [jax.ShapeDtypeStruct((T, N), BF16)]
    if w_aux is not None:
        args.append(w_aux)
        in_specs.append(_resident(w_aux.shape))
        out_specs.append(pl.BlockSpec((tm, w_aux.shape[1]), lambda i: (i, 0)))
        out_shape.append(jax.ShapeDtypeStruct((T, w_aux.shape[1]), F32))
    out = pl.pallas_call(
        functools.partial(_proj_kernel, n_norm=n_norm, rope=rope is not None, has_aux=w_aux is not None),
        grid=(T // tm,),
        in_specs=in_specs,
        out_specs=out_specs,
        out_shape=out_shape,
        compiler_params=_cparams("parallel"),
        name="proj",
    )(*args)
    return out if w_aux is not None else out[0]


def _outproj_kernel(*refs, mode):
    if mode == "plain":
        x_ref, a_ref, w_ref, o_ref = refs
        a = a_ref[...]
    elif mode == "sigmoid_gate":
        x_ref, a_ref, og_ref, w_ref, o_ref = refs
        a = (a_ref[...].astype(F32) * _sigmoid(og_ref[...].astype(F32))).astype(BF16)
    else:
        x_ref, oc_ref, os_ref, ow_ref, gate_ref, exp_ref, w_ref, o_ref = refs
        gs = _sigmoid(gate_ref[...]).astype(BF16)
        a = None
        for c, o_c in enumerate((oc_ref, os_ref, ow_ref)):
            ge = jnp.dot(gs, exp_ref[c], preferred_element_type=F32)
            t = ge * o_c[...].astype(F32)
            a = t if a is None else a + t
        a = a.astype(BF16)
    o_ref[...] = x_ref[...] + jnp.dot(a, w_ref[...], preferred_element_type=F32)


def _outproj(x, w, *, mode, acts):
    T, D = x.shape
    K = w.shape[0]
    tm = _pick_tile(T, 512)
    row = lambda i: (i, 0)
    args = [x]
    in_specs = [pl.BlockSpec((tm, D), row)]
    for a in acts:
        if a.ndim == 3:
            args.append(a)
            in_specs.append(pl.BlockSpec(a.shape, lambda i: (0, 0, 0)))
        else:
            args.append(a)
            in_specs.append(pl.BlockSpec((tm, a.shape[1]), row))
    args.append(w)
    in_specs.append(pl.BlockSpec((K, D), lambda i: (0, 0)))
    return pl.pallas_call(
        functools.partial(_outproj_kernel, mode=mode),
        grid=(T // tm,),
        in_specs=in_specs,
        out_specs=pl.BlockSpec((tm, D), row),
        out_shape=jax.ShapeDtypeStruct((T, D), F32),
        compiler_params=_cparams("parallel"),
        name="outproj_" + mode,
    )(*args)


def _fox_c_kernel(x_ref, g_ref, wf_ref, bf_ref, tri_ref, c_ref, carry_ref):
    @pl.when(pl.program_id(1) == 0)
    def _():
        carry_ref[...] = jnp.zeros_like(carry_ref)

    h = _rms_rows(x_ref[...], g_ref[...])
    f = lax.dot_general(wf_ref[...], h, _NT, precision=lax.Precision.HIGHEST,
                        preferred_element_type=F32) + bf_ref[...]
    ls = jnp.minimum(f, 0.0) - jnp.log(1.0 + jnp.exp(-jnp.abs(f)))
    cs = jnp.dot(ls, tri_ref[...], precision=lax.Precision.HIGHEST,
                 preferred_element_type=F32) + carry_ref[...]
    c_ref[...] = cs
    carry_ref[...] = cs[:, cs.shape[1] - 1:]


def _fox_c(x, g, wf_t, b_f, *, batch, seq):
    T, D = x.shape
    H = wf_t.shape[0]
    tm = _pick_tile(seq, 512)
    ns = seq // tm
    tri = jnp.asarray(np.triu(np.ones((tm, tm), np.float32)))
    return pl.pallas_call(
        _fox_c_kernel,
        grid=(batch, ns),
        in_specs=[
            pl.BlockSpec((tm, D), lambda b, s: (b * ns + s, 0)),
            pl.BlockSpec((1, D), lambda b, s: (0, 0)),
            pl.BlockSpec((H, D), lambda b, s: (0, 0)),
            pl.BlockSpec((H, 1), lambda b, s: (0, 0)),
            pl.BlockSpec((tm, tm), lambda b, s: (0, 0)),
        ],
        out_specs=pl.BlockSpec((None, H, tm), lambda b, s: (b, 0, s)),
        out_shape=jax.ShapeDtypeStruct((batch, H, seq), F32),
        scratch_shapes=[pltpu.VMEM((H, 1), F32)],
        compiler_params=_cparams("parallel", "arbitrary"),
        name="fox_c",
    )(x, g.reshape(1, D), wf_t, b_f.reshape(H, 1), tri)


def _fox_attn_kernel(q_ref, k_ref, v_ref, c_ref, o_ref, vaug_ref, sa_ref, sb_ref, m_ref, acc_ref, *, t):
    qi = pl.program_id(2)
    lane = lax.broadcasted_iota(jnp.int32, (1, LANES), 1)
    upper = lane >= HEAD_DIM

    @pl.when(qi == 0)
    def _():
        v = v_ref[...]
        one = jnp.ones_like(v)
        vaug_ref[0] = jnp.where(upper, one, v)
        vaug_ref[1] = jnp.where(upper, v, one)

    q0 = pl.multiple_of(qi * t, t)
    qslab = q_ref[...]
    zero = jnp.zeros_like(qslab)
    qpos = lax.broadcasted_iota(jnp.int32, (t, t), 0)
    kpos = lax.broadcasted_iota(jnp.int32, (t, t), 1)
    causal = kpos <= qpos

    heads = range(2)
    q2 = jnp.concatenate([jnp.where(upper, zero, qslab), jnp.where(upper, qslab, zero)], axis=0)
    c_q0 = [c_ref[i, :, pl.ds(q0, LANES)][:, 0:1] for i in heads]

    def produce(s_ref, j):
        keys = pl.ds(pl.multiple_of(j * t, t), t)
        s2 = lax.dot_general(q2, k_ref[keys, :], _NT, preferred_element_type=F32)
        for i in heads:
            s_ref[i] = s2[i * t:(i + 1) * t] + (c_q0[i] - c_ref[i, :, keys])

    def consume(s_ref, j, masked):
        keys = pl.ds(pl.multiple_of(j * t, t), t)
        score = (lambda i: jnp.where(causal, s_ref[i], NEG_INF)) if masked else (lambda i: s_ref[i])
        m_old = [m_ref[i] for i in heads]
        m_new = [jnp.maximum(m_old[i], jnp.max(score(i), axis=1, keepdims=True)) for i in heads]
        p = [jnp.exp(score(i) - jnp.tile(m_new[i], (1, t // LANES))).astype(BF16) for i in heads]
        for i in heads:
            m_ref[i] = m_new[i]
            acc_ref[i] = (jnp.exp(m_old[i] - m_new[i]) * acc_ref[i]
                          + jnp.dot(p[i], vaug_ref[i, keys, :], preferred_element_type=F32))

    m_ref[...] = jnp.full(m_ref.shape, NEG_INF, F32)
    acc_ref[...] = jnp.zeros(acc_ref.shape, F32)
    produce(sa_ref, 0)

    def body(jj, _):
        produce(sb_ref, 2 * jj + 1)
        consume(sa_ref, 2 * jj, False)
        produce(sa_ref, 2 * jj + 2)
        consume(sb_ref, 2 * jj + 1, False)
        return 0

    lax.fori_loop(0, qi // 2, body, 0)

    @pl.when(qi % 2 == 0)
    def _():
        consume(sa_ref, qi, True)

    @pl.when(qi % 2 == 1)
    def _():
        produce(sb_ref, qi)
        consume(sa_ref, qi - 1, False)
        consume(sb_ref, qi, True)

    accs = (acc_ref[0], acc_ref[1])
    num = jnp.where(upper, accs[1], accs[0])
    den = jnp.where(upper, accs[0], accs[1])
    den = pltpu.roll(den, HEAD_DIM, 1)
    o_ref[...] = (num / den).astype(o_ref.dtype)


def _fox_attn(qkvg, c, *, batch, seq):
    T = qkvg.shape[0]
    HD = N_HEADS * HEAD_DIM
    t = _pick_tile(seq, 512)
    nq = seq // t
    npair = HD // LANES
    return pl.pallas_call(
        functools.partial(_fox_attn_kernel, t=t),
        grid=(batch, npair, nq),
        in_specs=[
            pl.BlockSpec((t, LANES), lambda b, h, q: (b * nq + q, h)),
            pl.BlockSpec((seq, LANES), lambda b, h, q: (b, npair + h)),
            pl.BlockSpec((seq, LANES), lambda b, h, q: (b, 2 * npair + h)),
            pl.BlockSpec((None, 2, 1, seq), lambda b, h, q: (b, h, 0, 0)),
        ],
        out_specs=pl.BlockSpec((t, LANES), lambda b, h, q: (b * nq + q, h)),
        out_shape=jax.ShapeDtypeStruct((T, HD), BF16),
        scratch_shapes=[pltpu.VMEM((2, seq, LANES), BF16),
                        pltpu.VMEM((2, t, t), F32), pltpu.VMEM((2, t, t), F32),
                        pltpu.VMEM((2, t, LANES), F32), pltpu.VMEM((2, t, LANES), F32)],
        compiler_params=_cparams("parallel", "arbitrary", "arbitrary"),
        name="fox_attn",
    )(qkvg, qkvg, qkvg, c.reshape(batch, N_HEADS, 1, seq))


def _fox_mixer(x, g, w_in, b_f, g_q, g_k, w_out, *, batch, seq):
    HD = N_HEADS * HEAD_DIM
    scale = HEAD_DIM ** -0.5
    w = jnp.concatenate([w_in[:, :3 * HD], w_in[:, 3 * HD + N_HEADS:]], axis=1).astype(BF16)
    wf_t = w_in[:, 3 * HD:3 * HD + N_HEADS].T
    head_gain = jnp.concatenate([jnp.tile(g_q * scale, N_HEADS), jnp.tile(g_k, N_HEADS)])
    qkvg = _proj(x, g, w, seq=seq, head_gain=head_gain)
    c = _fox_c(x, g, wf_t, b_f, batch=batch, seq=seq)
    o = _fox_attn(qkvg, c, batch=batch, seq=seq)
    return _outproj_fox(x, o, qkvg, w_out.astype(BF16))


def _outproj_fox(x, o, qkvg, w):
    T, D = x.shape
    HD = o.shape[1]
    tm = _pick_tile(T, 512)
    return pl.pallas_call(
        functools.partial(_outproj_kernel, mode="sigmoid_gate"),
        grid=(T // tm,),
        in_specs=[
            pl.BlockSpec((tm, D), lambda i: (i, 0)),
            pl.BlockSpec((tm, HD), lambda i: (i, 0)),
            pl.BlockSpec((tm, HD), lambda i: (i, 3)),
            pl.BlockSpec((HD, D), lambda i: (0, 0)),
        ],
        out_specs=pl.BlockSpec((tm, D), lambda i: (i, 0)),
        out_shape=jax.ShapeDtypeStruct((T, D), F32),
        compiler_params=_cparams("parallel"),
        name="outproj_fox",
    )(x, o, qkvg, w)


def _sb_attn_kernel(q_ref, k_ref, v_ref, tri_ref, o_ref, wa_ref, wb_ref, ta_ref, tb_ref, rest_ref, acc_ref,
                    *, tq, tk):
    qi = pl.program_id(2)
    lane = lax.broadcasted_iota(jnp.int32, (1, LANES), 1)
    upper = lane >= HEAD_DIM
    q0 = pl.multiple_of(qi * tq, tq)
    qslab = q_ref[...]
    zero = jnp.zeros_like(qslab)
    qh = (jnp.concatenate([jnp.where(upper, zero, qslab), jnp.where(upper, qslab, zero)], axis=0),)
    qpos = lax.broadcasted_iota(jnp.int32, (2 * tq, tk), 0) % tq
    kpos = lax.broadcasted_iota(jnp.int32, (2 * tq, tk), 1)

    heads = range(1)

    def softplus_scores(j, diag_offset):
        kt = k_ref[pl.ds(pl.multiple_of(j * tk, tk), tk), :]
        z = [lax.dot_general(qh[i], kt, _NT, preferred_element_type=F32) for i in heads]
        sp = [jnp.maximum(z[i], 0.0) + jnp.log2(1.0 + jnp.exp2(_neg_abs(z[i]))) for i in heads]
        if diag_offset is not None:
            strict = kpos + diag_offset < qpos
            spm = [jnp.where(strict, sp[i], 0.0) for i in heads]
        else:
            spm = sp
        return [z[i] - sp[i] for i in heads], spm

    def store_tile(w_ref, tot_ref, log_sig, spm, diag_offset):
        cum = [jnp.dot(spm[i].astype(BF16), tri_ref[...], preferred_element_type=F32) for i in heads]
        w = [log_sig[i] - cum[i] for i in heads]
        if diag_offset is not None:
            strict = kpos + diag_offset < qpos
            w = [jnp.where(strict, w[i], NEG_INF) for i in heads]
        for i in heads:
            w_ref[i] = w[i]
            tot_ref[i] = jnp.broadcast_to(cum[i][:, 0:1] + spm[i][:, 0:1], (2 * tq, LANES))

    def produce(w_ref, tot_ref, j, diag_offset):
        store_tile(w_ref, tot_ref, *softplus_scores(j, diag_offset), diag_offset)

    def weights(w_ref):
        return [jnp.exp2(w_ref[i] - jnp.tile(rest_ref[i], (1, tk // LANES))).astype(BF16) for i in heads]

    def accumulate(a, tot_ref, j):
        vt = v_ref[pl.ds(pl.multiple_of(j * tk, tk), tk), :]
        for i in heads:
            acc_ref[i] += jnp.dot(a[i], vt, preferred_element_type=F32)
            rest_ref[i] += tot_ref[i]

    def consume(w_ref, tot_ref, j):
        accumulate(weights(w_ref), tot_ref, j)

    assert tq == 2 * tk
    last = 2 * qi + 1
    buf_a, buf_b = (wa_ref, ta_ref), (wb_ref, tb_ref)
    rest_ref[...] = jnp.zeros(rest_ref.shape, F32)
    acc_ref[...] = jnp.zeros(acc_ref.shape, F32)
    produce(*buf_a, last, tk)
    produce(*buf_b, last - 1, 0)
    consume(*buf_a, last)

    def body(m, _):
        a = weights(wb_ref)
        nxt = softplus_scores(last - 2 * m, None)
        accumulate(a, tb_ref, last - 2 * m + 1)
        store_tile(*buf_a, *nxt, None)
        a = weights(wa_ref)
        nxt = softplus_scores(last - 2 * m - 1, None)
        accumulate(a, ta_ref, last - 2 * m)
        store_tile(*buf_b, *nxt, None)
        return 0

    lax.fori_loop(1, qi + 1, body, 0)
    consume(*buf_b, 0)
    o_ref[...] = jnp.where(upper, acc_ref[0, tq:2 * tq], acc_ref[0, 0:tq]).astype(o_ref.dtype)


def _sb_attn(qkv, *, batch, seq):
    T = qkv.shape[0]
    HD = N_HEADS * HEAD_DIM
    tq = _pick_tile(seq, 512)
    tk = 256
    nq = seq // tq
    npair = HD // LANES
    tri = jnp.asarray(np.tril(np.ones((tk, tk), np.float32), -1), BF16)
    return pl.pallas_call(
        functools.partial(_sb_attn_kernel, tq=tq, tk=tk),
        grid=(batch, npair, nq),
        in_specs=[
            pl.BlockSpec((tq, LANES), lambda b, h, q: (b * nq + q, h)),
            pl.BlockSpec((seq, LANES), lambda b, h, q: (b, npair + h)),
            pl.BlockSpec((seq, LANES), lambda b, h, q: (b, 2 * npair + h)),
            pl.BlockSpec((tk, tk), lambda b, h, q: (0, 0)),
        ],
        out_specs=pl.BlockSpec((tq, LANES), lambda b, h, q: (b * nq + q, h)),
        out_shape=jax.ShapeDtypeStruct((T, HD), BF16),
        scratch_shapes=[pltpu.VMEM((1, 2 * tq, tk), F32), pltpu.VMEM((1, 2 * tq, tk), F32),
                        pltpu.VMEM((1, 2 * tq, LANES), F32), pltpu.VMEM((1, 2 * tq, LANES), F32),
                        pltpu.VMEM((1, 2 * tq, LANES), F32), pltpu.VMEM((1, 2 * tq, LANES), F32)],
        compiler_params=_cparams("parallel", "parallel", "arbitrary"),
        name="sb_attn",
    )(qkv, qkv, qkv, tri)


def _sb_mixer(x, g, w_in, w_out, *, batch, seq):
    HD = N_HEADS * HEAD_DIM
    scale = HEAD_DIM ** -0.5 * float(np.log2(np.e))
    w = jnp.concatenate([w_in[:, :HD] * scale, w_in[:, HD:]], axis=1).astype(BF16)
    qkv = _proj(x, g, w, seq=seq)
    o = _sb_attn(qkv, batch=batch, seq=seq)
    return _outproj(x, w_out.astype(BF16), mode="plain", acts=[o])


def _gla_kernel(q_ref, k_ref, v_ref, r_ref, glow_ref, wgu_ref, bg_ref, gout_ref, tri_ref,
                o_ref, state_ref, *, tm, dk):
    C = GLA_CHUNK

    @pl.when(pl.program_id(2) == 0)
    def _():
        state_ref[...] = jnp.zeros_like(state_ref)

    gate = jnp.dot(glow_ref[...], wgu_ref[...], precision=lax.Precision.HIGHEST,
                   preferred_element_type=F32) + bg_ref[...]
    log_a = (jnp.minimum(gate, 0.0) - jnp.log(1.0 + jnp.exp(-jnp.abs(gate)))) * (1.0 / GLA_TAU)
    row = lax.broadcasted_iota(jnp.int32, (C, C), 0)
    col = lax.broadcasted_iota(jnp.int32, (C, C), 1)
    causal = col <= row
    q_scale = dk ** -0.5
    chunks = [slice(c * C, (c + 1) * C) for c in range(tm // C)]

    tri = tri_ref[...]
    b = [jnp.dot(tri, log_a[sl], precision=lax.Precision.HIGHEST, preferred_element_type=F32)
         for sl in chunks]
    b_last = jnp.concatenate([jnp.broadcast_to(bc[C - 1:C], bc.shape) for bc in b], axis=0)
    b = jnp.concatenate(b, axis=0)
    q = q_ref[...].astype(F32) * q_scale
    k = k_ref[...].astype(F32)
    q_dec = (q * jnp.exp(b)).astype(BF16)
    k_neg = (k * jnp.exp(-b)).astype(BF16)
    k_rem = (k * jnp.exp(b_last - b)).astype(BF16)
    decay = jnp.exp(b_last)
    attn = [lax.dot_general(q_dec[sl], k_neg[sl], _NT, preferred_element_type=F32) for sl in chunks]
    attn = [jnp.where(causal, a, 0.0).astype(BF16) for a in attn]
    o_intra = [jnp.dot(attn[c], v_ref[sl, :], preferred_element_type=F32) for c, sl in enumerate(chunks)]
    u_t = [lax.dot_general(v_ref[sl, :], k_rem[sl], _TN, preferred_element_type=F32) for sl in chunks]

    state_t = state_ref[...]
    outs = []
    for c, sl in enumerate(chunks):
        outs.append(o_intra[c] + lax.dot_general(q_dec[sl], state_t.astype(BF16), _NT,
                                                 preferred_element_type=F32))
        state_t = state_t * decay[c * C:c * C + 1] + u_t[c]
    state_ref[...] = state_t

    o = jnp.concatenate(outs, axis=0)
    y = o * lax.rsqrt(jnp.mean(o * o, axis=-1, keepdims=True) + RMS_EPS) * gout_ref[...]
    r = r_ref[...].astype(F32)
    o_ref[...] = (y * (r * _sigmoid(r))).astype(o_ref.dtype)


def _gla(qkvr, glow, wgu, b_gate, g_out, *, batch, seq):
    T = qkvr.shape[0]
    Hg = GLA_HEADS
    dk_total = wgu.shape[1]
    dk = dk_total // Hg
    dv = (qkvr.shape[1] - 2 * dk_total) // 2 // Hg
    tm = _pick_tile(seq, 512)
    ns = seq // tm
    kq, kv = dk_total // dk, dk_total // dv
    tri = jnp.asarray(np.tril(np.ones((GLA_CHUNK, GLA_CHUNK), np.float32)))
    rowmap = lambda off: (lambda b, h, s: (b * ns + s, off + h))
    return pl.pallas_call(
        functools.partial(_gla_kernel, tm=tm, dk=dk),
        grid=(batch, Hg, ns),
        in_specs=[
            pl.BlockSpec((tm, dk), rowmap(0)),
            pl.BlockSpec((tm, dk), rowmap(kq)),
            pl.BlockSpec((tm, dv), rowmap(2 * kv)),
            pl.BlockSpec((tm, dv), rowmap(2 * kv + Hg)),
            pl.BlockSpec((tm, LANES), lambda b, h, s: (b * ns + s, 0)),
            pl.BlockSpec((LANES, dk), lambda b, h, s: (0, h)),
            pl.BlockSpec((1, dk), lambda b, h, s: (0, h)),
            pl.BlockSpec((1, dv), lambda b, h, s: (0, 0)),
            pl.BlockSpec((GLA_CHUNK, GLA_CHUNK), lambda b, h, s: (0, 0)),
        ],
        out_specs=pl.BlockSpec((tm, dv), rowmap(0)),
        out_shape=jax.ShapeDtypeStruct((T, Hg * dv), BF16),
        scratch_shapes=[pltpu.VMEM((dv, dk), F32)],
        compiler_params=_cparams("parallel", "parallel", "arbitrary"),
        name="gla",
    )(qkvr, qkvr, qkvr, qkvr, glow, wgu, b_gate.reshape(1, dk_total), g_out.reshape(1, dv), tri)


def _gla_mixer(x, g, w_in, w_gate_up, b_gate, g_out, w_out, *, batch, seq):
    dk_total = w_gate_up.shape[1]
    dv_total = w_out.shape[0]
    lo = 2 * dk_total + dv_total
    w_main = jnp.concatenate([w_in[:, :lo], w_in[:, lo + GLA_GATE_RANK:]], axis=1).astype(BF16)
    w_low = jnp.pad(w_in[:, lo:lo + GLA_GATE_RANK], ((0, 0), (0, LANES - GLA_GATE_RANK))).astype(BF16)
    wgu = jnp.pad(w_gate_up, ((0, LANES - GLA_GATE_RANK), (0, 0)))
    qkvr, glow = _proj(x, g, w_main, seq=seq, w_aux=w_low)
    o = _gla(qkvr, glow, wgu, b_gate, g_out, batch=batch, seq=seq)
    return _outproj(x, w_out.astype(BF16), mode="plain", acts=[o])


def _nsa_compress_kernel(kc_ref, vc_ref, pk_ref, pv_ref, wk_ref, wv_ref, gk_ref, ko_ref, vo_ref, *, n16):
    def compress(r_ref, pos_ref, w_ref):
        r = r_ref[...].astype(F32)
        first = jnp.dot((r + pos_ref[0:1, :]).astype(BF16), w_ref[0], preferred_element_type=F32)
        second = jnp.dot((r + pos_ref[1:2, :]).astype(BF16), w_ref[1], preferred_element_type=F32)
        return first + pltpu.roll(second, n16 - 1, 0)

    kc = compress(kc_ref, pk_ref, wk_ref)
    ko_ref[...] = _rms_rows(kc, gk_ref[...]).astype(ko_ref.dtype)
    vo_ref[...] = compress(vc_ref, pv_ref, wv_ref).astype(vo_ref.dtype)


def _nsa_compress(kc_r, vc_r, pos_k, pos_v, w_k, w_v, g_k):
    B, G, n16, width = kc_r.shape
    dh = HEAD_DIM
    half = NSA_CMP_LEN // 2
    dup = lambda w: jnp.concatenate([w, w], axis=-1).reshape(2, half * dh, 2 * dh).astype(BF16)
    spec_r = pl.BlockSpec((None, None, n16, width), lambda b, g: (b, g, 0, 0))
    spec_o = pl.BlockSpec((None, None, n16, 2 * dh), lambda b, g: (b, g, 0, 0))
    const2 = lambda shape: pl.BlockSpec(shape, lambda b, g: (0,) * len(shape))
    out = jax.ShapeDtypeStruct((B, G, n16, 2 * dh), BF16)
    return pl.pallas_call(
        functools.partial(_nsa_compress_kernel, n16=n16),
        grid=(B, G),
        in_specs=[spec_r, spec_r, const2((2, width)), const2((2, width)),
                  const2((2, width, 2 * dh)), const2((2, width, 2 * dh)), const2((1, 2 * dh))],
        out_specs=[spec_o, spec_o],
        out_shape=[out, out],
        compiler_params=_cparams("parallel", "parallel"),
        name="nsa_compress",
    )(kc_r, vc_r, pos_k.reshape(2, width), pos_v.reshape(2, width), dup(w_k), dup(w_v),
      jnp.tile(g_k, 2).reshape(1, 2 * dh))


def _stack_heads(q_ref, extra=None, rows_in=slice(None)):
    lane = lax.broadcasted_iota(jnp.int32, (1, LANES), 1)
    upper = lane >= HEAD_DIM
    rows = []
    for j in range(2):
        slab = q_ref[rows_in, j * LANES:(j + 1) * LANES]
        zero = jnp.zeros_like(slab)
        for i in range(2):
            qh = jnp.where(upper, slab, zero) if i else jnp.where(upper, zero, slab)
            rows.append(qh if extra is None else jnp.concatenate([qh, extra], axis=1))
    return jnp.concatenate(rows, axis=0)


def _unstack_heads(o, tq):
    lane = lax.broadcasted_iota(jnp.int32, (1, LANES), 1)
    upper = lane >= HEAD_DIM
    slabs = []
    for j in range(2):
        lo = o[(2 * j) * tq:(2 * j + 1) * tq]
        hi = pltpu.roll(o[(2 * j + 1) * tq:(2 * j + 2) * tq], HEAD_DIM, 1)
        slabs.append(jnp.where(upper, hi, lo))
    return jnp.concatenate(slabs, axis=1)


def _group_lower(slab_ref, g):
    x = slab_ref[...].astype(F32)
    return jnp.where(g % 2 == 1, pltpu.roll(x, HEAD_DIM, 1), x)


def _both_halves(x):
    lane = lax.broadcasted_iota(jnp.int32, (1, LANES), 1)
    return jnp.where(lane >= HEAD_DIM, pltpu.roll(x, HEAD_DIM, 1), x)


def _with_ones(x):
    lane = lax.broadcasted_iota(jnp.int32, (1, LANES), 1)
    return jnp.where(lane >= HEAD_DIM, 1.0, x)


def _normalise(acc):
    return acc / pltpu.roll(acc, HEAD_DIM, 1)


def _nsa_cmp_attn_kernel(q_ref, kc_ref, vc_ref, cover_ref, oc_ref, sel_ref, *, tq, n16):
    qi = pl.program_id(2)
    q4 = _stack_heads(q_ref)
    s = lax.dot_general(q4, kc_ref[...], _NT, preferred_element_type=F32)
    qpos = qi * tq + lax.broadcasted_iota(jnp.int32, (tq, n16), 0)
    cmp_end = lax.broadcasted_iota(jnp.int32, (tq, n16), 1) * NSA_CMP_STRIDE + (NSA_CMP_LEN - 1)
    valid = jnp.concatenate([cmp_end <= qpos] * 4, axis=0)
    s = jnp.where(valid, s, NEG_INF)
    e = jnp.exp(s - jnp.max(s, axis=1, keepdims=True))
    p = jnp.where(valid, e / jnp.sum(e, axis=1, keepdims=True), 0.0)
    o = jnp.dot(p.astype(BF16), vc_ref[...], preferred_element_type=F32)
    oc_ref[...] = _unstack_heads(o, tq).astype(oc_ref.dtype)

    psum = p[0:tq] + p[tq:2 * tq] + p[2 * tq:3 * tq] + p[3 * tq:4 * tq]
    cover = cover_ref[...]
    hi = psum.astype(BF16)
    r1 = psum - hi.astype(F32)
    mid = r1.astype(BF16)
    lo = (r1 - mid.astype(F32)).astype(BF16)
    imp = (jnp.dot(hi, cover, preferred_element_type=F32) + jnp.dot(mid, cover, preferred_element_type=F32)
           + jnp.dot(lo, cover, preferred_element_type=F32))

    blk = lax.broadcasted_iota(jnp.int32, (tq, LANES), 1)
    cur = (qi * tq + lax.broadcasted_iota(jnp.int32, (tq, LANES), 0)) // NSA_SEL_LEN
    forced = (blk == 0) | (blk == cur) | (blk == cur - 1)
    vals = jnp.where(forced, NSA_FORCED_SCORE, jnp.where(blk <= cur, imp, -1.0))
    blk_f = blk.astype(F32)
    chosen = jnp.zeros((tq, LANES), jnp.bool_)
    for _ in range(NSA_N_SEL):
        top = jnp.max(vals, axis=1, keepdims=True)
        first = jnp.min(jnp.where(vals == top, blk_f, float(LANES)), axis=1, keepdims=True)
        pick = blk_f == first
        chosen = chosen | pick
        vals = jnp.where(pick, -3e38, vals)
    sel_ref[...] = jnp.where(chosen, 0.0, NEG_INF).astype(sel_ref.dtype)


def _nsa_cmp_attn(qk, kcmp, vcmp, *, batch, seq):
    T = qk.shape[0]
    G = NSA_KV_HEADS
    tq = _pick_tile(seq, 512)
    nq = seq // tq
    n16 = kcmp.shape[2]
    n_sel = seq // NSA_SEL_LEN
    assert n_sel <= LANES and seq % tq == 0
    n = np.arange(n16)[:, None] * NSA_CMP_STRIDE
    j = np.arange(LANES)[None, :] * NSA_SEL_LEN
    cover = (n < j + NSA_SEL_LEN) & (n + NSA_CMP_LEN > j) & (np.arange(LANES)[None, :] < n_sel)
    cover = jnp.asarray(cover.astype(np.float32), BF16)
    spec_c = pl.BlockSpec((None, None, n16, LANES), lambda b, g, q: (b, g, 0, 0))
    return pl.pallas_call(
        functools.partial(_nsa_cmp_attn_kernel, tq=tq, n16=n16),
        grid=(batch, G, nq),
        in_specs=[pl.BlockSpec((tq, 2 * LANES), lambda b, g, q: (b * nq + q, g)), spec_c, spec_c,
                  pl.BlockSpec((n16, LANES), lambda b, g, q: (0, 0))],
        out_specs=[pl.BlockSpec((tq, 2 * LANES), lambda b, g, q: (b * nq + q, g)),
                   pl.BlockSpec((tq, LANES), lambda b, g, q: (b * nq + q, g))],
        out_shape=[jax.ShapeDtypeStruct((T, N_HEADS * HEAD_DIM), BF16),
                   jax.ShapeDtypeStruct((T, G * LANES), BF16)],
        compiler_params=_cparams("parallel", "parallel", "parallel"),
        name="nsa_cmp_attn",
    )(qk, kcmp, vcmp, cover)


def _nsa_sel_attn_kernel(q_ref, k_ref, v_ref, sel_ref, o_ref, kaug_ref, vaug_ref, sa_ref, sb_ref, m_ref, acc_ref,
                         *, tq, sub, tk, seq):
    g = pl.program_id(1)
    qi = pl.program_id(2)

    @pl.when(qi == 0)
    def _():
        kaug_ref[:, 0:LANES] = _both_halves(_group_lower(k_ref, g)).astype(BF16)
        blk_of_key = lax.broadcasted_iota(jnp.int32, (seq, LANES), 0) // NSA_SEL_LEN
        blk = lax.broadcasted_iota(jnp.int32, (seq, LANES), 1)
        kaug_ref[:, LANES:2 * LANES] = jnp.where(blk_of_key == blk, 1.0, 0.0).astype(BF16)
        vaug_ref[...] = _with_ones(_group_lower(v_ref, g)).astype(BF16)

    chains = range(tq // sub)
    rows = [slice(c * sub, (c + 1) * sub) for c in chains]
    q4 = [_stack_heads(q_ref, extra=sel_ref[rows[c], :], rows_in=rows[c]) for c in chains]
    q0 = qi * tq
    n_full = q0 // tk

    def produce(s_ref, j):
        kt = kaug_ref[pl.ds(pl.multiple_of(j * tk, tk), tk), :]
        for c in chains:
            s_ref[c] = lax.dot_general(q4[c], kt, _NT, preferred_element_type=F32)

    def consume(s_ref, j, masked):
        keys = pl.ds(pl.multiple_of(j * tk, tk), tk)
        if masked:
            kpos = j * tk + lax.broadcasted_iota(jnp.int32, (sub, tk), 1)
            qpos = q0 + lax.broadcasted_iota(jnp.int32, (sub, tk), 0)
            ok = [jnp.concatenate([kpos <= qpos + c * sub] * 4, axis=0) for c in chains]
            score = lambda c: jnp.where(ok[c], s_ref[c], NEG_INF)
        else:
            score = lambda c: s_ref[c]
        vt = vaug_ref[keys, :]
        m_old = [m_ref[c] for c in chains]
        m_new = [jnp.maximum(m_old[c], jnp.max(score(c), axis=1, keepdims=True)) for c in chains]
        p = [jnp.exp(score(c) - jnp.tile(m_new[c], (1, tk // LANES))).astype(BF16) for c in chains]
        for c in chains:
            m_ref[c] = m_new[c]
            acc_ref[c] = (jnp.exp(m_old[c] - m_new[c]) * acc_ref[c]
                          + jnp.dot(p[c], vt, preferred_element_type=F32))

    m_ref[...] = jnp.full(m_ref.shape, NEG_INF, F32)
    acc_ref[...] = jnp.zeros(acc_ref.shape, F32)
    produce(sa_ref, 0)

    def body(jj, _):
        produce(sb_ref, 2 * jj + 1)
        consume(sa_ref, 2 * jj, False)
        produce(sa_ref, 2 * jj + 2)
        consume(sb_ref, 2 * jj + 1, False)
        return 0

    lax.fori_loop(0, n_full // 2, body, 0)

    @pl.when(n_full % 2 == 0)
    def _():
        consume(sa_ref, n_full, True)

    @pl.when(n_full % 2 == 1)
    def _():
        produce(sb_ref, n_full)
        consume(sa_ref, n_full - 1, False)
        consume(sb_ref, n_full, True)

    for c in chains:
        o_ref[rows[c], :] = _unstack_heads(_normalise(acc_ref[c]), sub).astype(o_ref.dtype)


def _nsa_sel_attn(qk, vv, selbias, *, batch, seq, k_col, v_col):
    T = qk.shape[0]
    G = NSA_KV_HEADS
    tq = _pick_tile(seq, 256)
    sub = tq
    tk = _pick_tile(seq, 512)
    nq = seq // tq
    assert tk % tq == 0
    return pl.pallas_call(
        functools.partial(_nsa_sel_attn_kernel, tq=tq, sub=sub, tk=tk, seq=seq),
        grid=(batch, G, nq),
        in_specs=[pl.BlockSpec((tq, 2 * LANES), lambda b, g, q: (b * nq + q, g)),
                  pl.BlockSpec((seq, LANES), lambda b, g, q: (b, k_col + g // 2)),
                  pl.BlockSpec((seq, LANES), lambda b, g, q: (b, v_col + g // 2)),
                  pl.BlockSpec((tq, LANES), lambda b, g, q: (b * nq + q, g))],
        out_specs=pl.BlockSpec((tq, 2 * LANES), lambda b, g, q: (b * nq + q, g)),
        out_shape=jax.ShapeDtypeStruct((T, N_HEADS * HEAD_DIM), BF16),
        scratch_shapes=[pltpu.VMEM((seq, 2 * LANES), BF16), pltpu.VMEM((seq, LANES), BF16),
                        pltpu.VMEM((tq // sub, 4 * sub, tk), F32), pltpu.VMEM((tq // sub, 4 * sub, tk), F32),
                        pltpu.VMEM((tq // sub, 4 * sub, LANES), F32),
                        pltpu.VMEM((tq // sub, 4 * sub, LANES), F32)],
        compiler_params=_cparams("parallel", "arbitrary", "arbitrary"),
        name="nsa_sel_attn",
    )(qk, qk, vv, selbias)


def _nsa_win_attn_kernel(q_ref, k_ref, v_ref, o_ref, kb_ref, vaug_ref, *, tq, span):
    g = pl.program_id(1)
    qi = pl.program_id(2)

    @pl.when(qi == 0)
    def _():
        kb_ref[...] = _both_halves(_group_lower(k_ref, g)).astype(BF16)
        vaug_ref[...] = _with_ones(_group_lower(v_ref, g)).astype(BF16)

    q4 = _stack_heads(q_ref)
    q0 = qi * tq
    k_start = pl.multiple_of(jnp.maximum(q0 - NSA_WINDOW, 0), tq)
    keys = pl.ds(k_start, span)
    dist = (q0 - k_start) + (lax.broadcasted_iota(jnp.int32, (tq, span), 0)
                             - lax.broadcasted_iota(jnp.int32, (tq, span), 1))
    ok = (dist >= 0) & (dist < NSA_WINDOW)
    ok2 = jnp.concatenate([ok, ok], axis=0)
    pairs = [q4[0:2 * tq], q4[2 * tq:4 * tq]]
    s = [lax.dot_general(qp, kb_ref[keys, :], _NT, preferred_element_type=F32) for qp in pairs]
    s = [jnp.where(ok2, sc, NEG_INF) for sc in s]
    p = [jnp.exp(sc - jnp.max(sc, axis=1, keepdims=True)).astype(BF16) for sc in s]
    acc = [jnp.dot(pc, vaug_ref[keys, :], preferred_element_type=F32) for pc in p]
    acc = jnp.concatenate(acc, axis=0)
    o_ref[...] = _unstack_heads(_normalise(acc), tq).astype(o_ref.dtype)


def _nsa_win_attn(qk, vv, *, batch, seq, k_col, v_col):
    T = qk.shape[0]
    G = NSA_KV_HEADS
    tq = _pick_tile(seq, 256)
    nq = seq // tq
    span = NSA_WINDOW + tq
    assert NSA_WINDOW % tq == 0 and seq >= span
    return pl.pallas_call(
        functools.partial(_nsa_win_attn_kernel, tq=tq, span=span),
        grid=(batch, G, nq),
        in_specs=[pl.BlockSpec((tq, 2 * LANES), lambda b, g, q: (b * nq + q, g)),
                  pl.BlockSpec((seq, LANES), lambda b, g, q: (b, k_col + g // 2)),
                  pl.BlockSpec((seq, LANES), lambda b, g, q: (b, v_col + g // 2))],
        out_specs=pl.BlockSpec((tq, 2 * LANES), lambda b, g, q: (b * nq + q, g)),
        out_shape=jax.ShapeDtypeStruct((T, N_HEADS * HEAD_DIM), BF16),
        scratch_shapes=[pltpu.VMEM((seq, LANES), BF16), pltpu.VMEM((seq, LANES), BF16)],
        compiler_params=_cparams("parallel", "arbitrary", "arbitrary"),
        name="nsa_win_attn",
    )(qk, qk, vv)


def _nsa_mixer(x, g, w_in, pos_k, pos_v, w_cmp_k, w_cmp_v, g_q, g_k, w_out, *, batch, seq):
    HD = N_HEADS * HEAD_DIM
    G, dh = NSA_KV_HEADS, HEAD_DIM
    kvw = G * dh
    scale = dh ** -0.5
    col = lambda n: slice(HD + n * kvw, HD + (n + 1) * kvw)
    order = [slice(0, HD), col(0), col(2), col(4), col(1), col(3), col(5)]
    w = jnp.concatenate([w_in[:, c] for c in order], axis=1).astype(BF16)
    n_gate = 3 * N_HEADS
    w_gate = jnp.pad(w_in[:, HD + 6 * kvw:], ((0, 0), (0, LANES - n_gate))).astype(BF16)
    head_gain = jnp.concatenate([jnp.tile(g_q * scale, N_HEADS), jnp.tile(g_k, 3 * G)])

    inv = ROPE_THETA ** (-jnp.arange(0, dh, 2, dtype=F32) / dh)
    ang = jnp.arange(seq, dtype=F32)[:, None] * inv[None, :]
    cos, sin = jnp.cos(ang), jnp.sin(ang)
    cos_t = jnp.tile(cos, (1, PROJ_CHUNK // (dh // 2)))
    sin_t = jnp.tile(jnp.concatenate([-sin, sin], axis=1), (1, PROJ_CHUNK // dh))

    qkv, gate = _proj(x, g, w, seq=seq, head_gain=head_gain, rope=(cos_t, sin_t), w_aux=w_gate)

    def blocks16(t):
        half = NSA_CMP_LEN // 2
        t = t.reshape(batch, seq // half, half, G, dh).transpose(0, 3, 1, 2, 4)
        return t.reshape(batch, G, seq // half, half * dh)

    kc0, vc0 = HD, HD + 3 * kvw
    kcmp, vcmp = _nsa_compress(blocks16(qkv[:, kc0:kc0 + kvw]), blocks16(qkv[:, vc0:vc0 + kvw]),
                               pos_k, pos_v, w_cmp_k, w_cmp_v, g_k)
    o_c, selbias = _nsa_cmp_attn(qkv, kcmp, vcmp, batch=batch, seq=seq)
    o_s = _nsa_sel_attn(qkv, qkv, selbias, batch=batch, seq=seq,
                        k_col=(HD + kvw) // LANES, v_col=(HD + 4 * kvw) // LANES)
    o_w = _nsa_win_attn(qkv, qkv, batch=batch, seq=seq,
                        k_col=(HD + 2 * kvw) // LANES, v_col=(HD + 5 * kvw) // LANES)
    r = np.arange(LANES)[:, None]
    c = np.arange(HD)[None, :] // dh
    expand = np.stack([(r == 3 * c + b) for b in range(3)]).astype(np.float32)
    return _outproj(x, w_out.astype(BF16), mode="gated3",
                    acts=[o_c, o_s, o_w, gate, jnp.asarray(expand, BF16)])


def kernel(x, norm_g, ffn1_w_gate, ffn1_w_up, ffn1_w_down, ffn2_w_gate, ffn2_w_up, ffn2_w_down, fox_w_in, fox_b_f, fox_g_q, fox_g_k, fox_w_out, nsa_w_in, nsa_cmp_pos_k, nsa_cmp_pos_v, nsa_w_cmp_k, nsa_w_cmp_v, nsa_g_q, nsa_g_k, nsa_w_out, gla_w_in, gla_w_gate_up, gla_b_gate, gla_g_out, gla_w_out, sb_w_in, sb_w_out):
    B, S, D = x.shape
    depth = norm_g.shape[0]
    n_mixers = 4
    x2 = x.reshape(B * S, D)
    for i in range(depth):
        m, j = i % n_mixers, i // n_mixers
        x2 = _ffn(x2, norm_g[i, 0], ffn1_w_gate[i].astype(BF16), ffn1_w_up[i].astype(BF16),
                  ffn1_w_down[i].astype(BF16))
        g = norm_g[i, 1]
        if m == 0:
            x2 = _fox_mixer(x2, g, fox_w_in[j], fox_b_f[j], fox_g_q[j], fox_g_k[j], fox_w_out[j], batch=B, seq=S)
        elif m == 1:
            x2 = _nsa_mixer(x2, g, nsa_w_in[j], nsa_cmp_pos_k[j], nsa_cmp_pos_v[j], nsa_w_cmp_k[j],
                            nsa_w_cmp_v[j], nsa_g_q[j], nsa_g_k[j], nsa_w_out[j], batch=B, seq=S)
        elif m == 2:
            x2 = _gla_mixer(x2, g, gla_w_in[j], gla_w_gate_up[j], gla_b_gate[j], gla_g_out[j], gla_w_out[j],
                            batch=B, seq=S)
        else:
            x2 = _sb_mixer(x2, g, sb_w_in[j], sb_w_out[j], batch=B, seq=S)
        x2 = _ffn(x2, norm_g[i, 2], ffn2_w_gate[i].astype(BF16), ffn2_w_up[i].astype(BF16),
                  ffn2_w_down[i].astype(BF16))
    return x2.reshape(B, S, D)
```

```python
import functools

import numpy as np
import jax
import jax.numpy as jnp
from jax import lax
from jax.experimental import pallas as pl
from jax.experimental.pallas import tpu as pltpu

F32 = jnp.float32
BF16 = jnp.bfloat16

N_HEADS = 16
HEAD_DIM = 64
ROPE_THETA = 10000.0
RMS_EPS = 1e-6
NEG_INF = -1e30
NSA_KV_HEADS = 4
NSA_CMP_LEN = 32
NSA_CMP_STRIDE = 16
NSA_SEL_LEN = 64
NSA_N_SEL = 16
NSA_WINDOW = 512
NSA_FORCED_SCORE = 1e9
GLA_HEADS = 4
GLA_GATE_RANK = 16
GLA_TAU = 16.0
GLA_CHUNK = 64

LANES = 128
V7X_VMEM_BYTES = 64 * 1024 * 1024
VMEM_LIMIT = V7X_VMEM_BYTES - 8 * 1024 * 1024

_NT = (((1,), (1,)), ((), ()))
_TN = (((0,), (0,)), ((), ()))


def _cparams(*sem):
    return pltpu.CompilerParams(dimension_semantics=sem, vmem_limit_bytes=VMEM_LIMIT)


def _rms_rows(x, g):
    return x * lax.rsqrt(jnp.mean(x * x, axis=-1, keepdims=True) + RMS_EPS) * g


def _softplus(z):
    return jnp.maximum(z, 0.0) + jnp.log(1.0 + jnp.exp(-jnp.abs(z)))


def _neg_abs(z):
    bits = lax.bitcast_convert_type(z, jnp.int32) | jnp.int32(-2 ** 31)
    return lax.bitcast_convert_type(bits, F32)


def _sigmoid(z):
    return 1.0 / (1.0 + jnp.exp(-z))


def _pick_tile(n, target):
    t = min(n, target)
    while n % t:
        t //= 2
    return t


def _ffn_kernel(x_ref, g_ref, wg_ref, wu_ref, wd_ref, o_ref, a_ref, *, tf):
    x = x_ref[...]
    h = _rms_rows(x, g_ref[...]).astype(BF16)
    for c in range(a_ref.shape[1] // tf):
        cols = slice(c * tf, (c + 1) * tf)
        gate = jnp.dot(h, wg_ref[:, cols], preferred_element_type=F32)
        up = jnp.dot(h, wu_ref[:, cols], preferred_element_type=F32)
        a_ref[:, cols] = (gate * _sigmoid(gate) * up).astype(BF16)
    o_ref[...] = x + 0.5 * jnp.dot(a_ref[...], wd_ref[...], preferred_element_type=F32)


def _resident(shape):
    return pl.BlockSpec(shape, lambda *_: (0,) * len(shape), pipeline_mode=pl.Buffered(1))


def _ffn(x, g, wg, wu, wd):
    T, D = x.shape
    F = wg.shape[1]
    tm = _pick_tile(T, 1024)
    tf = 256 if F % 256 == 0 else F
    return pl.pallas_call(
        functools.partial(_ffn_kernel, tf=tf),
        grid=(T // tm,),
        in_specs=[
            pl.BlockSpec((tm, D), lambda i: (i, 0)),
            _resident((1, D)),
            _resident((D, F)),
            _resident((D, F)),
            _resident((F, D)),
        ],
        out_specs=pl.BlockSpec((tm, D), lambda i: (i, 0)),
        out_shape=jax.ShapeDtypeStruct((T, D), F32),
        scratch_shapes=[pltpu.VMEM((tm, F), BF16)],
        compiler_params=_cparams("parallel"),
        name="ffn",
    )(x, g.reshape(1, D), wg, wu, wd)


PROJ_CHUNK = 256


def _proj_kernel(*refs, n_norm, rope, has_aux):
    it = iter(refs)
    x_ref, g_ref, w_ref = next(it), next(it), next(it)
    hg_ref, gsum_ref = (next(it), next(it)) if n_norm else (None, None)
    cos_ref, sin_ref = (next(it), next(it)) if rope else (None, None)
    waux_ref = next(it) if has_aux else None
    o_ref = next(it)
    ch = PROJ_CHUNK
    h = _rms_rows(x_ref[...], g_ref[...]).astype(BF16)
    for c in range(o_ref.shape[1] // ch):
        cols = slice(c * ch, (c + 1) * ch)
        y = jnp.dot(h, w_ref[:, cols], preferred_element_type=F32)
        if c * ch < n_norm:
            ss = jnp.dot((y * y).astype(BF16), gsum_ref[...], preferred_element_type=F32)
            y = y * lax.rsqrt(ss * (1.0 / HEAD_DIM) + RMS_EPS) * hg_ref[:, cols]
            if rope:
                lane = lax.broadcasted_iota(jnp.int32, (1, ch), 1)
                first_half = (lane % HEAD_DIM) < (HEAD_DIM // 2)
                partner = jnp.where(first_half,
                                    pltpu.roll(y, ch - HEAD_DIM // 2, 1),
                                    pltpu.roll(y, HEAD_DIM // 2, 1))
                y = y * cos_ref[...] + partner * sin_ref[...]
        o_ref[:, cols] = y.astype(o_ref.dtype)
    if has_aux:
        next(it)[...] = jnp.dot(h, waux_ref[...], preferred_element_type=F32)


def _proj(x, g, w, *, seq, head_gain=None, rope=None, w_aux=None):
    T, D = x.shape
    N = w.shape[1]
    tm = _pick_tile(seq, 1024)
    ch = PROJ_CHUNK
    n_norm = 0 if head_gain is None else head_gain.shape[0]
    assert N % ch == 0 and n_norm % ch == 0 and T % tm == 0
    args = [x, g.reshape(1, D), w]
    in_specs = [pl.BlockSpec((tm, D), lambda i: (i, 0)), _resident((1, D)), _resident((D, N))]
    if n_norm:
        gidx = np.arange(ch) // HEAD_DIM
        gsum = jnp.asarray((gidx[:, None] == gidx[None, :]).astype(np.float32), BF16)
        args += [head_gain.reshape(1, n_norm), gsum]
        in_specs += [_resident((1, n_norm)), _resident((ch, ch))]
    if rope is not None:
        nblk = seq // tm
        args += list(rope)
        in_specs += [pl.BlockSpec((tm, ch), lambda i: (i % nblk, 0))] * 2
    out_specs = [pl.BlockSpec((tm, N), lambda i: (i, 0))]
    out_shape = [jax.ShapeDtypeStruct((T, N), BF16)]
    if w_aux is not None:
        args.append(w_aux)
        in_specs.append(_resident(w_aux.shape))
        out_specs.append(pl.BlockSpec((tm, w_aux.shape[1]), lambda i: (i, 0)))
        out_shape.append(jax.ShapeDtypeStruct((T, w_aux.shape[1]), F32))
    out = pl.pallas_call(
        functools.partial(_proj_kernel, n_norm=n_norm, rope=rope is not None, has_aux=w_aux is not None),
        grid=(T // tm,),
        in_specs=in_specs,
        out_specs=out_specs,
        out_shape=out_shape,
        compiler_params=_cparams("parallel"),
        name="proj",
    )(*args)
    return out if w_aux is not None else out[0]


def _outproj_kernel(*refs, mode):
    if mode == "plain":
        x_ref, a_ref, w_ref, o_ref = refs
        a = a_ref[...]
    elif mode == "sigmoid_gate":
        x_ref, a_ref, og_ref, w_ref, o_ref = refs
        a = (a_ref[...].astype(F32) * _sigmoid(og_ref[...].astype(F32))).astype(BF16)
    else:
        x_ref, oc_ref, os_ref, ow_ref, gate_ref, exp_ref, w_ref, o_ref = refs
        gs = _sigmoid(gate_ref[...]).astype(BF16)
        a = None
        for c, o_c in enumerate((oc_ref, os_ref, ow_ref)):
            ge = jnp.dot(gs, exp_ref[c], preferred_element_type=F32)
            t = ge * o_c[...].astype(F32)
            a = t if a is None else a + t
        a = a.astype(BF16)
    o_ref[...] = x_ref[...] + jnp.dot(a, w_ref[...], preferred_element_type=F32)


def _outproj(x, w, *, mode, acts):
    T, D = x.shape
    K = w.shape[0]
    tm = _pick_tile(T, 512)
    row = lambda i: (i, 0)
    args = [x]
    in_specs = [pl.BlockSpec((tm, D), row)]
    for a in acts:
        if a.ndim == 3:
            args.append(a)
            in_specs.append(pl.BlockSpec(a.shape, lambda i: (0, 0, 0)))
        else:
            args.append(a)
            in_specs.append(pl.BlockSpec((tm, a.shape[1]), row))
    args.append(w)
    in_specs.append(pl.BlockSpec((K, D), lambda i: (0, 0)))
    return pl.pallas_call(
        functools.partial(_outproj_kernel, mode=mode),
        grid=(T // tm,),
        in_specs=in_specs,
        out_specs=pl.BlockSpec((tm, D), row),
        out_shape=jax.ShapeDtypeStruct((T, D), F32),
        compiler_params=_cparams("parallel"),
        name="outproj_" + mode,
    )(*args)


def _fox_c_kernel(x_ref, g_ref, wf_ref, bf_ref, tri_ref, c_ref, carry_ref):
    @pl.when(pl.program_id(1) == 0)
    def _():
        carry_ref[...] = jnp.zeros_like(carry_ref)

    h = _rms_rows(x_ref[...], g_ref[...])
    f = lax.dot_general(wf_ref[...], h, _NT, precision=lax.Precision.HIGHEST,
                        preferred_element_type=F32) + bf_ref[...]
    ls = jnp.minimum(f, 0.0) - jnp.log(1.0 + jnp.exp(-jnp.abs(f)))
    cs = jnp.dot(ls, tri_ref[...], precision=lax.Precision.HIGHEST,
                 preferred_element_type=F32) + carry_ref[...]
    c_ref[...] = cs
    carry_ref[...] = cs[:, cs.shape[1] - 1:]


def _fox_c(x, g, wf_t, b_f, *, batch, seq):
    T, D = x.shape
    H = wf_t.shape[0]
    tm = _pick_tile(seq, 512)
    ns = seq // tm
    tri = jnp.asarray(np.triu(np.ones((tm, tm), np.float32)))
    return pl.pallas_call(
        _fox_c_kernel,
        grid=(batch, ns),
        in_specs=[
            pl.BlockSpec((tm, D), lambda b, s: (b * ns + s, 0)),
            pl.BlockSpec((1, D), lambda b, s: (0, 0)),
            pl.BlockSpec((H, D), lambda b, s: (0, 0)),
            pl.BlockSpec((H, 1), lambda b, s: (0, 0)),
            pl.BlockSpec((tm, tm), lambda b, s: (0, 0)),
        ],
        out_specs=pl.BlockSpec((None, H, tm), lambda b, s: (b, 0, s)),
        out_shape=jax.ShapeDtypeStruct((batch, H, seq), F32),
        scratch_shapes=[pltpu.VMEM((H, 1), F32)],
        compiler_params=_cparams("parallel", "arbitrary"),
        name="fox_c",
    )(x, g.reshape(1, D), wf_t, b_f.reshape(H, 1), tri)


def _fox_attn_kernel(q_ref, k_ref, v_ref, c_ref, o_ref, vaug_ref, sa_ref, sb_ref, m_ref, acc_ref, *, t):
    qi = pl.program_id(2)
    lane = lax.broadcasted_iota(jnp.int32, (1, LANES), 1)
    upper = lane >= HEAD_DIM

    @pl.when(qi == 0)
    def _():
        v = v_ref[...]
        one = jnp.ones_like(v)
        vaug_ref[0] = jnp.where(upper, one, v)
        vaug_ref[1] = jnp.where(upper, v, one)

    q0 = pl.multiple_of(qi * t, t)
    qslab = q_ref[...]
    zero = jnp.zeros_like(qslab)
    qpos = lax.broadcasted_iota(jnp.int32, (t, t), 0)
    kpos = lax.broadcasted_iota(jnp.int32, (t, t), 1)
    causal = kpos <= qpos

    heads = range(2)
    q2 = jnp.concatenate([jnp.where(upper, zero, qslab), jnp.where(upper, qslab, zero)], axis=0)
    c_q0 = [c_ref[i, :, pl.ds(q0, LANES)][:, 0:1] for i in heads]

    def produce(s_ref, j):
        keys = pl.ds(pl.multiple_of(j * t, t), t)
        s2 = lax.dot_general(q2, k_ref[keys, :], _NT, preferred_element_type=F32)
        for i in heads:
            s_ref[i] = s2[i * t:(i + 1) * t] + (c_q0[i] - c_ref[i, :, keys])

    def consume(s_ref, j, masked):
        keys = pl.ds(pl.multiple_of(j * t, t), t)
        score = (lambda i: jnp.where(causal, s_ref[i], NEG_INF)) if masked else (lambda i: s_ref[i])
        m_old = [m_ref[i] for i in heads]
        m_new = [jnp.maximum(m_old[i], jnp.max(score(i), axis=1, keepdims=True)) for i in heads]
        p = [jnp.exp(score(i) - jnp.tile(m_new[i], (1, t // LANES))).astype(BF16) for i in heads]
        for i in heads:
            m_ref[i] = m_new[i]
            acc_ref[i] = (jnp.exp(m_old[i] - m_new[i]) * acc_ref[i]
                          + jnp.dot(p[i], vaug_ref[i, keys, :], preferred_element_type=F32))

    m_ref[...] = jnp.full(m_ref.shape, NEG_INF, F32)
    acc_ref[...] = jnp.zeros(acc_ref.shape, F32)
    produce(sa_ref, 0)

    def body(jj, _):
        produce(sb_ref, 2 * jj + 1)
        consume(sa_ref, 2 * jj, False)
        produce(sa_ref, 2 * jj + 2)
        consume(sb_ref, 2 * jj + 1, False)
        return 0

    lax.fori_loop(0, qi // 2, body, 0)

    @pl.when(qi % 2 == 0)
    def _():
        consume(sa_ref, qi, True)

    @pl.when(qi % 2 == 1)
    def _():
        produce(sb_ref, qi)
        consume(sa_ref, qi - 1, False)
        consume(sb_ref, qi, True)

    accs = (acc_ref[0], acc_ref[1])
    num = jnp.where(upper, accs[1], accs[0])
    den = jnp.where(upper, accs[0], accs[1])
    den = pltpu.roll(den, HEAD_DIM, 1)
    o_ref[...] = (num / den).astype(o_ref.dtype)


def _fox_attn(qkvg, c, *, batch, seq):
    T = qkvg.shape[0]
    HD = N_HEADS * HEAD_DIM
    t = _pick_tile(seq, 512)
    nq = seq // t
    npair = HD // LANES
    return pl.pallas_call(
        functools.partial(_fox_attn_kernel, t=t),
        grid=(batch, npair, nq),
        in_specs=[
            pl.BlockSpec((t, LANES), lambda b, h, q: (b * nq + q, h)),
            pl.BlockSpec((seq, LANES), lambda b, h, q: (b, npair + h)),
            pl.BlockSpec((seq, LANES), lambda b, h, q: (b, 2 * npair + h)),
            pl.BlockSpec((None, 2, 1, seq), lambda b, h, q: (b, h, 0, 0)),
        ],
        out_specs=pl.BlockSpec((t, LANES), lambda b, h, q: (b * nq + q, h)),
        out_shape=jax.ShapeDtypeStruct((T, HD), BF16),
        scratch_shapes=[pltpu.VMEM((2, seq, LANES), BF16),
                        pltpu.VMEM((2, t, t), F32), pltpu.VMEM((2, t, t), F32),
                        pltpu.VMEM((2, t, LANES), F32), pltpu.VMEM((2, t, LANES), F32)],
        compiler_params=_cparams("parallel", "arbitrary", "arbitrary"),
        name="fox_attn",
    )(qkvg, qkvg, qkvg, c.reshape(batch, N_HEADS, 1, seq))


def _fox_mixer(x, g, w_in, b_f, g_q, g_k, w_out, *, batch, seq):
    HD = N_HEADS * HEAD_DIM
    scale = HEAD_DIM ** -0.5
    w = jnp.concatenate([w_in[:, :3 * HD], w_in[:, 3 * HD + N_HEADS:]], axis=1).astype(BF16)
    wf_t = w_in[:, 3 * HD:3 * HD + N_HEADS].T
    head_gain = jnp.concatenate([jnp.tile(g_q * scale, N_HEADS), jnp.tile(g_k, N_HEADS)])
    qkvg = _proj(x, g, w, seq=seq, head_gain=head_gain)
    c = _fox_c(x, g, wf_t, b_f, batch=batch, seq=seq)
    o = _fox_attn(qkvg, c, batch=batch, seq=seq)
    return _outproj_fox(x, o, qkvg, w_out.astype(BF16))


def _outproj_fox(x, o, qkvg, w):
    T, D = x.shape
    HD = o.shape[1]
    tm = _pick_tile(T, 512)
    return pl.pallas_call(
        functools.partial(_outproj_kernel, mode="sigmoid_gate"),
        grid=(T // tm,),
        in_specs=[
            pl.BlockSpec((tm, D), lambda i: (i, 0)),
            pl.BlockSpec((tm, HD), lambda i: (i, 0)),
            pl.BlockSpec((tm, HD), lambda i: (i, 3)),
            pl.BlockSpec((HD, D), lambda i: (0, 0)),
        ],
        out_specs=pl.BlockSpec((tm, D), lambda i: (i, 0)),
        out_shape=jax.ShapeDtypeStruct((T, D), F32),
        compiler_params=_cparams("parallel"),
        name="outproj_fox",
    )(x, o, qkvg, w)


def _sb_attn_kernel(q_ref, k_ref, v_ref, tri_ref, o_ref, wa_ref, wb_ref, ta_ref, tb_ref, rest_ref, acc_ref,
                    *, tq, tk):
    qi = pl.program_id(2)
    lane = lax.broadcasted_iota(jnp.int32, (1, LANES), 1)
    upper = lane >= HEAD_DIM
    qslab = q_ref[...]
    zero = jnp.zeros_like(qslab)
    q2 = jnp.concatenate([jnp.where(upper, zero, qslab), jnp.where(upper, qslab, zero)], axis=0)
    rows = 2 * tq
    qpos = lax.broadcasted_iota(jnp.int32, (rows, tk), 0) % tq
    kpos = lax.broadcasted_iota(jnp.int32, (rows, tk), 1)
    last = 2 * qi + 1

    def key_rows(k):
        return pl.ds(pl.multiple_of(jnp.maximum(last - k, 0) * tk, tk), tk)

    def scores(k):
        return lax.dot_general(q2, k_ref[key_rows(k), :], _NT, preferred_element_type=F32)

    def softplus_sums(z, w_ref, tot_ref, diag_offset):
        sp = jnp.maximum(z, 0.0) + jnp.log2(1.0 + jnp.exp2(_neg_abs(z)))
        if diag_offset is not None:
            strict = kpos + diag_offset < qpos
            spm = jnp.where(strict, sp, 0.0)
        else:
            spm = sp
        cum = jnp.dot(spm.astype(BF16), tri_ref[...], preferred_element_type=F32)
        w = (z - sp) - cum
        if diag_offset is not None:
            w = jnp.where(strict, w, NEG_INF)
        w_ref[...] = w
        tot_ref[...] = jnp.broadcast_to(cum[:, 0:1] + spm[:, 0:1], (rows, LANES))

    def accumulate(w_ref, tot_ref, k):
        a = jnp.exp2(w_ref[...] - jnp.tile(rest_ref[...], (1, tk // LANES))).astype(BF16)
        acc_ref[...] += jnp.dot(a, v_ref[key_rows(k), :], preferred_element_type=F32)
        rest_ref[...] += tot_ref[...]

    assert tq == 2 * tk
    even, odd = (wa_ref, ta_ref), (wb_ref, tb_ref)

    def produce(buf, k, diag_offset):
        softplus_sums(scores(k), *buf, diag_offset)

    def step(k, cur, nxt):
        z = scores(k + 1)
        accumulate(*cur, k)
        softplus_sums(z, *nxt, None)

    rest_ref[...] = jnp.zeros(rest_ref.shape, F32)
    acc_ref[...] = jnp.zeros(acc_ref.shape, F32)
    produce(even, 0, tk)
    produce(odd, 1, 0)
    accumulate(wa_ref, ta_ref, 0)

    def body(m, _):
        step(2 * m - 1, odd, even)
        step(2 * m, even, odd)
        return 0

    lax.fori_loop(1, qi + 1, body, 0)
    accumulate(wb_ref, tb_ref, last)
    o_ref[...] = jnp.where(upper, acc_ref[tq:2 * tq], acc_ref[0:tq]).astype(o_ref.dtype)


def _sb_attn(qkv, *, batch, seq):
    T = qkv.shape[0]
    HD = N_HEADS * HEAD_DIM
    tq = _pick_tile(seq, 512)
    tk = 256
    nq = seq // tq
    npair = HD // LANES
    tri = jnp.asarray(np.tril(np.ones((tk, tk), np.float32), -1), BF16)
    return pl.pallas_call(
        functools.partial(_sb_attn_kernel, tq=tq, tk=tk),
        grid=(batch, npair, nq),
        in_specs=[
            pl.BlockSpec((tq, LANES), lambda b, h, q: (b * nq + q, h)),
            pl.BlockSpec((seq, LANES), lambda b, h, q: (b, npair + h)),
            pl.BlockSpec((seq, LANES), lambda b, h, q: (b, 2 * npair + h)),
            pl.BlockSpec((tk, tk), lambda b, h, q: (0, 0)),
        ],
        out_specs=pl.BlockSpec((tq, LANES), lambda b, h, q: (b * nq + q, h)),
        out_shape=jax.ShapeDtypeStruct((T, HD), BF16),
        scratch_shapes=[pltpu.VMEM((2 * tq, tk), F32)] * 2 + [pltpu.VMEM((2 * tq, LANES), F32)] * 4,
        compiler_params=_cparams("parallel", "parallel", "arbitrary"),
        name="sb_attn",
    )(qkv, qkv, qkv, tri)


def _sb_mixer(x, g, w_in, w_out, *, batch, seq):
    HD = N_HEADS * HEAD_DIM
    scale = HEAD_DIM ** -0.5 * float(np.log2(np.e))
    w = jnp.concatenate([w_in[:, :HD] * scale, w_in[:, HD:]], axis=1).astype(BF16)
    qkv = _proj(x, g, w, seq=seq)
    o = _sb_attn(qkv, batch=batch, seq=seq)
    return _outproj(x, w_out.astype(BF16), mode="plain", acts=[o])


def _gla_kernel(q_ref, k_ref, v_ref, r_ref, glow_ref, wgu_ref, bg_ref, gout_ref, tri_ref,
                o_ref, state_ref, *, tm, dk):
    C = GLA_CHUNK

    @pl.when(pl.program_id(2) == 0)
    def _():
        state_ref[...] = jnp.zeros_like(state_ref)

    gate = jnp.dot(glow_ref[...], wgu_ref[...], precision=lax.Precision.HIGHEST,
                   preferred_element_type=F32) + bg_ref[...]
    log_a = (jnp.minimum(gate, 0.0) - jnp.log(1.0 + jnp.exp(-jnp.abs(gate)))) * (1.0 / GLA_TAU)
    row = lax.broadcasted_iota(jnp.int32, (C, C), 0)
    col = lax.broadcasted_iota(jnp.int32, (C, C), 1)
    causal = col <= row
    q_scale = dk ** -0.5
    chunks = [slice(c * C, (c + 1) * C) for c in range(tm // C)]

    tri = tri_ref[...]
    b = [jnp.dot(tri, log_a[sl], precision=lax.Precision.HIGHEST, preferred_element_type=F32)
         for sl in chunks]
    b_last = jnp.concatenate([jnp.broadcast_to(bc[C - 1:C], bc.shape) for bc in b], axis=0)
    b = jnp.concatenate(b, axis=0)
    q = q_ref[...].astype(F32) * q_scale
    k = k_ref[...].astype(F32)
    q_dec = (q * jnp.exp(b)).astype(BF16)
    k_neg = (k * jnp.exp(-b)).astype(BF16)
    k_rem = (k * jnp.exp(b_last - b)).astype(BF16)
    decay = jnp.exp(b_last)
    attn = [lax.dot_general(q_dec[sl], k_neg[sl], _NT, preferred_element_type=F32) for sl in chunks]
    attn = [jnp.where(causal, a, 0.0).astype(BF16) for a in attn]
    o_intra = [jnp.dot(attn[c], v_ref[sl, :], preferred_element_type=F32) for c, sl in enumerate(chunks)]
    u_t = [lax.dot_general(v_ref[sl, :], k_rem[sl], _TN, preferred_element_type=F32) for sl in chunks]

    state_t = state_ref[...]
    outs = []
    for c, sl in enumerate(chunks):
        outs.append(o_intra[c] + lax.dot_general(q_dec[sl], state_t.astype(BF16), _NT,
                                                 preferred_element_type=F32))
        state_t = state_t * decay[c * C:c * C + 1] + u_t[c]
    state_ref[...] = state_t

    o = jnp.concatenate(outs, axis=0)
    y = o * lax.rsqrt(jnp.mean(o * o, axis=-1, keepdims=True) + RMS_EPS) * gout_ref[...]
    r = r_ref[...].astype(F32)
    o_ref[...] = (y * (r * _sigmoid(r))).astype(o_ref.dtype)


def _gla(qkvr, glow, wgu, b_gate, g_out, *, batch, seq):
    T = qkvr.shape[0]
    Hg = GLA_HEADS
    dk_total = wgu.shape[1]
    dk = dk_total // Hg
    dv = (qkvr.shape[1] - 2 * dk_total) // 2 // Hg
    tm = _pick_tile(seq, 512)
    ns = seq // tm
    kq, kv = dk_total // dk, dk_total // dv
    tri = jnp.asarray(np.tril(np.ones((GLA_CHUNK, GLA_CHUNK), np.float32)))
    rowmap = lambda off: (lambda b, h, s: (b * ns + s, off + h))
    return pl.pallas_call(
        functools.partial(_gla_kernel, tm=tm, dk=dk),
        grid=(batch, Hg, ns),
        in_specs=[
            pl.BlockSpec((tm, dk), rowmap(0)),
            pl.BlockSpec((tm, dk), rowmap(kq)),
            pl.BlockSpec((tm, dv), rowmap(2 * kv)),
            pl.BlockSpec((tm, dv), rowmap(2 * kv + Hg)),
            pl.BlockSpec((tm, LANES), lambda b, h, s: (b * ns + s, 0)),
            pl.BlockSpec((LANES, dk), lambda b, h, s: (0, h)),
            pl.BlockSpec((1, dk), lambda b, h, s: (0, h)),
            pl.BlockSpec((1, dv), lambda b, h, s: (0, 0)),
            pl.BlockSpec((GLA_CHUNK, GLA_CHUNK), lambda b, h, s: (0, 0)),
        ],
        out_specs=pl.BlockSpec((tm, dv), rowmap(0)),
        out_shape=jax.ShapeDtypeStruct((T, Hg * dv), BF16),
        scratch_shapes=[pltpu.VMEM((dv, dk), F32)],
        compiler_params=_cparams("parallel", "parallel", "arbitrary"),
        name="gla",
    )(qkvr, qkvr, qkvr, qkvr, glow, wgu, b_gate.reshape(1, dk_total), g_out.reshape(1, dv), tri)


def _gla_mixer(x, g, w_in, w_gate_up, b_gate, g_out, w_out, *, batch, seq):
    dk_total = w_gate_up.shape[1]
    dv_total = w_out.shape[0]
    lo = 2 * dk_total + dv_total
    w_main = jnp.concatenate([w_in[:, :lo], w_in[:, lo + GLA_GATE_RANK:]], axis=1).astype(BF16)
    w_low = jnp.pad(w_in[:, lo:lo + GLA_GATE_RANK], ((0, 0), (0, LANES - GLA_GATE_RANK))).astype(BF16)
    wgu = jnp.pad(w_gate_up, ((0, LANES - GLA_GATE_RANK), (0, 0)))
    qkvr, glow = _proj(x, g, w_main, seq=seq, w_aux=w_low)
    o = _gla(qkvr, glow, wgu, b_gate, g_out, batch=batch, seq=seq)
    return _outproj(x, w_out.astype(BF16), mode="plain", acts=[o])


def _nsa_compress_kernel(kc_ref, vc_ref, pk_ref, pv_ref, wk_ref, wv_ref, gk_ref, ko_ref, vo_ref, *, n16):
    def compress(r_ref, pos_ref, w_ref):
        r = r_ref[...].astype(F32)
        first = jnp.dot((r + pos_ref[0:1, :]).astype(BF16), w_ref[0], preferred_element_type=F32)
        second = jnp.dot((r + pos_ref[1:2, :]).astype(BF16), w_ref[1], preferred_element_type=F32)
        return first + pltpu.roll(second, n16 - 1, 0)

    kc = compress(kc_ref, pk_ref, wk_ref)
    ko_ref[...] = _rms_rows(kc, gk_ref[...]).astype(ko_ref.dtype)
    vo_ref[...] = compress(vc_ref, pv_ref, wv_ref).astype(vo_ref.dtype)


def _nsa_compress(kc_r, vc_r, pos_k, pos_v, w_k, w_v, g_k):
    B, G, n16, width = kc_r.shape
    dh = HEAD_DIM
    half = NSA_CMP_LEN // 2
    dup = lambda w: jnp.concatenate([w, w], axis=-1).reshape(2, half * dh, 2 * dh).astype(BF16)
    spec_r = pl.BlockSpec((None, None, n16, width), lambda b, g: (b, g, 0, 0))
    spec_o = pl.BlockSpec((None, None, n16, 2 * dh), lambda b, g: (b, g, 0, 0))
    const2 = lambda shape: pl.BlockSpec(shape, lambda b, g: (0,) * len(shape))
    out = jax.ShapeDtypeStruct((B, G, n16, 2 * dh), BF16)
    return pl.pallas_call(
        functools.partial(_nsa_compress_kernel, n16=n16),
        grid=(B, G),
        in_specs=[spec_r, spec_r, const2((2, width)), const2((2, width)),
                  const2((2, width, 2 * dh)), const2((2, width, 2 * dh)), const2((1, 2 * dh))],
        out_specs=[spec_o, spec_o],
        out_shape=[out, out],
        compiler_params=_cparams("parallel", "parallel"),
        name="nsa_compress",
    )(kc_r, vc_r, pos_k.reshape(2, width), pos_v.reshape(2, width), dup(w_k), dup(w_v),
      jnp.tile(g_k, 2).reshape(1, 2 * dh))


def _stack_heads(q_ref, extra=None, rows_in=slice(None)):
    lane = lax.broadcasted_iota(jnp.int32, (1, LANES), 1)
    upper = lane >= HEAD_DIM
    rows = []
    for j in range(2):
        slab = q_ref[rows_in, j * LANES:(j + 1) * LANES]
        zero = jnp.zeros_like(slab)
        for i in range(2):
            qh = jnp.where(upper, slab, zero) if i else jnp.where(upper, zero, slab)
            rows.append(qh if extra is None else jnp.concatenate([qh, extra], axis=1))
    return jnp.concatenate(rows, axis=0)


def _unstack_heads(o, tq):
    lane = lax.broadcasted_iota(jnp.int32, (1, LANES), 1)
    upper = lane >= HEAD_DIM
    slabs = []
    for j in range(2):
        lo = o[(2 * j) * tq:(2 * j + 1) * tq]
        hi = pltpu.roll(o[(2 * j + 1) * tq:(2 * j + 2) * tq], HEAD_DIM, 1)
        slabs.append(jnp.where(upper, hi, lo))
    return jnp.concatenate(slabs, axis=1)


def _group_lower(slab_ref, g):
    x = slab_ref[...].astype(F32)
    return jnp.where(g % 2 == 1, pltpu.roll(x, HEAD_DIM, 1), x)


def _both_halves(x):
    lane = lax.broadcasted_iota(jnp.int32, (1, LANES), 1)
    return jnp.where(lane >= HEAD_DIM, pltpu.roll(x, HEAD_DIM, 1), x)


def _with_ones(x):
    lane = lax.broadcasted_iota(jnp.int32, (1, LANES), 1)
    return jnp.where(lane >= HEAD_DIM, 1.0, x)


def _normalise(acc):
    return acc / pltpu.roll(acc, HEAD_DIM, 1)


def _nsa_cmp_attn_kernel(q_ref, kc_ref, vc_ref, cover_ref, oc_ref, sel_ref, *, tq, n16):
    qi = pl.program_id(2)
    q4 = _stack_heads(q_ref)
    s = lax.dot_general(q4, kc_ref[...], _NT, preferred_element_type=F32)
    qpos = qi * tq + lax.broadcasted_iota(jnp.int32, (tq, n16), 0)
    cmp_end = lax.broadcasted_iota(jnp.int32, (tq, n16), 1) * NSA_CMP_STRIDE + (NSA_CMP_LEN - 1)
    valid = jnp.concatenate([cmp_end <= qpos] * 4, axis=0)
    s = jnp.where(valid, s, NEG_INF)
    e = jnp.exp(s - jnp.max(s, axis=1, keepdims=True))
    p = jnp.where(valid, e / jnp.sum(e, axis=1, keepdims=True), 0.0)
    o = jnp.dot(p.astype(BF16), vc_ref[...], preferred_element_type=F32)
    oc_ref[...] = _unstack_heads(o, tq).astype(oc_ref.dtype)

    psum = p[0:tq] + p[tq:2 * tq] + p[2 * tq:3 * tq] + p[3 * tq:4 * tq]
    cover = cover_ref[...]
    hi = psum.astype(BF16)
    r1 = psum - hi.astype(F32)
    mid = r1.astype(BF16)
    lo = (r1 - mid.astype(F32)).astype(BF16)
    imp = (jnp.dot(hi, cover, preferred_element_type=F32) + jnp.dot(mid, cover, preferred_element_type=F32)
           + jnp.dot(lo, cover, preferred_element_type=F32))

    blk = lax.broadcasted_iota(jnp.int32, (tq, LANES), 1)
    cur = (qi * tq + lax.broadcasted_iota(jnp.int32, (tq, LANES), 0)) // NSA_SEL_LEN
    forced = (blk == 0) | (blk == cur) | (blk == cur - 1)
    vals = jnp.where(forced, NSA_FORCED_SCORE, jnp.where(blk <= cur, imp, -1.0))
    taken = -3e38
    for _ in range(NSA_N_SEL):
        first = jnp.argmax(vals, axis=1, keepdims=True)
        vals = jnp.where(blk == first, taken, vals)
    sel_ref[...] = jnp.where(vals == taken, 0.0, NEG_INF).astype(sel_ref.dtype)


def _nsa_cmp_attn(qk, kcmp, vcmp, *, batch, seq):
    T = qk.shape[0]
    G = NSA_KV_HEADS
    tq = _pick_tile(seq, 512)
    nq = seq // tq
    n16 = kcmp.shape[2]
    n_sel = seq // NSA_SEL_LEN
    assert n_sel <= LANES and seq % tq == 0
    n = np.arange(n16)[:, None] * NSA_CMP_STRIDE
    j = np.arange(LANES)[None, :] * NSA_SEL_LEN
    cover = (n < j + NSA_SEL_LEN) & (n + NSA_CMP_LEN > j) & (np.arange(LANES)[None, :] < n_sel)
    cover = jnp.asarray(cover.astype(np.float32), BF16)
    spec_c = pl.BlockSpec((None, None, n16, LANES), lambda b, g, q: (b, g, 0, 0))
    return pl.pallas_call(
        functools.partial(_nsa_cmp_attn_kernel, tq=tq, n16=n16),
        grid=(batch, G, nq),
        in_specs=[pl.BlockSpec((tq, 2 * LANES), lambda b, g, q: (b * nq + q, g)), spec_c, spec_c,
                  pl.BlockSpec((n16, LANES), lambda b, g, q: (0, 0))],
        out_specs=[pl.BlockSpec((tq, 2 * LANES), lambda b, g, q: (b * nq + q, g)),
                   pl.BlockSpec((tq, LANES), lambda b, g, q: (b * nq + q, g))],
        out_shape=[jax.ShapeDtypeStruct((T, N_HEADS * HEAD_DIM), BF16),
                   jax.ShapeDtypeStruct((T, G * LANES), BF16)],
        compiler_params=_cparams("parallel", "parallel", "parallel"),
        name="nsa_cmp_attn",
    )(qk, kcmp, vcmp, cover)


def _nsa_sel_attn_kernel(q_ref, k_ref, v_ref, sel_ref, o_ref, kaug_ref, vaug_ref, sa_ref, sb_ref, m_ref, acc_ref,
                         *, tq, sub, tk, seq):
    g = pl.program_id(1)
    qi = pl.program_id(2)

    @pl.when(qi == 0)
    def _():
        kaug_ref[:, 0:LANES] = _both_halves(_group_lower(k_ref, g)).astype(BF16)
        blk_of_key = lax.broadcasted_iota(jnp.int32, (seq, LANES), 0) // NSA_SEL_LEN
        blk = lax.broadcasted_iota(jnp.int32, (seq, LANES), 1)
        kaug_ref[:, LANES:2 * LANES] = jnp.where(blk_of_key == blk, 1.0, 0.0).astype(BF16)
        vaug_ref[...] = _with_ones(_group_lower(v_ref, g)).astype(BF16)

    chains = range(tq // sub)
    rows = [slice(c * sub, (c + 1) * sub) for c in chains]
    q4 = [_stack_heads(q_ref, extra=sel_ref[rows[c], :], rows_in=rows[c]) for c in chains]
    q0 = qi * tq
    n_full = q0 // tk

    def produce(s_ref, j):
        kt = kaug_ref[pl.ds(pl.multiple_of(j * tk, tk), tk), :]
        for c in chains:
            s_ref[c] = lax.dot_general(q4[c], kt, _NT, preferred_element_type=F32)

    def consume(s_ref, j, masked):
        keys = pl.ds(pl.multiple_of(j * tk, tk), tk)
        if masked:
            kpos = j * tk + lax.broadcasted_iota(jnp.int32, (sub, tk), 1)
            qpos = q0 + lax.broadcasted_iota(jnp.int32, (sub, tk), 0)
            ok = [jnp.concatenate([kpos <= qpos + c * sub] * 4, axis=0) for c in chains]
            score = lambda c: jnp.where(ok[c], s_ref[c], NEG_INF)
        else:
            score = lambda c: s_ref[c]
        vt = vaug_ref[keys, :]
        m_old = [m_ref[c] for c in chains]
        m_new = [jnp.maximum(m_old[c], jnp.max(score(c), axis=1, keepdims=True)) for c in chains]
        p = [jnp.exp(score(c) - jnp.tile(m_new[c], (1, tk // LANES))).astype(BF16) for c in chains]
        for c in chains:
            m_ref[c] = m_new[c]
            acc_ref[c] = (jnp.exp(m_old[c] - m_new[c]) * acc_ref[c]
                          + jnp.dot(p[c], vt, preferred_element_type=F32))

    m_ref[...] = jnp.full(m_ref.shape, NEG_INF, F32)
    acc_ref[...] = jnp.zeros(acc_ref.shape, F32)
    produce(sa_ref, 0)

    def body(jj, _):
        produce(sb_ref, 2 * jj + 1)
        consume(sa_ref, 2 * jj, False)
        produce(sa_ref, 2 * jj + 2)
        consume(sb_ref, 2 * jj + 1, False)
        return 0

    lax.fori_loop(0, n_full // 2, body, 0)

    @pl.when(n_full % 2 == 0)
    def _():
        consume(sa_ref, n_full, True)

    @pl.when(n_full % 2 == 1)
    def _():
        produce(sb_ref, n_full)
        consume(sa_ref, n_full - 1, False)
        consume(sb_ref, n_full, True)

    for c in chains:
        o_ref[rows[c], :] = _unstack_heads(_normalise(acc_ref[c]), sub).astype(o_ref.dtype)


def _nsa_sel_attn(qk, vv, selbias, *, batch, seq, k_col, v_col):
    T = qk.shape[0]
    G = NSA_KV_HEADS
    tq = _pick_tile(seq, 512)
    sub = tq
    tk = _pick_tile(seq, 512)
    nq = seq // tq
    assert tk % tq == 0
    return pl.pallas_call(
        functools.partial(_nsa_sel_attn_kernel, tq=tq, sub=sub, tk=tk, seq=seq),
        grid=(batch, G, nq),
        in_specs=[pl.BlockSpec((tq, 2 * LANES), lambda b, g, q: (b * nq + q, g)),
                  pl.BlockSpec((seq, LANES), lambda b, g, q: (b, k_col + g // 2)),
                  pl.BlockSpec((seq, LANES), lambda b, g, q: (b, v_col + g // 2)),
                  pl.BlockSpec((tq, LANES), lambda b, g, q: (b * nq + q, g))],
        out_specs=pl.BlockSpec((tq, 2 * LANES), lambda b, g, q: (b * nq + q, g)),
        out_shape=jax.ShapeDtypeStruct((T, N_HEADS * HEAD_DIM), BF16),
        scratch_shapes=[pltpu.VMEM((seq, 2 * LANES), BF16), pltpu.VMEM((seq, LANES), BF16),
                        pltpu.VMEM((tq // sub, 4 * sub, tk), F32), pltpu.VMEM((tq // sub, 4 * sub, tk), F32),
                        pltpu.VMEM((tq // sub, 4 * sub, LANES), F32),
                        pltpu.VMEM((tq // sub, 4 * sub, LANES), F32)],
        compiler_params=_cparams("parallel", "arbitrary", "arbitrary"),
        name="nsa_sel_attn",
    )(qk, qk, vv, selbias)


def _nsa_win_attn_kernel(q_ref, k_ref, v_ref, o_ref, kb_ref, vaug_ref, *, tq, span):
    g = pl.program_id(1)
    qi = pl.program_id(2)

    @pl.when(qi == 0)
    def _():
        kb_ref[...] = _both_halves(_group_lower(k_ref, g)).astype(BF16)
        vaug_ref[...] = _with_ones(_group_lower(v_ref, g)).astype(BF16)

    q4 = _stack_heads(q_ref)
    q0 = qi * tq
    k_start = pl.multiple_of(jnp.maximum(q0 - NSA_WINDOW, 0), tq)
    keys = pl.ds(k_start, span)
    dist = (q0 - k_start) + (lax.broadcasted_iota(jnp.int32, (tq, span), 0)
                             - lax.broadcasted_iota(jnp.int32, (tq, span), 1))
    ok = (dist >= 0) & (dist < NSA_WINDOW)
    ok2 = jnp.concatenate([ok, ok], axis=0)
    pairs = [q4[0:2 * tq], q4[2 * tq:4 * tq]]
    s = [lax.dot_general(qp, kb_ref[keys, :], _NT, preferred_element_type=F32) for qp in pairs]
    s = [jnp.where(ok2, sc, NEG_INF) for sc in s]
    p = [jnp.exp(sc - jnp.max(sc, axis=1, keepdims=True)).astype(BF16) for sc in s]
    acc = [jnp.dot(pc, vaug_ref[keys, :], preferred_element_type=F32) for pc in p]
    acc = jnp.concatenate(acc, axis=0)
    o_ref[...] = _unstack_heads(_normalise(acc), tq).astype(o_ref.dtype)


def _nsa_win_attn(qk, vv, *, batch, seq, k_col, v_col):
    T = qk.shape[0]
    G = NSA_KV_HEADS
    tq = _pick_tile(seq, 256)
    nq = seq // tq
    span = NSA_WINDOW + tq
    assert NSA_WINDOW % tq == 0 and seq >= span
    return pl.pallas_call(
        functools.partial(_nsa_win_attn_kernel, tq=tq, span=span),
        grid=(batch, G, nq),
        in_specs=[pl.BlockSpec((tq, 2 * LANES), lambda b, g, q: (b * nq + q, g)),
                  pl.BlockSpec((seq, LANES), lambda b, g, q: (b, k_col + g // 2)),
                  pl.BlockSpec((seq, LANES), lambda b, g, q: (b, v_col + g // 2))],
        out_specs=pl.BlockSpec((tq, 2 * LANES), lambda b, g, q: (b * nq + q, g)),
        out_shape=jax.ShapeDtypeStruct((T, N_HEADS * HEAD_DIM), BF16),
        scratch_shapes=[pltpu.VMEM((seq, LANES), BF16), pltpu.VMEM((seq, LANES), BF16)],
        compiler_params=_cparams("parallel", "arbitrary", "arbitrary"),
        name="nsa_win_attn",
    )(qk, qk, vv)


def _nsa_mixer(x, g, w_in, pos_k, pos_v, w_cmp_k, w_cmp_v, g_q, g_k, w_out, *, batch, seq):
    HD = N_HEADS * HEAD_DIM
    G, dh = NSA_KV_HEADS, HEAD_DIM
    kvw = G * dh
    scale = dh ** -0.5
    col = lambda n: slice(HD + n * kvw, HD + (n + 1) * kvw)
    order = [slice(0, HD), col(0), col(2), col(4), col(1), col(3), col(5)]
    w = jnp.concatenate([w_in[:, c] for c in order], axis=1).astype(BF16)
    n_gate = 3 * N_HEADS
    w_gate = jnp.pad(w_in[:, HD + 6 * kvw:], ((0, 0), (0, LANES - n_gate))).astype(BF16)
    head_gain = jnp.concatenate([jnp.tile(g_q * scale, N_HEADS), jnp.tile(g_k, 3 * G)])

    inv = ROPE_THETA ** (-jnp.arange(0, dh, 2, dtype=F32) / dh)
    ang = jnp.arange(seq, dtype=F32)[:, None] * inv[None, :]
    cos, sin = jnp.cos(ang), jnp.sin(ang)
    cos_t = jnp.tile(cos, (1, PROJ_CHUNK // (dh // 2)))
    sin_t = jnp.tile(jnp.concatenate([-sin, sin], axis=1), (1, PROJ_CHUNK // dh))

    qkv, gate = _proj(x, g, w, seq=seq, head_gain=head_gain, rope=(cos_t, sin_t), w_aux=w_gate)

    def blocks16(t):
        half = NSA_CMP_LEN // 2
        t = t.reshape(batch, seq // half, half, G, dh).transpose(0, 3, 1, 2, 4)
        return t.reshape(batch, G, seq // half, half * dh)

    kc0, vc0 = HD, HD + 3 * kvw
    kcmp, vcmp = _nsa_compress(blocks16(qkv[:, kc0:kc0 + kvw]), blocks16(qkv[:, vc0:vc0 + kvw]),
                               pos_k, pos_v, w_cmp_k, w_cmp_v, g_k)
    o_c, selbias = _nsa_cmp_attn(qkv, kcmp, vcmp, batch=batch, seq=seq)
    o_s = _nsa_sel_attn(qkv, qkv, selbias, batch=batch, seq=seq,
                        k_col=(HD + kvw) // LANES, v_col=(HD + 4 * kvw) // LANES)
    o_w = _nsa_win_attn(qkv, qkv, batch=batch, seq=seq,
                        k_col=(HD + 2 * kvw) // LANES, v_col=(HD + 5 * kvw) // LANES)
    r = np.arange(LANES)[:, None]
    c = np.arange(HD)[None, :] // dh
    expand = np.stack([(r == 3 * c + b) for b in range(3)]).astype(np.float32)
    return _outproj(x, w_out.astype(BF16), mode="gated3",
                    acts=[o_c, o_s, o_w, gate, jnp.asarray(expand, BF16)])


def kernel(x, norm_g, ffn1_w_gate, ffn1_w_up, ffn1_w_down, ffn2_w_gate, ffn2_w_up, ffn2_w_down, fox_w_in, fox_b_f, fox_g_q, fox_g_k, fox_w_out, nsa_w_in, nsa_cmp_pos_k, nsa_cmp_pos_v, nsa_w_cmp_k, nsa_w_cmp_v, nsa_g_q, nsa_g_k, nsa_w_out, gla_w_in, gla_w_gate_up, gla_b_gate, gla_g_out, gla_w_out, sb_w_in, sb_w_out):
    B, S, D = x.shape
    depth = norm_g.shape[0]
    n_mixers = 4
    x2 = x.reshape(B * S, D)
    for i in range(depth):
        m, j = i % n_mixers, i // n_mixers
        x2 = _ffn(x2, norm_g[i, 0], ffn1_w_gate[i].astype(BF16), ffn1_w_up[i].astype(BF16),
                  ffn1_w_down[i].astype(BF16))
        g = norm_g[i, 1]
        if m == 0:
            x2 = _fox_mixer(x2, g, fox_w_in[j], fox_b_f[j], fox_g_q[j], fox_g_k[j], fox_w_out[j], batch=B, seq=S)
        elif m == 1:
            x2 = _nsa_mixer(x2, g, nsa_w_in[j], nsa_cmp_pos_k[j], nsa_cmp_pos_v[j], nsa_w_cmp_k[j],
                            nsa_w_cmp_v[j], nsa_g_q[j], nsa_g_k[j], nsa_w_out[j], batch=B, seq=S)
        elif m == 2:
            x2 = _gla_mixer(x2, g, gla_w_in[j], gla_w_gate_up[j], gla_b_gate[j], gla_g_out[j], gla_w_out[j],
                            batch=B, seq=S)
        else:
            x2 = _sb_mixer(x2, g, sb_w_in[j], sb_w_out[j], batch=B, seq=S)
        x2 = _ffn(x2, norm_g[i, 2], ffn2_w_gate[i].astype(BF16), ffn2_w_up[i].astype(BF16),
                  ffn2_w_down[i].astype(BF16))
    return x2.reshape(B, S, D)
```

```python
import functools

import numpy as np
import jax
import jax.numpy as jnp
from jax import lax
from jax.experimental import pallas as pl
from jax.experimental.pallas import tpu as pltpu

F32 = jnp.float32
BF16 = jnp.bfloat16

N_HEADS = 16
HEAD_DIM = 64
ROPE_THETA = 10000.0
RMS_EPS = 1e-6
NEG_INF = -1e30
NSA_KV_HEADS = 4
NSA_CMP_LEN = 32
NSA_CMP_STRIDE = 16
NSA_SEL_LEN = 64
NSA_N_SEL = 16
NSA_WINDOW = 512
NSA_FORCED_SCORE = 1e9
GLA_HEADS = 4
GLA_GATE_RANK = 16
GLA_TAU = 16.0
GLA_CHUNK = 64

LANES = 128
V7X_VMEM_BYTES = 64 * 1024 * 1024
VMEM_LIMIT = V7X_VMEM_BYTES - 8 * 1024 * 1024

_NT = (((1,), (1,)), ((), ()))
_TN = (((0,), (0,)), ((), ()))


def _cparams(*sem):
    return pltpu.CompilerParams(dimension_semantics=sem, vmem_limit_bytes=VMEM_LIMIT)


def _rms_rows(x, g):
    return x * lax.rsqrt(jnp.mean(x * x, axis=-1, keepdims=True) + RMS_EPS) * g


def _softplus(z):
    return jnp.maximum(z, 0.0) + jnp.log(1.0 + jnp.exp(-jnp.abs(z)))


def _neg_abs(z):
    bits = lax.bitcast_convert_type(z, jnp.int32) | jnp.int32(-2 ** 31)
    return lax.bitcast_convert_type(bits, F32)


def _sigmoid(z):
    return 1.0 / (1.0 + jnp.exp(-z))


def _pick_tile(n, target):
    t = min(n, target)
    while n % t:
        t //= 2
    return t


def _ffn_kernel(x_ref, g_ref, wg_ref, wu_ref, wd_ref, o_ref, a_ref, *, tf):
    x = x_ref[...]
    h = _rms_rows(x, g_ref[...]).astype(BF16)
    for c in range(a_ref.shape[1] // tf):
        cols = slice(c * tf, (c + 1) * tf)
        gate = jnp.dot(h, wg_ref[:, cols], preferred_element_type=F32)
        up = jnp.dot(h, wu_ref[:, cols], preferred_element_type=F32)
        a_ref[:, cols] = (gate * _sigmoid(gate) * up).astype(BF16)
    o_ref[...] = x + 0.5 * jnp.dot(a_ref[...], wd_ref[...], preferred_element_type=F32)


def _resident(shape):
    return pl.BlockSpec(shape, lambda *_: (0,) * len(shape), pipeline_mode=pl.Buffered(1))


def _ffn(x, g, wg, wu, wd):
    T, D = x.shape
    F = wg.shape[1]
    tm = _pick_tile(T, 1024)
    tf = 256 if F % 256 == 0 else F
    return pl.pallas_call(
        functools.partial(_ffn_kernel, tf=tf),
        grid=(T // tm,),
        in_specs=[
            pl.BlockSpec((tm, D), lambda i: (i, 0)),
            _resident((1, D)),
            _resident((D, F)),
            _resident((D, F)),
            _resident((F, D)),
        ],
        out_specs=pl.BlockSpec((tm, D), lambda i: (i, 0)),
        out_shape=jax.ShapeDtypeStruct((T, D), F32),
        scratch_shapes=[pltpu.VMEM((tm, F), BF16)],
        compiler_params=_cparams("parallel"),
        name="ffn",
    )(x, g.reshape(1, D), wg, wu, wd)


PROJ_CHUNK = 256


def _proj_kernel(*refs, n_norm, rope, has_aux):
    it = iter(refs)
    x_ref, g_ref, w_ref = next(it), next(it), next(it)
    hg_ref, gsum_ref = (next(it), next(it)) if n_norm else (None, None)
    cos_ref, sin_ref = (next(it), next(it)) if rope else (None, None)
    waux_ref = next(it) if has_aux else None
    o_ref = next(it)
    ch = PROJ_CHUNK
    h = _rms_rows(x_ref[...], g_ref[...]).astype(BF16)
    for c in range(o_ref.shape[1] // ch):
        cols = slice(c * ch, (c + 1) * ch)
        y = jnp.dot(h, w_ref[:, cols], preferred_element_type=F32)
        if c * ch < n_norm:
            ss = jnp.dot((y * y).astype(BF16), gsum_ref[...], preferred_element_type=F32)
            y = y * lax.rsqrt(ss * (1.0 / HEAD_DIM) + RMS_EPS) * hg_ref[:, cols]
            if rope:
                lane = lax.broadcasted_iota(jnp.int32, (1, ch), 1)
                first_half = (lane % HEAD_DIM) < (HEAD_DIM // 2)
                partner = jnp.where(first_half,
                                    pltpu.roll(y, ch - HEAD_DIM // 2, 1),
                                    pltpu.roll(y, HEAD_DIM // 2, 1))
                y = y * cos_ref[...] + partner * sin_ref[...]
        o_ref[:, cols] = y.astype(o_ref.dtype)
    if has_aux:
        next(it)[...] = jnp.dot(h, waux_ref[...], preferred_element_type=F32)


def _proj(x, g, w, *, seq, head_gain=None, rope=None, w_aux=None):
    T, D = x.shape
    N = w.shape[1]
    tm = _pick_tile(seq, 1024)
    ch = PROJ_CHUNK
    n_norm = 0 if head_gain is None else head_gain.shape[0]
    assert N % ch == 0 and n_norm % ch == 0 and T % tm == 0
    args = [x, g.reshape(1, D), w]
    in_specs = [pl.BlockSpec((tm, D), lambda i: (i, 0)), _resident((1, D)), _resident((D, N))]
    if n_norm:
        gidx = np.arange(ch) // HEAD_DIM
        gsum = jnp.asarray((gidx[:, None] == gidx[None, :]).astype(np.float32), BF16)
        args += [head_gain.reshape(1, n_norm), gsum]
        in_specs += [_resident((1, n_norm)), _resident((ch, ch))]
    if rope is not None:
        nblk = seq // tm
        args += list(rope)
        in_specs += [pl.BlockSpec((tm, ch), lambda i: (i % nblk, 0))] * 2
    out_specs = [pl.BlockSpec((tm, N), lambda i: (i, 0))]
    out_shape = [jax.ShapeDtypeStruct((T, N), BF16)]
    if w_aux is not None:
        args.append(w_aux)
        in_specs.append(_resident(w_aux.shape))
        out_specs.append(pl.BlockSpec((tm, w_aux.shape[1]), lambda i: (i, 0)))
        out_shape.append(jax.ShapeDtypeStruct((T, w_aux.shape[1]), F32))
    out = pl.pallas_call(
        functools.partial(_proj_kernel, n_norm=n_norm, rope=rope is not None, has_aux=w_aux is not None),
        grid=(T // tm,),
        in_specs=in_specs,
        out_specs=out_specs,
        out_shape=out_shape,
        compiler_params=_cparams("parallel"),
        name="proj",
    )(*args)
    return out if w_aux is not None else out[0]


def _outproj_kernel(*refs, mode):
    if mode == "plain":
        x_ref, a_ref, w_ref, o_ref = refs
        a = a_ref[...]
    elif mode == "sigmoid_gate":
        x_ref, a_ref, og_ref, w_ref, o_ref = refs
        a = (a_ref[...].astype(F32) * _sigmoid(og_ref[...].astype(F32))).astype(BF16)
    else:
        x_ref, oc_ref, os_ref, ow_ref, gate_ref, exp_ref, w_ref, o_ref = refs
        gs = _sigmoid(gate_ref[...]).astype(BF16)
        a = None
        for c, o_c in enumerate((oc_ref, os_ref, ow_ref)):
            ge = jnp.dot(gs, exp_ref[c], preferred_element_type=F32)
            t = ge * o_c[...].astype(F32)
            a = t if a is None else a + t
        a = a.astype(BF16)
    o_ref[...] = x_ref[...] + jnp.dot(a, w_ref[...], preferred_element_type=F32)


def _outproj(x, w, *, mode, acts):
    T, D = x.shape
    K = w.shape[0]
    tm = _pick_tile(T, 512)
    row = lambda i: (i, 0)
    args = [x]
    in_specs = [pl.BlockSpec((tm, D), row)]
    for a in acts:
        if a.ndim == 3:
            args.append(a)
            in_specs.append(pl.BlockSpec(a.shape, lambda i: (0, 0, 0)))
        else:
            args.append(a)
            in_specs.append(pl.BlockSpec((tm, a.shape[1]), row))
    args.append(w)
    in_specs.append(pl.BlockSpec((K, D), lambda i: (0, 0)))
    return pl.pallas_call(
        functools.partial(_outproj_kernel, mode=mode),
        grid=(T // tm,),
        in_specs=in_specs,
        out_specs=pl.BlockSpec((tm, D), row),
        out_shape=jax.ShapeDtypeStruct((T, D), F32),
        compiler_params=_cparams("parallel"),
        name="outproj_" + mode,
    )(*args)


def _fox_c_kernel(x_ref, g_ref, wf_ref, bf_ref, tri_ref, c_ref, carry_ref):
    @pl.when(pl.program_id(1) == 0)
    def _():
        carry_ref[...] = jnp.zeros_like(carry_ref)

    h = _rms_rows(x_ref[...], g_ref[...])
    f = lax.dot_general(wf_ref[...], h, _NT, precision=lax.Precision.HIGHEST,
                        preferred_element_type=F32) + bf_ref[...]
    ls = jnp.minimum(f, 0.0) - jnp.log(1.0 + jnp.exp(-jnp.abs(f)))
    cs = jnp.dot(ls, tri_ref[...], precision=lax.Precision.HIGHEST,
                 preferred_element_type=F32) + carry_ref[...]
    c_ref[...] = cs
    carry_ref[...] = cs[:, cs.shape[1] - 1:]


def _fox_c(x, g, wf_t, b_f, *, batch, seq):
    T, D = x.shape
    H = wf_t.shape[0]
    tm = _pick_tile(seq, 512)
    ns = seq // tm
    tri = jnp.asarray(np.triu(np.ones((tm, tm), np.float32)))
    return pl.pallas_call(
        _fox_c_kernel,
        grid=(batch, ns),
        in_specs=[
            pl.BlockSpec((tm, D), lambda b, s: (b * ns + s, 0)),
            pl.BlockSpec((1, D), lambda b, s: (0, 0)),
            pl.BlockSpec((H, D), lambda b, s: (0, 0)),
            pl.BlockSpec((H, 1), lambda b, s: (0, 0)),
            pl.BlockSpec((tm, tm), lambda b, s: (0, 0)),
        ],
        out_specs=pl.BlockSpec((None, H, tm), lambda b, s: (b, 0, s)),
        out_shape=jax.ShapeDtypeStruct((batch, H, seq), F32),
        scratch_shapes=[pltpu.VMEM((H, 1), F32)],
        compiler_params=_cparams("parallel", "arbitrary"),
        name="fox_c",
    )(x, g.reshape(1, D), wf_t, b_f.reshape(H, 1), tri)


def _fox_attn_kernel(q_ref, k_ref, v_ref, c_ref, o_ref, vaug_ref, sa_ref, sb_ref, m_ref, acc_ref, *, tq, tk):
    qi = pl.program_id(2)
    lane = lax.broadcasted_iota(jnp.int32, (1, LANES), 1)
    upper = lane >= HEAD_DIM

    @pl.when(qi == 0)
    def _():
        v = v_ref[...]
        one = jnp.ones_like(v)
        vaug_ref[0] = jnp.where(upper, one, v)
        vaug_ref[1] = jnp.where(upper, v, one)

    q0 = pl.multiple_of(qi * tq, tq)
    qslab = q_ref[...]
    zero = jnp.zeros_like(qslab)
    qpos = lax.broadcasted_iota(jnp.int32, (tq, tk), 0)
    kpos = lax.broadcasted_iota(jnp.int32, (tq, tk), 1)

    heads = range(2)
    q2 = jnp.concatenate([jnp.where(upper, zero, qslab), jnp.where(upper, qslab, zero)], axis=0)
    c_q0 = [c_ref[i, :, pl.ds(q0, LANES)][:, 0:1] for i in heads]

    n_diag = tq // tk
    assert n_diag == 2

    def produce(s_ref, j):
        keys = pl.ds(pl.multiple_of(j * tk, tk), tk)
        s2 = lax.dot_general(q2, k_ref[keys, :], _NT, preferred_element_type=F32)
        for i in heads:
            s_ref[i] = s2[i * tq:(i + 1) * tq] + (c_q0[i] - c_ref[i, :, keys])

    def consume(s_ref, j, diag_offset=None):
        keys = pl.ds(pl.multiple_of(j * tk, tk), tk)
        if diag_offset is None:
            score = lambda i: s_ref[i]
        else:
            causal = kpos + diag_offset <= qpos
            score = lambda i: jnp.where(causal, s_ref[i], NEG_INF)
        m_old = [m_ref[i] for i in heads]
        m_new = [jnp.maximum(m_old[i], jnp.max(score(i), axis=1, keepdims=True)) for i in heads]
        p = [jnp.exp(score(i) - jnp.tile(m_new[i], (1, tk // LANES))).astype(BF16) for i in heads]
        for i in heads:
            m_ref[i] = m_new[i]
            acc_ref[i] = (jnp.exp(m_old[i] - m_new[i]) * acc_ref[i]
                          + jnp.dot(p[i], vaug_ref[i, keys, :], preferred_element_type=F32))

    m_ref[...] = jnp.full(m_ref.shape, NEG_INF, F32)
    acc_ref[...] = jnp.zeros(acc_ref.shape, F32)
    produce(sa_ref, 0)

    def body(m, _):
        produce(sb_ref, 2 * m + 1)
        consume(sa_ref, 2 * m)
        produce(sa_ref, 2 * m + 2)
        consume(sb_ref, 2 * m + 1)
        return 0

    lax.fori_loop(0, qi, body, 0)
    produce(sb_ref, 2 * qi + 1)
    consume(sa_ref, 2 * qi, 0)
    consume(sb_ref, 2 * qi + 1, tk)

    accs = (acc_ref[0], acc_ref[1])
    num = jnp.where(upper, accs[1], accs[0])
    den = jnp.where(upper, accs[0], accs[1])
    den = pltpu.roll(den, HEAD_DIM, 1)
    o_ref[...] = (num / den).astype(o_ref.dtype)


def _fox_attn(qkvg, c, *, batch, seq):
    T = qkvg.shape[0]
    HD = N_HEADS * HEAD_DIM
    tq = _pick_tile(seq, 1024)
    tk = tq // 2
    nq = seq // tq
    npair = HD // LANES
    return pl.pallas_call(
        functools.partial(_fox_attn_kernel, tq=tq, tk=tk),
        grid=(batch, npair, nq),
        in_specs=[
            pl.BlockSpec((tq, LANES), lambda b, h, q: (b * nq + q, h)),
            pl.BlockSpec((seq, LANES), lambda b, h, q: (b, npair + h)),
            pl.BlockSpec((seq, LANES), lambda b, h, q: (b, 2 * npair + h)),
            pl.BlockSpec((None, 2, 1, seq), lambda b, h, q: (b, h, 0, 0)),
        ],
        out_specs=pl.BlockSpec((tq, LANES), lambda b, h, q: (b * nq + q, h)),
        out_shape=jax.ShapeDtypeStruct((T, HD), BF16),
        scratch_shapes=[pltpu.VMEM((2, seq, LANES), BF16),
                        pltpu.VMEM((2, tq, tk), F32), pltpu.VMEM((2, tq, tk), F32),
                        pltpu.VMEM((2, tq, LANES), F32), pltpu.VMEM((2, tq, LANES), F32)],
        compiler_params=_cparams("parallel", "arbitrary", "arbitrary"),
        name="fox_attn",
    )(qkvg, qkvg, qkvg, c.reshape(batch, N_HEADS, 1, seq))


def _fox_mixer(x, g, w_in, b_f, g_q, g_k, w_out, *, batch, seq):
    HD = N_HEADS * HEAD_DIM
    scale = HEAD_DIM ** -0.5
    w = jnp.concatenate([w_in[:, :3 * HD], w_in[:, 3 * HD + N_HEADS:]], axis=1).astype(BF16)
    wf_t = w_in[:, 3 * HD:3 * HD + N_HEADS].T
    head_gain = jnp.concatenate([jnp.tile(g_q * scale, N_HEADS), jnp.tile(g_k, N_HEADS)])
    qkvg = _proj(x, g, w, seq=seq, head_gain=head_gain)
    c = _fox_c(x, g, wf_t, b_f, batch=batch, seq=seq)
    o = _fox_attn(qkvg, c, batch=batch, seq=seq)
    return _outproj_fox(x, o, qkvg, w_out.astype(BF16))


def _outproj_fox(x, o, qkvg, w):
    T, D = x.shape
    HD = o.shape[1]
    tm = _pick_tile(T, 512)
    return pl.pallas_call(
        functools.partial(_outproj_kernel, mode="sigmoid_gate"),
        grid=(T // tm,),
        in_specs=[
            pl.BlockSpec((tm, D), lambda i: (i, 0)),
            pl.BlockSpec((tm, HD), lambda i: (i, 0)),
            pl.BlockSpec((tm, HD), lambda i: (i, 3)),
            pl.BlockSpec((HD, D), lambda i: (0, 0)),
        ],
        out_specs=pl.BlockSpec((tm, D), lambda i: (i, 0)),
        out_shape=jax.ShapeDtypeStruct((T, D), F32),
        compiler_params=_cparams("parallel"),
        name="outproj_fox",
    )(x, o, qkvg, w)


def _sb_attn_kernel(q_ref, k_ref, v_ref, tri_ref, o_ref, wa_ref, wb_ref, ta_ref, tb_ref, rest_ref, acc_ref,
                    *, tq, tk):
    qi = pl.program_id(2)
    lane = lax.broadcasted_iota(jnp.int32, (1, LANES), 1)
    upper = lane >= HEAD_DIM
    qslab = q_ref[...]
    zero = jnp.zeros_like(qslab)
    q2 = jnp.concatenate([jnp.where(upper, zero, qslab), jnp.where(upper, qslab, zero)], axis=0)
    rows = 2 * tq
    qpos = lax.broadcasted_iota(jnp.int32, (rows, tk), 0) % tq
    kpos = lax.broadcasted_iota(jnp.int32, (rows, tk), 1)
    n_diag = tq // tk
    last = n_diag * (qi + 1) - 1

    def key_rows(k):
        return pl.ds(pl.multiple_of(jnp.maximum(last - k, 0) * tk, tk), tk)

    def scores(k):
        return lax.dot_general(q2, k_ref[key_rows(k), :], _NT, preferred_element_type=F32)

    def softplus_sums(z, w_ref, tot_ref, diag_offset):
        sp = jnp.maximum(z, 0.0) + jnp.log2(1.0 + jnp.exp2(_neg_abs(z)))
        if diag_offset is not None:
            strict = kpos + diag_offset < qpos
            spm = jnp.where(strict, sp, 0.0)
        else:
            spm = sp
        cum = jnp.dot(spm.astype(BF16), tri_ref[...], preferred_element_type=F32)
        w = (z - sp) - cum
        if diag_offset is not None:
            w = jnp.where(strict, w, NEG_INF)
        w_ref[...] = w
        tot_ref[...] = jnp.broadcast_to(cum[:, 0:1] + spm[:, 0:1], (rows, LANES))

    def accumulate(w_ref, tot_ref, k):
        a = jnp.exp2(w_ref[...] - jnp.tile(rest_ref[...], (1, tk // LANES))).astype(BF16)
        acc_ref[...] += jnp.dot(a, v_ref[key_rows(k), :], preferred_element_type=F32)
        rest_ref[...] += tot_ref[...]

    assert n_diag % 2 == 0
    even, odd = (wa_ref, ta_ref), (wb_ref, tb_ref)
    offset = lambda k: (n_diag - 1 - k) * tk if k < n_diag else None

    def produce(buf, k):
        softplus_sums(scores(k), *buf, offset(k))

    def step(k, cur, nxt, diag_offset=None):
        z = scores(k + 1)
        accumulate(*cur, k)
        softplus_sums(z, *nxt, diag_offset)

    rest_ref[...] = jnp.zeros(rest_ref.shape, F32)
    acc_ref[...] = jnp.zeros(acc_ref.shape, F32)
    produce(even, 0)
    produce(odd, 1)
    accumulate(wa_ref, ta_ref, 0)
    for m in range(1, n_diag // 2):
        step(2 * m - 1, odd, even, offset(2 * m))
        step(2 * m, even, odd, offset(2 * m + 1))

    def body(m, _):
        step(2 * m - 1, odd, even)
        step(2 * m, even, odd)
        return 0

    lax.fori_loop(n_diag // 2, (n_diag // 2) * (qi + 1), body, 0)
    accumulate(wb_ref, tb_ref, last)
    o_ref[...] = jnp.where(upper, acc_ref[tq:2 * tq], acc_ref[0:tq]).astype(o_ref.dtype)


def _sb_attn(qkv, *, batch, seq):
    T = qkv.shape[0]
    HD = N_HEADS * HEAD_DIM
    tq = _pick_tile(seq, 1024)
    tk = 256
    nq = seq // tq
    npair = HD // LANES
    tri = jnp.asarray(np.tril(np.ones((tk, tk), np.float32), -1), BF16)
    return pl.pallas_call(
        functools.partial(_sb_attn_kernel, tq=tq, tk=tk),
        grid=(batch, npair, nq),
        in_specs=[
            pl.BlockSpec((tq, LANES), lambda b, h, q: (b * nq + q, h)),
            pl.BlockSpec((seq, LANES), lambda b, h, q: (b, npair + h)),
            pl.BlockSpec((seq, LANES), lambda b, h, q: (b, 2 * npair + h)),
            pl.BlockSpec((tk, tk), lambda b, h, q: (0, 0)),
        ],
        out_specs=pl.BlockSpec((tq, LANES), lambda b, h, q: (b * nq + q, h)),
        out_shape=jax.ShapeDtypeStruct((T, HD), BF16),
        scratch_shapes=[pltpu.VMEM((2 * tq, tk), F32)] * 2 + [pltpu.VMEM((2 * tq, LANES), F32)] * 4,
        compiler_params=_cparams("parallel", "parallel", "arbitrary"),
        name="sb_attn",
    )(qkv, qkv, qkv, tri)


def _sb_mixer(x, g, w_in, w_out, *, batch, seq):
    HD = N_HEADS * HEAD_DIM
    scale = HEAD_DIM ** -0.5 * float(np.log2(np.e))
    w = jnp.concatenate([w_in[:, :HD] * scale, w_in[:, HD:]], axis=1).astype(BF16)
    qkv = _proj(x, g, w, seq=seq)
    o = _sb_attn(qkv, batch=batch, seq=seq)
    return _outproj(x, w_out.astype(BF16), mode="plain", acts=[o])


def _gla_kernel(q_ref, k_ref, v_ref, r_ref, glow_ref, wgu_ref, bg_ref, gout_ref, tri_ref,
                o_ref, state_ref, *, tm, dk):
    C = GLA_CHUNK

    @pl.when(pl.program_id(2) == 0)
    def _():
        state_ref[...] = jnp.zeros_like(state_ref)

    gate = jnp.dot(glow_ref[...], wgu_ref[...], precision=lax.Precision.HIGHEST,
                   preferred_element_type=F32) + bg_ref[...]
    log_a = (jnp.minimum(gate, 0.0) - jnp.log(1.0 + jnp.exp(-jnp.abs(gate)))) * (1.0 / GLA_TAU)
    row = lax.broadcasted_iota(jnp.int32, (C, C), 0)
    col = lax.broadcasted_iota(jnp.int32, (C, C), 1)
    causal = col <= row
    q_scale = dk ** -0.5
    chunks = [slice(c * C, (c + 1) * C) for c in range(tm // C)]

    tri = tri_ref[...]
    b = [jnp.dot(tri, log_a[sl], precision=lax.Precision.HIGHEST, preferred_element_type=F32)
         for sl in chunks]
    b_last = jnp.concatenate([jnp.broadcast_to(bc[C - 1:C], bc.shape) for bc in b], axis=0)
    b = jnp.concatenate(b, axis=0)
    q = q_ref[...].astype(F32) * q_scale
    k = k_ref[...].astype(F32)
    q_dec = (q * jnp.exp(b)).astype(BF16)
    k_neg = (k * jnp.exp(-b)).astype(BF16)
    k_rem = (k * jnp.exp(b_last - b)).astype(BF16)
    decay = jnp.exp(b_last)
    attn = [lax.dot_general(q_dec[sl], k_neg[sl], _NT, preferred_element_type=F32) for sl in chunks]
    attn = [jnp.where(causal, a, 0.0).astype(BF16) for a in attn]
    o_intra = [jnp.dot(attn[c], v_ref[sl, :], preferred_element_type=F32) for c, sl in enumerate(chunks)]
    u_t = [lax.dot_general(v_ref[sl, :], k_rem[sl], _TN, preferred_element_type=F32) for sl in chunks]

    state_t = state_ref[...]
    outs = []
    for c, sl in enumerate(chunks):
        outs.append(o_intra[c] + lax.dot_general(q_dec[sl], state_t.astype(BF16), _NT,
                                                 preferred_element_type=F32))
        state_t = state_t * decay[c * C:c * C + 1] + u_t[c]
    state_ref[...] = state_t

    o = jnp.concatenate(outs, axis=0)
    y = o * lax.rsqrt(jnp.mean(o * o, axis=-1, keepdims=True) + RMS_EPS) * gout_ref[...]
    r = r_ref[...].astype(F32)
    o_ref[...] = (y * (r * _sigmoid(r))).astype(o_ref.dtype)


def _gla(qkvr, glow, wgu, b_gate, g_out, *, batch, seq):
    T = qkvr.shape[0]
    Hg = GLA_HEADS
    dk_total = wgu.shape[1]
    dk = dk_total // Hg
    dv = (qkvr.shape[1] - 2 * dk_total) // 2 // Hg
    tm = _pick_tile(seq, 512)
    ns = seq // tm
    kq, kv = dk_total // dk, dk_total // dv
    tri = jnp.asarray(np.tril(np.ones((GLA_CHUNK, GLA_CHUNK), np.float32)))
    rowmap = lambda off: (lambda b, h, s: (b * ns + s, off + h))
    return pl.pallas_call(
        functools.partial(_gla_kernel, tm=tm, dk=dk),
        grid=(batch, Hg, ns),
        in_specs=[
            pl.BlockSpec((tm, dk), rowmap(0)),
            pl.BlockSpec((tm, dk), rowmap(kq)),
            pl.BlockSpec((tm, dv), rowmap(2 * kv)),
            pl.BlockSpec((tm, dv), rowmap(2 * kv + Hg)),
            pl.BlockSpec((tm, LANES), lambda b, h, s: (b * ns + s, 0)),
            pl.BlockSpec((LANES, dk), lambda b, h, s: (0, h)),
            pl.BlockSpec((1, dk), lambda b, h, s: (0, h)),
            pl.BlockSpec((1, dv), lambda b, h, s: (0, 0)),
            pl.BlockSpec((GLA_CHUNK, GLA_CHUNK), lambda b, h, s: (0, 0)),
        ],
        out_specs=pl.BlockSpec((tm, dv), rowmap(0)),
        out_shape=jax.ShapeDtypeStruct((T, Hg * dv), BF16),
        scratch_shapes=[pltpu.VMEM((dv, dk), F32)],
        compiler_params=_cparams("parallel", "parallel", "arbitrary"),
        name="gla",
    )(qkvr, qkvr, qkvr, qkvr, glow, wgu, b_gate.reshape(1, dk_total), g_out.reshape(1, dv), tri)


def _gla_mixer(x, g, w_in, w_gate_up, b_gate, g_out, w_out, *, batch, seq):
    dk_total = w_gate_up.shape[1]
    dv_total = w_out.shape[0]
    lo = 2 * dk_total + dv_total
    w_main = jnp.concatenate([w_in[:, :lo], w_in[:, lo + GLA_GATE_RANK:]], axis=1).astype(BF16)
    w_low = jnp.pad(w_in[:, lo:lo + GLA_GATE_RANK], ((0, 0), (0, LANES - GLA_GATE_RANK))).astype(BF16)
    wgu = jnp.pad(w_gate_up, ((0, LANES - GLA_GATE_RANK), (0, 0)))
    qkvr, glow = _proj(x, g, w_main, seq=seq, w_aux=w_low)
    o = _gla(qkvr, glow, wgu, b_gate, g_out, batch=batch, seq=seq)
    return _outproj(x, w_out.astype(BF16), mode="plain", acts=[o])


def _nsa_compress_kernel(kc_ref, vc_ref, pk_ref, pv_ref, wk_ref, wv_ref, gk_ref, ko_ref, vo_ref, *, n16):
    def compress(r_ref, pos_ref, w_ref):
        r = r_ref[...].astype(F32)
        first = jnp.dot((r + pos_ref[0:1, :]).astype(BF16), w_ref[0], preferred_element_type=F32)
        second = jnp.dot((r + pos_ref[1:2, :]).astype(BF16), w_ref[1], preferred_element_type=F32)
        return first + pltpu.roll(second, n16 - 1, 0)

    kc = compress(kc_ref, pk_ref, wk_ref)
    ko_ref[...] = _rms_rows(kc, gk_ref[...]).astype(ko_ref.dtype)
    vo_ref[...] = compress(vc_ref, pv_ref, wv_ref).astype(vo_ref.dtype)


def _nsa_compress(kc_r, vc_r, pos_k, pos_v, w_k, w_v, g_k):
    B, G, n16, width = kc_r.shape
    dh = HEAD_DIM
    half = NSA_CMP_LEN // 2
    dup = lambda w: jnp.concatenate([w, w], axis=-1).reshape(2, half * dh, 2 * dh).astype(BF16)
    spec_r = pl.BlockSpec((None, None, n16, width), lambda b, g: (b, g, 0, 0))
    spec_o = pl.BlockSpec((None, None, n16, 2 * dh), lambda b, g: (b, g, 0, 0))
    const2 = lambda shape: pl.BlockSpec(shape, lambda b, g: (0,) * len(shape))
    out = jax.ShapeDtypeStruct((B, G, n16, 2 * dh), BF16)
    return pl.pallas_call(
        functools.partial(_nsa_compress_kernel, n16=n16),
        grid=(B, G),
        in_specs=[spec_r, spec_r, const2((2, width)), const2((2, width)),
                  const2((2, width, 2 * dh)), const2((2, width, 2 * dh)), const2((1, 2 * dh))],
        out_specs=[spec_o, spec_o],
        out_shape=[out, out],
        compiler_params=_cparams("parallel", "parallel"),
        name="nsa_compress",
    )(kc_r, vc_r, pos_k.reshape(2, width), pos_v.reshape(2, width), dup(w_k), dup(w_v),
      jnp.tile(g_k, 2).reshape(1, 2 * dh))


def _stack_heads(q_ref, extra=None, rows_in=slice(None)):
    lane = lax.broadcasted_iota(jnp.int32, (1, LANES), 1)
    upper = lane >= HEAD_DIM
    rows = []
    for j in range(2):
        slab = q_ref[rows_in, j * LANES:(j + 1) * LANES]
        zero = jnp.zeros_like(slab)
        for i in range(2):
            qh = jnp.where(upper, slab, zero) if i else jnp.where(upper, zero, slab)
            rows.append(qh if extra is None else jnp.concatenate([qh, extra], axis=1))
    return jnp.concatenate(rows, axis=0)


def _unstack_heads(o, tq):
    lane = lax.broadcasted_iota(jnp.int32, (1, LANES), 1)
    upper = lane >= HEAD_DIM
    slabs = []
    for j in range(2):
        lo = o[(2 * j) * tq:(2 * j + 1) * tq]
        hi = pltpu.roll(o[(2 * j + 1) * tq:(2 * j + 2) * tq], HEAD_DIM, 1)
        slabs.append(jnp.where(upper, hi, lo))
    return jnp.concatenate(slabs, axis=1)


def _group_lower(slab_ref, g):
    x = slab_ref[...].astype(F32)
    return jnp.where(g % 2 == 1, pltpu.roll(x, HEAD_DIM, 1), x)


def _both_halves(x):
    lane = lax.broadcasted_iota(jnp.int32, (1, LANES), 1)
    return jnp.where(lane >= HEAD_DIM, pltpu.roll(x, HEAD_DIM, 1), x)


def _with_ones(x):
    lane = lax.broadcasted_iota(jnp.int32, (1, LANES), 1)
    return jnp.where(lane >= HEAD_DIM, 1.0, x)


def _normalise(acc):
    return acc / pltpu.roll(acc, HEAD_DIM, 1)


def _nsa_cmp_attn_kernel(q_ref, kc_ref, vc_ref, cover_ref, oc_ref, sel_ref, *, tq, n16):
    qi = pl.program_id(2)
    q4 = _stack_heads(q_ref)
    s = lax.dot_general(q4, kc_ref[...], _NT, preferred_element_type=F32)
    qpos = qi * tq + lax.broadcasted_iota(jnp.int32, (tq, n16), 0)
    cmp_end = lax.broadcasted_iota(jnp.int32, (tq, n16), 1) * NSA_CMP_STRIDE + (NSA_CMP_LEN - 1)
    valid = jnp.concatenate([cmp_end <= qpos] * 4, axis=0)
    s = jnp.where(valid, s, NEG_INF)
    e = jnp.exp(s - jnp.max(s, axis=1, keepdims=True))
    p = jnp.where(valid, e / jnp.sum(e, axis=1, keepdims=True), 0.0)
    o = jnp.dot(p.astype(BF16), vc_ref[...], preferred_element_type=F32)
    oc_ref[...] = _unstack_heads(o, tq).astype(oc_ref.dtype)

    psum = p[0:tq] + p[tq:2 * tq] + p[2 * tq:3 * tq] + p[3 * tq:4 * tq]
    cover = cover_ref[...]
    hi = psum.astype(BF16)
    r1 = psum - hi.astype(F32)
    mid = r1.astype(BF16)
    lo = (r1 - mid.astype(F32)).astype(BF16)
    imp = (jnp.dot(hi, cover, preferred_element_type=F32) + jnp.dot(mid, cover, preferred_element_type=F32)
           + jnp.dot(lo, cover, preferred_element_type=F32))

    blk = lax.broadcasted_iota(jnp.int32, (tq, LANES), 1)
    cur = (qi * tq + lax.broadcasted_iota(jnp.int32, (tq, LANES), 0)) // NSA_SEL_LEN
    forced = (blk == 0) | (blk == cur) | (blk == cur - 1)
    vals = jnp.where(forced, NSA_FORCED_SCORE, jnp.where(blk <= cur, imp, -1.0))
    taken = -3e38
    for _ in range(NSA_N_SEL):
        first = jnp.argmax(vals, axis=1, keepdims=True)
        vals = jnp.where(blk == first, taken, vals)
    sel_ref[...] = jnp.where(vals == taken, 0.0, NEG_INF).astype(sel_ref.dtype)


def _nsa_cmp_attn(qk, kcmp, vcmp, *, batch, seq):
    T = qk.shape[0]
    G = NSA_KV_HEADS
    tq = _pick_tile(seq, 512)
    nq = seq // tq
    n16 = kcmp.shape[2]
    n_sel = seq // NSA_SEL_LEN
    assert n_sel <= LANES and seq % tq == 0
    n = np.arange(n16)[:, None] * NSA_CMP_STRIDE
    j = np.arange(LANES)[None, :] * NSA_SEL_LEN
    cover = (n < j + NSA_SEL_LEN) & (n + NSA_CMP_LEN > j) & (np.arange(LANES)[None, :] < n_sel)
    cover = jnp.asarray(cover.astype(np.float32), BF16)
    spec_c = pl.BlockSpec((None, None, n16, LANES), lambda b, g, q: (b, g, 0, 0))
    return pl.pallas_call(
        functools.partial(_nsa_cmp_attn_kernel, tq=tq, n16=n16),
        grid=(batch, G, nq),
        in_specs=[pl.BlockSpec((tq, 2 * LANES), lambda b, g, q: (b * nq + q, g)), spec_c, spec_c,
                  pl.BlockSpec((n16, LANES), lambda b, g, q: (0, 0))],
        out_specs=[pl.BlockSpec((tq, 2 * LANES), lambda b, g, q: (b * nq + q, g)),
                   pl.BlockSpec((tq, LANES), lambda b, g, q: (b * nq + q, g))],
        out_shape=[jax.ShapeDtypeStruct((T, N_HEADS * HEAD_DIM), BF16),
                   jax.ShapeDtypeStruct((T, G * LANES), BF16)],
        compiler_params=_cparams("parallel", "parallel", "parallel"),
        name="nsa_cmp_attn",
    )(qk, kcmp, vcmp, cover)


def _nsa_sel_attn_kernel(q_ref, k_ref, v_ref, sel_ref, o_ref, kaug_ref, vaug_ref, sa_ref, sb_ref, m_ref, acc_ref,
                         *, tq, sub, tk, seq):
    g = pl.program_id(1)
    qi = pl.program_id(2)

    @pl.when(qi == 0)
    def _():
        kaug_ref[:, 0:LANES] = _both_halves(_group_lower(k_ref, g)).astype(BF16)
        blk_of_key = lax.broadcasted_iota(jnp.int32, (seq, LANES), 0) // NSA_SEL_LEN
        blk = lax.broadcasted_iota(jnp.int32, (seq, LANES), 1)
        kaug_ref[:, LANES:2 * LANES] = jnp.where(blk_of_key == blk, 1.0, 0.0).astype(BF16)
        vaug_ref[...] = _with_ones(_group_lower(v_ref, g)).astype(BF16)

    chains = range(tq // sub)
    rows = [slice(c * sub, (c + 1) * sub) for c in chains]
    q4 = [_stack_heads(q_ref, extra=sel_ref[rows[c], :], rows_in=rows[c]) for c in chains]
    q0 = qi * tq
    n_full = q0 // tk

    def produce(s_ref, j):
        kt = kaug_ref[pl.ds(pl.multiple_of(j * tk, tk), tk), :]
        for c in chains:
            s_ref[c] = lax.dot_general(q4[c], kt, _NT, preferred_element_type=F32)

    def consume(s_ref, j, masked):
        keys = pl.ds(pl.multiple_of(j * tk, tk), tk)
        if masked:
            kpos = j * tk + lax.broadcasted_iota(jnp.int32, (sub, tk), 1)
            qpos = q0 + lax.broadcasted_iota(jnp.int32, (sub, tk), 0)
            ok = [jnp.concatenate([kpos <= qpos + c * sub] * 4, axis=0) for c in chains]
            score = lambda c: jnp.where(ok[c], s_ref[c], NEG_INF)
        else:
            score = lambda c: s_ref[c]
        vt = vaug_ref[keys, :]
        m_old = [m_ref[c] for c in chains]
        m_new = [jnp.maximum(m_old[c], jnp.max(score(c), axis=1, keepdims=True)) for c in chains]
        p = [jnp.exp(score(c) - jnp.tile(m_new[c], (1, tk // LANES))).astype(BF16) for c in chains]
        for c in chains:
            m_ref[c] = m_new[c]
            acc_ref[c] = (jnp.exp(m_old[c] - m_new[c]) * acc_ref[c]
                          + jnp.dot(p[c], vt, preferred_element_type=F32))

    m_ref[...] = jnp.full(m_ref.shape, NEG_INF, F32)
    acc_ref[...] = jnp.zeros(acc_ref.shape, F32)
    produce(sa_ref, 0)

    def body(jj, _):
        produce(sb_ref, 2 * jj + 1)
        consume(sa_ref, 2 * jj, False)
        produce(sa_ref, 2 * jj + 2)
        consume(sb_ref, 2 * jj + 1, False)
        return 0

    lax.fori_loop(0, n_full // 2, body, 0)

    @pl.when(n_full % 2 == 0)
    def _():
        consume(sa_ref, n_full, True)

    @pl.when(n_full % 2 == 1)
    def _():
        produce(sb_ref, n_full)
        consume(sa_ref, n_full - 1, False)
        consume(sb_ref, n_full, True)

    for c in chains:
        o_ref[rows[c], :] = _unstack_heads(_normalise(acc_ref[c]), sub).astype(o_ref.dtype)


def _nsa_sel_attn(qk, vv, selbias, *, batch, seq, k_col, v_col):
    T = qk.shape[0]
    G = NSA_KV_HEADS
    tq = _pick_tile(seq, 512)
    sub = tq
    tk = _pick_tile(seq, 512)
    nq = seq // tq
    assert tk % tq == 0
    return pl.pallas_call(
        functools.partial(_nsa_sel_attn_kernel, tq=tq, sub=sub, tk=tk, seq=seq),
        grid=(batch, G, nq),
        in_specs=[pl.BlockSpec((tq, 2 * LANES), lambda b, g, q: (b * nq + q, g)),
                  pl.BlockSpec((seq, LANES), lambda b, g, q: (b, k_col + g // 2)),
                  pl.BlockSpec((seq, LANES), lambda b, g, q: (b, v_col + g // 2)),
                  pl.BlockSpec((tq, LANES), lambda b, g, q: (b * nq + q, g))],
        out_specs=pl.BlockSpec((tq, 2 * LANES), lambda b, g, q: (b * nq + q, g)),
        out_shape=jax.ShapeDtypeStruct((T, N_HEADS * HEAD_DIM), BF16),
        scratch_shapes=[pltpu.VMEM((seq, 2 * LANES), BF16), pltpu.VMEM((seq, LANES), BF16),
                        pltpu.VMEM((tq // sub, 4 * sub, tk), F32), pltpu.VMEM((tq // sub, 4 * sub, tk), F32),
                        pltpu.VMEM((tq // sub, 4 * sub, LANES), F32),
                        pltpu.VMEM((tq // sub, 4 * sub, LANES), F32)],
        compiler_params=_cparams("parallel", "arbitrary", "arbitrary"),
        name="nsa_sel_attn",
    )(qk, qk, vv, selbias)


def _nsa_win_attn_kernel(q_ref, k_ref, v_ref, o_ref, kb_ref, vaug_ref, *, tq, span):
    g = pl.program_id(1)
    qi = pl.program_id(2)

    @pl.when(qi == 0)
    def _():
        kb_ref[...] = _both_halves(_group_lower(k_ref, g)).astype(BF16)
        vaug_ref[...] = _with_ones(_group_lower(v_ref, g)).astype(BF16)

    q4 = _stack_heads(q_ref)
    q0 = qi * tq
    k_start = pl.multiple_of(jnp.maximum(q0 - NSA_WINDOW, 0), tq)
    keys = pl.ds(k_start, span)
    dist = (q0 - k_start) + (lax.broadcasted_iota(jnp.int32, (tq, span), 0)
                             - lax.broadcasted_iota(jnp.int32, (tq, span), 1))
    ok = (dist >= 0) & (dist < NSA_WINDOW)
    ok2 = jnp.concatenate([ok, ok], axis=0)
    pairs = [q4[0:2 * tq], q4[2 * tq:4 * tq]]
    s = [lax.dot_general(qp, kb_ref[keys, :], _NT, preferred_element_type=F32) for qp in pairs]
    s = [jnp.where(ok2, sc, NEG_INF) for sc in s]
    p = [jnp.exp(sc - jnp.max(sc, axis=1, keepdims=True)).astype(BF16) for sc in s]
    acc = [jnp.dot(pc, vaug_ref[keys, :], preferred_element_type=F32) for pc in p]
    acc = jnp.concatenate(acc, axis=0)
    o_ref[...] = _unstack_heads(_normalise(acc), tq).astype(o_ref.dtype)


def _nsa_win_attn(qk, vv, *, batch, seq, k_col, v_col):
    T = qk.shape[0]
    G = NSA_KV_HEADS
    tq = _pick_tile(seq, 256)
    nq = seq // tq
    span = NSA_WINDOW + tq
    assert NSA_WINDOW % tq == 0 and seq >= span
    return pl.pallas_call(
        functools.partial(_nsa_win_attn_kernel, tq=tq, span=span),
        grid=(batch, G, nq),
        in_specs=[pl.BlockSpec((tq, 2 * LANES), lambda b, g, q: (b * nq + q, g)),
                  pl.BlockSpec((seq, LANES), lambda b, g, q: (b, k_col + g // 2)),
                  pl.BlockSpec((seq, LANES), lambda b, g, q: (b, v_col + g // 2))],
        out_specs=pl.BlockSpec((tq, 2 * LANES), lambda b, g, q: (b * nq + q, g)),
        out_shape=jax.ShapeDtypeStruct((T, N_HEADS * HEAD_DIM), BF16),
        scratch_shapes=[pltpu.VMEM((seq, LANES), BF16), pltpu.VMEM((seq, LANES), BF16)],
        compiler_params=_cparams("parallel", "arbitrary", "arbitrary"),
        name="nsa_win_attn",
    )(qk, qk, vv)


def _nsa_mixer(x, g, w_in, pos_k, pos_v, w_cmp_k, w_cmp_v, g_q, g_k, w_out, *, batch, seq):
    HD = N_HEADS * HEAD_DIM
    G, dh = NSA_KV_HEADS, HEAD_DIM
    kvw = G * dh
    scale = dh ** -0.5
    col = lambda n: slice(HD + n * kvw, HD + (n + 1) * kvw)
    order = [slice(0, HD), col(0), col(2), col(4), col(1), col(3), col(5)]
    w = jnp.concatenate([w_in[:, c] for c in order], axis=1).astype(BF16)
    n_gate = 3 * N_HEADS
    w_gate = jnp.pad(w_in[:, HD + 6 * kvw:], ((0, 0), (0, LANES - n_gate))).astype(BF16)
    head_gain = jnp.concatenate([jnp.tile(g_q * scale, N_HEADS), jnp.tile(g_k, 3 * G)])

    inv = ROPE_THETA ** (-jnp.arange(0, dh, 2, dtype=F32) / dh)
    ang = jnp.arange(seq, dtype=F32)[:, None] * inv[None, :]
    cos, sin = jnp.cos(ang), jnp.sin(ang)
    cos_t = jnp.tile(cos, (1, PROJ_CHUNK // (dh // 2)))
    sin_t = jnp.tile(jnp.concatenate([-sin, sin], axis=1), (1, PROJ_CHUNK // dh))

    qkv, gate = _proj(x, g, w, seq=seq, head_gain=head_gain, rope=(cos_t, sin_t), w_aux=w_gate)

    def blocks16(t):
        half = NSA_CMP_LEN // 2
        t = t.reshape(batch, seq // half, half, G, dh).transpose(0, 3, 1, 2, 4)
        return t.reshape(batch, G, seq // half, half * dh)

    kc0, vc0 = HD, HD + 3 * kvw
    kcmp, vcmp = _nsa_compress(blocks16(qkv[:, kc0:kc0 + kvw]), blocks16(qkv[:, vc0:vc0 + kvw]),
                               pos_k, pos_v, w_cmp_k, w_cmp_v, g_k)
    o_c, selbias = _nsa_cmp_attn(qkv, kcmp, vcmp, batch=batch, seq=seq)
    o_s = _nsa_sel_attn(qkv, qkv, selbias, batch=batch, seq=seq,
                        k_col=(HD + kvw) // LANES, v_col=(HD + 4 * kvw) // LANES)
    o_w = _nsa_win_attn(qkv, qkv, batch=batch, seq=seq,
                        k_col=(HD + 2 * kvw) // LANES, v_col=(HD + 5 * kvw) // LANES)
    r = np.arange(LANES)[:, None]
    c = np.arange(HD)[None, :] // dh
    expand = np.stack([(r == 3 * c + b) for b in range(3)]).astype(np.float32)
    return _outproj(x, w_out.astype(BF16), mode="gated3",
                    acts=[o_c, o_s, o_w, gate, jnp.asarray(expand, BF16)])


def kernel(x, norm_g, ffn1_w_gate, ffn1_w_up, ffn1_w_down, ffn2_w_gate, ffn2_w_up, ffn2_w_down, fox_w_in, fox_b_f, fox_g_q, fox_g_k, fox_w_out, nsa_w_in, nsa_cmp_pos_k, nsa_cmp_pos_v, nsa_w_cmp_k, nsa_w_cmp_v, nsa_g_q, nsa_g_k, nsa_w_out, gla_w_in, gla_w_gate_up, gla_b_gate, gla_g_out, gla_w_out, sb_w_in, sb_w_out):
    B, S, D = x.shape
    depth = norm_g.shape[0]
    n_mixers = 4
    x2 = x.reshape(B * S, D)
    for i in range(depth):
        m, j = i % n_mixers, i // n_mixers
        x2 = _ffn(x2, norm_g[i, 0], ffn1_w_gate[i].astype(BF16), ffn1_w_up[i].astype(BF16),
                  ffn1_w_down[i].astype(BF16))
        g = norm_g[i, 1]
        if m == 0:
            x2 = _fox_mixer(x2, g, fox_w_in[j], fox_b_f[j], fox_g_q[j], fox_g_k[j], fox_w_out[j], batch=B, seq=S)
        elif m == 1:
            x2 = _nsa_mixer(x2, g, nsa_w_in[j], nsa_cmp_pos_k[j], nsa_cmp_pos_v[j], nsa_w_cmp_k[j],
                            nsa_w_cmp_v[j], nsa_g_q[j], nsa_g_k[j], nsa_w_out[j], batch=B, seq=S)
        elif m == 2:
            x2 = _gla_mixer(x2, g, gla_w_in[j], gla_w_gate_up[j], gla_b_gate[j], gla_g_out[j], gla_w_out[j],
                            batch=B, seq=S)
        else:
            x2 = _sb_mixer(x2, g, sb_w_in[j], sb_w_out[j], batch=B, seq=S)
        x2 = _ffn(x2, norm_g[i, 2], ffn2_w_gate[i].astype(BF16), ffn2_w_up[i].astype(BF16),
                  ffn2_w_down[i].astype(BF16))
    return x2.reshape(B, S, D)
```

```python
import functools

import numpy as np
import jax
import jax.numpy as jnp
from jax import lax
from jax.experimental import pallas as pl
from jax.experimental.pallas import tpu as pltpu

F32 = jnp.float32
BF16 = jnp.bfloat16

N_HEADS = 16
HEAD_DIM = 64
ROPE_THETA = 10000.0
RMS_EPS = 1e-6
NEG_INF = -1e30
NSA_KV_HEADS = 4
NSA_CMP_LEN = 32
NSA_CMP_STRIDE = 16
NSA_SEL_LEN = 64
NSA_N_SEL = 16
NSA_WINDOW = 512
NSA_FORCED_SCORE = 1e9
GLA_HEADS = 4
GLA_GATE_RANK = 16
GLA_TAU = 16.0
GLA_CHUNK = 64

LANES = 128
V7X_VMEM_BYTES = 64 * 1024 * 1024
VMEM_LIMIT = V7X_VMEM_BYTES - 8 * 1024 * 1024

_NT = (((1,), (1,)), ((), ()))
_TN = (((0,), (0,)), ((), ()))


def _cparams(*sem):
    return pltpu.CompilerParams(dimension_semantics=sem, vmem_limit_bytes=VMEM_LIMIT)


def _rms_rows(x, g):
    return x * lax.rsqrt(jnp.mean(x * x, axis=-1, keepdims=True) + RMS_EPS) * g


def _softplus(z):
    return jnp.maximum(z, 0.0) + jnp.log(1.0 + jnp.exp(-jnp.abs(z)))


def _neg_abs(z):
    bits = lax.bitcast_convert_type(z, jnp.int32) | jnp.int32(-2 ** 31)
    return lax.bitcast_convert_type(bits, F32)


def _sigmoid(z):
    return 1.0 / (1.0 + jnp.exp(-z))


def _pick_tile(n, target):
    t = min(n, target)
    while n % t:
        t //= 2
    return t


def _ffn_kernel(x_ref, g_ref, wg_ref, wu_ref, wd_ref, o_ref, a_ref, *, tf):
    x = x_ref[...]
    h = _rms_rows(x, g_ref[...]).astype(BF16)
    for c in range(a_ref.shape[1] // tf):
        cols = slice(c * tf, (c + 1) * tf)
        gate = jnp.dot(h, wg_ref[:, cols], preferred_element_type=F32)
        up = jnp.dot(h, wu_ref[:, cols], preferred_element_type=F32)
        a_ref[:, cols] = (gate * _sigmoid(gate) * up).astype(BF16)
    o_ref[...] = x + 0.5 * jnp.dot(a_ref[...], wd_ref[...], preferred_element_type=F32)


def _resident(shape):
    return pl.BlockSpec(shape, lambda *_: (0,) * len(shape), pipeline_mode=pl.Buffered(1))


def _ffn(x, g, wg, wu, wd):
    T, D = x.shape
    F = wg.shape[1]
    tm = _pick_tile(T, 1024)
    tf = 256 if F % 256 == 0 else F
    return pl.pallas_call(
        functools.partial(_ffn_kernel, tf=tf),
        grid=(T // tm,),
        in_specs=[
            pl.BlockSpec((tm, D), lambda i: (i, 0)),
            _resident((1, D)),
            _resident((D, F)),
            _resident((D, F)),
            _resident((F, D)),
        ],
        out_specs=pl.BlockSpec((tm, D), lambda i: (i, 0)),
        out_shape=jax.ShapeDtypeStruct((T, D), F32),
        scratch_shapes=[pltpu.VMEM((tm, F), BF16)],
        compiler_params=_cparams("parallel"),
        name="ffn",
    )(x, g.reshape(1, D), wg, wu, wd)


PROJ_CHUNK = 256


def _proj_kernel(*refs, n_norm, rope, has_aux):
    it = iter(refs)
    x_ref, g_ref, w_ref = next(it), next(it), next(it)
    hg_ref, gsum_ref = (next(it), next(it)) if n_norm else (None, None)
    cos_ref, sin_ref = (next(it), next(it)) if rope else (None, None)
    waux_ref = next(it) if has_aux else None
    o_ref = next(it)
    ch = PROJ_CHUNK
    h = _rms_rows(x_ref[...], g_ref[...]).astype(BF16)
    for c in range(o_ref.shape[1] // ch):
        cols = slice(c * ch, (c + 1) * ch)
        y = jnp.dot(h, w_ref[:, cols], preferred_element_type=F32)
        if c * ch < n_norm:
            ss = jnp.dot((y * y).astype(BF16), gsum_ref[...], preferred_element_type=F32)
            y = y * lax.rsqrt(ss * (1.0 / HEAD_DIM) + RMS_EPS) * hg_ref[:, cols]
            if rope:
                lane = lax.broadcasted_iota(jnp.int32, (1, ch), 1)
                first_half = (lane % HEAD_DIM) < (HEAD_DIM // 2)
                partner = jnp.where(first_half,
                                    pltpu.roll(y, ch - HEAD_DIM // 2, 1),
                                    pltpu.roll(y, HEAD_DIM // 2, 1))
                y = y * cos_ref[...] + partner * sin_ref[...]
        o_ref[:, cols] = y.astype(o_ref.dtype)
    if has_aux:
        next(it)[...] = jnp.dot(h, waux_ref[...], preferred_element_type=F32)


def _proj(x, g, w, *, seq, head_gain=None, rope=None, w_aux=None):
    T, D = x.shape
    N = w.shape[1]
    tm = _pick_tile(seq, 1024)
    ch = PROJ_CHUNK
    n_norm = 0 if head_gain is None else head_gain.shape[0]
    assert N % ch == 0 and n_norm % ch == 0 and T % tm == 0
    args = [x, g.reshape(1, D), w]
    in_specs = [pl.BlockSpec((tm, D), lambda i: (i, 0)), _resident((1, D)), _resident((D, N))]
    if n_norm:
        gidx = np.arange(ch) // HEAD_DIM
        gsum = jnp.asarray((gidx[:, None] == gidx[None, :]).astype(np.float32), BF16)
        args += [head_gain.reshape(1, n_norm), gsum]
        in_specs += [_resident((1, n_norm)), _resident((ch, ch))]
    if rope is not None:
        nblk = seq // tm
        args += list(rope)
        in_specs += [pl.BlockSpec((tm, ch), lambda i: (i % nblk, 0))] * 2
    out_specs = [pl.BlockSpec((tm, N), lambda i: (i, 0))]
    out_shape = [jax.ShapeDtypeStruct((T, N), BF16)]
    if w_aux is not None:
        args.append(w_aux)
        in_specs.append(_resident(w_aux.shape))
        out_specs.append(pl.BlockSpec((tm, w_aux.shape[1]), lambda i: (i, 0)))
        out_shape.append(jax.ShapeDtypeStruct((T, w_aux.shape[1]), F32))
    out = pl.pallas_call(
        functools.partial(_proj_kernel, n_norm=n_norm, rope=rope is not None, has_aux=w_aux is not None),
        grid=(T // tm,),
        in_specs=in_specs,
        out_specs=out_specs,
        out_shape=out_shape,
        compiler_params=_cparams("parallel"),
        name="proj",
    )(*args)
    return out if w_aux is not None else out[0]


def _outproj_kernel(*refs, mode):
    if mode == "plain":
        x_ref, a_ref, w_ref, o_ref = refs
        a = a_ref[...]
    elif mode == "sigmoid_gate":
        x_ref, a_ref, og_ref, w_ref, o_ref = refs
        a = (a_ref[...].astype(F32) * _sigmoid(og_ref[...].astype(F32))).astype(BF16)
    else:
        x_ref, oc_ref, os_ref, ow_ref, gate_ref, exp_ref, w_ref, o_ref = refs
        gs = _sigmoid(gate_ref[...]).astype(BF16)
        a = None
        for c, o_c in enumerate((oc_ref, os_ref, ow_ref)):
            ge = jnp.dot(gs, exp_ref[c], preferred_element_type=F32)
            t = ge * o_c[...].astype(F32)
            a = t if a is None else a + t
        a = a.astype(BF16)
    o_ref[...] = x_ref[...] + jnp.dot(a, w_ref[...], preferred_element_type=F32)


def _outproj(x, w, *, mode, acts):
    T, D = x.shape
    K = w.shape[0]
    tm = _pick_tile(T, 512)
    row = lambda i: (i, 0)
    args = [x]
    in_specs = [pl.BlockSpec((tm, D), row)]
    for a in acts:
        if a.ndim == 3:
            args.append(a)
            in_specs.append(pl.BlockSpec(a.shape, lambda i: (0, 0, 0)))
        else:
            args.append(a)
            in_specs.append(pl.BlockSpec((tm, a.shape[1]), row))
    args.append(w)
    in_specs.append(pl.BlockSpec((K, D), lambda i: (0, 0)))
    return pl.pallas_call(
        functools.partial(_outproj_kernel, mode=mode),
        grid=(T // tm,),
        in_specs=in_specs,
        out_specs=pl.BlockSpec((tm, D), row),
        out_shape=jax.ShapeDtypeStruct((T, D), F32),
        compiler_params=_cparams("parallel"),
        name="outproj_" + mode,
    )(*args)


def _fox_c_kernel(x_ref, g_ref, wf_ref, bf_ref, tri_ref, c_ref, carry_ref):
    @pl.when(pl.program_id(1) == 0)
    def _():
        carry_ref[...] = jnp.zeros_like(carry_ref)

    h = _rms_rows(x_ref[...], g_ref[...])
    f = lax.dot_general(wf_ref[...], h, _NT, precision=lax.Precision.HIGHEST,
                        preferred_element_type=F32) + bf_ref[...]
    ls = jnp.minimum(f, 0.0) - jnp.log(1.0 + jnp.exp(-jnp.abs(f)))
    cs = jnp.dot(ls, tri_ref[...], precision=lax.Precision.HIGHEST,
                 preferred_element_type=F32) + carry_ref[...]
    c_ref[...] = cs
    carry_ref[...] = cs[:, cs.shape[1] - 1:]


def _fox_c(x, g, wf_t, b_f, *, batch, seq):
    T, D = x.shape
    H = wf_t.shape[0]
    tm = _pick_tile(seq, 512)
    ns = seq // tm
    tri = jnp.asarray(np.triu(np.ones((tm, tm), np.float32)))
    return pl.pallas_call(
        _fox_c_kernel,
        grid=(batch, ns),
        in_specs=[
            pl.BlockSpec((tm, D), lambda b, s: (b * ns + s, 0)),
            pl.BlockSpec((1, D), lambda b, s: (0, 0)),
            pl.BlockSpec((H, D), lambda b, s: (0, 0)),
            pl.BlockSpec((H, 1), lambda b, s: (0, 0)),
            pl.BlockSpec((tm, tm), lambda b, s: (0, 0)),
        ],
        out_specs=pl.BlockSpec((None, H, tm), lambda b, s: (b, 0, s)),
        out_shape=jax.ShapeDtypeStruct((batch, H, seq), F32),
        scratch_shapes=[pltpu.VMEM((H, 1), F32)],
        compiler_params=_cparams("parallel", "arbitrary"),
        name="fox_c",
    )(x, g.reshape(1, D), wf_t, b_f.reshape(H, 1), tri)


def _fox_attn_kernel(q_ref, k_ref, v_ref, c_ref, o_ref, vaug_ref, sa_ref, sb_ref, m_ref, acc_ref, *, tq, tk):
    qi = pl.program_id(2)
    lane = lax.broadcasted_iota(jnp.int32, (1, LANES), 1)
    upper = lane >= HEAD_DIM

    @pl.when(qi == 0)
    def _():
        v = v_ref[...]
        one = jnp.ones_like(v)
        vaug_ref[0] = jnp.where(upper, one, v)
        vaug_ref[1] = jnp.where(upper, v, one)

    q0 = pl.multiple_of(qi * tq, tq)
    qslab = q_ref[...]
    zero = jnp.zeros_like(qslab)
    qpos = lax.broadcasted_iota(jnp.int32, (tq, tk), 0)
    kpos = lax.broadcasted_iota(jnp.int32, (tq, tk), 1)

    heads = range(2)
    q2 = jnp.concatenate([jnp.where(upper, zero, qslab), jnp.where(upper, qslab, zero)], axis=0)
    c_q0 = [c_ref[i, :, pl.ds(q0, LANES)][:, 0:1] for i in heads]

    n_diag = tq // tk
    assert n_diag == 2

    def produce(s_ref, j):
        keys = pl.ds(pl.multiple_of(j * tk, tk), tk)
        s2 = lax.dot_general(q2, k_ref[keys, :], _NT, preferred_element_type=F32)
        for i in heads:
            s_ref[i] = s2[i * tq:(i + 1) * tq] + (c_q0[i] - c_ref[i, :, keys])

    def consume(s_ref, j, diag_offset=None):
        keys = pl.ds(pl.multiple_of(j * tk, tk), tk)
        if diag_offset is None:
            score = lambda i: s_ref[i]
        else:
            causal = kpos + diag_offset <= qpos
            score = lambda i: jnp.where(causal, s_ref[i], NEG_INF)
        m_old = [m_ref[i] for i in heads]
        m_new = [jnp.maximum(m_old[i], jnp.max(score(i), axis=1, keepdims=True)) for i in heads]
        p = [jnp.exp(score(i) - jnp.tile(m_new[i], (1, tk // LANES))).astype(BF16) for i in heads]
        for i in heads:
            m_ref[i] = m_new[i]
            acc_ref[i] = (jnp.exp(m_old[i] - m_new[i]) * acc_ref[i]
                          + jnp.dot(p[i], vaug_ref[i, keys, :], preferred_element_type=F32))

    m_ref[...] = jnp.full(m_ref.shape, NEG_INF, F32)
    acc_ref[...] = jnp.zeros(acc_ref.shape, F32)
    produce(sa_ref, 0)

    def body(m, _):
        produce(sb_ref, 2 * m + 1)
        consume(sa_ref, 2 * m)
        produce(sa_ref, 2 * m + 2)
        consume(sb_ref, 2 * m + 1)
        return 0

    lax.fori_loop(0, qi, body, 0)
    produce(sb_ref, 2 * qi + 1)
    consume(sa_ref, 2 * qi, 0)
    consume(sb_ref, 2 * qi + 1, tk)

    accs = (acc_ref[0], acc_ref[1])
    num = jnp.where(upper, accs[1], accs[0])
    den = jnp.where(upper, accs[0], accs[1])
    den = pltpu.roll(den, HEAD_DIM, 1)
    o_ref[...] = (num / den).astype(o_ref.dtype)


def _fox_attn(qkvg, c, *, batch, seq):
    T = qkvg.shape[0]
    HD = N_HEADS * HEAD_DIM
    tq = _pick_tile(seq, 1024)
    tk = tq // 2
    nq = seq // tq
    npair = HD // LANES
    return pl.pallas_call(
        functools.partial(_fox_attn_kernel, tq=tq, tk=tk),
        grid=(batch, npair, nq),
        in_specs=[
            pl.BlockSpec((tq, LANES), lambda b, h, q: (b * nq + q, h)),
            pl.BlockSpec((seq, LANES), lambda b, h, q: (b, npair + h)),
            pl.BlockSpec((seq, LANES), lambda b, h, q: (b, 2 * npair + h)),
            pl.BlockSpec((None, 2, 1, seq), lambda b, h, q: (b, h, 0, 0)),
        ],
        out_specs=pl.BlockSpec((tq, LANES), lambda b, h, q: (b * nq + q, h)),
        out_shape=jax.ShapeDtypeStruct((T, HD), BF16),
        scratch_shapes=[pltpu.VMEM((2, seq, LANES), BF16),
                        pltpu.VMEM((2, tq, tk), F32), pltpu.VMEM((2, tq, tk), F32),
                        pltpu.VMEM((2, tq, LANES), F32), pltpu.VMEM((2, tq, LANES), F32)],
        compiler_params=_cparams("parallel", "arbitrary", "arbitrary"),
        name="fox_attn",
    )(qkvg, qkvg, qkvg, c.reshape(batch, N_HEADS, 1, seq))


def _fox_mixer(x, g, w_in, b_f, g_q, g_k, w_out, *, batch, seq):
    HD = N_HEADS * HEAD_DIM
    scale = HEAD_DIM ** -0.5
    w = jnp.concatenate([w_in[:, :3 * HD], w_in[:, 3 * HD + N_HEADS:]], axis=1).astype(BF16)
    wf_t = w_in[:, 3 * HD:3 * HD + N_HEADS].T
    head_gain = jnp.concatenate([jnp.tile(g_q * scale, N_HEADS), jnp.tile(g_k, N_HEADS)])
    qkvg = _proj(x, g, w, seq=seq, head_gain=head_gain)
    c = _fox_c(x, g, wf_t, b_f, batch=batch, seq=seq)
    o = _fox_attn(qkvg, c, batch=batch, seq=seq)
    return _outproj_fox(x, o, qkvg, w_out.astype(BF16))


def _outproj_fox(x, o, qkvg, w):
    T, D = x.shape
    HD = o.shape[1]
    tm = _pick_tile(T, 512)
    return pl.pallas_call(
        functools.partial(_outproj_kernel, mode="sigmoid_gate"),
        grid=(T // tm,),
        in_specs=[
            pl.BlockSpec((tm, D), lambda i: (i, 0)),
            pl.BlockSpec((tm, HD), lambda i: (i, 0)),
            pl.BlockSpec((tm, HD), lambda i: (i, 3)),
            pl.BlockSpec((HD, D), lambda i: (0, 0)),
        ],
        out_specs=pl.BlockSpec((tm, D), lambda i: (i, 0)),
        out_shape=jax.ShapeDtypeStruct((T, D), F32),
        compiler_params=_cparams("parallel"),
        name="outproj_fox",
    )(x, o, qkvg, w)


def _sb_attn_kernel(q_ref, k_ref, v_ref, tri_ref, o_ref, wa_ref, wb_ref, ta_ref, tb_ref, rest_ref, acc_ref,
                    *, tq, tk):
    qi = pl.program_id(2)
    lane = lax.broadcasted_iota(jnp.int32, (1, LANES), 1)
    upper = lane >= HEAD_DIM
    qslab = q_ref[...]
    zero = jnp.zeros_like(qslab)
    q2 = jnp.concatenate([jnp.where(upper, zero, qslab), jnp.where(upper, qslab, zero)], axis=0)
    rows = 2 * tq
    n_diag = tq // tk
    last = n_diag * (qi + 1) - 1

    def key_rows(k):
        return pl.ds(pl.multiple_of(jnp.maximum(last - k, 0) * tk, tk), tk)

    def head_rows(off):
        return (slice(off, tq), slice(tq + off, rows))

    def take(x, off):
        if off == 0:
            return x[...]
        lo, hi = head_rows(off)
        return jnp.concatenate([x[lo], x[hi]], axis=0)

    def put(ref, val, off, add=False):
        parts = [(slice(None), val)] if off == 0 else list(zip(head_rows(off), (val[:tq - off], val[tq - off:])))
        for sl, part in parts:
            if add:
                ref[sl] += part
            else:
                ref[sl] = part

    def scores(k, off=0):
        return lax.dot_general(take(q2, off), k_ref[key_rows(k), :], _NT, preferred_element_type=F32)

    def softplus_sums(z, w_ref, tot_ref, diag_offset, off=0):
        n2 = z.shape[0]
        sp = jnp.maximum(z, 0.0) + jnp.log2(1.0 + jnp.exp2(_neg_abs(z)))
        if diag_offset is not None:
            qpos = lax.broadcasted_iota(jnp.int32, (n2, tk), 0) % (tq - off) + off
            strict = lax.broadcasted_iota(jnp.int32, (n2, tk), 1) + diag_offset < qpos
            spm = jnp.where(strict, sp, 0.0)
        else:
            spm = sp
        cum = jnp.dot(spm.astype(BF16), tri_ref[...], preferred_element_type=F32)
        w = (z - sp) - cum
        if diag_offset is not None:
            w = jnp.where(strict, w, NEG_INF)
        put(w_ref, w, off)
        put(tot_ref, jnp.broadcast_to(cum[:, 0:1] + spm[:, 0:1], (n2, LANES)), off)

    def accumulate(w_ref, tot_ref, k, off=0):
        a = jnp.exp2(take(w_ref, off) - jnp.tile(take(rest_ref, off), (1, tk // LANES))).astype(BF16)
        put(acc_ref, jnp.dot(a, v_ref[key_rows(k), :], preferred_element_type=F32), off, add=True)
        put(rest_ref, take(tot_ref, off), off, add=True)

    assert n_diag % 2 == 0
    even, odd = (wa_ref, ta_ref), (wb_ref, tb_ref)
    offset = lambda k: (n_diag - 1 - k) * tk

    def produce_diag(buf, k):
        softplus_sums(scores(k, offset(k)), *buf, offset(k), offset(k))

    def step(k, cur, nxt, diag=False):
        off_cur, off_nxt = (offset(k), offset(k + 1)) if diag else (0, 0)
        z = scores(k + 1, off_nxt)
        accumulate(*cur, k, off_cur)
        softplus_sums(z, *nxt, off_nxt if diag else None, off_nxt)

    rest_ref[...] = jnp.zeros(rest_ref.shape, F32)
    acc_ref[...] = jnp.zeros(acc_ref.shape, F32)
    produce_diag(even, 0)
    produce_diag(odd, 1)
    accumulate(wa_ref, ta_ref, 0, offset(0))
    for m in range(1, n_diag // 2):
        step(2 * m - 1, odd, even, diag=True)
        step(2 * m, even, odd, diag=True)

    def body(m, _):
        step(2 * m - 1, odd, even)
        step(2 * m, even, odd)
        return 0

    lax.fori_loop(n_diag // 2, (n_diag // 2) * (qi + 1), body, 0)
    accumulate(wb_ref, tb_ref, last)
    o_ref[...] = jnp.where(upper, acc_ref[tq:2 * tq], acc_ref[0:tq]).astype(o_ref.dtype)


def _sb_attn(qkv, *, batch, seq):
    T = qkv.shape[0]
    HD = N_HEADS * HEAD_DIM
    tq = _pick_tile(seq, 1024)
    tk = 256
    nq = seq // tq
    npair = HD // LANES
    tri = jnp.asarray(np.tril(np.ones((tk, tk), np.float32), -1), BF16)
    return pl.pallas_call(
        functools.partial(_sb_attn_kernel, tq=tq, tk=tk),
        grid=(batch, npair, nq),
        in_specs=[
            pl.BlockSpec((tq, LANES), lambda b, h, q: (b * nq + q, h)),
            pl.BlockSpec((seq, LANES), lambda b, h, q: (b, npair + h)),
            pl.BlockSpec((seq, LANES), lambda b, h, q: (b, 2 * npair + h)),
            pl.BlockSpec((tk, tk), lambda b, h, q: (0, 0)),
        ],
        out_specs=pl.BlockSpec((tq, LANES), lambda b, h, q: (b * nq + q, h)),
        out_shape=jax.ShapeDtypeStruct((T, HD), BF16),
        scratch_shapes=[pltpu.VMEM((2 * tq, tk), F32)] * 2 + [pltpu.VMEM((2 * tq, LANES), F32)] * 4,
        compiler_params=_cparams("parallel", "parallel", "arbitrary"),
        name="sb_attn",
    )(qkv, qkv, qkv, tri)


def _sb_mixer(x, g, w_in, w_out, *, batch, seq):
    HD = N_HEADS * HEAD_DIM
    scale = HEAD_DIM ** -0.5 * float(np.log2(np.e))
    w = jnp.concatenate([w_in[:, :HD] * scale, w_in[:, HD:]], axis=1).astype(BF16)
    qkv = _proj(x, g, w, seq=seq)
    o = _sb_attn(qkv, batch=batch, seq=seq)
    return _outproj(x, w_out.astype(BF16), mode="plain", acts=[o])


def _gla_kernel(q_ref, k_ref, v_ref, r_ref, glow_ref, wgu_ref, bg_ref, gout_ref, tri_ref,
                o_ref, state_ref, *, tm, dk):
    C = GLA_CHUNK

    @pl.when(pl.program_id(2) == 0)
    def _():
        state_ref[...] = jnp.zeros_like(state_ref)

    gate = jnp.dot(glow_ref[...], wgu_ref[...], precision=lax.Precision.HIGHEST,
                   preferred_element_type=F32) + bg_ref[...]
    log_a = (jnp.minimum(gate, 0.0) - jnp.log(1.0 + jnp.exp(-jnp.abs(gate)))) * (1.0 / GLA_TAU)
    row = lax.broadcasted_iota(jnp.int32, (C, C), 0)
    col = lax.broadcasted_iota(jnp.int32, (C, C), 1)
    causal = col <= row
    q_scale = dk ** -0.5
    chunks = [slice(c * C, (c + 1) * C) for c in range(tm // C)]

    tri = tri_ref[...]
    b = [jnp.dot(tri, log_a[sl], precision=lax.Precision.HIGHEST, preferred_element_type=F32)
         for sl in chunks]
    b_last = jnp.concatenate([jnp.broadcast_to(bc[C - 1:C], bc.shape) for bc in b], axis=0)
    b = jnp.concatenate(b, axis=0)
    q = q_ref[...].astype(F32) * q_scale
    k = k_ref[...].astype(F32)
    q_dec = (q * jnp.exp(b)).astype(BF16)
    k_neg = (k * jnp.exp(-b)).astype(BF16)
    k_rem = (k * jnp.exp(b_last - b)).astype(BF16)
    decay = jnp.exp(b_last)
    attn = [lax.dot_general(q_dec[sl], k_neg[sl], _NT, preferred_element_type=F32) for sl in chunks]
    attn = [jnp.where(causal, a, 0.0).astype(BF16) for a in attn]
    o_intra = [jnp.dot(attn[c], v_ref[sl, :], preferred_element_type=F32) for c, sl in enumerate(chunks)]
    u_t = [lax.dot_general(v_ref[sl, :], k_rem[sl], _TN, preferred_element_type=F32) for sl in chunks]

    state_t = state_ref[...]
    outs = []
    for c, sl in enumerate(chunks):
        outs.append(o_intra[c] + lax.dot_general(q_dec[sl], state_t.astype(BF16), _NT,
                                                 preferred_element_type=F32))
        state_t = state_t * decay[c * C:c * C + 1] + u_t[c]
    state_ref[...] = state_t

    o = jnp.concatenate(outs, axis=0)
    y = o * lax.rsqrt(jnp.mean(o * o, axis=-1, keepdims=True) + RMS_EPS) * gout_ref[...]
    r = r_ref[...].astype(F32)
    o_ref[...] = (y * (r * _sigmoid(r))).astype(o_ref.dtype)


def _gla(qkvr, glow, wgu, b_gate, g_out, *, batch, seq):
    T = qkvr.shape[0]
    Hg = GLA_HEADS
    dk_total = wgu.shape[1]
    dk = dk_total // Hg
    dv = (qkvr.shape[1] - 2 * dk_total) // 2 // Hg
    tm = _pick_tile(seq, 512)
    ns = seq // tm
    kq, kv = dk_total // dk, dk_total // dv
    tri = jnp.asarray(np.tril(np.ones((GLA_CHUNK, GLA_CHUNK), np.float32)))
    rowmap = lambda off: (lambda b, h, s: (b * ns + s, off + h))
    return pl.pallas_call(
        functools.partial(_gla_kernel, tm=tm, dk=dk),
        grid=(batch, Hg, ns),
        in_specs=[
            pl.BlockSpec((tm, dk), rowmap(0)),
            pl.BlockSpec((tm, dk), rowmap(kq)),
            pl.BlockSpec((tm, dv), rowmap(2 * kv)),
            pl.BlockSpec((tm, dv), rowmap(2 * kv + Hg)),
            pl.BlockSpec((tm, LANES), lambda b, h, s: (b * ns + s, 0)),
            pl.BlockSpec((LANES, dk), lambda b, h, s: (0, h)),
            pl.BlockSpec((1, dk), lambda b, h, s: (0, h)),
            pl.BlockSpec((1, dv), lambda b, h, s: (0, 0)),
            pl.BlockSpec((GLA_CHUNK, GLA_CHUNK), lambda b, h, s: (0, 0)),
        ],
        out_specs=pl.BlockSpec((tm, dv), rowmap(0)),
        out_shape=jax.ShapeDtypeStruct((T, Hg * dv), BF16),
        scratch_shapes=[pltpu.VMEM((dv, dk), F32)],
        compiler_params=_cparams("parallel", "parallel", "arbitrary"),
        name="gla",
    )(qkvr, qkvr, qkvr, qkvr, glow, wgu, b_gate.reshape(1, dk_total), g_out.reshape(1, dv), tri)


def _gla_mixer(x, g, w_in, w_gate_up, b_gate, g_out, w_out, *, batch, seq):
    dk_total = w_gate_up.shape[1]
    dv_total = w_out.shape[0]
    lo = 2 * dk_total + dv_total
    w_main = jnp.concatenate([w_in[:, :lo], w_in[:, lo + GLA_GATE_RANK:]], axis=1).astype(BF16)
    w_low = jnp.pad(w_in[:, lo:lo + GLA_GATE_RANK], ((0, 0), (0, LANES - GLA_GATE_RANK))).astype(BF16)
    wgu = jnp.pad(w_gate_up, ((0, LANES - GLA_GATE_RANK), (0, 0)))
    qkvr, glow = _proj(x, g, w_main, seq=seq, w_aux=w_low)
    o = _gla(qkvr, glow, wgu, b_gate, g_out, batch=batch, seq=seq)
    return _outproj(x, w_out.astype(BF16), mode="plain", acts=[o])


def _nsa_compress_kernel(kc_ref, vc_ref, pk_ref, pv_ref, wk_ref, wv_ref, gk_ref, ko_ref, vo_ref, *, n16):
    def compress(r_ref, pos_ref, w_ref):
        r = r_ref[...].astype(F32)
        first = jnp.dot((r + pos_ref[0:1, :]).astype(BF16), w_ref[0], preferred_element_type=F32)
        second = jnp.dot((r + pos_ref[1:2, :]).astype(BF16), w_ref[1], preferred_element_type=F32)
        return first + pltpu.roll(second, n16 - 1, 0)

    kc = compress(kc_ref, pk_ref, wk_ref)
    ko_ref[...] = _rms_rows(kc, gk_ref[...]).astype(ko_ref.dtype)
    vo_ref[...] = compress(vc_ref, pv_ref, wv_ref).astype(vo_ref.dtype)


def _nsa_compress(kc_r, vc_r, pos_k, pos_v, w_k, w_v, g_k):
    B, G, n16, width = kc_r.shape
    dh = HEAD_DIM
    half = NSA_CMP_LEN // 2
    dup = lambda w: jnp.concatenate([w, w], axis=-1).reshape(2, half * dh, 2 * dh).astype(BF16)
    spec_r = pl.BlockSpec((None, None, n16, width), lambda b, g: (b, g, 0, 0))
    spec_o = pl.BlockSpec((None, None, n16, 2 * dh), lambda b, g: (b, g, 0, 0))
    const2 = lambda shape: pl.BlockSpec(shape, lambda b, g: (0,) * len(shape))
    out = jax.ShapeDtypeStruct((B, G, n16, 2 * dh), BF16)
    return pl.pallas_call(
        functools.partial(_nsa_compress_kernel, n16=n16),
        grid=(B, G),
        in_specs=[spec_r, spec_r, const2((2, width)), const2((2, width)),
                  const2((2, width, 2 * dh)), const2((2, width, 2 * dh)), const2((1, 2 * dh))],
        out_specs=[spec_o, spec_o],
        out_shape=[out, out],
        compiler_params=_cparams("parallel", "parallel"),
        name="nsa_compress",
    )(kc_r, vc_r, pos_k.reshape(2, width), pos_v.reshape(2, width), dup(w_k), dup(w_v),
      jnp.tile(g_k, 2).reshape(1, 2 * dh))


def _stack_heads(q_ref, extra=None, rows_in=slice(None)):
    lane = lax.broadcasted_iota(jnp.int32, (1, LANES), 1)
    upper = lane >= HEAD_DIM
    rows = []
    for j in range(2):
        slab = q_ref[rows_in, j * LANES:(j + 1) * LANES]
        zero = jnp.zeros_like(slab)
        for i in range(2):
            qh = jnp.where(upper, slab, zero) if i else jnp.where(upper, zero, slab)
            rows.append(qh if extra is None else jnp.concatenate([qh, extra], axis=1))
    return jnp.concatenate(rows, axis=0)


def _unstack_heads(o, tq):
    lane = lax.broadcasted_iota(jnp.int32, (1, LANES), 1)
    upper = lane >= HEAD_DIM
    slabs = []
    for j in range(2):
        lo = o[(2 * j) * tq:(2 * j + 1) * tq]
        hi = pltpu.roll(o[(2 * j + 1) * tq:(2 * j + 2) * tq], HEAD_DIM, 1)
        slabs.append(jnp.where(upper, hi, lo))
    return jnp.concatenate(slabs, axis=1)


def _group_lower(slab_ref, g):
    x = slab_ref[...].astype(F32)
    return jnp.where(g % 2 == 1, pltpu.roll(x, HEAD_DIM, 1), x)


def _both_halves(x):
    lane = lax.broadcasted_iota(jnp.int32, (1, LANES), 1)
    return jnp.where(lane >= HEAD_DIM, pltpu.roll(x, HEAD_DIM, 1), x)


def _with_ones(x):
    lane = lax.broadcasted_iota(jnp.int32, (1, LANES), 1)
    return jnp.where(lane >= HEAD_DIM, 1.0, x)


def _normalise(acc):
    return acc / pltpu.roll(acc, HEAD_DIM, 1)


def _nsa_cmp_attn_kernel(q_ref, kc_ref, vc_ref, cover_ref, oc_ref, sel_ref, *, tq, n16):
    qi = pl.program_id(2)
    q4 = _stack_heads(q_ref)
    s = lax.dot_general(q4, kc_ref[...], _NT, preferred_element_type=F32)
    qpos = qi * tq + lax.broadcasted_iota(jnp.int32, (tq, n16), 0)
    cmp_end = lax.broadcasted_iota(jnp.int32, (tq, n16), 1) * NSA_CMP_STRIDE + (NSA_CMP_LEN - 1)
    valid = jnp.concatenate([cmp_end <= qpos] * 4, axis=0)
    s = jnp.where(valid, s, NEG_INF)
    e = jnp.exp(s - jnp.max(s, axis=1, keepdims=True))
    p = jnp.where(valid, e / jnp.sum(e, axis=1, keepdims=True), 0.0)
    o = jnp.dot(p.astype(BF16), vc_ref[...], preferred_element_type=F32)
    oc_ref[...] = _unstack_heads(o, tq).astype(oc_ref.dtype)

    psum = p[0:tq] + p[tq:2 * tq] + p[2 * tq:3 * tq] + p[3 * tq:4 * tq]
    cover = cover_ref[...]
    hi = psum.astype(BF16)
    r1 = psum - hi.astype(F32)
    mid = r1.astype(BF16)
    lo = (r1 - mid.astype(F32)).astype(BF16)
    imp = (jnp.dot(hi, cover, preferred_element_type=F32) + jnp.dot(mid, cover, preferred_element_type=F32)
           + jnp.dot(lo, cover, preferred_element_type=F32))

    blk = lax.broadcasted_iota(jnp.int32, (tq, LANES), 1)
    cur = (qi * tq + lax.broadcasted_iota(jnp.int32, (tq, LANES), 0)) // NSA_SEL_LEN
    forced = (blk == 0) | (blk == cur) | (blk == cur - 1)
    vals = jnp.where(forced, NSA_FORCED_SCORE, jnp.where(blk <= cur, imp, -1.0))
    taken = -3e38
    for _ in range(NSA_N_SEL):
        first = jnp.argmax(vals, axis=1, keepdims=True)
        vals = jnp.where(blk == first, taken, vals)
    sel_ref[...] = jnp.where(vals == taken, 0.0, NEG_INF).astype(sel_ref.dtype)


def _nsa_cmp_attn(qk, kcmp, vcmp, *, batch, seq):
    T = qk.shape[0]
    G = NSA_KV_HEADS
    tq = _pick_tile(seq, 512)
    nq = seq // tq
    n16 = kcmp.shape[2]
    n_sel = seq // NSA_SEL_LEN
    assert n_sel <= LANES and seq % tq == 0
    n = np.arange(n16)[:, None] * NSA_CMP_STRIDE
    j = np.arange(LANES)[None, :] * NSA_SEL_LEN
    cover = (n < j + NSA_SEL_LEN) & (n + NSA_CMP_LEN > j) & (np.arange(LANES)[None, :] < n_sel)
    cover = jnp.asarray(cover.astype(np.float32), BF16)
    spec_c = pl.BlockSpec((None, None, n16, LANES), lambda b, g, q: (b, g, 0, 0))
    return pl.pallas_call(
        functools.partial(_nsa_cmp_attn_kernel, tq=tq, n16=n16),
        grid=(batch, G, nq),
        in_specs=[pl.BlockSpec((tq, 2 * LANES), lambda b, g, q: (b * nq + q, g)), spec_c, spec_c,
                  pl.BlockSpec((n16, LANES), lambda b, g, q: (0, 0))],
        out_specs=[pl.BlockSpec((tq, 2 * LANES), lambda b, g, q: (b * nq + q, g)),
                   pl.BlockSpec((tq, LANES), lambda b, g, q: (b * nq + q, g))],
        out_shape=[jax.ShapeDtypeStruct((T, N_HEADS * HEAD_DIM), BF16),
                   jax.ShapeDtypeStruct((T, G * LANES), BF16)],
        compiler_params=_cparams("parallel", "parallel", "parallel"),
        name="nsa_cmp_attn",
    )(qk, kcmp, vcmp, cover)


def _nsa_sel_attn_kernel(q_ref, k_ref, v_ref, sel_ref, o_ref, kaug_ref, vaug_ref, sa_ref, sb_ref, m_ref, acc_ref,
                         *, tq, sub, tk, seq):
    g = pl.program_id(1)
    qi = pl.program_id(2)

    @pl.when(qi == 0)
    def _():
        kaug_ref[:, 0:LANES] = _both_halves(_group_lower(k_ref, g)).astype(BF16)
        blk_of_key = lax.broadcasted_iota(jnp.int32, (seq, LANES), 0) // NSA_SEL_LEN
        blk = lax.broadcasted_iota(jnp.int32, (seq, LANES), 1)
        kaug_ref[:, LANES:2 * LANES] = jnp.where(blk_of_key == blk, 1.0, 0.0).astype(BF16)
        vaug_ref[...] = _with_ones(_group_lower(v_ref, g)).astype(BF16)

    chains = range(tq // sub)
    rows = [slice(c * sub, (c + 1) * sub) for c in chains]
    q4 = [_stack_heads(q_ref, extra=sel_ref[rows[c], :], rows_in=rows[c]) for c in chains]
    n_diag = tq // tk
    assert n_diag == 2

    def produce(s_ref, j):
        kt = kaug_ref[pl.ds(pl.multiple_of(j * tk, tk), tk), :]
        for c in chains:
            s_ref[c] = lax.dot_general(q4[c], kt, _NT, preferred_element_type=F32)

    def consume(s_ref, j, diag_offset=None):
        keys = pl.ds(pl.multiple_of(j * tk, tk), tk)
        if diag_offset is not None:
            kpos = diag_offset + lax.broadcasted_iota(jnp.int32, (sub, tk), 1)
            qpos = lax.broadcasted_iota(jnp.int32, (sub, tk), 0)
            ok = [jnp.concatenate([kpos <= qpos + c * sub] * 4, axis=0) for c in chains]
            score = lambda c: jnp.where(ok[c], s_ref[c], NEG_INF)
        else:
            score = lambda c: s_ref[c]
        vt = vaug_ref[keys, :]
        m_old = [m_ref[c] for c in chains]
        m_new = [jnp.maximum(m_old[c], jnp.max(score(c), axis=1, keepdims=True)) for c in chains]
        p = [jnp.exp(score(c) - jnp.tile(m_new[c], (1, tk // LANES))).astype(BF16) for c in chains]
        for c in chains:
            m_ref[c] = m_new[c]
            acc_ref[c] = (jnp.exp(m_old[c] - m_new[c]) * acc_ref[c]
                          + jnp.dot(p[c], vt, preferred_element_type=F32))

    m_ref[...] = jnp.full(m_ref.shape, NEG_INF, F32)
    acc_ref[...] = jnp.zeros(acc_ref.shape, F32)
    produce(sa_ref, 0)

    def body(m, _):
        produce(sb_ref, 2 * m + 1)
        consume(sa_ref, 2 * m)
        produce(sa_ref, 2 * m + 2)
        consume(sb_ref, 2 * m + 1)
        return 0

    lax.fori_loop(0, qi, body, 0)
    produce(sb_ref, 2 * qi + 1)
    consume(sa_ref, 2 * qi, 0)
    consume(sb_ref, 2 * qi + 1, tk)

    for c in chains:
        o_ref[rows[c], :] = _unstack_heads(_normalise(acc_ref[c]), sub).astype(o_ref.dtype)


def _nsa_sel_attn(qk, vv, selbias, *, batch, seq, k_col, v_col):
    T = qk.shape[0]
    G = NSA_KV_HEADS
    tq = _pick_tile(seq, 1024)
    sub = tq
    tk = tq // 2
    nq = seq // tq
    return pl.pallas_call(
        functools.partial(_nsa_sel_attn_kernel, tq=tq, sub=sub, tk=tk, seq=seq),
        grid=(batch, G, nq),
        in_specs=[pl.BlockSpec((tq, 2 * LANES), lambda b, g, q: (b * nq + q, g)),
                  pl.BlockSpec((seq, LANES), lambda b, g, q: (b, k_col + g // 2)),
                  pl.BlockSpec((seq, LANES), lambda b, g, q: (b, v_col + g // 2)),
                  pl.BlockSpec((tq, LANES), lambda b, g, q: (b * nq + q, g))],
        out_specs=pl.BlockSpec((tq, 2 * LANES), lambda b, g, q: (b * nq + q, g)),
        out_shape=jax.ShapeDtypeStruct((T, N_HEADS * HEAD_DIM), BF16),
        scratch_shapes=[pltpu.VMEM((seq, 2 * LANES), BF16), pltpu.VMEM((seq, LANES), BF16),
                        pltpu.VMEM((tq // sub, 4 * sub, tk), F32), pltpu.VMEM((tq // sub, 4 * sub, tk), F32),
                        pltpu.VMEM((tq // sub, 4 * sub, LANES), F32),
                        pltpu.VMEM((tq // sub, 4 * sub, LANES), F32)],
        compiler_params=_cparams("parallel", "arbitrary", "arbitrary"),
        name="nsa_sel_attn",
    )(qk, qk, vv, selbias)


def _nsa_win_attn_kernel(q_ref, k_ref, v_ref, o_ref, kb_ref, vaug_ref, *, tq, span):
    g = pl.program_id(1)
    qi = pl.program_id(2)

    @pl.when(qi == 0)
    def _():
        kb_ref[...] = _both_halves(_group_lower(k_ref, g)).astype(BF16)
        vaug_ref[...] = _with_ones(_group_lower(v_ref, g)).astype(BF16)

    q4 = _stack_heads(q_ref)
    q0 = qi * tq
    k_start = pl.multiple_of(jnp.maximum(q0 - NSA_WINDOW, 0), tq)
    keys = pl.ds(k_start, span)
    dist = (q0 - k_start) + (lax.broadcasted_iota(jnp.int32, (tq, span), 0)
                             - lax.broadcasted_iota(jnp.int32, (tq, span), 1))
    ok = (dist >= 0) & (dist < NSA_WINDOW)
    ok2 = jnp.concatenate([ok, ok], axis=0)
    pairs = [q4[0:2 * tq], q4[2 * tq:4 * tq]]
    s = [lax.dot_general(qp, kb_ref[keys, :], _NT, preferred_element_type=F32) for qp in pairs]
    s = [jnp.where(ok2, sc, NEG_INF) for sc in s]
    p = [jnp.exp(sc - jnp.max(sc, axis=1, keepdims=True)).astype(BF16) for sc in s]
    acc = [jnp.dot(pc, vaug_ref[keys, :], preferred_element_type=F32) for pc in p]
    acc = jnp.concatenate(acc, axis=0)
    o_ref[...] = _unstack_heads(_normalise(acc), tq).astype(o_ref.dtype)


def _nsa_win_attn(qk, vv, *, batch, seq, k_col, v_col):
    T = qk.shape[0]
    G = NSA_KV_HEADS
    tq = _pick_tile(seq, 256)
    nq = seq // tq
    span = NSA_WINDOW + tq
    assert NSA_WINDOW % tq == 0 and seq >= span
    return pl.pallas_call(
        functools.partial(_nsa_win_attn_kernel, tq=tq, span=span),
        grid=(batch, G, nq),
        in_specs=[pl.BlockSpec((tq, 2 * LANES), lambda b, g, q: (b * nq + q, g)),
                  pl.BlockSpec((seq, LANES), lambda b, g, q: (b, k_col + g // 2)),
                  pl.BlockSpec((seq, LANES), lambda b, g, q: (b, v_col + g // 2))],
        out_specs=pl.BlockSpec((tq, 2 * LANES), lambda b, g, q: (b * nq + q, g)),
        out_shape=jax.ShapeDtypeStruct((T, N_HEADS * HEAD_DIM), BF16),
        scratch_shapes=[pltpu.VMEM((seq, LANES), BF16), pltpu.VMEM((seq, LANES), BF16)],
        compiler_params=_cparams("parallel", "arbitrary", "arbitrary"),
        name="nsa_win_attn",
    )(qk, qk, vv)


def _nsa_mixer(x, g, w_in, pos_k, pos_v, w_cmp_k, w_cmp_v, g_q, g_k, w_out, *, batch, seq):
    HD = N_HEADS * HEAD_DIM
    G, dh = NSA_KV_HEADS, HEAD_DIM
    kvw = G * dh
    scale = dh ** -0.5
    col = lambda n: slice(HD + n * kvw, HD + (n + 1) * kvw)
    order = [slice(0, HD), col(0), col(2), col(4), col(1), col(3), col(5)]
    w = jnp.concatenate([w_in[:, c] for c in order], axis=1).astype(BF16)
    n_gate = 3 * N_HEADS
    w_gate = jnp.pad(w_in[:, HD + 6 * kvw:], ((0, 0), (0, LANES - n_gate))).astype(BF16)
    head_gain = jnp.concatenate([jnp.tile(g_q * scale, N_HEADS), jnp.tile(g_k, 3 * G)])

    inv = ROPE_THETA ** (-jnp.arange(0, dh, 2, dtype=F32) / dh)
    ang = jnp.arange(seq, dtype=F32)[:, None] * inv[None, :]
    cos, sin = jnp.cos(ang), jnp.sin(ang)
    cos_t = jnp.tile(cos, (1, PROJ_CHUNK // (dh // 2)))
    sin_t = jnp.tile(jnp.concatenate([-sin, sin], axis=1), (1, PROJ_CHUNK // dh))

    qkv, gate = _proj(x, g, w, seq=seq, head_gain=head_gain, rope=(cos_t, sin_t), w_aux=w_gate)

    def blocks16(t):
        half = NSA_CMP_LEN // 2
        t = t.reshape(batch, seq // half, half, G, dh).transpose(0, 3, 1, 2, 4)
        return t.reshape(batch, G, seq // half, half * dh)

    kc0, vc0 = HD, HD + 3 * kvw
    kcmp, vcmp = _nsa_compress(blocks16(qkv[:, kc0:kc0 + kvw]), blocks16(qkv[:, vc0:vc0 + kvw]),
                               pos_k, pos_v, w_cmp_k, w_cmp_v, g_k)
    o_c, selbias = _nsa_cmp_attn(qkv, kcmp, vcmp, batch=batch, seq=seq)
    o_s = _nsa_sel_attn(qkv, qkv, selbias, batch=batch, seq=seq,
                        k_col=(HD + kvw) // LANES, v_col=(HD + 4 * kvw) // LANES)
    o_w = _nsa_win_attn(qkv, qkv, batch=batch, seq=seq,
                        k_col=(HD + 2 * kvw) // LANES, v_col=(HD + 5 * kvw) // LANES)
    r = np.arange(LANES)[:, None]
    c = np.arange(HD)[None, :] // dh
    expand = np.stack([(r == 3 * c + b) for b in range(3)]).astype(np.float32)
    return _outproj(x, w_out.astype(BF16), mode="gated3",
                    acts=[o_c, o_s, o_w, gate, jnp.asarray(expand, BF16)])


def kernel(x, norm_g, ffn1_w_gate, ffn1_w_up, ffn1_w_down, ffn2_w_gate, ffn2_w_up, ffn2_w_down, fox_w_in, fox_b_f, fox_g_q, fox_g_k, fox_w_out, nsa_w_in, nsa_cmp_pos_k, nsa_cmp_pos_v, nsa_w_cmp_k, nsa_w_cmp_v, nsa_g_q, nsa_g_k, nsa_w_out, gla_w_in, gla_w_gate_up, gla_b_gate, gla_g_out, gla_w_out, sb_w_in, sb_w_out):
    B, S, D = x.shape
    depth = norm_g.shape[0]
    n_mixers = 4
    x2 = x.reshape(B * S, D)
    for i in range(depth):
        m, j = i % n_mixers, i // n_mixers
        x2 = _ffn(x2, norm_g[i, 0], ffn1_w_gate[i].astype(BF16), ffn1_w_up[i].astype(BF16),
                  ffn1_w_down[i].astype(BF16))
        g = norm_g[i, 1]
        if m == 0:
            x2 = _fox_mixer(x2, g, fox_w_in[j], fox_b_f[j], fox_g_q[j], fox_g_k[j], fox_w_out[j], batch=B, seq=S)
        elif m == 1:
            x2 = _nsa_mixer(x2, g, nsa_w_in[j], nsa_cmp_pos_k[j], nsa_cmp_pos_v[j], nsa_w_cmp_k[j],
                            nsa_w_cmp_v[j], nsa_g_q[j], nsa_g_k[j], nsa_w_out[j], batch=B, seq=S)
        elif m == 2:
            x2 = _gla_mixer(x2, g, gla_w_in[j], gla_w_gate_up[j], gla_b_gate[j], gla_g_out[j], gla_w_out[j],
                            batch=B, seq=S)
        else:
            x2 = _sb_mixer(x2, g, sb_w_in[j], sb_w_out[j], batch=B, seq=S)
        x2 = _ffn(x2, norm_g[i, 2], ffn2_w_gate[i].astype(BF16), ffn2_w_up[i].astype(BF16),
                  ffn2_w_down[i].astype(BF16))
    return x2.reshape(B, S, D)
```

```python
import functools

import numpy as np
import jax
import jax.numpy as jnp
from jax import lax
from jax.experimental import pallas as pl
from jax.experimental.pallas import tpu as pltpu

F32 = jnp.float32
BF16 = jnp.bfloat16

N_HEADS = 16
HEAD_DIM = 64
ROPE_THETA = 10000.0
RMS_EPS = 1e-6
NEG_INF = -1e30
NSA_KV_HEADS = 4
NSA_CMP_LEN = 32
NSA_CMP_STRIDE = 16
NSA_SEL_LEN = 64
NSA_N_SEL = 16
NSA_WINDOW = 512
NSA_FORCED_SCORE = 1e9
GLA_HEADS = 4
GLA_GATE_RANK = 16
GLA_TAU = 16.0
GLA_CHUNK = 64

LANES = 128
V7X_VMEM_BYTES = 64 * 1024 * 1024
VMEM_LIMIT = V7X_VMEM_BYTES - 8 * 1024 * 1024

_NT = (((1,), (1,)), ((), ()))
_TN = (((0,), (0,)), ((), ()))


def _cparams(*sem):
    return pltpu.CompilerParams(dimension_semantics=sem, vmem_limit_bytes=VMEM_LIMIT)


def _rms_rows(x, g):
    return x * lax.rsqrt(jnp.mean(x * x, axis=-1, keepdims=True) + RMS_EPS) * g


def _softplus(z):
    return jnp.maximum(z, 0.0) + jnp.log(1.0 + jnp.exp(-jnp.abs(z)))


def _neg_abs(z):
    bits = lax.bitcast_convert_type(z, jnp.int32) | jnp.int32(-2 ** 31)
    return lax.bitcast_convert_type(bits, F32)


def _sigmoid(z):
    return 1.0 / (1.0 + jnp.exp(-z))


def _pick_tile(n, target):
    t = min(n, target)
    while n % t:
        t //= 2
    return t


def _ffn_kernel(*refs, tf, pre_mode):
    if pre_mode is None:
        x_ref, g_ref, wg_ref, wu_ref, wd_ref, o_ref, a_ref = refs
        x = x_ref[...]
    else:
        n_act = {"plain": 1, "sigmoid_gate": 2}[pre_mode]
        x_ref, *act_refs = refs[:1 + n_act]
        wout_ref, g_ref, wg_ref, wu_ref, wd_ref, o_ref, a_ref = refs[1 + n_act:]
        act = act_refs[0][...]
        if pre_mode == "sigmoid_gate":
            act = (act.astype(F32) * _sigmoid(act_refs[1][...].astype(F32))).astype(BF16)
        x = x_ref[...] + jnp.dot(act, wout_ref[...], preferred_element_type=F32)
    h = _rms_rows(x, g_ref[...]).astype(BF16)
    for c in range(a_ref.shape[1] // tf):
        cols = slice(c * tf, (c + 1) * tf)
        gate = jnp.dot(h, wg_ref[:, cols], preferred_element_type=F32)
        up = jnp.dot(h, wu_ref[:, cols], preferred_element_type=F32)
        a_ref[:, cols] = (gate * _sigmoid(gate) * up).astype(BF16)
    o_ref[...] = x + 0.5 * jnp.dot(a_ref[...], wd_ref[...], preferred_element_type=F32)


def _resident(shape):
    return pl.BlockSpec(shape, lambda *_: (0,) * len(shape), pipeline_mode=pl.Buffered(1))


def _ffn(x, g, wg, wu, wd, pre=None):
    T, D = x.shape
    F = wg.shape[1]
    tm = _pick_tile(T, 1024)
    tf = 256 if F % 256 == 0 else F
    args = [x]
    in_specs = [pl.BlockSpec((tm, D), lambda i: (i, 0))]
    pre_mode = None
    if pre is not None:
        pre_mode, acts, w_out = pre
        for arr, col_block in acts:
            args.append(arr)
            in_specs.append(pl.BlockSpec((tm, w_out.shape[0]), functools.partial(lambda i, c: (i, c), c=col_block)))
        args.append(w_out)
        in_specs.append(_resident(w_out.shape))
    args += [g.reshape(1, D), wg, wu, wd]
    in_specs += [_resident((1, D)), _resident((D, F)), _resident((D, F)), _resident((F, D))]
    return pl.pallas_call(
        functools.partial(_ffn_kernel, tf=tf, pre_mode=pre_mode),
        grid=(T // tm,),
        in_specs=in_specs,
        out_specs=pl.BlockSpec((tm, D), lambda i: (i, 0)),
        out_shape=jax.ShapeDtypeStruct((T, D), F32),
        scratch_shapes=[pltpu.VMEM((tm, F), BF16)],
        compiler_params=_cparams("parallel"),
        name="ffn" if pre is None else "outproj_ffn",
    )(*args)


PROJ_CHUNK = 256


def _proj_kernel(*refs, n_norm, rope, has_aux):
    it = iter(refs)
    x_ref, g_ref, w_ref = next(it), next(it), next(it)
    hg_ref, gsum_ref = (next(it), next(it)) if n_norm else (None, None)
    cos_ref, sin_ref = (next(it), next(it)) if rope else (None, None)
    waux_ref = next(it) if has_aux else None
    o_ref = next(it)
    ch = PROJ_CHUNK
    h = _rms_rows(x_ref[...], g_ref[...]).astype(BF16)
    for c in range(o_ref.shape[1] // ch):
        cols = slice(c * ch, (c + 1) * ch)
        y = jnp.dot(h, w_ref[:, cols], preferred_element_type=F32)
        if c * ch < n_norm:
            ss = jnp.dot((y * y).astype(BF16), gsum_ref[...], preferred_element_type=F32)
            y = y * lax.rsqrt(ss * (1.0 / HEAD_DIM) + RMS_EPS) * hg_ref[:, cols]
            if rope:
                lane = lax.broadcasted_iota(jnp.int32, (1, ch), 1)
                first_half = (lane % HEAD_DIM) < (HEAD_DIM // 2)
                partner = jnp.where(first_half,
                                    pltpu.roll(y, ch - HEAD_DIM // 2, 1),
                                    pltpu.roll(y, HEAD_DIM // 2, 1))
                y = y * cos_ref[...] + partner * sin_ref[...]
        o_ref[:, cols] = y.astype(o_ref.dtype)
    if has_aux:
        next(it)[...] = jnp.dot(h, waux_ref[...], preferred_element_type=F32)


def _proj(x, g, w, *, seq, head_gain=None, rope=None, w_aux=None):
    T, D = x.shape
    N = w.shape[1]
    tm = _pick_tile(seq, 1024)
    ch = PROJ_CHUNK
    n_norm = 0 if head_gain is None else head_gain.shape[0]
    assert N % ch == 0 and n_norm % ch == 0 and T % tm == 0
    args = [x, g.reshape(1, D), w]
    in_specs = [pl.BlockSpec((tm, D), lambda i: (i, 0)), _resident((1, D)), _resident((D, N))]
    if n_norm:
        gidx = np.arange(ch) // HEAD_DIM
        gsum = jnp.asarray((gidx[:, None] == gidx[None, :]).astype(np.float32), BF16)
        args += [head_gain.reshape(1, n_norm), gsum]
        in_specs += [_resident((1, n_norm)), _resident((ch, ch))]
    if rope is not None:
        nblk = seq // tm
        args += list(rope)
        in_specs += [pl.BlockSpec((tm, ch), lambda i: (i % nblk, 0))] * 2
    out_specs = [pl.BlockSpec((tm, N), lambda i: (i, 0))]
    out_shape = [jax.ShapeDtypeStruct((T, N), BF16)]
    if w_aux is not None:
        args.append(w_aux)
        in_specs.append(_resident(w_aux.shape))
        out_specs.append(pl.BlockSpec((tm, w_aux.shape[1]), lambda i: (i, 0)))
        out_shape.append(jax.ShapeDtypeStruct((T, w_aux.shape[1]), F32))
    out = pl.pallas_call(
        functools.partial(_proj_kernel, n_norm=n_norm, rope=rope is not None, has_aux=w_aux is not None),
        grid=(T // tm,),
        in_specs=in_specs,
        out_specs=out_specs,
        out_shape=out_shape,
        compiler_params=_cparams("parallel"),
        name="proj",
    )(*args)
    return out if w_aux is not None else out[0]


def _outproj_gated3_kernel(x_ref, oc_ref, os_ref, ow_ref, gate_ref, exp_ref, w_ref, o_ref):
    gs = _sigmoid(gate_ref[...]).astype(BF16)
    a = None
    for c, o_c in enumerate((oc_ref, os_ref, ow_ref)):
        t = jnp.dot(gs, exp_ref[c], preferred_element_type=F32) * o_c[...].astype(F32)
        a = t if a is None else a + t
    o_ref[...] = x_ref[...] + jnp.dot(a.astype(BF16), w_ref[...], preferred_element_type=F32)


def _outproj_gated3(x, w, branches, gate, expand):
    T, D = x.shape
    K = w.shape[0]
    tm = _pick_tile(T, 512)
    row = lambda i: (i, 0)
    return pl.pallas_call(
        _outproj_gated3_kernel,
        grid=(T // tm,),
        in_specs=[pl.BlockSpec((tm, D), row)] + [pl.BlockSpec((tm, K), row)] * 3
        + [pl.BlockSpec((tm, gate.shape[1]), row), _resident(expand.shape), _resident((K, D))],
        out_specs=pl.BlockSpec((tm, D), row),
        out_shape=jax.ShapeDtypeStruct((T, D), F32),
        compiler_params=_cparams("parallel"),
        name="outproj_gated3",
    )(x, *branches, gate, expand, w)


def _fox_c_kernel(x_ref, g_ref, wf_ref, bf_ref, tri_ref, c_ref, carry_ref):
    @pl.when(pl.program_id(1) == 0)
    def _():
        carry_ref[...] = jnp.zeros_like(carry_ref)

    h = _rms_rows(x_ref[...], g_ref[...])
    f = lax.dot_general(wf_ref[...], h, _NT, precision=lax.Precision.HIGHEST,
                        preferred_element_type=F32) + bf_ref[...]
    ls = jnp.minimum(f, 0.0) - jnp.log(1.0 + jnp.exp(-jnp.abs(f)))
    cs = jnp.dot(ls, tri_ref[...], precision=lax.Precision.HIGHEST,
                 preferred_element_type=F32) + carry_ref[...]
    c_ref[...] = cs
    carry_ref[...] = cs[:, cs.shape[1] - 1:]


def _fox_c(x, g, wf_t, b_f, *, batch, seq):
    T, D = x.shape
    H = wf_t.shape[0]
    tm = _pick_tile(seq, 512)
    ns = seq // tm
    tri = jnp.asarray(np.triu(np.ones((tm, tm), np.float32)))
    return pl.pallas_call(
        _fox_c_kernel,
        grid=(batch, ns),
        in_specs=[
            pl.BlockSpec((tm, D), lambda b, s: (b * ns + s, 0)),
            pl.BlockSpec((1, D), lambda b, s: (0, 0)),
            pl.BlockSpec((H, D), lambda b, s: (0, 0)),
            pl.BlockSpec((H, 1), lambda b, s: (0, 0)),
            pl.BlockSpec((tm, tm), lambda b, s: (0, 0)),
        ],
        out_specs=pl.BlockSpec((None, H, tm), lambda b, s: (b, 0, s)),
        out_shape=jax.ShapeDtypeStruct((batch, H, seq), F32),
        scratch_shapes=[pltpu.VMEM((H, 1), F32)],
        compiler_params=_cparams("parallel", "arbitrary"),
        name="fox_c",
    )(x, g.reshape(1, D), wf_t, b_f.reshape(H, 1), tri)


def _fox_attn_kernel(q_ref, k_ref, v_ref, c_ref, o_ref, vaug_ref, sa_ref, sb_ref, m_ref, acc_ref, *, tq, tk):
    qi = pl.program_id(2)
    lane = lax.broadcasted_iota(jnp.int32, (1, LANES), 1)
    upper = lane >= HEAD_DIM

    @pl.when(qi == 0)
    def _():
        v = v_ref[...]
        one = jnp.ones_like(v)
        vaug_ref[0] = jnp.where(upper, one, v)
        vaug_ref[1] = jnp.where(upper, v, one)

    q0 = pl.multiple_of(qi * tq, tq)
    qslab = q_ref[...]
    zero = jnp.zeros_like(qslab)
    qpos = lax.broadcasted_iota(jnp.int32, (tq, tk), 0)
    kpos = lax.broadcasted_iota(jnp.int32, (tq, tk), 1)

    heads = range(2)
    q2 = jnp.concatenate([jnp.where(upper, zero, qslab), jnp.where(upper, qslab, zero)], axis=0)
    c_q0 = [c_ref[i, :, pl.ds(q0, LANES)][:, 0:1] for i in heads]

    n_diag = tq // tk
    assert n_diag == 2

    def produce(s_ref, j, off=0):
        keys = pl.ds(pl.multiple_of(j * tk, tk), tk)
        n = tq - off
        qs = q2 if off == 0 else jnp.concatenate([q2[off:tq], q2[tq + off:2 * tq]], axis=0)
        s2 = lax.dot_general(qs, k_ref[keys, :], _NT, preferred_element_type=F32)
        for i in heads:
            s_ref[i, off:tq] = s2[i * n:(i + 1) * n] + (c_q0[i] - c_ref[i, :, keys])

    def consume(s_ref, j, diag_offset=None, off=0):
        keys = pl.ds(pl.multiple_of(j * tk, tk), tk)
        live = slice(off, tq)
        if diag_offset is None:
            score = lambda i: s_ref[i, live]
        else:
            causal = (kpos + diag_offset <= qpos)[live]
            score = lambda i: jnp.where(causal, s_ref[i, live], NEG_INF)
        m_old = [m_ref[i, live] for i in heads]
        m_new = [jnp.maximum(m_old[i], jnp.max(score(i), axis=1, keepdims=True)) for i in heads]
        p = [jnp.exp(score(i) - jnp.tile(m_new[i], (1, tk // LANES))).astype(BF16) for i in heads]
        for i in heads:
            m_ref[i, live] = m_new[i]
            acc_ref[i, live] = (jnp.exp(m_old[i] - m_new[i]) * acc_ref[i, live]
                                + jnp.dot(p[i], vaug_ref[i, keys, :], preferred_element_type=F32))

    m_ref[...] = jnp.full(m_ref.shape, NEG_INF, F32)
    acc_ref[...] = jnp.zeros(acc_ref.shape, F32)
    produce(sa_ref, 0)

    def body(m, _):
        produce(sb_ref, 2 * m + 1)
        consume(sa_ref, 2 * m)
        produce(sa_ref, 2 * m + 2)
        consume(sb_ref, 2 * m + 1)
        return 0

    lax.fori_loop(0, qi, body, 0)
    produce(sb_ref, 2 * qi + 1, off=tk)
    consume(sa_ref, 2 * qi, 0)
    consume(sb_ref, 2 * qi + 1, tk, off=tk)

    accs = (acc_ref[0], acc_ref[1])
    num = jnp.where(upper, accs[1], accs[0])
    den = jnp.where(upper, accs[0], accs[1])
    den = pltpu.roll(den, HEAD_DIM, 1)
    o_ref[...] = (num / den).astype(o_ref.dtype)


def _fox_attn(qkvg, c, *, batch, seq):
    T = qkvg.shape[0]
    HD = N_HEADS * HEAD_DIM
    tq = _pick_tile(seq, 1024)
    tk = tq // 2
    nq = seq // tq
    npair = HD // LANES
    return pl.pallas_call(
        functools.partial(_fox_attn_kernel, tq=tq, tk=tk),
        grid=(batch, npair, nq),
        in_specs=[
            pl.BlockSpec((tq, LANES), lambda b, h, q: (b * nq + q, h)),
            pl.BlockSpec((seq, LANES), lambda b, h, q: (b, npair + h)),
            pl.BlockSpec((seq, LANES), lambda b, h, q: (b, 2 * npair + h)),
            pl.BlockSpec((None, 2, 1, seq), lambda b, h, q: (b, h, 0, 0)),
        ],
        out_specs=pl.BlockSpec((tq, LANES), lambda b, h, q: (b * nq + q, h)),
        out_shape=jax.ShapeDtypeStruct((T, HD), BF16),
        scratch_shapes=[pltpu.VMEM((2, seq, LANES), BF16),
                        pltpu.VMEM((2, tq, tk), F32), pltpu.VMEM((2, tq, tk), F32),
                        pltpu.VMEM((2, tq, LANES), F32), pltpu.VMEM((2, tq, LANES), F32)],
        compiler_params=_cparams("parallel", "arbitrary", "arbitrary"),
        name="fox_attn",
    )(qkvg, qkvg, qkvg, c.reshape(batch, N_HEADS, 1, seq))


def _fox_mixer(x, g, w_in, b_f, g_q, g_k, w_out, *, batch, seq):
    HD = N_HEADS * HEAD_DIM
    scale = HEAD_DIM ** -0.5
    w = jnp.concatenate([w_in[:, :3 * HD], w_in[:, 3 * HD + N_HEADS:]], axis=1).astype(BF16)
    wf_t = w_in[:, 3 * HD:3 * HD + N_HEADS].T
    head_gain = jnp.concatenate([jnp.tile(g_q * scale, N_HEADS), jnp.tile(g_k, N_HEADS)])
    qkvg = _proj(x, g, w, seq=seq, head_gain=head_gain)
    c = _fox_c(x, g, wf_t, b_f, batch=batch, seq=seq)
    o = _fox_attn(qkvg, c, batch=batch, seq=seq)
    return "sigmoid_gate", [(o, 0), (qkvg, 3)], w_out.astype(BF16)


def _sb_attn_kernel(q_ref, k_ref, v_ref, tri_ref, o_ref, wa_ref, wb_ref, ta_ref, tb_ref, rest_ref, acc_ref,
                    *, tq, tk):
    qi = pl.program_id(2)
    lane = lax.broadcasted_iota(jnp.int32, (1, LANES), 1)
    upper = lane >= HEAD_DIM
    qslab = q_ref[...]
    zero = jnp.zeros_like(qslab)
    q2 = jnp.concatenate([jnp.where(upper, zero, qslab), jnp.where(upper, qslab, zero)], axis=0)
    rows = 2 * tq
    n_diag = tq // tk
    last = n_diag * (qi + 1) - 1

    def key_rows(k):
        return pl.ds(pl.multiple_of(jnp.maximum(last - k, 0) * tk, tk), tk)

    def head_rows(off):
        return (slice(off, tq), slice(tq + off, rows))

    def take(x, off):
        if off == 0:
            return x[...]
        lo, hi = head_rows(off)
        return jnp.concatenate([x[lo], x[hi]], axis=0)

    def put(ref, val, off, add=False):
        parts = [(slice(None), val)] if off == 0 else list(zip(head_rows(off), (val[:tq - off], val[tq - off:])))
        for sl, part in parts:
            if add:
                ref[sl] += part
            else:
                ref[sl] = part

    def scores(k, off=0):
        return lax.dot_general(take(q2, off), k_ref[key_rows(k), :], _NT, preferred_element_type=F32)

    def softplus_sums(z, w_ref, tot_ref, diag_offset, off=0):
        n2 = z.shape[0]
        sp = jnp.maximum(z, 0.0) + jnp.log2(1.0 + jnp.exp2(_neg_abs(z)))
        if diag_offset is not None:
            qpos = lax.broadcasted_iota(jnp.int32, (n2, tk), 0) % (tq - off) + off
            strict = lax.broadcasted_iota(jnp.int32, (n2, tk), 1) + diag_offset < qpos
            spm = jnp.where(strict, sp, 0.0)
        else:
            spm = sp
        cum = jnp.dot(spm.astype(BF16), tri_ref[...], preferred_element_type=F32)
        w = (z - sp) - cum
        if diag_offset is not None:
            w = jnp.where(strict, w, NEG_INF)
        put(w_ref, w, off)
        put(tot_ref, jnp.broadcast_to(cum[:, 0:1] + spm[:, 0:1], (n2, LANES)), off)

    def accumulate(w_ref, tot_ref, k, off=0):
        a = jnp.exp2(take(w_ref, off) - jnp.tile(take(rest_ref, off), (1, tk // LANES))).astype(BF16)
        put(acc_ref, jnp.dot(a, v_ref[key_rows(k), :], preferred_element_type=F32), off, add=True)
        put(rest_ref, take(tot_ref, off), off, add=True)

    assert n_diag % 2 == 0
    even, odd = (wa_ref, ta_ref), (wb_ref, tb_ref)
    offset = lambda k: (n_diag - 1 - k) * tk

    def produce_diag(buf, k):
        softplus_sums(scores(k, offset(k)), *buf, offset(k), offset(k))

    def step(k, cur, nxt, diag=False):
        off_cur, off_nxt = (offset(k), offset(k + 1)) if diag else (0, 0)
        z = scores(k + 1, off_nxt)
        accumulate(*cur, k, off_cur)
        softplus_sums(z, *nxt, off_nxt if diag else None, off_nxt)

    rest_ref[...] = jnp.zeros(rest_ref.shape, F32)
    acc_ref[...] = jnp.zeros(acc_ref.shape, F32)
    produce_diag(even, 0)
    produce_diag(odd, 1)
    accumulate(wa_ref, ta_ref, 0, offset(0))
    for m in range(1, n_diag // 2):
        step(2 * m - 1, odd, even, diag=True)
        step(2 * m, even, odd, diag=True)

    def body(m, _):
        step(2 * m - 1, odd, even)
        step(2 * m, even, odd)
        return 0

    lax.fori_loop(n_diag // 2, (n_diag // 2) * (qi + 1), body, 0)
    accumulate(wb_ref, tb_ref, last)
    o_ref[...] = jnp.where(upper, acc_ref[tq:2 * tq], acc_ref[0:tq]).astype(o_ref.dtype)


def _sb_attn(qkv, *, batch, seq):
    T = qkv.shape[0]
    HD = N_HEADS * HEAD_DIM
    tq = _pick_tile(seq, 1024)
    tk = 256
    nq = seq // tq
    npair = HD // LANES
    tri = jnp.asarray(np.tril(np.ones((tk, tk), np.float32), -1), BF16)
    return pl.pallas_call(
        functools.partial(_sb_attn_kernel, tq=tq, tk=tk),
        grid=(batch, npair, nq),
        in_specs=[
            pl.BlockSpec((tq, LANES), lambda b, h, q: (b * nq + q, h)),
            pl.BlockSpec((seq, LANES), lambda b, h, q: (b, npair + h)),
            pl.BlockSpec((seq, LANES), lambda b, h, q: (b, 2 * npair + h)),
            pl.BlockSpec((tk, tk), lambda b, h, q: (0, 0)),
        ],
        out_specs=pl.BlockSpec((tq, LANES), lambda b, h, q: (b * nq + q, h)),
        out_shape=jax.ShapeDtypeStruct((T, HD), BF16),
        scratch_shapes=[pltpu.VMEM((2 * tq, tk), F32)] * 2 + [pltpu.VMEM((2 * tq, LANES), F32)] * 4,
        compiler_params=_cparams("parallel", "parallel", "arbitrary"),
        name="sb_attn",
    )(qkv, qkv, qkv, tri)


def _sb_mixer(x, g, w_in, w_out, *, batch, seq):
    HD = N_HEADS * HEAD_DIM
    scale = HEAD_DIM ** -0.5 * float(np.log2(np.e))
    w = jnp.concatenate([w_in[:, :HD] * scale, w_in[:, HD:]], axis=1).astype(BF16)
    qkv = _proj(x, g, w, seq=seq)
    o = _sb_attn(qkv, batch=batch, seq=seq)
    return "plain", [(o, 0)], w_out.astype(BF16)


def _gla_kernel(q_ref, k_ref, v_ref, r_ref, glow_ref, wgu_ref, bg_ref, gout_ref, tri_ref,
                o_ref, state_ref, *, tm, dk):
    C = GLA_CHUNK

    @pl.when(pl.program_id(2) == 0)
    def _():
        state_ref[...] = jnp.zeros_like(state_ref)

    gate = jnp.dot(glow_ref[...], wgu_ref[...], precision=lax.Precision.HIGHEST,
                   preferred_element_type=F32) + bg_ref[...]
    log_a = (jnp.minimum(gate, 0.0) - jnp.log(1.0 + jnp.exp(-jnp.abs(gate)))) * (1.0 / GLA_TAU)
    row = lax.broadcasted_iota(jnp.int32, (C, C), 0)
    col = lax.broadcasted_iota(jnp.int32, (C, C), 1)
    causal = col <= row
    q_scale = dk ** -0.5
    chunks = [slice(c * C, (c + 1) * C) for c in range(tm // C)]

    tri = tri_ref[...]
    b = [jnp.dot(tri, log_a[sl], precision=lax.Precision.HIGHEST, preferred_element_type=F32)
         for sl in chunks]
    b_last = jnp.concatenate([jnp.broadcast_to(bc[C - 1:C], bc.shape) for bc in b], axis=0)
    b = jnp.concatenate(b, axis=0)
    q = q_ref[...].astype(F32) * q_scale
    k = k_ref[...].astype(F32)
    q_dec = (q * jnp.exp(b)).astype(BF16)
    k_neg = (k * jnp.exp(-b)).astype(BF16)
    k_rem = (k * jnp.exp(b_last - b)).astype(BF16)
    decay = jnp.exp(b_last)
    attn = [lax.dot_general(q_dec[sl], k_neg[sl], _NT, preferred_element_type=F32) for sl in chunks]
    attn = [jnp.where(causal, a, 0.0).astype(BF16) for a in attn]
    o_intra = [jnp.dot(attn[c], v_ref[sl, :], preferred_element_type=F32) for c, sl in enumerate(chunks)]
    u_t = [lax.dot_general(v_ref[sl, :], k_rem[sl], _TN, preferred_element_type=F32) for sl in chunks]

    state_t = state_ref[...]
    outs = []
    for c, sl in enumerate(chunks):
        outs.append(o_intra[c] + lax.dot_general(q_dec[sl], state_t.astype(BF16), _NT,
                                                 preferred_element_type=F32))
        state_t = state_t * decay[c * C:c * C + 1] + u_t[c]
    state_ref[...] = state_t

    o = jnp.concatenate(outs, axis=0)
    y = o * lax.rsqrt(jnp.mean(o * o, axis=-1, keepdims=True) + RMS_EPS) * gout_ref[...]
    r = r_ref[...].astype(F32)
    o_ref[...] = (y * (r * _sigmoid(r))).astype(o_ref.dtype)


def _gla(qkvr, glow, wgu, b_gate, g_out, *, batch, seq):
    T = qkvr.shape[0]
    Hg = GLA_HEADS
    dk_total = wgu.shape[1]
    dk = dk_total // Hg
    dv = (qkvr.shape[1] - 2 * dk_total) // 2 // Hg
    tm = _pick_tile(seq, 512)
    ns = seq // tm
    kq, kv = dk_total // dk, dk_total // dv
    tri = jnp.asarray(np.tril(np.ones((GLA_CHUNK, GLA_CHUNK), np.float32)))
    rowmap = lambda off: (lambda b, h, s: (b * ns + s, off + h))
    return pl.pallas_call(
        functools.partial(_gla_kernel, tm=tm, dk=dk),
        grid=(batch, Hg, ns),
        in_specs=[
            pl.BlockSpec((tm, dk), rowmap(0)),
            pl.BlockSpec((tm, dk), rowmap(kq)),
            pl.BlockSpec((tm, dv), rowmap(2 * kv)),
            pl.BlockSpec((tm, dv), rowmap(2 * kv + Hg)),
            pl.BlockSpec((tm, LANES), lambda b, h, s: (b * ns + s, 0)),
            pl.BlockSpec((LANES, dk), lambda b, h, s: (0, h)),
            pl.BlockSpec((1, dk), lambda b, h, s: (0, h)),
            pl.BlockSpec((1, dv), lambda b, h, s: (0, 0)),
            pl.BlockSpec((GLA_CHUNK, GLA_CHUNK), lambda b, h, s: (0, 0)),
        ],
        out_specs=pl.BlockSpec((tm, dv), rowmap(0)),
        out_shape=jax.ShapeDtypeStruct((T, Hg * dv), BF16),
        scratch_shapes=[pltpu.VMEM((dv, dk), F32)],
        compiler_params=_cparams("parallel", "parallel", "arbitrary"),
        name="gla",
    )(qkvr, qkvr, qkvr, qkvr, glow, wgu, b_gate.reshape(1, dk_total), g_out.reshape(1, dv), tri)


def _gla_mixer(x, g, w_in, w_gate_up, b_gate, g_out, w_out, *, batch, seq):
    dk_total = w_gate_up.shape[1]
    dv_total = w_out.shape[0]
    lo = 2 * dk_total + dv_total
    w_main = jnp.concatenate([w_in[:, :lo], w_in[:, lo + GLA_GATE_RANK:]], axis=1).astype(BF16)
    w_low = jnp.pad(w_in[:, lo:lo + GLA_GATE_RANK], ((0, 0), (0, LANES - GLA_GATE_RANK))).astype(BF16)
    wgu = jnp.pad(w_gate_up, ((0, LANES - GLA_GATE_RANK), (0, 0)))
    qkvr, glow = _proj(x, g, w_main, seq=seq, w_aux=w_low)
    o = _gla(qkvr, glow, wgu, b_gate, g_out, batch=batch, seq=seq)
    return "plain", [(o, 0)], w_out.astype(BF16)


def _nsa_compress_kernel(kc_ref, vc_ref, pk_ref, pv_ref, wk_ref, wv_ref, gk_ref, ko_ref, vo_ref, *, n16):
    def compress(r_ref, pos_ref, w_ref):
        r = r_ref[...].astype(F32)
        first = jnp.dot((r + pos_ref[0:1, :]).astype(BF16), w_ref[0], preferred_element_type=F32)
        second = jnp.dot((r + pos_ref[1:2, :]).astype(BF16), w_ref[1], preferred_element_type=F32)
        return first + pltpu.roll(second, n16 - 1, 0)

    kc = compress(kc_ref, pk_ref, wk_ref)
    ko_ref[...] = _rms_rows(kc, gk_ref[...]).astype(ko_ref.dtype)
    vo_ref[...] = compress(vc_ref, pv_ref, wv_ref).astype(vo_ref.dtype)


def _nsa_compress(kc_r, vc_r, pos_k, pos_v, w_k, w_v, g_k):
    B, G, n16, width = kc_r.shape
    dh = HEAD_DIM
    half = NSA_CMP_LEN // 2
    dup = lambda w: jnp.concatenate([w, w], axis=-1).reshape(2, half * dh, 2 * dh).astype(BF16)
    spec_r = pl.BlockSpec((None, None, n16, width), lambda b, g: (b, g, 0, 0))
    spec_o = pl.BlockSpec((None, None, n16, 2 * dh), lambda b, g: (b, g, 0, 0))
    const2 = lambda shape: pl.BlockSpec(shape, lambda b, g: (0,) * len(shape))
    out = jax.ShapeDtypeStruct((B, G, n16, 2 * dh), BF16)
    return pl.pallas_call(
        functools.partial(_nsa_compress_kernel, n16=n16),
        grid=(B, G),
        in_specs=[spec_r, spec_r, const2((2, width)), const2((2, width)),
                  const2((2, width, 2 * dh)), const2((2, width, 2 * dh)), const2((1, 2 * dh))],
        out_specs=[spec_o, spec_o],
        out_shape=[out, out],
        compiler_params=_cparams("parallel", "parallel"),
        name="nsa_compress",
    )(kc_r, vc_r, pos_k.reshape(2, width), pos_v.reshape(2, width), dup(w_k), dup(w_v),
      jnp.tile(g_k, 2).reshape(1, 2 * dh))


def _stack_heads(q_ref, extra=None, rows_in=slice(None)):
    lane = lax.broadcasted_iota(jnp.int32, (1, LANES), 1)
    upper = lane >= HEAD_DIM
    rows = []
    for j in range(2):
        slab = q_ref[rows_in, j * LANES:(j + 1) * LANES]
        zero = jnp.zeros_like(slab)
        for i in range(2):
            qh = jnp.where(upper, slab, zero) if i else jnp.where(upper, zero, slab)
            rows.append(qh if extra is None else jnp.concatenate([qh, extra], axis=1))
    return jnp.concatenate(rows, axis=0)


def _unstack_heads(o, tq):
    lane = lax.broadcasted_iota(jnp.int32, (1, LANES), 1)
    upper = lane >= HEAD_DIM
    slabs = []
    for j in range(2):
        lo = o[(2 * j) * tq:(2 * j + 1) * tq]
        hi = pltpu.roll(o[(2 * j + 1) * tq:(2 * j + 2) * tq], HEAD_DIM, 1)
        slabs.append(jnp.where(upper, hi, lo))
    return jnp.concatenate(slabs, axis=1)


def _group_lower(slab_ref, g):
    x = slab_ref[...].astype(F32)
    return jnp.where(g % 2 == 1, pltpu.roll(x, HEAD_DIM, 1), x)


def _both_halves(x):
    lane = lax.broadcasted_iota(jnp.int32, (1, LANES), 1)
    return jnp.where(lane >= HEAD_DIM, pltpu.roll(x, HEAD_DIM, 1), x)


def _with_ones(x):
    lane = lax.broadcasted_iota(jnp.int32, (1, LANES), 1)
    return jnp.where(lane >= HEAD_DIM, 1.0, x)


def _normalise(acc):
    return acc / pltpu.roll(acc, HEAD_DIM, 1)


def _nsa_cmp_attn_kernel(q_ref, kc_ref, vc_ref, cover_ref, oc_ref, sel_ref, *, tq, n16):
    qi = pl.program_id(2)
    q4 = _stack_heads(q_ref)
    s = lax.dot_general(q4, kc_ref[...], _NT, preferred_element_type=F32)
    qpos = qi * tq + lax.broadcasted_iota(jnp.int32, (tq, n16), 0)
    cmp_end = lax.broadcasted_iota(jnp.int32, (tq, n16), 1) * NSA_CMP_STRIDE + (NSA_CMP_LEN - 1)
    valid = jnp.concatenate([cmp_end <= qpos] * 4, axis=0)
    s = jnp.where(valid, s, NEG_INF)
    e = jnp.exp(s - jnp.max(s, axis=1, keepdims=True))
    p = jnp.where(valid, e / jnp.sum(e, axis=1, keepdims=True), 0.0)
    o = jnp.dot(p.astype(BF16), vc_ref[...], preferred_element_type=F32)
    oc_ref[...] = _unstack_heads(o, tq).astype(oc_ref.dtype)

    psum = p[0:tq] + p[tq:2 * tq] + p[2 * tq:3 * tq] + p[3 * tq:4 * tq]
    cover = cover_ref[...]
    hi = psum.astype(BF16)
    r1 = psum - hi.astype(F32)
    mid = r1.astype(BF16)
    lo = (r1 - mid.astype(F32)).astype(BF16)
    imp = (jnp.dot(hi, cover, preferred_element_type=F32) + jnp.dot(mid, cover, preferred_element_type=F32)
           + jnp.dot(lo, cover, preferred_element_type=F32))

    blk = lax.broadcasted_iota(jnp.int32, (tq, LANES), 1)
    cur = (qi * tq + lax.broadcasted_iota(jnp.int32, (tq, LANES), 0)) // NSA_SEL_LEN
    forced = (blk == 0) | (blk == cur) | (blk == cur - 1)
    vals = jnp.where(forced, NSA_FORCED_SCORE, jnp.where(blk <= cur, imp, -1.0))
    taken = -3e38
    for _ in range(NSA_N_SEL):
        first = jnp.argmax(vals, axis=1, keepdims=True)
        vals = jnp.where(blk == first, taken, vals)
    sel_ref[...] = jnp.where(vals == taken, 0.0, NEG_INF).astype(sel_ref.dtype)


def _nsa_cmp_attn(qk, kcmp, vcmp, *, batch, seq):
    T = qk.shape[0]
    G = NSA_KV_HEADS
    tq = _pick_tile(seq, 512)
    nq = seq // tq
    n16 = kcmp.shape[2]
    n_sel = seq // NSA_SEL_LEN
    assert n_sel <= LANES and seq % tq == 0
    n = np.arange(n16)[:, None] * NSA_CMP_STRIDE
    j = np.arange(LANES)[None, :] * NSA_SEL_LEN
    cover = (n < j + NSA_SEL_LEN) & (n + NSA_CMP_LEN > j) & (np.arange(LANES)[None, :] < n_sel)
    cover = jnp.asarray(cover.astype(np.float32), BF16)
    spec_c = pl.BlockSpec((None, None, n16, LANES), lambda b, g, q: (b, g, 0, 0))
    return pl.pallas_call(
        functools.partial(_nsa_cmp_attn_kernel, tq=tq, n16=n16),
        grid=(batch, G, nq),
        in_specs=[pl.BlockSpec((tq, 2 * LANES), lambda b, g, q: (b * nq + q, g)), spec_c, spec_c,
                  pl.BlockSpec((n16, LANES), lambda b, g, q: (0, 0))],
        out_specs=[pl.BlockSpec((tq, 2 * LANES), lambda b, g, q: (b * nq + q, g)),
                   pl.BlockSpec((tq, LANES), lambda b, g, q: (b * nq + q, g))],
        out_shape=[jax.ShapeDtypeStruct((T, N_HEADS * HEAD_DIM), BF16),
                   jax.ShapeDtypeStruct((T, G * LANES), BF16)],
        compiler_params=_cparams("parallel", "parallel", "parallel"),
        name="nsa_cmp_attn",
    )(qk, kcmp, vcmp, cover)


def _nsa_sel_attn_kernel(q_ref, k_ref, v_ref, sel_ref, o_ref, kaug_ref, vaug_ref, sa_ref, sb_ref, m_ref, acc_ref,
                         *, tq, sub, tk, seq):
    g = pl.program_id(1)
    qi = pl.program_id(2)

    @pl.when(qi == 0)
    def _():
        kaug_ref[:, 0:LANES] = _both_halves(_group_lower(k_ref, g)).astype(BF16)
        blk_of_key = lax.broadcasted_iota(jnp.int32, (seq, LANES), 0) // NSA_SEL_LEN
        blk = lax.broadcasted_iota(jnp.int32, (seq, LANES), 1)
        kaug_ref[:, LANES:2 * LANES] = jnp.where(blk_of_key == blk, 1.0, 0.0).astype(BF16)
        vaug_ref[...] = _with_ones(_group_lower(v_ref, g)).astype(BF16)

    chains = range(tq // sub)
    rows = [slice(c * sub, (c + 1) * sub) for c in chains]
    q4 = [_stack_heads(q_ref, extra=sel_ref[rows[c], :], rows_in=rows[c]) for c in chains]
    n_diag = tq // tk
    assert n_diag == 2

    def head_rows(off):
        return [slice(h * sub + off, (h + 1) * sub) for h in range(4)]

    def take(x, off):
        return x[...] if off == 0 else jnp.concatenate([x[sl] for sl in head_rows(off)], axis=0)

    def put(ref, val, off):
        if off == 0:
            ref[...] = val
        else:
            for h, sl in enumerate(head_rows(off)):
                ref[sl] = val[h * (sub - off):(h + 1) * (sub - off)]

    def produce(s_ref, j, off=0):
        kt = kaug_ref[pl.ds(pl.multiple_of(j * tk, tk), tk), :]
        for c in chains:
            s_ref[c, 0:4 * (sub - off)] = lax.dot_general(take(q4[c], off), kt, _NT,
                                                          preferred_element_type=F32)

    def consume(s_ref, j, diag_offset=None, off=0):
        keys = pl.ds(pl.multiple_of(j * tk, tk), tk)
        live = slice(0, 4 * (sub - off))
        if diag_offset is not None:
            kpos = diag_offset + lax.broadcasted_iota(jnp.int32, (sub - off, tk), 1)
            qpos = off + lax.broadcasted_iota(jnp.int32, (sub - off, tk), 0)
            ok = [jnp.concatenate([kpos <= qpos + c * sub] * 4, axis=0) for c in chains]
            score = lambda c: jnp.where(ok[c], s_ref[c, live], NEG_INF)
        else:
            score = lambda c: s_ref[c, live]
        vt = vaug_ref[keys, :]
        m_old = [take(m_ref.at[c], off) for c in chains]
        m_new = [jnp.maximum(m_old[c], jnp.max(score(c), axis=1, keepdims=True)) for c in chains]
        p = [jnp.exp(score(c) - jnp.tile(m_new[c], (1, tk // LANES))).astype(BF16) for c in chains]
        for c in chains:
            put(m_ref.at[c], m_new[c], off)
            put(acc_ref.at[c], (jnp.exp(m_old[c] - m_new[c]) * take(acc_ref.at[c], off)
                                + jnp.dot(p[c], vt, preferred_element_type=F32)), off)

    m_ref[...] = jnp.full(m_ref.shape, NEG_INF, F32)
    acc_ref[...] = jnp.zeros(acc_ref.shape, F32)
    produce(sa_ref, 0)

    def body(m, _):
        produce(sb_ref, 2 * m + 1)
        consume(sa_ref, 2 * m)
        produce(sa_ref, 2 * m + 2)
        consume(sb_ref, 2 * m + 1)
        return 0

    lax.fori_loop(0, qi, body, 0)
    produce(sb_ref, 2 * qi + 1, off=tk)
    consume(sa_ref, 2 * qi, 0)
    consume(sb_ref, 2 * qi + 1, tk, off=tk)

    for c in chains:
        o_ref[rows[c], :] = _unstack_heads(_normalise(acc_ref[c]), sub).astype(o_ref.dtype)


def _nsa_sel_attn(qk, vv, selbias, *, batch, seq, k_col, v_col):
    T = qk.shape[0]
    G = NSA_KV_HEADS
    tq = _pick_tile(seq, 1024)
    sub = tq
    tk = tq // 2
    nq = seq // tq
    return pl.pallas_call(
        functools.partial(_nsa_sel_attn_kernel, tq=tq, sub=sub, tk=tk, seq=seq),
        grid=(batch, G, nq),
        in_specs=[pl.BlockSpec((tq, 2 * LANES), lambda b, g, q: (b * nq + q, g)),
                  pl.BlockSpec((seq, LANES), lambda b, g, q: (b, k_col + g // 2)),
                  pl.BlockSpec((seq, LANES), lambda b, g, q: (b, v_col + g // 2)),
                  pl.BlockSpec((tq, LANES), lambda b, g, q: (b * nq + q, g))],
        out_specs=pl.BlockSpec((tq, 2 * LANES), lambda b, g, q: (b * nq + q, g)),
        out_shape=jax.ShapeDtypeStruct((T, N_HEADS * HEAD_DIM), BF16),
        scratch_shapes=[pltpu.VMEM((seq, 2 * LANES), BF16), pltpu.VMEM((seq, LANES), BF16),
                        pltpu.VMEM((tq // sub, 4 * sub, tk), F32), pltpu.VMEM((tq // sub, 4 * sub, tk), F32),
                        pltpu.VMEM((tq // sub, 4 * sub, LANES), F32),
                        pltpu.VMEM((tq // sub, 4 * sub, LANES), F32)],
        compiler_params=_cparams("parallel", "arbitrary", "arbitrary"),
        name="nsa_sel_attn",
    )(qk, qk, vv, selbias)


def _nsa_win_attn_kernel(q_ref, k_ref, v_ref, o_ref, kb_ref, vaug_ref, *, tq, span):
    g = pl.program_id(1)
    qi = pl.program_id(2)

    @pl.when(qi == 0)
    def _():
        kb_ref[...] = _both_halves(_group_lower(k_ref, g)).astype(BF16)
        vaug_ref[...] = _with_ones(_group_lower(v_ref, g)).astype(BF16)

    q4 = _stack_heads(q_ref)
    q0 = qi * tq
    k_start = pl.multiple_of(jnp.maximum(q0 - NSA_WINDOW, 0), tq)
    keys = pl.ds(k_start, span)
    dist = (q0 - k_start) + (lax.broadcasted_iota(jnp.int32, (tq, span), 0)
                             - lax.broadcasted_iota(jnp.int32, (tq, span), 1))
    ok = (dist >= 0) & (dist < NSA_WINDOW)
    ok2 = jnp.concatenate([ok, ok], axis=0)
    pairs = [q4[0:2 * tq], q4[2 * tq:4 * tq]]
    s = [lax.dot_general(qp, kb_ref[keys, :], _NT, preferred_element_type=F32) for qp in pairs]
    s = [jnp.where(ok2, sc, NEG_INF) for sc in s]
    p = [jnp.exp(sc - jnp.max(sc, axis=1, keepdims=True)).astype(BF16) for sc in s]
    acc = [jnp.dot(pc, vaug_ref[keys, :], preferred_element_type=F32) for pc in p]
    acc = jnp.concatenate(acc, axis=0)
    o_ref[...] = _unstack_heads(_normalise(acc), tq).astype(o_ref.dtype)


def _nsa_win_attn(qk, vv, *, batch, seq, k_col, v_col):
    T = qk.shape[0]
    G = NSA_KV_HEADS
    tq = _pick_tile(seq, 256)
    nq = seq // tq
    span = NSA_WINDOW + tq
    assert NSA_WINDOW % tq == 0 and seq >= span
    return pl.pallas_call(
        functools.partial(_nsa_win_attn_kernel, tq=tq, span=span),
        grid=(batch, G, nq),
        in_specs=[pl.BlockSpec((tq, 2 * LANES), lambda b, g, q: (b * nq + q, g)),
                  pl.BlockSpec((seq, LANES), lambda b, g, q: (b, k_col + g // 2)),
                  pl.BlockSpec((seq, LANES), lambda b, g, q: (b, v_col + g // 2))],
        out_specs=pl.BlockSpec((tq, 2 * LANES), lambda b, g, q: (b * nq + q, g)),
        out_shape=jax.ShapeDtypeStruct((T, N_HEADS * HEAD_DIM), BF16),
        scratch_shapes=[pltpu.VMEM((seq, LANES), BF16), pltpu.VMEM((seq, LANES), BF16)],
        compiler_params=_cparams("parallel", "arbitrary", "arbitrary"),
        name="nsa_win_attn",
    )(qk, qk, vv)


def _nsa_mixer(x, g, w_in, pos_k, pos_v, w_cmp_k, w_cmp_v, g_q, g_k, w_out, *, batch, seq):
    HD = N_HEADS * HEAD_DIM
    G, dh = NSA_KV_HEADS, HEAD_DIM
    kvw = G * dh
    scale = dh ** -0.5
    col = lambda n: slice(HD + n * kvw, HD + (n + 1) * kvw)
    order = [slice(0, HD), col(0), col(2), col(4), col(1), col(3), col(5)]
    w = jnp.concatenate([w_in[:, c] for c in order], axis=1).astype(BF16)
    n_gate = 3 * N_HEADS
    w_gate = jnp.pad(w_in[:, HD + 6 * kvw:], ((0, 0), (0, LANES - n_gate))).astype(BF16)
    head_gain = jnp.concatenate([jnp.tile(g_q * scale, N_HEADS), jnp.tile(g_k, 3 * G)])

    inv = ROPE_THETA ** (-jnp.arange(0, dh, 2, dtype=F32) / dh)
    ang = jnp.arange(seq, dtype=F32)[:, None] * inv[None, :]
    cos, sin = jnp.cos(ang), jnp.sin(ang)
    cos_t = jnp.tile(cos, (1, PROJ_CHUNK // (dh // 2)))
    sin_t = jnp.tile(jnp.concatenate([-sin, sin], axis=1), (1, PROJ_CHUNK // dh))

    qkv, gate = _proj(x, g, w, seq=seq, head_gain=head_gain, rope=(cos_t, sin_t), w_aux=w_gate)

    def blocks16(t):
        half = NSA_CMP_LEN // 2
        t = t.reshape(batch, seq // half, half, G, dh).transpose(0, 3, 1, 2, 4)
        return t.reshape(batch, G, seq // half, half * dh)

    kc0, vc0 = HD, HD + 3 * kvw
    kcmp, vcmp = _nsa_compress(blocks16(qkv[:, kc0:kc0 + kvw]), blocks16(qkv[:, vc0:vc0 + kvw]),
                               pos_k, pos_v, w_cmp_k, w_cmp_v, g_k)
    o_c, selbias = _nsa_cmp_attn(qkv, kcmp, vcmp, batch=batch, seq=seq)
    o_s = _nsa_sel_attn(qkv, qkv, selbias, batch=batch, seq=seq,
                        k_col=(HD + kvw) // LANES, v_col=(HD + 4 * kvw) // LANES)
    o_w = _nsa_win_attn(qkv, qkv, batch=batch, seq=seq,
                        k_col=(HD + 2 * kvw) // LANES, v_col=(HD + 5 * kvw) // LANES)
    r = np.arange(LANES)[:, None]
    c = np.arange(HD)[None, :] // dh
    expand = np.stack([(r == 3 * c + b) for b in range(3)]).astype(np.float32)
    return _outproj_gated3(x, w_out.astype(BF16), (o_c, o_s, o_w), gate, jnp.asarray(expand, BF16))


def kernel(x, norm_g, ffn1_w_gate, ffn1_w_up, ffn1_w_down, ffn2_w_gate, ffn2_w_up, ffn2_w_down, fox_w_in, fox_b_f, fox_g_q, fox_g_k, fox_w_out, nsa_w_in, nsa_cmp_pos_k, nsa_cmp_pos_v, nsa_w_cmp_k, nsa_w_cmp_v, nsa_g_q, nsa_g_k, nsa_w_out, gla_w_in, gla_w_gate_up, gla_b_gate, gla_g_out, gla_w_out, sb_w_in, sb_w_out):
    B, S, D = x.shape
    depth = norm_g.shape[0]
    n_mixers = 4
    x2 = x.reshape(B * S, D)
    for i in range(depth):
        m, j = i % n_mixers, i // n_mixers
        x2 = _ffn(x2, norm_g[i, 0], ffn1_w_gate[i].astype(BF16), ffn1_w_up[i].astype(BF16),
                  ffn1_w_down[i].astype(BF16))
        g = norm_g[i, 1]
        pending = None
        if m == 0:
            pending = _fox_mixer(x2, g, fox_w_in[j], fox_b_f[j], fox_g_q[j], fox_g_k[j], fox_w_out[j],
                                 batch=B, seq=S)
        elif m == 1:
            x2 = _nsa_mixer(x2, g, nsa_w_in[j], nsa_cmp_pos_k[j], nsa_cmp_pos_v[j], nsa_w_cmp_k[j],
                            nsa_w_cmp_v[j], nsa_g_q[j], nsa_g_k[j], nsa_w_out[j], batch=B, seq=S)
        elif m == 2:
            pending = _gla_mixer(x2, g, gla_w_in[j], gla_w_gate_up[j], gla_b_gate[j], gla_g_out[j],
                                 gla_w_out[j], batch=B, seq=S)
        else:
            pending = _sb_mixer(x2, g, sb_w_in[j], sb_w_out[j], batch=B, seq=S)
        x2 = _ffn(x2, norm_g[i, 2], ffn2_w_gate[i].astype(BF16), ffn2_w_up[i].astype(BF16),
                  ffn2_w_down[i].astype(BF16), pre=pending)
    return x2.reshape(B, S, D)
```

```python
import functools

import numpy as np
import jax
import jax.numpy as jnp
from jax import lax
from jax.experimental import pallas as pl
from jax.experimental.pallas import tpu as pltpu

F32 = jnp.float32
BF16 = jnp.bfloat16

N_HEADS = 16
HEAD_DIM = 64
ROPE_THETA = 10000.0
RMS_EPS = 1e-6
NEG_INF = -1e30
NSA_KV_HEADS = 4
NSA_CMP_LEN = 32
NSA_CMP_STRIDE = 16
NSA_SEL_LEN = 64
NSA_N_SEL = 16
NSA_WINDOW = 512
NSA_FORCED_SCORE = 1e9
GLA_HEADS = 4
GLA_GATE_RANK = 16
GLA_TAU = 16.0
GLA_CHUNK = 64

LANES = 128
V7X_VMEM_BYTES = 64 * 1024 * 1024
VMEM_LIMIT = V7X_VMEM_BYTES - 8 * 1024 * 1024

_NT = (((1,), (1,)), ((), ()))
_TN = (((0,), (0,)), ((), ()))


def _cparams(*sem):
    return pltpu.CompilerParams(dimension_semantics=sem, vmem_limit_bytes=VMEM_LIMIT)


def _rms_rows(x, g):
    return x * lax.rsqrt(jnp.mean(x * x, axis=-1, keepdims=True) + RMS_EPS) * g


def _softplus(z):
    return jnp.maximum(z, 0.0) + jnp.log(1.0 + jnp.exp(-jnp.abs(z)))


def _neg_abs(z):
    bits = lax.bitcast_convert_type(z, jnp.int32) | jnp.int32(-2 ** 31)
    return lax.bitcast_convert_type(bits, F32)


def _sigmoid(z):
    return 1.0 / (1.0 + jnp.exp(-z))


def _pick_tile(n, target):
    t = min(n, target)
    while n % t:
        t //= 2
    return t


def _ffn_kernel(*refs, tf, pre_mode):
    if pre_mode is None:
        x_ref, g_ref, wg_ref, wu_ref, wd_ref, o_ref, a_ref = refs
        x = x_ref[...]
    else:
        n_act = {"plain": 1, "sigmoid_gate": 2}[pre_mode]
        x_ref, *act_refs = refs[:1 + n_act]
        wout_ref, g_ref, wg_ref, wu_ref, wd_ref, o_ref, a_ref = refs[1 + n_act:]
        act = act_refs[0][...]
        if pre_mode == "sigmoid_gate":
            act = (act.astype(F32) * _sigmoid(act_refs[1][...].astype(F32))).astype(BF16)
        x = x_ref[...] + jnp.dot(act, wout_ref[...], preferred_element_type=F32)
    h = _rms_rows(x, g_ref[...]).astype(BF16)
    for c in range(a_ref.shape[1] // tf):
        cols = slice(c * tf, (c + 1) * tf)
        gate = jnp.dot(h, wg_ref[:, cols], preferred_element_type=F32)
        up = jnp.dot(h, wu_ref[:, cols], preferred_element_type=F32)
        a_ref[:, cols] = (gate * _sigmoid(gate) * up).astype(BF16)
    o_ref[...] = x + 0.5 * jnp.dot(a_ref[...], wd_ref[...], preferred_element_type=F32)


def _resident(shape):
    return pl.BlockSpec(shape, lambda *_: (0,) * len(shape), pipeline_mode=pl.Buffered(1))


def _ffn(x, g, wg, wu, wd, pre=None):
    T, D = x.shape
    F = wg.shape[1]
    tm = _pick_tile(T, 1024)
    tf = 256 if F % 256 == 0 else F
    args = [x]
    in_specs = [pl.BlockSpec((tm, D), lambda i: (i, 0))]
    pre_mode = None
    if pre is not None:
        pre_mode, acts, w_out = pre
        for arr, col_block in acts:
            args.append(arr)
            in_specs.append(pl.BlockSpec((tm, w_out.shape[0]), functools.partial(lambda i, c: (i, c), c=col_block)))
        args.append(w_out)
        in_specs.append(_resident(w_out.shape))
    args += [g.reshape(1, D), wg, wu, wd]
    in_specs += [_resident((1, D)), _resident((D, F)), _resident((D, F)), _resident((F, D))]
    return pl.pallas_call(
        functools.partial(_ffn_kernel, tf=tf, pre_mode=pre_mode),
        grid=(T // tm,),
        in_specs=in_specs,
        out_specs=pl.BlockSpec((tm, D), lambda i: (i, 0)),
        out_shape=jax.ShapeDtypeStruct((T, D), F32),
        scratch_shapes=[pltpu.VMEM((tm, F), BF16)],
        compiler_params=_cparams("parallel"),
        name="ffn" if pre is None else "outproj_ffn",
    )(*args)


PROJ_CHUNK = 256


def _proj_kernel(*refs, n_norm, rope, has_aux):
    it = iter(refs)
    x_ref, g_ref, w_ref = next(it), next(it), next(it)
    hg_ref, gsum_ref = (next(it), next(it)) if n_norm else (None, None)
    cos_ref, sin_ref = (next(it), next(it)) if rope else (None, None)
    waux_ref = next(it) if has_aux else None
    o_ref = next(it)
    ch = PROJ_CHUNK
    h = _rms_rows(x_ref[...], g_ref[...]).astype(BF16)
    for c in range(o_ref.shape[1] // ch):
        cols = slice(c * ch, (c + 1) * ch)
        y = jnp.dot(h, w_ref[:, cols], preferred_element_type=F32)
        if c * ch < n_norm:
            ss = jnp.dot((y * y).astype(BF16), gsum_ref[...], preferred_element_type=F32)
            y = y * lax.rsqrt(ss * (1.0 / HEAD_DIM) + RMS_EPS) * hg_ref[:, cols]
            if rope:
                lane = lax.broadcasted_iota(jnp.int32, (1, ch), 1)
                first_half = (lane % HEAD_DIM) < (HEAD_DIM // 2)
                partner = jnp.where(first_half,
                                    pltpu.roll(y, ch - HEAD_DIM // 2, 1),
                                    pltpu.roll(y, HEAD_DIM // 2, 1))
                y = y * cos_ref[...] + partner * sin_ref[...]
        o_ref[:, cols] = y.astype(o_ref.dtype)
    if has_aux:
        next(it)[...] = jnp.dot(h, waux_ref[...], preferred_element_type=F32)


def _proj(x, g, w, *, seq, head_gain=None, rope=None, w_aux=None):
    T, D = x.shape
    N = w.shape[1]
    tm = _pick_tile(seq, 1024)
    ch = PROJ_CHUNK
    n_norm = 0 if head_gain is None else head_gain.shape[0]
    assert N % ch == 0 and n_norm % ch == 0 and T % tm == 0
    args = [x, g.reshape(1, D), w]
    in_specs = [pl.BlockSpec((tm, D), lambda i: (i, 0)), _resident((1, D)), _resident((D, N))]
    if n_norm:
        gidx = np.arange(ch) // HEAD_DIM
        gsum = jnp.asarray((gidx[:, None] == gidx[None, :]).astype(np.float32), BF16)
        args += [head_gain.reshape(1, n_norm), gsum]
        in_specs += [_resident((1, n_norm)), _resident((ch, ch))]
    if rope is not None:
        nblk = seq // tm
        args += list(rope)
        in_specs += [pl.BlockSpec((tm, ch), lambda i: (i % nblk, 0))] * 2
    out_specs = [pl.BlockSpec((tm, N), lambda i: (i, 0))]
    out_shape = [jax.ShapeDtypeStruct((T, N), BF16)]
    if w_aux is not None:
        args.append(w_aux)
        in_specs.append(_resident(w_aux.shape))
        out_specs.append(pl.BlockSpec((tm, w_aux.shape[1]), lambda i: (i, 0)))
        out_shape.append(jax.ShapeDtypeStruct((T, w_aux.shape[1]), F32))
    out = pl.pallas_call(
        functools.partial(_proj_kernel, n_norm=n_norm, rope=rope is not None, has_aux=w_aux is not None),
        grid=(T // tm,),
        in_specs=in_specs,
        out_specs=out_specs,
        out_shape=out_shape,
        compiler_params=_cparams("parallel"),
        name="proj",
    )(*args)
    return out if w_aux is not None else out[0]


def _outproj_gated3_kernel(x_ref, oc_ref, os_ref, ow_ref, gate_ref, exp_ref, w_ref, o_ref):
    gs = _sigmoid(gate_ref[...]).astype(BF16)
    a = None
    for c, o_c in enumerate((oc_ref, os_ref, ow_ref)):
        t = jnp.dot(gs, exp_ref[c], preferred_element_type=F32) * o_c[...].astype(F32)
        a = t if a is None else a + t
    o_ref[...] = x_ref[...] + jnp.dot(a.astype(BF16), w_ref[...], preferred_element_type=F32)


def _outproj_gated3(x, w, branches, gate, expand):
    T, D = x.shape
    K = w.shape[0]
    tm = _pick_tile(T, 512)
    row = lambda i: (i, 0)
    return pl.pallas_call(
        _outproj_gated3_kernel,
        grid=(T // tm,),
        in_specs=[pl.BlockSpec((tm, D), row)] + [pl.BlockSpec((tm, K), row)] * 3
        + [pl.BlockSpec((tm, gate.shape[1]), row), _resident(expand.shape), _resident((K, D))],
        out_specs=pl.BlockSpec((tm, D), row),
        out_shape=jax.ShapeDtypeStruct((T, D), F32),
        compiler_params=_cparams("parallel"),
        name="outproj_gated3",
    )(x, *branches, gate, expand, w)


def _fox_c_kernel(x_ref, g_ref, wf_ref, bf_ref, tri_ref, c_ref, carry_ref):
    @pl.when(pl.program_id(1) == 0)
    def _():
        carry_ref[...] = jnp.zeros_like(carry_ref)

    h = _rms_rows(x_ref[...], g_ref[...])
    f = lax.dot_general(wf_ref[...], h, _NT, precision=lax.Precision.HIGHEST,
                        preferred_element_type=F32) + bf_ref[...]
    ls = jnp.minimum(f, 0.0) - jnp.log(1.0 + jnp.exp(-jnp.abs(f)))
    cs = jnp.dot(ls, tri_ref[...], precision=lax.Precision.HIGHEST,
                 preferred_element_type=F32) + carry_ref[...]
    c_ref[...] = cs
    carry_ref[...] = cs[:, cs.shape[1] - 1:]


def _fox_c(x, g, wf_t, b_f, *, batch, seq):
    T, D = x.shape
    H = wf_t.shape[0]
    tm = _pick_tile(seq, 512)
    ns = seq // tm
    tri = jnp.asarray(np.triu(np.ones((tm, tm), np.float32)))
    return pl.pallas_call(
        _fox_c_kernel,
        grid=(batch, ns),
        in_specs=[
            pl.BlockSpec((tm, D), lambda b, s: (b * ns + s, 0)),
            pl.BlockSpec((1, D), lambda b, s: (0, 0)),
            pl.BlockSpec((H, D), lambda b, s: (0, 0)),
            pl.BlockSpec((H, 1), lambda b, s: (0, 0)),
            pl.BlockSpec((tm, tm), lambda b, s: (0, 0)),
        ],
        out_specs=pl.BlockSpec((None, H, tm), lambda b, s: (b, 0, s)),
        out_shape=jax.ShapeDtypeStruct((batch, H, seq), F32),
        scratch_shapes=[pltpu.VMEM((H, 1), F32)],
        compiler_params=_cparams("parallel", "arbitrary"),
        name="fox_c",
    )(x, g.reshape(1, D), wf_t, b_f.reshape(H, 1), tri)


def _fox_attn_kernel(q_ref, k_ref, v_ref, c_ref, o_ref, vaug_ref, sa_ref, sb_ref, m_ref, acc_ref, *, tq, tk):
    qi = pl.program_id(2)
    lane = lax.broadcasted_iota(jnp.int32, (1, LANES), 1)
    upper = lane >= HEAD_DIM

    @pl.when(qi == 0)
    def _():
        v = v_ref[...]
        one = jnp.ones_like(v)
        vaug_ref[0] = jnp.where(upper, one, v)
        vaug_ref[1] = jnp.where(upper, v, one)

    q0 = pl.multiple_of(qi * tq, tq)
    qslab = q_ref[...]
    zero = jnp.zeros_like(qslab)
    qpos = lax.broadcasted_iota(jnp.int32, (tq, tk), 0)
    kpos = lax.broadcasted_iota(jnp.int32, (tq, tk), 1)

    heads = range(2)
    q2 = jnp.concatenate([jnp.where(upper, zero, qslab), jnp.where(upper, qslab, zero)], axis=0)
    c_q0 = [c_ref[i, :, pl.ds(q0, LANES)][:, 0:1] for i in heads]

    n_diag = tq // tk
    assert n_diag == 2

    def produce(s_ref, j, off=0):
        keys = pl.ds(pl.multiple_of(j * tk, tk), tk)
        n = tq - off
        qs = q2 if off == 0 else jnp.concatenate([q2[off:tq], q2[tq + off:2 * tq]], axis=0)
        s2 = lax.dot_general(qs, k_ref[keys, :], _NT, preferred_element_type=F32)
        for i in heads:
            s_ref[i, off:tq] = s2[i * n:(i + 1) * n] + (c_q0[i] - c_ref[i, :, keys])

    def consume(s_ref, j, diag_offset=None, off=0):
        keys = pl.ds(pl.multiple_of(j * tk, tk), tk)
        live = slice(off, tq)
        if diag_offset is None:
            score = lambda i: s_ref[i, live]
        else:
            causal = (kpos + diag_offset <= qpos)[live]
            score = lambda i: jnp.where(causal, s_ref[i, live], NEG_INF)
        m_old = [m_ref[i, live] for i in heads]
        m_new = [jnp.maximum(m_old[i], jnp.max(score(i), axis=1, keepdims=True)) for i in heads]
        p = [jnp.exp(score(i) - jnp.tile(m_new[i], (1, tk // LANES))).astype(BF16) for i in heads]
        for i in heads:
            m_ref[i, live] = m_new[i]
            acc_ref[i, live] = (jnp.exp(m_old[i] - m_new[i]) * acc_ref[i, live]
                                + jnp.dot(p[i], vaug_ref[i, keys, :], preferred_element_type=F32))

    m_ref[...] = jnp.full(m_ref.shape, NEG_INF, F32)
    acc_ref[...] = jnp.zeros(acc_ref.shape, F32)
    produce(sa_ref, 0)

    def body(m, _):
        produce(sb_ref, 2 * m + 1)
        consume(sa_ref, 2 * m)
        produce(sa_ref, 2 * m + 2)
        consume(sb_ref, 2 * m + 1)
        return 0

    lax.fori_loop(0, qi, body, 0)
    produce(sb_ref, 2 * qi + 1, off=tk)
    consume(sa_ref, 2 * qi, 0)
    consume(sb_ref, 2 * qi + 1, tk, off=tk)

    accs = (acc_ref[0], acc_ref[1])
    num = jnp.where(upper, accs[1], accs[0])
    den = jnp.where(upper, accs[0], accs[1])
    den = pltpu.roll(den, HEAD_DIM, 1)
    o_ref[...] = (num / den).astype(o_ref.dtype)


def _fox_attn(qkvg, c, *, batch, seq):
    T = qkvg.shape[0]
    HD = N_HEADS * HEAD_DIM
    tq = _pick_tile(seq, 1024)
    tk = tq // 2
    nq = seq // tq
    npair = HD // LANES
    return pl.pallas_call(
        functools.partial(_fox_attn_kernel, tq=tq, tk=tk),
        grid=(batch, npair, nq),
        in_specs=[
            pl.BlockSpec((tq, LANES), lambda b, h, q: (b * nq + q, h)),
            pl.BlockSpec((seq, LANES), lambda b, h, q: (b, npair + h)),
            pl.BlockSpec((seq, LANES), lambda b, h, q: (b, 2 * npair + h)),
            pl.BlockSpec((None, 2, 1, seq), lambda b, h, q: (b, h, 0, 0)),
        ],
        out_specs=pl.BlockSpec((tq, LANES), lambda b, h, q: (b * nq + q, h)),
        out_shape=jax.ShapeDtypeStruct((T, HD), BF16),
        scratch_shapes=[pltpu.VMEM((2, seq, LANES), BF16),
                        pltpu.VMEM((2, tq, tk), F32), pltpu.VMEM((2, tq, tk), F32),
                        pltpu.VMEM((2, tq, LANES), F32), pltpu.VMEM((2, tq, LANES), F32)],
        compiler_params=_cparams("parallel", "arbitrary", "arbitrary"),
        name="fox_attn",
    )(qkvg, qkvg, qkvg, c.reshape(batch, N_HEADS, 1, seq))


def _fox_mixer(x, g, w_in, b_f, g_q, g_k, w_out, *, batch, seq):
    HD = N_HEADS * HEAD_DIM
    scale = HEAD_DIM ** -0.5
    w = jnp.concatenate([w_in[:, :3 * HD], w_in[:, 3 * HD + N_HEADS:]], axis=1).astype(BF16)
    wf_t = w_in[:, 3 * HD:3 * HD + N_HEADS].T
    head_gain = jnp.concatenate([jnp.tile(g_q * scale, N_HEADS), jnp.tile(g_k, N_HEADS)])
    qkvg = _proj(x, g, w, seq=seq, head_gain=head_gain)
    c = _fox_c(x, g, wf_t, b_f, batch=batch, seq=seq)
    o = _fox_attn(qkvg, c, batch=batch, seq=seq)
    return "sigmoid_gate", [(o, 0), (qkvg, 3)], w_out.astype(BF16)


def _sb_attn_kernel(q_ref, k_ref, v_ref, tri_ref, o_ref, wa_ref, wb_ref, ta_ref, tb_ref, rest_ref, acc_ref,
                    *, tq, tk):
    qi = pl.program_id(2)
    lane = lax.broadcasted_iota(jnp.int32, (1, LANES), 1)
    upper = lane >= HEAD_DIM
    qslab = q_ref[...]
    zero = jnp.zeros_like(qslab)
    q2 = jnp.concatenate([jnp.where(upper, zero, qslab), jnp.where(upper, qslab, zero)], axis=0)
    rows = 2 * tq
    n_diag = tq // tk
    last = n_diag * (qi + 1) - 1

    def key_rows(k):
        return pl.ds(pl.multiple_of(jnp.maximum(last - k, 0) * tk, tk), tk)

    def head_rows(off):
        return (slice(off, tq), slice(tq + off, rows))

    def take(x, off):
        if off == 0:
            return x[...]
        lo, hi = head_rows(off)
        return jnp.concatenate([x[lo], x[hi]], axis=0)

    def put(ref, val, off, add=False):
        parts = [(slice(None), val)] if off == 0 else list(zip(head_rows(off), (val[:tq - off], val[tq - off:])))
        for sl, part in parts:
            if add:
                ref[sl] += part
            else:
                ref[sl] = part

    def scores(k, off=0):
        return lax.dot_general(take(q2, off), k_ref[key_rows(k), :], _NT, preferred_element_type=F32)

    def softplus_sums(z, w_ref, tot_ref, diag_offset, off=0):
        n2 = z.shape[0]
        sp = jnp.maximum(z, 0.0) + jnp.log2(1.0 + jnp.exp2(_neg_abs(z)))
        if diag_offset is not None:
            qpos = lax.broadcasted_iota(jnp.int32, (n2, tk), 0) % (tq - off) + off
            strict = lax.broadcasted_iota(jnp.int32, (n2, tk), 1) + diag_offset < qpos
            spm = jnp.where(strict, sp, 0.0)
        else:
            spm = sp
        cum = jnp.dot(spm.astype(BF16), tri_ref[...], preferred_element_type=F32)
        w = (z - sp) - cum
        if diag_offset is not None:
            w = jnp.where(strict, w, NEG_INF)
        put(w_ref, w, off)
        put(tot_ref, jnp.broadcast_to(cum[:, 0:1] + spm[:, 0:1], (n2, LANES)), off)

    def accumulate(w_ref, tot_ref, k, off=0):
        a = jnp.exp2(take(w_ref, off) - jnp.tile(take(rest_ref, off), (1, tk // LANES))).astype(BF16)
        put(acc_ref, jnp.dot(a, v_ref[key_rows(k), :], preferred_element_type=F32), off, add=True)
        put(rest_ref, take(tot_ref, off), off, add=True)

    assert n_diag % 2 == 0
    even, odd = (wa_ref, ta_ref), (wb_ref, tb_ref)
    offset = lambda k: (n_diag - 1 - k) * tk

    def produce_diag(buf, k):
        softplus_sums(scores(k, offset(k)), *buf, offset(k), offset(k))

    def step(k, cur, nxt, diag=False):
        off_cur, off_nxt = (offset(k), offset(k + 1)) if diag else (0, 0)
        z = scores(k + 1, off_nxt)
        accumulate(*cur, k, off_cur)
        softplus_sums(z, *nxt, off_nxt if diag else None, off_nxt)

    rest_ref[...] = jnp.zeros(rest_ref.shape, F32)
    acc_ref[...] = jnp.zeros(acc_ref.shape, F32)
    produce_diag(even, 0)
    produce_diag(odd, 1)
    accumulate(wa_ref, ta_ref, 0, offset(0))
    for m in range(1, n_diag // 2):
        step(2 * m - 1, odd, even, diag=True)
        step(2 * m, even, odd, diag=True)

    def body(m, _):
        step(2 * m - 1, odd, even)
        step(2 * m, even, odd)
        return 0

    lax.fori_loop(n_diag // 2, (n_diag // 2) * (qi + 1), body, 0)
    accumulate(wb_ref, tb_ref, last)
    o_ref[...] = jnp.where(upper, acc_ref[tq:2 * tq], acc_ref[0:tq]).astype(o_ref.dtype)


def _sb_attn(qkv, *, batch, seq):
    T = qkv.shape[0]
    HD = N_HEADS * HEAD_DIM
    tq = _pick_tile(seq, 1024)
    tk = 256
    nq = seq // tq
    npair = HD // LANES
    tri = jnp.asarray(np.tril(np.ones((tk, tk), np.float32), -1), BF16)
    return pl.pallas_call(
        functools.partial(_sb_attn_kernel, tq=tq, tk=tk),
        grid=(batch, npair, nq),
        in_specs=[
            pl.BlockSpec((tq, LANES), lambda b, h, q: (b * nq + q, h)),
            pl.BlockSpec((seq, LANES), lambda b, h, q: (b, npair + h)),
            pl.BlockSpec((seq, LANES), lambda b, h, q: (b, 2 * npair + h)),
            pl.BlockSpec((tk, tk), lambda b, h, q: (0, 0)),
        ],
        out_specs=pl.BlockSpec((tq, LANES), lambda b, h, q: (b * nq + q, h)),
        out_shape=jax.ShapeDtypeStruct((T, HD), BF16),
        scratch_shapes=[pltpu.VMEM((2 * tq, tk), F32)] * 2 + [pltpu.VMEM((2 * tq, LANES), F32)] * 4,
        compiler_params=_cparams("parallel", "parallel", "arbitrary"),
        name="sb_attn",
    )(qkv, qkv, qkv, tri)


def _sb_mixer(x, g, w_in, w_out, *, batch, seq):
    HD = N_HEADS * HEAD_DIM
    scale = HEAD_DIM ** -0.5 * float(np.log2(np.e))
    w = jnp.concatenate([w_in[:, :HD] * scale, w_in[:, HD:]], axis=1).astype(BF16)
    qkv = _proj(x, g, w, seq=seq)
    o = _sb_attn(qkv, batch=batch, seq=seq)
    return "plain", [(o, 0)], w_out.astype(BF16)


def _gla_kernel(q_ref, k_ref, v_ref, r_ref, glow_ref, wgu_ref, bg_ref, gout_ref, tri_ref,
                o_ref, state_ref, *, tm, dk):
    C = GLA_CHUNK

    @pl.when(pl.program_id(2) == 0)
    def _():
        state_ref[...] = jnp.zeros_like(state_ref)

    gate = jnp.dot(glow_ref[...], wgu_ref[...], precision=lax.Precision.HIGHEST,
                   preferred_element_type=F32) + bg_ref[...]
    log_a = (jnp.minimum(gate, 0.0) - jnp.log(1.0 + jnp.exp(-jnp.abs(gate)))) * (1.0 / GLA_TAU)
    row = lax.broadcasted_iota(jnp.int32, (C, C), 0)
    col = lax.broadcasted_iota(jnp.int32, (C, C), 1)
    causal = col <= row
    q_scale = dk ** -0.5
    chunks = [slice(c * C, (c + 1) * C) for c in range(tm // C)]

    tri = tri_ref[...]
    b = [jnp.dot(tri, log_a[sl], precision=lax.Precision.HIGHEST, preferred_element_type=F32)
         for sl in chunks]
    b_last = jnp.concatenate([jnp.broadcast_to(bc[C - 1:C], bc.shape) for bc in b], axis=0)
    b = jnp.concatenate(b, axis=0)
    q = q_ref[...].astype(F32) * q_scale
    k = k_ref[...].astype(F32)
    q_dec = (q * jnp.exp(b)).astype(BF16)
    k_neg = (k * jnp.exp(-b)).astype(BF16)
    k_rem = (k * jnp.exp(b_last - b)).astype(BF16)
    decay = jnp.exp(b_last)
    attn = [lax.dot_general(q_dec[sl], k_neg[sl], _NT, preferred_element_type=F32) for sl in chunks]
    attn = [jnp.where(causal, a, 0.0).astype(BF16) for a in attn]
    o_intra = [jnp.dot(attn[c], v_ref[sl, :], preferred_element_type=F32) for c, sl in enumerate(chunks)]
    u_t = [lax.dot_general(v_ref[sl, :], k_rem[sl], _TN, preferred_element_type=F32) for sl in chunks]

    state_t = state_ref[...]
    outs = []
    for c, sl in enumerate(chunks):
        outs.append(o_intra[c] + lax.dot_general(q_dec[sl], state_t.astype(BF16), _NT,
                                                 preferred_element_type=F32))
        state_t = state_t * decay[c * C:c * C + 1] + u_t[c]
    state_ref[...] = state_t

    o = jnp.concatenate(outs, axis=0)
    y = o * lax.rsqrt(jnp.mean(o * o, axis=-1, keepdims=True) + RMS_EPS) * gout_ref[...]
    r = r_ref[...].astype(F32)
    o_ref[...] = (y * (r * _sigmoid(r))).astype(o_ref.dtype)


def _gla(qkvr, glow, wgu, b_gate, g_out, *, batch, seq):
    T = qkvr.shape[0]
    Hg = GLA_HEADS
    dk_total = wgu.shape[1]
    dk = dk_total // Hg
    dv = (qkvr.shape[1] - 2 * dk_total) // 2 // Hg
    tm = _pick_tile(seq, 512)
    ns = seq // tm
    kq, kv = dk_total // dk, dk_total // dv
    tri = jnp.asarray(np.tril(np.ones((GLA_CHUNK, GLA_CHUNK), np.float32)))
    rowmap = lambda off: (lambda b, h, s: (b * ns + s, off + h))
    return pl.pallas_call(
        functools.partial(_gla_kernel, tm=tm, dk=dk),
        grid=(batch, Hg, ns),
        in_specs=[
            pl.BlockSpec((tm, dk), rowmap(0)),
            pl.BlockSpec((tm, dk), rowmap(kq)),
            pl.BlockSpec((tm, dv), rowmap(2 * kv)),
            pl.BlockSpec((tm, dv), rowmap(2 * kv + Hg)),
            pl.BlockSpec((tm, LANES), lambda b, h, s: (b * ns + s, 0)),
            pl.BlockSpec((LANES, dk), lambda b, h, s: (0, h)),
            pl.BlockSpec((1, dk), lambda b, h, s: (0, h)),
            pl.BlockSpec((1, dv), lambda b, h, s: (0, 0)),
            pl.BlockSpec((GLA_CHUNK, GLA_CHUNK), lambda b, h, s: (0, 0)),
        ],
        out_specs=pl.BlockSpec((tm, dv), rowmap(0)),
        out_shape=jax.ShapeDtypeStruct((T, Hg * dv), BF16),
        scratch_shapes=[pltpu.VMEM((dv, dk), F32)],
        compiler_params=_cparams("parallel", "parallel", "arbitrary"),
        name="gla",
    )(qkvr, qkvr, qkvr, qkvr, glow, wgu, b_gate.reshape(1, dk_total), g_out.reshape(1, dv), tri)


def _gla_mixer(x, g, w_in, w_gate_up, b_gate, g_out, w_out, *, batch, seq):
    dk_total = w_gate_up.shape[1]
    dv_total = w_out.shape[0]
    lo = 2 * dk_total + dv_total
    w_main = jnp.concatenate([w_in[:, :lo], w_in[:, lo + GLA_GATE_RANK:]], axis=1).astype(BF16)
    w_low = jnp.pad(w_in[:, lo:lo + GLA_GATE_RANK], ((0, 0), (0, LANES - GLA_GATE_RANK))).astype(BF16)
    wgu = jnp.pad(w_gate_up, ((0, LANES - GLA_GATE_RANK), (0, 0)))
    qkvr, glow = _proj(x, g, w_main, seq=seq, w_aux=w_low)
    o = _gla(qkvr, glow, wgu, b_gate, g_out, batch=batch, seq=seq)
    return "plain", [(o, 0)], w_out.astype(BF16)


def _nsa_compress_kernel(kc_ref, vc_ref, pk_ref, pv_ref, wk_ref, wv_ref, gk_ref, ko_ref, vo_ref, *, n16):
    def compress(r_ref, pos_ref, w_ref):
        r = r_ref[...].astype(F32)
        first = jnp.dot((r + pos_ref[0:1, :]).astype(BF16), w_ref[0], preferred_element_type=F32)
        second = jnp.dot((r + pos_ref[1:2, :]).astype(BF16), w_ref[1], preferred_element_type=F32)
        return first + pltpu.roll(second, n16 - 1, 0)

    kc = compress(kc_ref, pk_ref, wk_ref)
    ko_ref[...] = _rms_rows(kc, gk_ref[...]).astype(ko_ref.dtype)
    vo_ref[...] = compress(vc_ref, pv_ref, wv_ref).astype(vo_ref.dtype)


def _nsa_compress(kc_r, vc_r, pos_k, pos_v, w_k, w_v, g_k):
    B, G, n16, width = kc_r.shape
    dh = HEAD_DIM
    half = NSA_CMP_LEN // 2
    dup = lambda w: jnp.concatenate([w, w], axis=-1).reshape(2, half * dh, 2 * dh).astype(BF16)
    spec_r = pl.BlockSpec((None, None, n16, width), lambda b, g: (b, g, 0, 0))
    spec_o = pl.BlockSpec((None, None, n16, 2 * dh), lambda b, g: (b, g, 0, 0))
    const2 = lambda shape: pl.BlockSpec(shape, lambda b, g: (0,) * len(shape))
    out = jax.ShapeDtypeStruct((B, G, n16, 2 * dh), BF16)
    return pl.pallas_call(
        functools.partial(_nsa_compress_kernel, n16=n16),
        grid=(B, G),
        in_specs=[spec_r, spec_r, const2((2, width)), const2((2, width)),
                  const2((2, width, 2 * dh)), const2((2, width, 2 * dh)), const2((1, 2 * dh))],
        out_specs=[spec_o, spec_o],
        out_shape=[out, out],
        compiler_params=_cparams("parallel", "parallel"),
        name="nsa_compress",
    )(kc_r, vc_r, pos_k.reshape(2, width), pos_v.reshape(2, width), dup(w_k), dup(w_v),
      jnp.tile(g_k, 2).reshape(1, 2 * dh))


def _stack_heads(q_ref, extra=None, rows_in=slice(None)):
    lane = lax.broadcasted_iota(jnp.int32, (1, LANES), 1)
    upper = lane >= HEAD_DIM
    rows = []
    for j in range(2):
        slab = q_ref[rows_in, j * LANES:(j + 1) * LANES]
        zero = jnp.zeros_like(slab)
        for i in range(2):
            qh = jnp.where(upper, slab, zero) if i else jnp.where(upper, zero, slab)
            rows.append(qh if extra is None else jnp.concatenate([qh, extra], axis=1))
    return jnp.concatenate(rows, axis=0)


def _unstack_heads(o, tq):
    lane = lax.broadcasted_iota(jnp.int32, (1, LANES), 1)
    upper = lane >= HEAD_DIM
    slabs = []
    for j in range(2):
        lo = o[(2 * j) * tq:(2 * j + 1) * tq]
        hi = pltpu.roll(o[(2 * j + 1) * tq:(2 * j + 2) * tq], HEAD_DIM, 1)
        slabs.append(jnp.where(upper, hi, lo))
    return jnp.concatenate(slabs, axis=1)


def _group_lower(slab_ref, g):
    x = slab_ref[...].astype(F32)
    return jnp.where(g % 2 == 1, pltpu.roll(x, HEAD_DIM, 1), x)


def _both_halves(x):
    lane = lax.broadcasted_iota(jnp.int32, (1, LANES), 1)
    return jnp.where(lane >= HEAD_DIM, pltpu.roll(x, HEAD_DIM, 1), x)


def _with_ones(x):
    lane = lax.broadcasted_iota(jnp.int32, (1, LANES), 1)
    return jnp.where(lane >= HEAD_DIM, 1.0, x)


def _normalise(acc):
    return acc / pltpu.roll(acc, HEAD_DIM, 1)


def _nsa_cmp_attn_kernel(q_ref, kc_ref, vc_ref, cover_ref, oc_ref, sel_ref, *, tq, n16, parts):
    qi = pl.program_id(2)
    tp = tq // parts
    for part in range(parts):
        rows = slice(part * tp, (part + 1) * tp)
        row0 = qi * tq + part * tp
        q4 = _stack_heads(q_ref, rows_in=rows)
        s = lax.dot_general(q4, kc_ref[...], _NT, preferred_element_type=F32)
        qpos = row0 + lax.broadcasted_iota(jnp.int32, (tp, n16), 0)
        cmp_end = lax.broadcasted_iota(jnp.int32, (tp, n16), 1) * NSA_CMP_STRIDE + (NSA_CMP_LEN - 1)
        valid = jnp.concatenate([cmp_end <= qpos] * 4, axis=0)
        s = jnp.where(valid, s, NEG_INF)
        e = jnp.exp(s - jnp.max(s, axis=1, keepdims=True))
        p = jnp.where(valid, e / jnp.sum(e, axis=1, keepdims=True), 0.0)
        o = jnp.dot(p.astype(BF16), vc_ref[...], preferred_element_type=F32)
        oc_ref[rows, :] = _unstack_heads(o, tp).astype(oc_ref.dtype)

        psum = p[0:tp] + p[tp:2 * tp] + p[2 * tp:3 * tp] + p[3 * tp:4 * tp]
        cover = cover_ref[...]
        hi = psum.astype(BF16)
        r1 = psum - hi.astype(F32)
        mid = r1.astype(BF16)
        lo = (r1 - mid.astype(F32)).astype(BF16)
        imp = (jnp.dot(hi, cover, preferred_element_type=F32) + jnp.dot(mid, cover, preferred_element_type=F32)
               + jnp.dot(lo, cover, preferred_element_type=F32))

        blk = lax.broadcasted_iota(jnp.int32, (tp, LANES), 1)
        cur = (row0 + lax.broadcasted_iota(jnp.int32, (tp, LANES), 0)) // NSA_SEL_LEN
        forced = (blk == 0) | (blk == cur) | (blk == cur - 1)
        vals = jnp.where(forced, NSA_FORCED_SCORE, jnp.where(blk <= cur, imp, -1.0))
        taken = -3e38
        for _ in range(NSA_N_SEL):
            first = jnp.argmax(vals, axis=1, keepdims=True)
            vals = jnp.where(blk == first, taken, vals)
        sel_ref[rows, :] = jnp.where(vals == taken, 0.0, NEG_INF).astype(sel_ref.dtype)


def _nsa_cmp_attn(qk, kcmp, vcmp, *, batch, seq):
    T = qk.shape[0]
    G = NSA_KV_HEADS
    tq = _pick_tile(seq, 1024)
    nq = seq // tq
    n16 = kcmp.shape[2]
    n_sel = seq // NSA_SEL_LEN
    assert n_sel <= LANES and seq % tq == 0
    n = np.arange(n16)[:, None] * NSA_CMP_STRIDE
    j = np.arange(LANES)[None, :] * NSA_SEL_LEN
    cover = (n < j + NSA_SEL_LEN) & (n + NSA_CMP_LEN > j) & (np.arange(LANES)[None, :] < n_sel)
    cover = jnp.asarray(cover.astype(np.float32), BF16)
    spec_c = pl.BlockSpec((None, None, n16, LANES), lambda b, g, q: (b, g, 0, 0))
    return pl.pallas_call(
        functools.partial(_nsa_cmp_attn_kernel, tq=tq, n16=n16, parts=2),
        grid=(batch, G, nq),
        in_specs=[pl.BlockSpec((tq, 2 * LANES), lambda b, g, q: (b * nq + q, g)), spec_c, spec_c,
                  pl.BlockSpec((n16, LANES), lambda b, g, q: (0, 0))],
        out_specs=[pl.BlockSpec((tq, 2 * LANES), lambda b, g, q: (b * nq + q, g)),
                   pl.BlockSpec((tq, LANES), lambda b, g, q: (b * nq + q, g))],
        out_shape=[jax.ShapeDtypeStruct((T, N_HEADS * HEAD_DIM), BF16),
                   jax.ShapeDtypeStruct((T, G * LANES), BF16)],
        compiler_params=_cparams("parallel", "parallel", "parallel"),
        name="nsa_cmp_attn",
    )(qk, kcmp, vcmp, cover)


def _nsa_sel_attn_kernel(q_ref, k_ref, v_ref, sel_ref, o_ref, kaug_ref, vaug_ref, sa_ref, sb_ref, m_ref, acc_ref,
                         *, tq, sub, tk, seq):
    g = pl.program_id(1)
    qi = pl.program_id(2)

    @pl.when(qi == 0)
    def _():
        kaug_ref[:, 0:LANES] = _both_halves(_group_lower(k_ref, g)).astype(BF16)
        blk_of_key = lax.broadcasted_iota(jnp.int32, (seq, LANES), 0) // NSA_SEL_LEN
        blk = lax.broadcasted_iota(jnp.int32, (seq, LANES), 1)
        kaug_ref[:, LANES:2 * LANES] = jnp.where(blk_of_key == blk, 1.0, 0.0).astype(BF16)
        vaug_ref[...] = _with_ones(_group_lower(v_ref, g)).astype(BF16)

    chains = range(tq // sub)
    rows = [slice(c * sub, (c + 1) * sub) for c in chains]
    q4 = [_stack_heads(q_ref, extra=sel_ref[rows[c], :], rows_in=rows[c]) for c in chains]
    n_diag = tq // tk
    assert n_diag == 2

    def head_rows(off):
        return [slice(h * sub + off, (h + 1) * sub) for h in range(4)]

    def take(x, off):
        return x[...] if off == 0 else jnp.concatenate([x[sl] for sl in head_rows(off)], axis=0)

    def put(ref, val, off):
        if off == 0:
            ref[...] = val
        else:
            for h, sl in enumerate(head_rows(off)):
                ref[sl] = val[h * (sub - off):(h + 1) * (sub - off)]

    def produce(s_ref, j, off=0):
        kt = kaug_ref[pl.ds(pl.multiple_of(j * tk, tk), tk), :]
        for c in chains:
            s_ref[c, 0:4 * (sub - off)] = lax.dot_general(take(q4[c], off), kt, _NT,
                                                          preferred_element_type=F32)

    def consume(s_ref, j, diag_offset=None, off=0):
        keys = pl.ds(pl.multiple_of(j * tk, tk), tk)
        live = slice(0, 4 * (sub - off))
        if diag_offset is not None:
            kpos = diag_offset + lax.broadcasted_iota(jnp.int32, (sub - off, tk), 1)
            qpos = off + lax.broadcasted_iota(jnp.int32, (sub - off, tk), 0)
            ok = [jnp.concatenate([kpos <= qpos + c * sub] * 4, axis=0) for c in chains]
            score = lambda c: jnp.where(ok[c], s_ref[c, live], NEG_INF)
        else:
            score = lambda c: s_ref[c, live]
        vt = vaug_ref[keys, :]
        m_old = [take(m_ref.at[c], off) for c in chains]
        m_new = [jnp.maximum(m_old[c], jnp.max(score(c), axis=1, keepdims=True)) for c in chains]
        p = [jnp.exp(score(c) - jnp.tile(m_new[c], (1, tk // LANES))).astype(BF16) for c in chains]
        for c in chains:
            put(m_ref.at[c], m_new[c], off)
            put(acc_ref.at[c], (jnp.exp(m_old[c] - m_new[c]) * take(acc_ref.at[c], off)
                                + jnp.dot(p[c], vt, preferred_element_type=F32)), off)

    m_ref[...] = jnp.full(m_ref.shape, NEG_INF, F32)
    acc_ref[...] = jnp.zeros(acc_ref.shape, F32)
    produce(sa_ref, 0)

    def body(m, _):
        produce(sb_ref, 2 * m + 1)
        consume(sa_ref, 2 * m)
        produce(sa_ref, 2 * m + 2)
        consume(sb_ref, 2 * m + 1)
        return 0

    lax.fori_loop(0, qi, body, 0)
    produce(sb_ref, 2 * qi + 1, off=tk)
    consume(sa_ref, 2 * qi, 0)
    consume(sb_ref, 2 * qi + 1, tk, off=tk)

    for c in chains:
        o_ref[rows[c], :] = _unstack_heads(_normalise(acc_ref[c]), sub).astype(o_ref.dtype)


def _nsa_sel_attn(qk, vv, selbias, *, batch, seq, k_col, v_col):
    T = qk.shape[0]
    G = NSA_KV_HEADS
    tq = _pick_tile(seq, 1024)
    sub = tq
    tk = tq // 2
    nq = seq // tq
    return pl.pallas_call(
        functools.partial(_nsa_sel_attn_kernel, tq=tq, sub=sub, tk=tk, seq=seq),
        grid=(batch, G, nq),
        in_specs=[pl.BlockSpec((tq, 2 * LANES), lambda b, g, q: (b * nq + q, g)),
                  pl.BlockSpec((seq, LANES), lambda b, g, q: (b, k_col + g // 2)),
                  pl.BlockSpec((seq, LANES), lambda b, g, q: (b, v_col + g // 2)),
                  pl.BlockSpec((tq, LANES), lambda b, g, q: (b * nq + q, g))],
        out_specs=pl.BlockSpec((tq, 2 * LANES), lambda b, g, q: (b * nq + q, g)),
        out_shape=jax.ShapeDtypeStruct((T, N_HEADS * HEAD_DIM), BF16),
        scratch_shapes=[pltpu.VMEM((seq, 2 * LANES), BF16), pltpu.VMEM((seq, LANES), BF16),
                        pltpu.VMEM((tq // sub, 4 * sub, tk), F32), pltpu.VMEM((tq // sub, 4 * sub, tk), F32),
                        pltpu.VMEM((tq // sub, 4 * sub, LANES), F32),
                        pltpu.VMEM((tq // sub, 4 * sub, LANES), F32)],
        compiler_params=_cparams("parallel", "arbitrary", "arbitrary"),
        name="nsa_sel_attn",
    )(qk, qk, vv, selbias)


def _nsa_win_attn_kernel(q_ref, k_ref, v_ref, o_ref, kb_ref, vaug_ref, *, tq, tp, span):
    g = pl.program_id(1)
    qi = pl.program_id(2)

    @pl.when(qi == 0)
    def _():
        kb_ref[...] = _both_halves(_group_lower(k_ref, g)).astype(BF16)
        vaug_ref[...] = _with_ones(_group_lower(v_ref, g)).astype(BF16)

    for part in range(tq // tp):
        rows = slice(part * tp, (part + 1) * tp)
        q4 = _stack_heads(q_ref, rows_in=rows)
        q0 = qi * tq + part * tp
        k_start = pl.multiple_of(jnp.maximum(q0 - NSA_WINDOW, 0), tp)
        keys = pl.ds(k_start, span)
        dist = (q0 - k_start) + (lax.broadcasted_iota(jnp.int32, (tp, span), 0)
                                 - lax.broadcasted_iota(jnp.int32, (tp, span), 1))
        ok = (dist >= 0) & (dist < NSA_WINDOW)
        ok2 = jnp.concatenate([ok, ok], axis=0)
        pairs = [q4[0:2 * tp], q4[2 * tp:4 * tp]]
        s = [lax.dot_general(qp, kb_ref[keys, :], _NT, preferred_element_type=F32) for qp in pairs]
        s = [jnp.where(ok2, sc, NEG_INF) for sc in s]
        p = [jnp.exp(sc - jnp.max(sc, axis=1, keepdims=True)).astype(BF16) for sc in s]
        acc = [jnp.dot(pc, vaug_ref[keys, :], preferred_element_type=F32) for pc in p]
        acc = jnp.concatenate(acc, axis=0)
        o_ref[rows, :] = _unstack_heads(_normalise(acc), tp).astype(o_ref.dtype)


def _nsa_win_attn(qk, vv, *, batch, seq, k_col, v_col):
    T = qk.shape[0]
    G = NSA_KV_HEADS
    tp = _pick_tile(seq, 256)
    tq = _pick_tile(seq, 4 * tp)
    nq = seq // tq
    span = NSA_WINDOW + tp
    assert NSA_WINDOW % tp == 0 and seq >= span
    return pl.pallas_call(
        functools.partial(_nsa_win_attn_kernel, tq=tq, tp=tp, span=span),
        grid=(batch, G, nq),
        in_specs=[pl.BlockSpec((tq, 2 * LANES), lambda b, g, q: (b * nq + q, g)),
                  pl.BlockSpec((seq, LANES), lambda b, g, q: (b, k_col + g // 2)),
                  pl.BlockSpec((seq, LANES), lambda b, g, q: (b, v_col + g // 2))],
        out_specs=pl.BlockSpec((tq, 2 * LANES), lambda b, g, q: (b * nq + q, g)),
        out_shape=jax.ShapeDtypeStruct((T, N_HEADS * HEAD_DIM), BF16),
        scratch_shapes=[pltpu.VMEM((seq, LANES), BF16), pltpu.VMEM((seq, LANES), BF16)],
        compiler_params=_cparams("parallel", "arbitrary", "arbitrary"),
        name="nsa_win_attn",
    )(qk, qk, vv)


def _nsa_mixer(x, g, w_in, pos_k, pos_v, w_cmp_k, w_cmp_v, g_q, g_k, w_out, *, batch, seq):
    HD = N_HEADS * HEAD_DIM
    G, dh = NSA_KV_HEADS, HEAD_DIM
    kvw = G * dh
    scale = dh ** -0.5
    col = lambda n: slice(HD + n * kvw, HD + (n + 1) * kvw)
    order = [slice(0, HD), col(0), col(2), col(4), col(1), col(3), col(5)]
    w = jnp.concatenate([w_in[:, c] for c in order], axis=1).astype(BF16)
    n_gate = 3 * N_HEADS
    w_gate = jnp.pad(w_in[:, HD + 6 * kvw:], ((0, 0), (0, LANES - n_gate))).astype(BF16)
    head_gain = jnp.concatenate([jnp.tile(g_q * scale, N_HEADS), jnp.tile(g_k, 3 * G)])

    inv = ROPE_THETA ** (-jnp.arange(0, dh, 2, dtype=F32) / dh)
    ang = jnp.arange(seq, dtype=F32)[:, None] * inv[None, :]
    cos, sin = jnp.cos(ang), jnp.sin(ang)
    cos_t = jnp.tile(cos, (1, PROJ_CHUNK // (dh // 2)))
    sin_t = jnp.tile(jnp.concatenate([-sin, sin], axis=1), (1, PROJ_CHUNK // dh))

    qkv, gate = _proj(x, g, w, seq=seq, head_gain=head_gain, rope=(cos_t, sin_t), w_aux=w_gate)

    def blocks16(t):
        half = NSA_CMP_LEN // 2
        t = t.reshape(batch, seq // half, half, G, dh).transpose(0, 3, 1, 2, 4)
        return t.reshape(batch, G, seq // half, half * dh)

    kc0, vc0 = HD, HD + 3 * kvw
    kcmp, vcmp = _nsa_compress(blocks16(qkv[:, kc0:kc0 + kvw]), blocks16(qkv[:, vc0:vc0 + kvw]),
                               pos_k, pos_v, w_cmp_k, w_cmp_v, g_k)
    o_c, selbias = _nsa_cmp_attn(qkv, kcmp, vcmp, batch=batch, seq=seq)
    o_s = _nsa_sel_attn(qkv, qkv, selbias, batch=batch, seq=seq,
                        k_col=(HD + kvw) // LANES, v_col=(HD + 4 * kvw) // LANES)
    o_w = _nsa_win_attn(qkv, qkv, batch=batch, seq=seq,
                        k_col=(HD + 2 * kvw) // LANES, v_col=(HD + 5 * kvw) // LANES)
    r = np.arange(LANES)[:, None]
    c = np.arange(HD)[None, :] // dh
    expand = np.stack([(r == 3 * c + b) for b in range(3)]).astype(np.float32)
    return _outproj_gated3(x, w_out.astype(BF16), (o_c, o_s, o_w), gate, jnp.asarray(expand, BF16))


def kernel(x, norm_g, ffn1_w_gate, ffn1_w_up, ffn1_w_down, ffn2_w_gate, ffn2_w_up, ffn2_w_down, fox_w_in, fox_b_f, fox_g_q, fox_g_k, fox_w_out, nsa_w_in, nsa_cmp_pos_k, nsa_cmp_pos_v, nsa_w_cmp_k, nsa_w_cmp_v, nsa_g_q, nsa_g_k, nsa_w_out, gla_w_in, gla_w_gate_up, gla_b_gate, gla_g_out, gla_w_out, sb_w_in, sb_w_out):
    B, S, D = x.shape
    depth = norm_g.shape[0]
    n_mixers = 4
    x2 = x.reshape(B * S, D)
    for i in range(depth):
        m, j = i % n_mixers, i // n_mixers
        x2 = _ffn(x2, norm_g[i, 0], ffn1_w_gate[i].astype(BF16), ffn1_w_up[i].astype(BF16),
                  ffn1_w_down[i].astype(BF16))
        g = norm_g[i, 1]
        pending = None
        if m == 0:
            pending = _fox_mixer(x2, g, fox_w_in[j], fox_b_f[j], fox_g_q[j], fox_g_k[j], fox_w_out[j],
                                 batch=B, seq=S)
        elif m == 1:
            x2 = _nsa_mixer(x2, g, nsa_w_in[j], nsa_cmp_pos_k[j], nsa_cmp_pos_v[j], nsa_w_cmp_k[j],
                            nsa_w_cmp_v[j], nsa_g_q[j], nsa_g_k[j], nsa_w_out[j], batch=B, seq=S)
        elif m == 2:
            pending = _gla_mixer(x2, g, gla_w_in[j], gla_w_gate_up[j], gla_b_gate[j], gla_g_out[j],
                                 gla_w_out[j], batch=B, seq=S)
        else:
            pending = _sb_mixer(x2, g, sb_w_in[j], sb_w_out[j], batch=B, seq=S)
        x2 = _ffn(x2, norm_g[i, 2], ffn2_w_gate[i].astype(BF16), ffn2_w_up[i].astype(BF16),
                  ffn2_w_down[i].astype(BF16), pre=pending)
    return x2.reshape(B, S, D)
```

```python
import functools

import numpy as np
import jax
import jax.numpy as jnp
from jax import lax
from jax.experimental import pallas as pl
from jax.experimental.pallas import tpu as pltpu

F32 = jnp.float32
BF16 = jnp.bfloat16

N_HEADS = 16
HEAD_DIM = 64
ROPE_THETA = 10000.0
RMS_EPS = 1e-6
NEG_INF = -1e30
NSA_KV_HEADS = 4
NSA_CMP_LEN = 32
NSA_CMP_STRIDE = 16
NSA_SEL_LEN = 64
NSA_N_SEL = 16
NSA_WINDOW = 512
NSA_FORCED_SCORE = 1e9
GLA_HEADS = 4
GLA_GATE_RANK = 16
GLA_TAU = 16.0
GLA_CHUNK = 64

LANES = 128
V7X_VMEM_BYTES = 64 * 1024 * 1024
VMEM_LIMIT = V7X_VMEM_BYTES - 8 * 1024 * 1024

_NT = (((1,), (1,)), ((), ()))
_TN = (((0,), (0,)), ((), ()))


def _cparams(*sem):
    return pltpu.CompilerParams(dimension_semantics=sem, vmem_limit_bytes=VMEM_LIMIT)


def _rms_rows(x, g):
    return x * lax.rsqrt(jnp.mean(x * x, axis=-1, keepdims=True) + RMS_EPS) * g


def _softplus(z):
    return jnp.maximum(z, 0.0) + jnp.log(1.0 + jnp.exp(-jnp.abs(z)))


def _neg_abs(z):
    bits = lax.bitcast_convert_type(z, jnp.int32) | jnp.int32(-2 ** 31)
    return lax.bitcast_convert_type(bits, F32)


def _sigmoid(z):
    return 1.0 / (1.0 + jnp.exp(-z))


def _pick_tile(n, target):
    t = min(n, target)
    while n % t:
        t //= 2
    return t


def _ffn_kernel(*refs, tf, pre_mode):
    if pre_mode is None:
        x_ref, g_ref, wg_ref, wu_ref, wd_ref, o_ref, a_ref = refs
        x = x_ref[...]
    else:
        n_act = {"plain": 1, "sigmoid_gate": 2}[pre_mode]
        x_ref, *act_refs = refs[:1 + n_act]
        wout_ref, g_ref, wg_ref, wu_ref, wd_ref, o_ref, a_ref = refs[1 + n_act:]
        act = act_refs[0][...]
        if pre_mode == "sigmoid_gate":
            act = (act.astype(F32) * _sigmoid(act_refs[1][...].astype(F32))).astype(BF16)
        x = x_ref[...] + jnp.dot(act, wout_ref[...], preferred_element_type=F32)
    h = _rms_rows(x, g_ref[...]).astype(BF16)
    for c in range(a_ref.shape[1] // tf):
        cols = slice(c * tf, (c + 1) * tf)
        gate = jnp.dot(h, wg_ref[:, cols], preferred_element_type=F32)
        up = jnp.dot(h, wu_ref[:, cols], preferred_element_type=F32)
        a_ref[:, cols] = (gate * _sigmoid(gate) * up).astype(BF16)
    o_ref[...] = x + 0.5 * jnp.dot(a_ref[...], wd_ref[...], preferred_element_type=F32)


def _resident(shape):
    return pl.BlockSpec(shape, lambda *_: (0,) * len(shape), pipeline_mode=pl.Buffered(1))


def _ffn(x, g, wg, wu, wd, pre=None):
    T, D = x.shape
    F = wg.shape[1]
    tm = _pick_tile(T, 1024)
    tf = 256 if F % 256 == 0 else F
    args = [x]
    in_specs = [pl.BlockSpec((tm, D), lambda i: (i, 0))]
    pre_mode = None
    if pre is not None:
        pre_mode, acts, w_out = pre
        for arr, col_block in acts:
            args.append(arr)
            in_specs.append(pl.BlockSpec((tm, w_out.shape[0]), functools.partial(lambda i, c: (i, c), c=col_block)))
        args.append(w_out)
        in_specs.append(_resident(w_out.shape))
    args += [g.reshape(1, D), wg, wu, wd]
    in_specs += [_resident((1, D)), _resident((D, F)), _resident((D, F)), _resident((F, D))]
    return pl.pallas_call(
        functools.partial(_ffn_kernel, tf=tf, pre_mode=pre_mode),
        grid=(T // tm,),
        in_specs=in_specs,
        out_specs=pl.BlockSpec((tm, D), lambda i: (i, 0)),
        out_shape=jax.ShapeDtypeStruct((T, D), F32),
        scratch_shapes=[pltpu.VMEM((tm, F), BF16)],
        compiler_params=_cparams("parallel"),
        name="ffn" if pre is None else "outproj_ffn",
    )(*args)


PROJ_CHUNK = 256


def _proj_kernel(*refs, n_norm, rope, has_aux):
    it = iter(refs)
    x_ref, g_ref, w_ref = next(it), next(it), next(it)
    hg_ref, gsum_ref = (next(it), next(it)) if n_norm else (None, None)
    cos_ref, sin_ref = (next(it), next(it)) if rope else (None, None)
    waux_ref = next(it) if has_aux else None
    o_ref = next(it)
    ch = PROJ_CHUNK
    h = _rms_rows(x_ref[...], g_ref[...]).astype(BF16)
    for c in range(o_ref.shape[1] // ch):
        cols = slice(c * ch, (c + 1) * ch)
        y = jnp.dot(h, w_ref[:, cols], preferred_element_type=F32)
        if c * ch < n_norm:
            ss = jnp.dot((y * y).astype(BF16), gsum_ref[...], preferred_element_type=F32)
            y = y * lax.rsqrt(ss * (1.0 / HEAD_DIM) + RMS_EPS) * hg_ref[:, cols]
            if rope:
                lane = lax.broadcasted_iota(jnp.int32, (1, ch), 1)
                first_half = (lane % HEAD_DIM) < (HEAD_DIM // 2)
                partner = jnp.where(first_half,
                                    pltpu.roll(y, ch - HEAD_DIM // 2, 1),
                                    pltpu.roll(y, HEAD_DIM // 2, 1))
                y = y * cos_ref[...] + partner * sin_ref[...]
        o_ref[:, cols] = y.astype(o_ref.dtype)
    if has_aux:
        next(it)[...] = jnp.dot(h, waux_ref[...], preferred_element_type=F32)


def _proj(x, g, w, *, seq, head_gain=None, rope=None, w_aux=None):
    T, D = x.shape
    N = w.shape[1]
    tm = _pick_tile(seq, 1024)
    ch = PROJ_CHUNK
    n_norm = 0 if head_gain is None else head_gain.shape[0]
    assert N % ch == 0 and n_norm % ch == 0 and T % tm == 0
    args = [x, g.reshape(1, D), w]
    in_specs = [pl.BlockSpec((tm, D), lambda i: (i, 0)), _resident((1, D)), _resident((D, N))]
    if n_norm:
        gidx = np.arange(ch) // HEAD_DIM
        gsum = jnp.asarray((gidx[:, None] == gidx[None, :]).astype(np.float32), BF16)
        args += [head_gain.reshape(1, n_norm), gsum]
        in_specs += [_resident((1, n_norm)), _resident((ch, ch))]
    if rope is not None:
        nblk = seq // tm
        args += list(rope)
        in_specs += [pl.BlockSpec((tm, ch), lambda i: (i % nblk, 0))] * 2
    out_specs = [pl.BlockSpec((tm, N), lambda i: (i, 0))]
    out_shape = [jax.ShapeDtypeStruct((T, N), BF16)]
    if w_aux is not None:
        args.append(w_aux)
        in_specs.append(_resident(w_aux.shape))
        out_specs.append(pl.BlockSpec((tm, w_aux.shape[1]), lambda i: (i, 0)))
        out_shape.append(jax.ShapeDtypeStruct((T, w_aux.shape[1]), F32))
    out = pl.pallas_call(
        functools.partial(_proj_kernel, n_norm=n_norm, rope=rope is not None, has_aux=w_aux is not None),
        grid=(T // tm,),
        in_specs=in_specs,
        out_specs=out_specs,
        out_shape=out_shape,
        compiler_params=_cparams("parallel"),
        name="proj",
    )(*args)
    return out if w_aux is not None else out[0]


def _outproj_gated3_kernel(x_ref, oc_ref, os_ref, ow_ref, gate_ref, exp_ref, w_ref, o_ref):
    gs = _sigmoid(gate_ref[...]).astype(BF16)
    a = None
    for c, o_c in enumerate((oc_ref, os_ref, ow_ref)):
        t = jnp.dot(gs, exp_ref[c], preferred_element_type=F32) * o_c[...].astype(F32)
        a = t if a is None else a + t
    o_ref[...] = x_ref[...] + jnp.dot(a.astype(BF16), w_ref[...], preferred_element_type=F32)


def _outproj_gated3(x, w, branches, gate, expand):
    T, D = x.shape
    K = w.shape[0]
    tm = _pick_tile(T, 512)
    row = lambda i: (i, 0)
    return pl.pallas_call(
        _outproj_gated3_kernel,
        grid=(T // tm,),
        in_specs=[pl.BlockSpec((tm, D), row)] + [pl.BlockSpec((tm, K), row)] * 3
        + [pl.BlockSpec((tm, gate.shape[1]), row), _resident(expand.shape), _resident((K, D))],
        out_specs=pl.BlockSpec((tm, D), row),
        out_shape=jax.ShapeDtypeStruct((T, D), F32),
        compiler_params=_cparams("parallel"),
        name="outproj_gated3",
    )(x, *branches, gate, expand, w)


def _fox_c_kernel(x_ref, g_ref, wf_ref, bf_ref, tri_ref, c_ref, carry_ref):
    @pl.when(pl.program_id(1) == 0)
    def _():
        carry_ref[...] = jnp.zeros_like(carry_ref)

    h = _rms_rows(x_ref[...], g_ref[...])
    f = lax.dot_general(wf_ref[...], h, _NT, precision=lax.Precision.HIGHEST,
                        preferred_element_type=F32) + bf_ref[...]
    ls = jnp.minimum(f, 0.0) - jnp.log(1.0 + jnp.exp(-jnp.abs(f)))
    cs = jnp.dot(ls, tri_ref[...], precision=lax.Precision.HIGHEST,
                 preferred_element_type=F32) + carry_ref[...]
    c_ref[...] = cs
    carry_ref[...] = cs[:, cs.shape[1] - 1:]


def _fox_c(x, g, wf_t, b_f, *, batch, seq):
    T, D = x.shape
    H = wf_t.shape[0]
    tm = _pick_tile(seq, 512)
    ns = seq // tm
    tri = jnp.asarray(np.triu(np.ones((tm, tm), np.float32)))
    return pl.pallas_call(
        _fox_c_kernel,
        grid=(batch, ns),
        in_specs=[
            pl.BlockSpec((tm, D), lambda b, s: (b * ns + s, 0)),
            pl.BlockSpec((1, D), lambda b, s: (0, 0)),
            pl.BlockSpec((H, D), lambda b, s: (0, 0)),
            pl.BlockSpec((H, 1), lambda b, s: (0, 0)),
            pl.BlockSpec((tm, tm), lambda b, s: (0, 0)),
        ],
        out_specs=pl.BlockSpec((None, H, tm), lambda b, s: (b, 0, s)),
        out_shape=jax.ShapeDtypeStruct((batch, H, seq), F32),
        scratch_shapes=[pltpu.VMEM((H, 1), F32)],
        compiler_params=_cparams("parallel", "arbitrary"),
        name="fox_c",
    )(x, g.reshape(1, D), wf_t, b_f.reshape(H, 1), tri)


def _fox_attn_kernel(q_ref, k_ref, v_ref, c_ref, o_ref, vaug_ref, sa_ref, sb_ref, m_ref, acc_ref, *, tq, tk):
    qi = pl.program_id(2)
    lane = lax.broadcasted_iota(jnp.int32, (1, LANES), 1)
    upper = lane >= HEAD_DIM

    @pl.when(qi == 0)
    def _():
        v = v_ref[...]
        one = jnp.ones_like(v)
        vaug_ref[0] = jnp.where(upper, one, v)
        vaug_ref[1] = jnp.where(upper, v, one)

    q0 = pl.multiple_of(qi * tq, tq)
    qslab = q_ref[...]
    zero = jnp.zeros_like(qslab)
    qpos = lax.broadcasted_iota(jnp.int32, (tq, tk), 0)
    kpos = lax.broadcasted_iota(jnp.int32, (tq, tk), 1)

    heads = range(2)
    q2 = jnp.concatenate([jnp.where(upper, zero, qslab), jnp.where(upper, qslab, zero)], axis=0)
    c_q0 = [c_ref[i, :, pl.ds(q0, LANES)][:, 0:1] for i in heads]

    n_diag = tq // tk
    assert n_diag == 2

    def produce(s_ref, j, off=0):
        keys = pl.ds(pl.multiple_of(j * tk, tk), tk)
        n = tq - off
        qs = q2 if off == 0 else jnp.concatenate([q2[off:tq], q2[tq + off:2 * tq]], axis=0)
        s2 = lax.dot_general(qs, k_ref[keys, :], _NT, preferred_element_type=F32)
        for i in heads:
            s_ref[i, off:tq] = s2[i * n:(i + 1) * n] + (c_q0[i] - c_ref[i, :, keys])

    def consume(s_ref, j, diag_offset=None, off=0):
        keys = pl.ds(pl.multiple_of(j * tk, tk), tk)
        live = slice(off, tq)
        if diag_offset is None:
            score = lambda i: s_ref[i, live]
        else:
            causal = (kpos + diag_offset <= qpos)[live]
            score = lambda i: jnp.where(causal, s_ref[i, live], NEG_INF)
        m_old = [m_ref[i, live] for i in heads]
        m_new = [jnp.maximum(m_old[i], jnp.max(score(i), axis=1, keepdims=True)) for i in heads]
        p = [jnp.exp(score(i) - jnp.tile(m_new[i], (1, tk // LANES))).astype(BF16) for i in heads]
        for i in heads:
            m_ref[i, live] = m_new[i]
            acc_ref[i, live] = (jnp.exp(m_old[i] - m_new[i]) * acc_ref[i, live]
                                + jnp.dot(p[i], vaug_ref[i, keys, :], preferred_element_type=F32))

    m_ref[...] = jnp.full(m_ref.shape, NEG_INF, F32)
    acc_ref[...] = jnp.zeros(acc_ref.shape, F32)
    produce(sa_ref, 0)

    def body(m, _):
        produce(sb_ref, 2 * m + 1)
        consume(sa_ref, 2 * m)
        produce(sa_ref, 2 * m + 2)
        consume(sb_ref, 2 * m + 1)
        return 0

    lax.fori_loop(0, qi, body, 0)
    produce(sb_ref, 2 * qi + 1, off=tk)
    consume(sa_ref, 2 * qi, 0)
    consume(sb_ref, 2 * qi + 1, tk, off=tk)

    accs = (acc_ref[0], acc_ref[1])
    num = jnp.where(upper, accs[1], accs[0])
    den = jnp.where(upper, accs[0], accs[1])
    den = pltpu.roll(den, HEAD_DIM, 1)
    o_ref[...] = (num / den).astype(o_ref.dtype)


def _fox_attn(qkvg, c, *, batch, seq):
    T = qkvg.shape[0]
    HD = N_HEADS * HEAD_DIM
    tq = _pick_tile(seq, 1024)
    tk = tq // 2
    nq = seq // tq
    npair = HD // LANES
    return pl.pallas_call(
        functools.partial(_fox_attn_kernel, tq=tq, tk=tk),
        grid=(batch, npair, nq),
        in_specs=[
            pl.BlockSpec((tq, LANES), lambda b, h, q: (b * nq + q, h)),
            pl.BlockSpec((seq, LANES), lambda b, h, q: (b, npair + h)),
            pl.BlockSpec((seq, LANES), lambda b, h, q: (b, 2 * npair + h)),
            pl.BlockSpec((None, 2, 1, seq), lambda b, h, q: (b, h, 0, 0)),
        ],
        out_specs=pl.BlockSpec((tq, LANES), lambda b, h, q: (b * nq + q, h)),
        out_shape=jax.ShapeDtypeStruct((T, HD), BF16),
        scratch_shapes=[pltpu.VMEM((2, seq, LANES), BF16),
                        pltpu.VMEM((2, tq, tk), F32), pltpu.VMEM((2, tq, tk), F32),
                        pltpu.VMEM((2, tq, LANES), F32), pltpu.VMEM((2, tq, LANES), F32)],
        compiler_params=_cparams("parallel", "arbitrary", "arbitrary"),
        name="fox_attn",
    )(qkvg, qkvg, qkvg, c.reshape(batch, N_HEADS, 1, seq))


def _fox_mixer(x, g, w_in, b_f, g_q, g_k, w_out, *, batch, seq):
    HD = N_HEADS * HEAD_DIM
    scale = HEAD_DIM ** -0.5
    w = jnp.concatenate([w_in[:, :3 * HD], w_in[:, 3 * HD + N_HEADS:]], axis=1).astype(BF16)
    wf_t = w_in[:, 3 * HD:3 * HD + N_HEADS].T
    head_gain = jnp.concatenate([jnp.tile(g_q * scale, N_HEADS), jnp.tile(g_k, N_HEADS)])
    qkvg = _proj(x, g, w, seq=seq, head_gain=head_gain)
    c = _fox_c(x, g, wf_t, b_f, batch=batch, seq=seq)
    o = _fox_attn(qkvg, c, batch=batch, seq=seq)
    return "sigmoid_gate", [(o, 0), (qkvg, 3)], w_out.astype(BF16)


def _sb_attn_kernel(q_ref, k_ref, v_ref, tri_ref, o_ref, aa_ref, ab_ref, rest_ref, acc_ref, *, tq, tk):
    qi = pl.program_id(2)
    lane = lax.broadcasted_iota(jnp.int32, (1, LANES), 1)
    upper = lane >= HEAD_DIM
    qslab = q_ref[...]
    zero = jnp.zeros_like(qslab)
    q2 = jnp.concatenate([jnp.where(upper, zero, qslab), jnp.where(upper, qslab, zero)], axis=0)
    rows = 2 * tq
    n_diag = tq // tk
    last = n_diag * (qi + 1) - 1

    def key_rows(k):
        return pl.ds(pl.multiple_of(jnp.maximum(last - k, 0) * tk, tk), tk)

    def head_rows(off):
        return (slice(off, tq), slice(tq + off, rows))

    def take(x, off):
        if off == 0:
            return x[...]
        lo, hi = head_rows(off)
        return jnp.concatenate([x[lo], x[hi]], axis=0)

    def put(ref, val, off, add=False):
        parts = [(slice(None), val)] if off == 0 else list(zip(head_rows(off), (val[:tq - off], val[tq - off:])))
        for sl, part in parts:
            if add:
                ref[sl] += part
            else:
                ref[sl] = part

    def scores(k, off=0):
        return lax.dot_general(take(q2, off), k_ref[key_rows(k), :], _NT, preferred_element_type=F32)

    def weights(z, a_ref, diag_offset, off=0):
        n2 = z.shape[0]
        sp = jnp.maximum(z, 0.0) + jnp.log2(1.0 + jnp.exp2(_neg_abs(z)))
        if diag_offset is not None:
            qpos = lax.broadcasted_iota(jnp.int32, (n2, tk), 0) % (tq - off) + off
            strict = lax.broadcasted_iota(jnp.int32, (n2, tk), 1) + diag_offset < qpos
            spm = jnp.where(strict, sp, 0.0)
        else:
            spm = sp
        cum = jnp.dot(spm.astype(BF16), tri_ref[...], preferred_element_type=F32)
        w = (z - sp) - cum
        if diag_offset is not None:
            w = jnp.where(strict, w, NEG_INF)
        rest = take(rest_ref, off)
        put(a_ref, jnp.exp2(w - jnp.tile(rest, (1, tk // LANES))).astype(BF16), off)
        put(rest_ref, rest + jnp.broadcast_to(cum[:, 0:1] + spm[:, 0:1], (n2, LANES)), off)

    def accumulate(a_ref, k, off=0):
        put(acc_ref, jnp.dot(take(a_ref, off), v_ref[key_rows(k), :], preferred_element_type=F32), off, add=True)

    assert n_diag % 2 == 0
    even, odd = aa_ref, ab_ref
    offset = lambda k: (n_diag - 1 - k) * tk

    def produce_diag(buf, k):
        weights(scores(k, offset(k)), buf, offset(k), offset(k))

    def step(k, cur, nxt, diag=False):
        off_cur, off_nxt = (offset(k), offset(k + 1)) if diag else (0, 0)
        z = scores(k + 1, off_nxt)
        accumulate(cur, k, off_cur)
        weights(z, nxt, off_nxt if diag else None, off_nxt)

    rest_ref[...] = jnp.zeros(rest_ref.shape, F32)
    acc_ref[...] = jnp.zeros(acc_ref.shape, F32)
    produce_diag(even, 0)
    produce_diag(odd, 1)
    accumulate(aa_ref, 0, offset(0))
    for m in range(1, n_diag // 2):
        step(2 * m - 1, odd, even, diag=True)
        step(2 * m, even, odd, diag=True)

    def body(m, _):
        step(2 * m - 1, odd, even)
        step(2 * m, even, odd)
        return 0

    lax.fori_loop(n_diag // 2, (n_diag // 2) * (qi + 1), body, 0)
    accumulate(ab_ref, last)
    o_ref[...] = jnp.where(upper, acc_ref[tq:2 * tq], acc_ref[0:tq]).astype(o_ref.dtype)


def _sb_attn(qkv, *, batch, seq):
    T = qkv.shape[0]
    HD = N_HEADS * HEAD_DIM
    tq = _pick_tile(seq, 1024)
    tk = 256
    nq = seq // tq
    npair = HD // LANES
    tri = jnp.asarray(np.tril(np.ones((tk, tk), np.float32), -1), BF16)
    return pl.pallas_call(
        functools.partial(_sb_attn_kernel, tq=tq, tk=tk),
        grid=(batch, npair, nq),
        in_specs=[
            pl.BlockSpec((tq, LANES), lambda b, h, q: (b * nq + q, h)),
            pl.BlockSpec((seq, LANES), lambda b, h, q: (b, npair + h)),
            pl.BlockSpec((seq, LANES), lambda b, h, q: (b, 2 * npair + h)),
            pl.BlockSpec((tk, tk), lambda b, h, q: (0, 0)),
        ],
        out_specs=pl.BlockSpec((tq, LANES), lambda b, h, q: (b * nq + q, h)),
        out_shape=jax.ShapeDtypeStruct((T, HD), BF16),
        scratch_shapes=[pltpu.VMEM((2 * tq, tk), BF16)] * 2 + [pltpu.VMEM((2 * tq, LANES), F32)] * 2,
        compiler_params=_cparams("parallel", "parallel", "arbitrary"),
        name="sb_attn",
    )(qkv, qkv, qkv, tri)


def _sb_mixer(x, g, w_in, w_out, *, batch, seq):
    HD = N_HEADS * HEAD_DIM
    scale = HEAD_DIM ** -0.5 * float(np.log2(np.e))
    w = jnp.concatenate([w_in[:, :HD] * scale, w_in[:, HD:]], axis=1).astype(BF16)
    qkv = _proj(x, g, w, seq=seq)
    o = _sb_attn(qkv, batch=batch, seq=seq)
    return "plain", [(o, 0)], w_out.astype(BF16)


def _gla_kernel(q_ref, k_ref, v_ref, r_ref, glow_ref, wgu_ref, bg_ref, gout_ref, tri_ref,
                o_ref, state_ref, *, tm, dk):
    C = GLA_CHUNK

    @pl.when(pl.program_id(2) == 0)
    def _():
        state_ref[...] = jnp.zeros_like(state_ref)

    gate = jnp.dot(glow_ref[...], wgu_ref[...], precision=lax.Precision.HIGHEST,
                   preferred_element_type=F32) + bg_ref[...]
    log_a = (jnp.minimum(gate, 0.0) - jnp.log(1.0 + jnp.exp(-jnp.abs(gate)))) * (1.0 / GLA_TAU)
    row = lax.broadcasted_iota(jnp.int32, (C, C), 0)
    col = lax.broadcasted_iota(jnp.int32, (C, C), 1)
    causal = col <= row
    q_scale = dk ** -0.5
    chunks = [slice(c * C, (c + 1) * C) for c in range(tm // C)]

    tri = tri_ref[...]
    b = [jnp.dot(tri, log_a[sl], precision=lax.Precision.HIGHEST, preferred_element_type=F32)
         for sl in chunks]
    b_last = jnp.concatenate([jnp.broadcast_to(bc[C - 1:C], bc.shape) for bc in b], axis=0)
    b = jnp.concatenate(b, axis=0)
    q = q_ref[...].astype(F32) * q_scale
    k = k_ref[...].astype(F32)
    q_dec = (q * jnp.exp(b)).astype(BF16)
    k_neg = (k * jnp.exp(-b)).astype(BF16)
    k_rem = (k * jnp.exp(b_last - b)).astype(BF16)
    decay = jnp.exp(b_last)
    attn = [lax.dot_general(q_dec[sl], k_neg[sl], _NT, preferred_element_type=F32) for sl in chunks]
    attn = [jnp.where(causal, a, 0.0).astype(BF16) for a in attn]
    o_intra = [jnp.dot(attn[c], v_ref[sl, :], preferred_element_type=F32) for c, sl in enumerate(chunks)]
    u_t = [lax.dot_general(v_ref[sl, :], k_rem[sl], _TN, preferred_element_type=F32) for sl in chunks]

    state_t = state_ref[...]
    outs = []
    for c, sl in enumerate(chunks):
        outs.append(o_intra[c] + lax.dot_general(q_dec[sl], state_t.astype(BF16), _NT,
                                                 preferred_element_type=F32))
        state_t = state_t * decay[c * C:c * C + 1] + u_t[c]
    state_ref[...] = state_t

    o = jnp.concatenate(outs, axis=0)
    y = o * lax.rsqrt(jnp.mean(o * o, axis=-1, keepdims=True) + RMS_EPS) * gout_ref[...]
    r = r_ref[...].astype(F32)
    o_ref[...] = (y * (r * _sigmoid(r))).astype(o_ref.dtype)


def _gla(qkvr, glow, wgu, b_gate, g_out, *, batch, seq):
    T = qkvr.shape[0]
    Hg = GLA_HEADS
    dk_total = wgu.shape[1]
    dk = dk_total // Hg
    dv = (qkvr.shape[1] - 2 * dk_total) // 2 // Hg
    tm = _pick_tile(seq, 1024)
    ns = seq // tm
    kq, kv = dk_total // dk, dk_total // dv
    tri = jnp.asarray(np.tril(np.ones((GLA_CHUNK, GLA_CHUNK), np.float32)))
    rowmap = lambda off: (lambda b, h, s: (b * ns + s, off + h))
    return pl.pallas_call(
        functools.partial(_gla_kernel, tm=tm, dk=dk),
        grid=(batch, Hg, ns),
        in_specs=[
            pl.BlockSpec((tm, dk), rowmap(0)),
            pl.BlockSpec((tm, dk), rowmap(kq)),
            pl.BlockSpec((tm, dv), rowmap(2 * kv)),
            pl.BlockSpec((tm, dv), rowmap(2 * kv + Hg)),
            pl.BlockSpec((tm, LANES), lambda b, h, s: (b * ns + s, 0)),
            pl.BlockSpec((LANES, dk), lambda b, h, s: (0, h)),
            pl.BlockSpec((1, dk), lambda b, h, s: (0, h)),
            pl.BlockSpec((1, dv), lambda b, h, s: (0, 0)),
            pl.BlockSpec((GLA_CHUNK, GLA_CHUNK), lambda b, h, s: (0, 0)),
        ],
        out_specs=pl.BlockSpec((tm, dv), rowmap(0)),
        out_shape=jax.ShapeDtypeStruct((T, Hg * dv), BF16),
        scratch_shapes=[pltpu.VMEM((dv, dk), F32)],
        compiler_params=_cparams("parallel", "parallel", "arbitrary"),
        name="gla",
    )(qkvr, qkvr, qkvr, qkvr, glow, wgu, b_gate.reshape(1, dk_total), g_out.reshape(1, dv), tri)


def _gla_mixer(x, g, w_in, w_gate_up, b_gate, g_out, w_out, *, batch, seq):
    dk_total = w_gate_up.shape[1]
    dv_total = w_out.shape[0]
    lo = 2 * dk_total + dv_total
    w_main = jnp.concatenate([w_in[:, :lo], w_in[:, lo + GLA_GATE_RANK:]], axis=1).astype(BF16)
    w_low = jnp.pad(w_in[:, lo:lo + GLA_GATE_RANK], ((0, 0), (0, LANES - GLA_GATE_RANK))).astype(BF16)
    wgu = jnp.pad(w_gate_up, ((0, LANES - GLA_GATE_RANK), (0, 0)))
    qkvr, glow = _proj(x, g, w_main, seq=seq, w_aux=w_low)
    o = _gla(qkvr, glow, wgu, b_gate, g_out, batch=batch, seq=seq)
    return "plain", [(o, 0)], w_out.astype(BF16)


def _nsa_compress_kernel(kc_ref, vc_ref, pk_ref, pv_ref, wk_ref, wv_ref, gk_ref, ko_ref, vo_ref, *, n16):
    def compress(r_ref, pos_ref, w_ref):
        r = r_ref[...].astype(F32)
        first = jnp.dot((r + pos_ref[0:1, :]).astype(BF16), w_ref[0], preferred_element_type=F32)
        second = jnp.dot((r + pos_ref[1:2, :]).astype(BF16), w_ref[1], preferred_element_type=F32)
        return first + pltpu.roll(second, n16 - 1, 0)

    kc = compress(kc_ref, pk_ref, wk_ref)
    ko_ref[...] = _rms_rows(kc, gk_ref[...]).astype(ko_ref.dtype)
    vo_ref[...] = compress(vc_ref, pv_ref, wv_ref).astype(vo_ref.dtype)


def _nsa_compress(kc_r, vc_r, pos_k, pos_v, w_k, w_v, g_k):
    B, G, n16, width = kc_r.shape
    dh = HEAD_DIM
    half = NSA_CMP_LEN // 2
    dup = lambda w: jnp.concatenate([w, w], axis=-1).reshape(2, half * dh, 2 * dh).astype(BF16)
    spec_r = pl.BlockSpec((None, None, n16, width), lambda b, g: (b, g, 0, 0))
    spec_o = pl.BlockSpec((None, None, n16, 2 * dh), lambda b, g: (b, g, 0, 0))
    const2 = lambda shape: pl.BlockSpec(shape, lambda b, g: (0,) * len(shape))
    out = jax.ShapeDtypeStruct((B, G, n16, 2 * dh), BF16)
    return pl.pallas_call(
        functools.partial(_nsa_compress_kernel, n16=n16),
        grid=(B, G),
        in_specs=[spec_r, spec_r, const2((2, width)), const2((2, width)),
                  const2((2, width, 2 * dh)), const2((2, width, 2 * dh)), const2((1, 2 * dh))],
        out_specs=[spec_o, spec_o],
        out_shape=[out, out],
        compiler_params=_cparams("parallel", "parallel"),
        name="nsa_compress",
    )(kc_r, vc_r, pos_k.reshape(2, width), pos_v.reshape(2, width), dup(w_k), dup(w_v),
      jnp.tile(g_k, 2).reshape(1, 2 * dh))


def _stack_heads(q_ref, extra=None, rows_in=slice(None)):
    lane = lax.broadcasted_iota(jnp.int32, (1, LANES), 1)
    upper = lane >= HEAD_DIM
    rows = []
    for j in range(2):
        slab = q_ref[rows_in, j * LANES:(j + 1) * LANES]
        zero = jnp.zeros_like(slab)
        for i in range(2):
            qh = jnp.where(upper, slab, zero) if i else jnp.where(upper, zero, slab)
            rows.append(qh if extra is None else jnp.concatenate([qh, extra], axis=1))
    return jnp.concatenate(rows, axis=0)


def _unstack_heads(o, tq):
    lane = lax.broadcasted_iota(jnp.int32, (1, LANES), 1)
    upper = lane >= HEAD_DIM
    slabs = []
    for j in range(2):
        lo = o[(2 * j) * tq:(2 * j + 1) * tq]
        hi = pltpu.roll(o[(2 * j + 1) * tq:(2 * j + 2) * tq], HEAD_DIM, 1)
        slabs.append(jnp.where(upper, hi, lo))
    return jnp.concatenate(slabs, axis=1)


def _group_lower(slab_ref, g):
    x = slab_ref[...].astype(F32)
    return jnp.where(g % 2 == 1, pltpu.roll(x, HEAD_DIM, 1), x)


def _both_halves(x):
    lane = lax.broadcasted_iota(jnp.int32, (1, LANES), 1)
    return jnp.where(lane >= HEAD_DIM, pltpu.roll(x, HEAD_DIM, 1), x)


def _with_ones(x):
    lane = lax.broadcasted_iota(jnp.int32, (1, LANES), 1)
    return jnp.where(lane >= HEAD_DIM, 1.0, x)


def _normalise(acc):
    return acc / pltpu.roll(acc, HEAD_DIM, 1)


def _nsa_cmp_attn_kernel(q_ref, kc_ref, vc_ref, cover_ref, oc_ref, sel_ref, *, tq, n16, parts):
    qi = pl.program_id(2)
    tp = tq // parts
    for part in range(parts):
        rows = slice(part * tp, (part + 1) * tp)
        row0 = qi * tq + part * tp
        q4 = _stack_heads(q_ref, rows_in=rows)
        s = lax.dot_general(q4, kc_ref[...], _NT, preferred_element_type=F32)
        qpos = row0 + lax.broadcasted_iota(jnp.int32, (tp, n16), 0)
        cmp_end = lax.broadcasted_iota(jnp.int32, (tp, n16), 1) * NSA_CMP_STRIDE + (NSA_CMP_LEN - 1)
        valid = jnp.concatenate([cmp_end <= qpos] * 4, axis=0)
        s = jnp.where(valid, s, NEG_INF)
        e = jnp.exp(s - jnp.max(s, axis=1, keepdims=True))
        p = jnp.where(valid, e / jnp.sum(e, axis=1, keepdims=True), 0.0)
        o = jnp.dot(p.astype(BF16), vc_ref[...], preferred_element_type=F32)
        oc_ref[rows, :] = _unstack_heads(o, tp).astype(oc_ref.dtype)

        psum = p[0:tp] + p[tp:2 * tp] + p[2 * tp:3 * tp] + p[3 * tp:4 * tp]
        cover = cover_ref[...]
        hi = psum.astype(BF16)
        r1 = psum - hi.astype(F32)
        mid = r1.astype(BF16)
        lo = (r1 - mid.astype(F32)).astype(BF16)
        imp = (jnp.dot(hi, cover, preferred_element_type=F32) + jnp.dot(mid, cover, preferred_element_type=F32)
               + jnp.dot(lo, cover, preferred_element_type=F32))

        blk = lax.broadcasted_iota(jnp.int32, (tp, LANES), 1)
        cur = (row0 + lax.broadcasted_iota(jnp.int32, (tp, LANES), 0)) // NSA_SEL_LEN
        forced = (blk == 0) | (blk == cur) | (blk == cur - 1)
        vals = jnp.where(forced, NSA_FORCED_SCORE, jnp.where(blk <= cur, imp, -1.0))
        taken = -3e38
        for _ in range(NSA_N_SEL):
            first = jnp.argmax(vals, axis=1, keepdims=True)
            vals = jnp.where(blk == first, taken, vals)
        sel_ref[rows, :] = jnp.where(vals == taken, 0.0, NEG_INF).astype(sel_ref.dtype)


def _nsa_cmp_attn(qk, kcmp, vcmp, *, batch, seq):
    T = qk.shape[0]
    G = NSA_KV_HEADS
    tq = _pick_tile(seq, 1024)
    nq = seq // tq
    n16 = kcmp.shape[2]
    n_sel = seq // NSA_SEL_LEN
    assert n_sel <= LANES and seq % tq == 0
    n = np.arange(n16)[:, None] * NSA_CMP_STRIDE
    j = np.arange(LANES)[None, :] * NSA_SEL_LEN
    cover = (n < j + NSA_SEL_LEN) & (n + NSA_CMP_LEN > j) & (np.arange(LANES)[None, :] < n_sel)
    cover = jnp.asarray(cover.astype(np.float32), BF16)
    spec_c = pl.BlockSpec((None, None, n16, LANES), lambda b, g, q: (b, g, 0, 0))
    return pl.pallas_call(
        functools.partial(_nsa_cmp_attn_kernel, tq=tq, n16=n16, parts=2),
        grid=(batch, G, nq),
        in_specs=[pl.BlockSpec((tq, 2 * LANES), lambda b, g, q: (b * nq + q, g)), spec_c, spec_c,
                  pl.BlockSpec((n16, LANES), lambda b, g, q: (0, 0))],
        out_specs=[pl.BlockSpec((tq, 2 * LANES), lambda b, g, q: (b * nq + q, g)),
                   pl.BlockSpec((tq, LANES), lambda b, g, q: (b * nq + q, g))],
        out_shape=[jax.ShapeDtypeStruct((T, N_HEADS * HEAD_DIM), BF16),
                   jax.ShapeDtypeStruct((T, G * LANES), BF16)],
        compiler_params=_cparams("parallel", "parallel", "parallel"),
        name="nsa_cmp_attn",
    )(qk, kcmp, vcmp, cover)


def _nsa_sel_attn_kernel(q_ref, k_ref, v_ref, sel_ref, o_ref, kaug_ref, vaug_ref, sa_ref, sb_ref, m_ref, acc_ref,
                         *, tq, sub, tk, seq):
    g = pl.program_id(1)
    qi = pl.program_id(2)

    @pl.when(qi == 0)
    def _():
        kaug_ref[:, 0:LANES] = _both_halves(_group_lower(k_ref, g)).astype(BF16)
        blk_of_key = lax.broadcasted_iota(jnp.int32, (seq, LANES), 0) // NSA_SEL_LEN
        blk = lax.broadcasted_iota(jnp.int32, (seq, LANES), 1)
        kaug_ref[:, LANES:2 * LANES] = jnp.where(blk_of_key == blk, 1.0, 0.0).astype(BF16)
        vaug_ref[...] = _with_ones(_group_lower(v_ref, g)).astype(BF16)

    chains = range(tq // sub)
    rows = [slice(c * sub, (c + 1) * sub) for c in chains]
    q4 = [_stack_heads(q_ref, extra=sel_ref[rows[c], :], rows_in=rows[c]) for c in chains]
    n_diag = tq // tk
    assert n_diag == 2

    def head_rows(off):
        return [slice(h * sub + off, (h + 1) * sub) for h in range(4)]

    def take(x, off):
        return x[...] if off == 0 else jnp.concatenate([x[sl] for sl in head_rows(off)], axis=0)

    def put(ref, val, off):
        if off == 0:
            ref[...] = val
        else:
            for h, sl in enumerate(head_rows(off)):
                ref[sl] = val[h * (sub - off):(h + 1) * (sub - off)]

    def produce(s_ref, j, off=0):
        kt = kaug_ref[pl.ds(pl.multiple_of(j * tk, tk), tk), :]
        for c in chains:
            s_ref[c, 0:4 * (sub - off)] = lax.dot_general(take(q4[c], off), kt, _NT,
                                                          preferred_element_type=F32)

    def consume(s_ref, j, diag_offset=None, off=0):
        keys = pl.ds(pl.multiple_of(j * tk, tk), tk)
        live = slice(0, 4 * (sub - off))
        if diag_offset is not None:
            kpos = diag_offset + lax.broadcasted_iota(jnp.int32, (sub - off, tk), 1)
            qpos = off + lax.broadcasted_iota(jnp.int32, (sub - off, tk), 0)
            ok = [jnp.concatenate([kpos <= qpos + c * sub] * 4, axis=0) for c in chains]
            score = lambda c: jnp.where(ok[c], s_ref[c, live], NEG_INF)
        else:
            score = lambda c: s_ref[c, live]
        vt = vaug_ref[keys, :]
        m_old = [take(m_ref.at[c], off) for c in chains]
        m_new = [jnp.maximum(m_old[c], jnp.max(score(c), axis=1, keepdims=True)) for c in chains]
        p = [jnp.exp(score(c) - jnp.tile(m_new[c], (1, tk // LANES))).astype(BF16) for c in chains]
        for c in chains:
            put(m_ref.at[c], m_new[c], off)
            put(acc_ref.at[c], (jnp.exp(m_old[c] - m_new[c]) * take(acc_ref.at[c], off)
                                + jnp.dot(p[c], vt, preferred_element_type=F32)), off)

    m_ref[...] = jnp.full(m_ref.shape, NEG_INF, F32)
    acc_ref[...] = jnp.zeros(acc_ref.shape, F32)
    produce(sa_ref, 0)

    def body(m, _):
        produce(sb_ref, 2 * m + 1)
        consume(sa_ref, 2 * m)
        produce(sa_ref, 2 * m + 2)
        consume(sb_ref, 2 * m + 1)
        return 0

    lax.fori_loop(0, qi, body, 0)
    produce(sb_ref, 2 * qi + 1, off=tk)
    consume(sa_ref, 2 * qi, 0)
    consume(sb_ref, 2 * qi + 1, tk, off=tk)

    for c in chains:
        o_ref[rows[c], :] = _unstack_heads(_normalise(acc_ref[c]), sub).astype(o_ref.dtype)


def _nsa_sel_attn(qk, vv, selbias, *, batch, seq, k_col, v_col):
    T = qk.shape[0]
    G = NSA_KV_HEADS
    tq = _pick_tile(seq, 1024)
    sub = tq
    tk = tq // 2
    nq = seq // tq
    return pl.pallas_call(
        functools.partial(_nsa_sel_attn_kernel, tq=tq, sub=sub, tk=tk, seq=seq),
        grid=(batch, G, nq),
        in_specs=[pl.BlockSpec((tq, 2 * LANES), lambda b, g, q: (b * nq + q, g)),
                  pl.BlockSpec((seq, LANES), lambda b, g, q: (b, k_col + g // 2)),
                  pl.BlockSpec((seq, LANES), lambda b, g, q: (b, v_col + g // 2)),
                  pl.BlockSpec((tq, LANES), lambda b, g, q: (b * nq + q, g))],
        out_specs=pl.BlockSpec((tq, 2 * LANES), lambda b, g, q: (b * nq + q, g)),
        out_shape=jax.ShapeDtypeStruct((T, N_HEADS * HEAD_DIM), BF16),
        scratch_shapes=[pltpu.VMEM((seq, 2 * LANES), BF16), pltpu.VMEM((seq, LANES), BF16),
                        pltpu.VMEM((tq // sub, 4 * sub, tk), F32), pltpu.VMEM((tq // sub, 4 * sub, tk), F32),
                        pltpu.VMEM((tq // sub, 4 * sub, LANES), F32),
                        pltpu.VMEM((tq // sub, 4 * sub, LANES), F32)],
        compiler_params=_cparams("parallel", "arbitrary", "arbitrary"),
        name="nsa_sel_attn",
    )(qk, qk, vv, selbias)


def _nsa_win_attn_kernel(q_ref, k_ref, v_ref, o_ref, kb_ref, vaug_ref, *, tq, tp, span):
    g = pl.program_id(1)
    qi = pl.program_id(2)

    @pl.when(qi == 0)
    def _():
        kb_ref[...] = _both_halves(_group_lower(k_ref, g)).astype(BF16)
        vaug_ref[...] = _with_ones(_group_lower(v_ref, g)).astype(BF16)

    for part in range(tq // tp):
        rows = slice(part * tp, (part + 1) * tp)
        q4 = _stack_heads(q_ref, rows_in=rows)
        q0 = qi * tq + part * tp
        k_start = pl.multiple_of(jnp.maximum(q0 - NSA_WINDOW, 0), tp)
        keys = pl.ds(k_start, span)
        dist = (q0 - k_start) + (lax.broadcasted_iota(jnp.int32, (tp, span), 0)
                                 - lax.broadcasted_iota(jnp.int32, (tp, span), 1))
        ok = (dist >= 0) & (dist < NSA_WINDOW)
        ok2 = jnp.concatenate([ok, ok], axis=0)
        pairs = [q4[0:2 * tp], q4[2 * tp:4 * tp]]
        s = [lax.dot_general(qp, kb_ref[keys, :], _NT, preferred_element_type=F32) for qp in pairs]
        s = [jnp.where(ok2, sc, NEG_INF) for sc in s]
        p = [jnp.exp(sc - jnp.max(sc, axis=1, keepdims=True)).astype(BF16) for sc in s]
        acc = [jnp.dot(pc, vaug_ref[keys, :], preferred_element_type=F32) for pc in p]
        acc = jnp.concatenate(acc, axis=0)
        o_ref[rows, :] = _unstack_heads(_normalise(acc), tp).astype(o_ref.dtype)


def _nsa_win_attn(qk, vv, *, batch, seq, k_col, v_col):
    T = qk.shape[0]
    G = NSA_KV_HEADS
    tp = _pick_tile(seq, 256)
    tq = _pick_tile(seq, 4 * tp)
    nq = seq // tq
    span = NSA_WINDOW + tp
    assert NSA_WINDOW % tp == 0 and seq >= span
    return pl.pallas_call(
        functools.partial(_nsa_win_attn_kernel, tq=tq, tp=tp, span=span),
        grid=(batch, G, nq),
        in_specs=[pl.BlockSpec((tq, 2 * LANES), lambda b, g, q: (b * nq + q, g)),
                  pl.BlockSpec((seq, LANES), lambda b, g, q: (b, k_col + g // 2)),
                  pl.BlockSpec((seq, LANES), lambda b, g, q: (b, v_col + g // 2))],
        out_specs=pl.BlockSpec((tq, 2 * LANES), lambda b, g, q: (b * nq + q, g)),
        out_shape=jax.ShapeDtypeStruct((T, N_HEADS * HEAD_DIM), BF16),
        scratch_shapes=[pltpu.VMEM((seq, LANES), BF16), pltpu.VMEM((seq, LANES), BF16)],
        compiler_params=_cparams("parallel", "arbitrary", "arbitrary"),
        name="nsa_win_attn",
    )(qk, qk, vv)


def _nsa_mixer(x, g, w_in, pos_k, pos_v, w_cmp_k, w_cmp_v, g_q, g_k, w_out, *, batch, seq):
    HD = N_HEADS * HEAD_DIM
    G, dh = NSA_KV_HEADS, HEAD_DIM
    kvw = G * dh
    scale = dh ** -0.5
    col = lambda n: slice(HD + n * kvw, HD + (n + 1) * kvw)
    order = [slice(0, HD), col(0), col(2), col(4), col(1), col(3), col(5)]
    w = jnp.concatenate([w_in[:, c] for c in order], axis=1).astype(BF16)
    n_gate = 3 * N_HEADS
    w_gate = jnp.pad(w_in[:, HD + 6 * kvw:], ((0, 0), (0, LANES - n_gate))).astype(BF16)
    head_gain = jnp.concatenate([jnp.tile(g_q * scale, N_HEADS), jnp.tile(g_k, 3 * G)])

    inv = ROPE_THETA ** (-jnp.arange(0, dh, 2, dtype=F32) / dh)
    ang = jnp.arange(seq, dtype=F32)[:, None] * inv[None, :]
    cos, sin = jnp.cos(ang), jnp.sin(ang)
    cos_t = jnp.tile(cos, (1, PROJ_CHUNK // (dh // 2)))
    sin_t = jnp.tile(jnp.concatenate([-sin, sin], axis=1), (1, PROJ_CHUNK // dh))

    qkv, gate = _proj(x, g, w, seq=seq, head_gain=head_gain, rope=(cos_t, sin_t), w_aux=w_gate)

    def blocks16(t):
        half = NSA_CMP_LEN // 2
        t = t.reshape(batch, seq // half, half, G, dh).transpose(0, 3, 1, 2, 4)
        return t.reshape(batch, G, seq // half, half * dh)

    kc0, vc0 = HD, HD + 3 * kvw
    kcmp, vcmp = _nsa_compress(blocks16(qkv[:, kc0:kc0 + kvw]), blocks16(qkv[:, vc0:vc0 + kvw]),
                               pos_k, pos_v, w_cmp_k, w_cmp_v, g_k)
    o_c, selbias = _nsa_cmp_attn(qkv, kcmp, vcmp, batch=batch, seq=seq)
    o_s = _nsa_sel_attn(qkv, qkv, selbias, batch=batch, seq=seq,
                        k_col=(HD + kvw) // LANES, v_col=(HD + 4 * kvw) // LANES)
    o_w = _nsa_win_attn(qkv, qkv, batch=batch, seq=seq,
                        k_col=(HD + 2 * kvw) // LANES, v_col=(HD + 5 * kvw) // LANES)
    r = np.arange(LANES)[:, None]
    c = np.arange(HD)[None, :] // dh
    expand = np.stack([(r == 3 * c + b) for b in range(3)]).astype(np.float32)
    return _outproj_gated3(x, w_out.astype(BF16), (o_c, o_s, o_w), gate, jnp.asarray(expand, BF16))


def kernel(x, norm_g, ffn1_w_gate, ffn1_w_up, ffn1_w_down, ffn2_w_gate, ffn2_w_up, ffn2_w_down, fox_w_in, fox_b_f, fox_g_q, fox_g_k, fox_w_out, nsa_w_in, nsa_cmp_pos_k, nsa_cmp_pos_v, nsa_w_cmp_k, nsa_w_cmp_v, nsa_g_q, nsa_g_k, nsa_w_out, gla_w_in, gla_w_gate_up, gla_b_gate, gla_g_out, gla_w_out, sb_w_in, sb_w_out):
    B, S, D = x.shape
    depth = norm_g.shape[0]
    n_mixers = 4
    x2 = x.reshape(B * S, D)
    for i in range(depth):
        m, j = i % n_mixers, i // n_mixers
        x2 = _ffn(x2, norm_g[i, 0], ffn1_w_gate[i].astype(BF16), ffn1_w_up[i].astype(BF16),
                  ffn1_w_down[i].astype(BF16))
        g = norm_g[i, 1]
        pending = None
        if m == 0:
            pending = _fox_mixer(x2, g, fox_w_in[j], fox_b_f[j], fox_g_q[j], fox_g_k[j], fox_w_out[j],
                                 batch=B, seq=S)
        elif m == 1:
            x2 = _nsa_mixer(x2, g, nsa_w_in[j], nsa_cmp_pos_k[j], nsa_cmp_pos_v[j], nsa_w_cmp_k[j],
                            nsa_w_cmp_v[j], nsa_g_q[j], nsa_g_k[j], nsa_w_out[j], batch=B, seq=S)
        elif m == 2:
            pending = _gla_mixer(x2, g, gla_w_in[j], gla_w_gate_up[j], gla_b_gate[j], gla_g_out[j],
                                 gla_w_out[j], batch=B, seq=S)
        else:
            pending = _sb_mixer(x2, g, sb_w_in[j], sb_w_out[j], batch=B, seq=S)
        x2 = _ffn(x2, norm_g[i, 2], ffn2_w_gate[i].astype(BF16), ffn2_w_up[i].astype(BF16),
                  ffn2_w_down[i].astype(BF16), pre=pending)
    return x2.reshape(B, S, D)
```

```python
import functools

import numpy as np
import jax
import jax.numpy as jnp
from jax import lax
from jax.experimental import pallas as pl
from jax.experimental.pallas import tpu as pltpu

F32 = jnp.float32
BF16 = jnp.bfloat16

N_HEADS = 16
HEAD_DIM = 64
ROPE_THETA = 10000.0
RMS_EPS = 1e-6
NEG_INF = -1e30
NSA_KV_HEADS = 4
NSA_CMP_LEN = 32
NSA_CMP_STRIDE = 16
NSA_SEL_LEN = 64
NSA_N_SEL = 16
NSA_WINDOW = 512
GLA_HEADS = 4
GLA_GATE_RANK = 16
GLA_TAU = 16.0
GLA_CHUNK = 64

LANES = 128
MXU_DIM = 256
V7X_VMEM_BYTES = 64 * 1024 * 1024
VMEM_LIMIT = V7X_VMEM_BYTES - 8 * 1024 * 1024

TOKEN_TILE = 1024
SMALL_TOKEN_TILE = 512
ATTN_Q_TILE = 1024
SB_KEY_TILE = MXU_DIM
WIN_Q_GROUP = 256
WIN_GROUPS_PER_STEP = 4
CMP_ROW_GROUPS = 2

_NT = (((1,), (1,)), ((), ()))
_TN = (((0,), (0,)), ((), ()))


def _cparams(*sem):
    return pltpu.CompilerParams(dimension_semantics=sem, vmem_limit_bytes=VMEM_LIMIT)


def _rms_rows(x, g):
    return x * lax.rsqrt(jnp.mean(x * x, axis=-1, keepdims=True) + RMS_EPS) * g


def _neg_abs(z):
    bits = lax.bitcast_convert_type(z, jnp.int32) | jnp.int32(-2 ** 31)
    return lax.bitcast_convert_type(bits, F32)


def _sigmoid(z):
    return 1.0 / (1.0 + jnp.exp(-z))


def _pick_tile(n, target):
    t = min(n, target)
    while n % t:
        t //= 2
    return t


def _ffn_kernel(*refs, tf, pre_mode):
    if pre_mode is None:
        x_ref, g_ref, wg_ref, wu_ref, wd_ref, o_ref, a_ref = refs
        x = x_ref[...]
    else:
        n_act = {"plain": 1, "sigmoid_gate": 2}[pre_mode]
        x_ref, *act_refs = refs[:1 + n_act]
        wout_ref, g_ref, wg_ref, wu_ref, wd_ref, o_ref, a_ref = refs[1 + n_act:]
        act = act_refs[0][...]
        if pre_mode == "sigmoid_gate":
            act = (act.astype(F32) * _sigmoid(act_refs[1][...].astype(F32))).astype(BF16)
        x = x_ref[...] + jnp.dot(act, wout_ref[...], preferred_element_type=F32)
    h = _rms_rows(x, g_ref[...]).astype(BF16)
    for c in range(a_ref.shape[1] // tf):
        cols = slice(c * tf, (c + 1) * tf)
        gate = jnp.dot(h, wg_ref[:, cols], preferred_element_type=F32)
        up = jnp.dot(h, wu_ref[:, cols], preferred_element_type=F32)
        a_ref[:, cols] = (gate * _sigmoid(gate) * up).astype(BF16)
    o_ref[...] = x + 0.5 * jnp.dot(a_ref[...], wd_ref[...], preferred_element_type=F32)


def _resident(shape):
    return pl.BlockSpec(shape, lambda *_: (0,) * len(shape), pipeline_mode=pl.Buffered(1))


def _ffn(x, g, wg, wu, wd, pre=None):
    T, D = x.shape
    F = wg.shape[1]
    tm = _pick_tile(T, TOKEN_TILE)
    tf = MXU_DIM if F % MXU_DIM == 0 else F
    args = [x]
    in_specs = [pl.BlockSpec((tm, D), lambda i: (i, 0))]
    pre_mode = None
    if pre is not None:
        pre_mode, acts, w_out = pre
        for arr, col_block in acts:
            args.append(arr)
            in_specs.append(pl.BlockSpec((tm, w_out.shape[0]), functools.partial(lambda i, c: (i, c), c=col_block)))
        args.append(w_out)
        in_specs.append(_resident(w_out.shape))
    args += [g.reshape(1, D), wg, wu, wd]
    in_specs += [_resident((1, D)), _resident((D, F)), _resident((D, F)), _resident((F, D))]
    return pl.pallas_call(
        functools.partial(_ffn_kernel, tf=tf, pre_mode=pre_mode),
        grid=(T // tm,),
        in_specs=in_specs,
        out_specs=pl.BlockSpec((tm, D), lambda i: (i, 0)),
        out_shape=jax.ShapeDtypeStruct((T, D), F32),
        scratch_shapes=[pltpu.VMEM((tm, F), BF16)],
        compiler_params=_cparams("parallel"),
        name="ffn" if pre is None else "outproj_ffn",
    )(*args)


PROJ_CHUNK = MXU_DIM


def _proj_kernel(*refs, n_norm, rope, has_aux):
    it = iter(refs)
    x_ref, g_ref, w_ref = next(it), next(it), next(it)
    hg_ref, gsum_ref = (next(it), next(it)) if n_norm else (None, None)
    cos_ref, sin_ref = (next(it), next(it)) if rope else (None, None)
    waux_ref = next(it) if has_aux else None
    o_ref = next(it)
    ch = PROJ_CHUNK
    h = _rms_rows(x_ref[...], g_ref[...]).astype(BF16)
    for c in range(o_ref.shape[1] // ch):
        cols = slice(c * ch, (c + 1) * ch)
        y = jnp.dot(h, w_ref[:, cols], preferred_element_type=F32)
        if c * ch < n_norm:
            ss = jnp.dot((y * y).astype(BF16), gsum_ref[...], preferred_element_type=F32)
            y = y * lax.rsqrt(ss * (1.0 / HEAD_DIM) + RMS_EPS) * hg_ref[:, cols]
            if rope:
                lane = lax.broadcasted_iota(jnp.int32, (1, ch), 1)
                first_half = (lane % HEAD_DIM) < (HEAD_DIM // 2)
                partner = jnp.where(first_half,
                                    pltpu.roll(y, ch - HEAD_DIM // 2, 1),
                                    pltpu.roll(y, HEAD_DIM // 2, 1))
                y = y * cos_ref[...] + partner * sin_ref[...]
        o_ref[:, cols] = y.astype(o_ref.dtype)
    if has_aux:
        next(it)[...] = jnp.dot(h, waux_ref[...], preferred_element_type=F32)


def _proj(x, g, w, *, seq, head_gain=None, rope=None, w_aux=None):
    T, D = x.shape
    N = w.shape[1]
    tm = _pick_tile(seq, TOKEN_TILE)
    ch = PROJ_CHUNK
    n_norm = 0 if head_gain is None else head_gain.shape[0]
    assert N % ch == 0 and n_norm % ch == 0 and T % tm == 0
    args = [x, g.reshape(1, D), w]
    in_specs = [pl.BlockSpec((tm, D), lambda i: (i, 0)), _resident((1, D)), _resident((D, N))]
    if n_norm:
        gidx = np.arange(ch) // HEAD_DIM
        gsum = jnp.asarray((gidx[:, None] == gidx[None, :]).astype(np.float32), BF16)
        args += [head_gain.reshape(1, n_norm), gsum]
        in_specs += [_resident((1, n_norm)), _resident((ch, ch))]
    if rope is not None:
        nblk = seq // tm
        args += list(rope)
        in_specs += [pl.BlockSpec((tm, ch), lambda i: (i % nblk, 0))] * 2
    out_specs = [pl.BlockSpec((tm, N), lambda i: (i, 0))]
    out_shape = [jax.ShapeDtypeStruct((T, N), BF16)]
    if w_aux is not None:
        args.append(w_aux)
        in_specs.append(_resident(w_aux.shape))
        out_specs.append(pl.BlockSpec((tm, w_aux.shape[1]), lambda i: (i, 0)))
        out_shape.append(jax.ShapeDtypeStruct((T, w_aux.shape[1]), F32))
    out = pl.pallas_call(
        functools.partial(_proj_kernel, n_norm=n_norm, rope=rope is not None, has_aux=w_aux is not None),
        grid=(T // tm,),
        in_specs=in_specs,
        out_specs=out_specs,
        out_shape=out_shape,
        compiler_params=_cparams("parallel"),
        name="proj",
    )(*args)
    return out if w_aux is not None else out[0]


def _outproj_gated3_kernel(x_ref, oc_ref, os_ref, ow_ref, gate_ref, exp_ref, w_ref, o_ref):
    gs = _sigmoid(gate_ref[...]).astype(BF16)
    a = None
    for c, o_c in enumerate((oc_ref, os_ref, ow_ref)):
        t = jnp.dot(gs, exp_ref[c], preferred_element_type=F32) * o_c[...].astype(F32)
        a = t if a is None else a + t
    o_ref[...] = x_ref[...] + jnp.dot(a.astype(BF16), w_ref[...], preferred_element_type=F32)


def _outproj_gated3(x, w, branches, gate, expand):
    T, D = x.shape
    K = w.shape[0]
    tm = _pick_tile(T, SMALL_TOKEN_TILE)
    row = lambda i: (i, 0)
    return pl.pallas_call(
        _outproj_gated3_kernel,
        grid=(T // tm,),
        in_specs=[pl.BlockSpec((tm, D), row)] + [pl.BlockSpec((tm, K), row)] * 3
        + [pl.BlockSpec((tm, gate.shape[1]), row), _resident(expand.shape), _resident((K, D))],
        out_specs=pl.BlockSpec((tm, D), row),
        out_shape=jax.ShapeDtypeStruct((T, D), F32),
        compiler_params=_cparams("parallel"),
        name="outproj_gated3",
    )(x, *branches, gate, expand, w)


def _fox_c_kernel(x_ref, g_ref, wf_ref, bf_ref, tri_ref, c_ref, carry_ref):
    @pl.when(pl.program_id(1) == 0)
    def _():
        carry_ref[...] = jnp.zeros_like(carry_ref)

    h = _rms_rows(x_ref[...], g_ref[...])
    f = lax.dot_general(wf_ref[...], h, _NT, precision=lax.Precision.HIGHEST,
                        preferred_element_type=F32) + bf_ref[...]
    ls = jnp.minimum(f, 0.0) - jnp.log(1.0 + jnp.exp(-jnp.abs(f)))
    cs = jnp.dot(ls, tri_ref[...], precision=lax.Precision.HIGHEST,
                 preferred_element_type=F32) + carry_ref[...]
    c_ref[...] = cs
    carry_ref[...] = cs[:, cs.shape[1] - 1:]


def _fox_c(x, g, wf_t, b_f, *, batch, seq):
    T, D = x.shape
    H = wf_t.shape[0]
    tm = _pick_tile(seq, SMALL_TOKEN_TILE)
    ns = seq // tm
    tri = jnp.asarray(np.triu(np.ones((tm, tm), np.float32)))
    return pl.pallas_call(
        _fox_c_kernel,
        grid=(batch, ns),
        in_specs=[
            pl.BlockSpec((tm, D), lambda b, s: (b * ns + s, 0)),
            pl.BlockSpec((1, D), lambda b, s: (0, 0)),
            pl.BlockSpec((H, D), lambda b, s: (0, 0)),
            pl.BlockSpec((H, 1), lambda b, s: (0, 0)),
            pl.BlockSpec((tm, tm), lambda b, s: (0, 0)),
        ],
        out_specs=pl.BlockSpec((None, H, tm), lambda b, s: (b, 0, s)),
        out_shape=jax.ShapeDtypeStruct((batch, H, seq), F32),
        scratch_shapes=[pltpu.VMEM((H, 1), F32)],
        compiler_params=_cparams("parallel", "arbitrary"),
        name="fox_c",
    )(x, g.reshape(1, D), wf_t, b_f.reshape(H, 1), tri)


def _fox_attn_kernel(q_ref, k_ref, v_ref, c_ref, o_ref, vaug_ref, sa_ref, sb_ref, m_ref, acc_ref, *, tq, tk):
    qi = pl.program_id(2)
    lane = lax.broadcasted_iota(jnp.int32, (1, LANES), 1)
    upper = lane >= HEAD_DIM

    @pl.when(qi == 0)
    def _():
        v = v_ref[...]
        one = jnp.ones_like(v)
        vaug_ref[0] = jnp.where(upper, one, v)
        vaug_ref[1] = jnp.where(upper, v, one)

    q0 = pl.multiple_of(qi * tq, tq)
    qslab = q_ref[...]
    zero = jnp.zeros_like(qslab)
    qpos = lax.broadcasted_iota(jnp.int32, (tq, tk), 0)
    kpos = lax.broadcasted_iota(jnp.int32, (tq, tk), 1)

    heads = range(2)
    q2 = jnp.concatenate([jnp.where(upper, zero, qslab), jnp.where(upper, qslab, zero)], axis=0)
    c_q0 = [c_ref[i, :, pl.ds(q0, LANES)][:, 0:1] for i in heads]

    n_diag = tq // tk
    assert n_diag == 2

    def produce(s_ref, j, off=0):
        keys = pl.ds(pl.multiple_of(j * tk, tk), tk)
        n = tq - off
        qs = q2 if off == 0 else jnp.concatenate([q2[off:tq], q2[tq + off:2 * tq]], axis=0)
        s2 = lax.dot_general(qs, k_ref[keys, :], _NT, preferred_element_type=F32)
        for i in heads:
            s_ref[i, off:tq] = s2[i * n:(i + 1) * n] + (c_q0[i] - c_ref[i, :, keys])

    def consume(s_ref, j, diag_offset=None, off=0):
        keys = pl.ds(pl.multiple_of(j * tk, tk), tk)
        live = slice(off, tq)
        if diag_offset is None:
            score = lambda i: s_ref[i, live]
        else:
            causal = (kpos + diag_offset <= qpos)[live]
            score = lambda i: jnp.where(causal, s_ref[i, live], NEG_INF)
        m_old = [m_ref[i, live] for i in heads]
        m_new = [jnp.maximum(m_old[i], jnp.max(score(i), axis=1, keepdims=True)) for i in heads]
        p = [jnp.exp(score(i) - jnp.tile(m_new[i], (1, tk // LANES))).astype(BF16) for i in heads]
        for i in heads:
            m_ref[i, live] = m_new[i]
            acc_ref[i, live] = (jnp.exp(m_old[i] - m_new[i]) * acc_ref[i, live]
                                + jnp.dot(p[i], vaug_ref[i, keys, :], preferred_element_type=F32))

    m_ref[...] = jnp.full(m_ref.shape, NEG_INF, F32)
    acc_ref[...] = jnp.zeros(acc_ref.shape, F32)
    produce(sa_ref, 0)

    def body(m, _):
        produce(sb_ref, 2 * m + 1)
        consume(sa_ref, 2 * m)
        produce(sa_ref, 2 * m + 2)
        consume(sb_ref, 2 * m + 1)
        return 0

    lax.fori_loop(0, qi, body, 0)
    produce(sb_ref, 2 * qi + 1, off=tk)
    consume(sa_ref, 2 * qi, 0)
    consume(sb_ref, 2 * qi + 1, tk, off=tk)

    accs = (acc_ref[0], acc_ref[1])
    num = jnp.where(upper, accs[1], accs[0])
    den = jnp.where(upper, accs[0], accs[1])
    den = pltpu.roll(den, HEAD_DIM, 1)
    o_ref[...] = (num / den).astype(o_ref.dtype)


def _fox_attn(qkvg, c, *, batch, seq):
    T = qkvg.shape[0]
    HD = N_HEADS * HEAD_DIM
    tq = _pick_tile(seq, ATTN_Q_TILE)
    tk = tq // 2
    nq = seq // tq
    npair = HD // LANES
    return pl.pallas_call(
        functools.partial(_fox_attn_kernel, tq=tq, tk=tk),
        grid=(batch, npair, nq),
        in_specs=[
            pl.BlockSpec((tq, LANES), lambda b, h, q: (b * nq + q, h)),
            pl.BlockSpec((seq, LANES), lambda b, h, q: (b, npair + h)),
            pl.BlockSpec((seq, LANES), lambda b, h, q: (b, 2 * npair + h)),
            pl.BlockSpec((None, 2, 1, seq), lambda b, h, q: (b, h, 0, 0)),
        ],
        out_specs=pl.BlockSpec((tq, LANES), lambda b, h, q: (b * nq + q, h)),
        out_shape=jax.ShapeDtypeStruct((T, HD), BF16),
        scratch_shapes=[pltpu.VMEM((2, seq, LANES), BF16),
                        pltpu.VMEM((2, tq, tk), F32), pltpu.VMEM((2, tq, tk), F32),
                        pltpu.VMEM((2, tq, LANES), F32), pltpu.VMEM((2, tq, LANES), F32)],
        compiler_params=_cparams("parallel", "arbitrary", "arbitrary"),
        name="fox_attn",
    )(qkvg, qkvg, qkvg, c.reshape(batch, N_HEADS, 1, seq))


def _fox_mixer(x, g, w_in, b_f, g_q, g_k, w_out, *, batch, seq):
    HD = N_HEADS * HEAD_DIM
    scale = HEAD_DIM ** -0.5
    w = jnp.concatenate([w_in[:, :3 * HD], w_in[:, 3 * HD + N_HEADS:]], axis=1).astype(BF16)
    wf_t = w_in[:, 3 * HD:3 * HD + N_HEADS].T
    head_gain = jnp.concatenate([jnp.tile(g_q * scale, N_HEADS), jnp.tile(g_k, N_HEADS)])
    qkvg = _proj(x, g, w, seq=seq, head_gain=head_gain)
    c = _fox_c(x, g, wf_t, b_f, batch=batch, seq=seq)
    o = _fox_attn(qkvg, c, batch=batch, seq=seq)
    return "sigmoid_gate", [(o, 0), (qkvg, 3)], w_out.astype(BF16)


def _sb_attn_kernel(q_ref, k_ref, v_ref, tri_ref, o_ref, aa_ref, ab_ref, rest_ref, acc_ref, *, tq, tk):
    qi = pl.program_id(2)
    lane = lax.broadcasted_iota(jnp.int32, (1, LANES), 1)
    upper = lane >= HEAD_DIM
    qslab = q_ref[...]
    zero = jnp.zeros_like(qslab)
    q2 = jnp.concatenate([jnp.where(upper, zero, qslab), jnp.where(upper, qslab, zero)], axis=0)
    rows = 2 * tq
    n_diag = tq // tk
    last = n_diag * (qi + 1) - 1

    def key_rows(k):
        return pl.ds(pl.multiple_of(jnp.maximum(last - k, 0) * tk, tk), tk)

    def head_rows(off):
        return (slice(off, tq), slice(tq + off, rows))

    def take(x, off):
        if off == 0:
            return x[...]
        lo, hi = head_rows(off)
        return jnp.concatenate([x[lo], x[hi]], axis=0)

    def put(ref, val, off, add=False):
        parts = [(slice(None), val)] if off == 0 else list(zip(head_rows(off), (val[:tq - off], val[tq - off:])))
        for sl, part in parts:
            if add:
                ref[sl] += part
            else:
                ref[sl] = part

    def scores(k, off=0):
        return lax.dot_general(take(q2, off), k_ref[key_rows(k), :], _NT, preferred_element_type=F32)

    def weights(z, a_ref, diag_offset, off=0):
        n2 = z.shape[0]
        sp = jnp.maximum(z, 0.0) + jnp.log2(1.0 + jnp.exp2(_neg_abs(z)))
        if diag_offset is not None:
            qpos = lax.broadcasted_iota(jnp.int32, (n2, tk), 0) % (tq - off) + off
            strict = lax.broadcasted_iota(jnp.int32, (n2, tk), 1) + diag_offset < qpos
            spm = jnp.where(strict, sp, 0.0)
        else:
            spm = sp
        cum = jnp.dot(spm.astype(BF16), tri_ref[...], preferred_element_type=F32)
        w = (z - sp) - cum
        if diag_offset is not None:
            w = jnp.where(strict, w, NEG_INF)
        rest = take(rest_ref, off)
        put(a_ref, jnp.exp2(w - jnp.tile(rest, (1, tk // LANES))).astype(BF16), off)
        put(rest_ref, rest + jnp.broadcast_to(cum[:, 0:1] + spm[:, 0:1], (n2, LANES)), off)

    def accumulate(a_ref, k, off=0):
        put(acc_ref, jnp.dot(take(a_ref, off), v_ref[key_rows(k), :], preferred_element_type=F32), off, add=True)

    assert n_diag % 2 == 0
    even, odd = aa_ref, ab_ref
    offset = lambda k: (n_diag - 1 - k) * tk

    def produce_diag(buf, k):
        weights(scores(k, offset(k)), buf, offset(k), offset(k))

    def step(k, cur, nxt, diag=False):
        off_cur, off_nxt = (offset(k), offset(k + 1)) if diag else (0, 0)
        z = scores(k + 1, off_nxt)
        accumulate(cur, k, off_cur)
        weights(z, nxt, off_nxt if diag else None, off_nxt)

    rest_ref[...] = jnp.zeros(rest_ref.shape, F32)
    acc_ref[...] = jnp.zeros(acc_ref.shape, F32)
    produce_diag(even, 0)
    produce_diag(odd, 1)
    accumulate(aa_ref, 0, offset(0))
    for m in range(1, n_diag // 2):
        step(2 * m - 1, odd, even, diag=True)
        step(2 * m, even, odd, diag=True)

    def body(m, _):
        step(2 * m - 1, odd, even)
        step(2 * m, even, odd)
        return 0

    lax.fori_loop(n_diag // 2, (n_diag // 2) * (qi + 1), body, 0)
    accumulate(ab_ref, last)
    o_ref[...] = jnp.where(upper, acc_ref[tq:2 * tq], acc_ref[0:tq]).astype(o_ref.dtype)


def _sb_attn(qkv, *, batch, seq):
    T = qkv.shape[0]
    HD = N_HEADS * HEAD_DIM
    tq = _pick_tile(seq, ATTN_Q_TILE)
    tk = SB_KEY_TILE
    nq = seq // tq
    npair = HD // LANES
    tri = jnp.asarray(np.tril(np.ones((tk, tk), np.float32), -1), BF16)
    return pl.pallas_call(
        functools.partial(_sb_attn_kernel, tq=tq, tk=tk),
        grid=(batch, npair, nq),
        in_specs=[
            pl.BlockSpec((tq, LANES), lambda b, h, q: (b * nq + q, h)),
            pl.BlockSpec((seq, LANES), lambda b, h, q: (b, npair + h)),
            pl.BlockSpec((seq, LANES), lambda b, h, q: (b, 2 * npair + h)),
            pl.BlockSpec((tk, tk), lambda b, h, q: (0, 0)),
        ],
        out_specs=pl.BlockSpec((tq, LANES), lambda b, h, q: (b * nq + q, h)),
        out_shape=jax.ShapeDtypeStruct((T, HD), BF16),
        scratch_shapes=[pltpu.VMEM((2 * tq, tk), BF16)] * 2 + [pltpu.VMEM((2 * tq, LANES), F32)] * 2,
        compiler_params=_cparams("parallel", "parallel", "arbitrary"),
        name="sb_attn",
    )(qkv, qkv, qkv, tri)


def _sb_mixer(x, g, w_in, w_out, *, batch, seq):
    HD = N_HEADS * HEAD_DIM
    scale = HEAD_DIM ** -0.5 * float(np.log2(np.e))
    w = jnp.concatenate([w_in[:, :HD] * scale, w_in[:, HD:]], axis=1).astype(BF16)
    qkv = _proj(x, g, w, seq=seq)
    o = _sb_attn(qkv, batch=batch, seq=seq)
    return "plain", [(o, 0)], w_out.astype(BF16)


def _gla_kernel(q_ref, k_ref, v_ref, r_ref, glow_ref, wgu_ref, bg_ref, gout_ref, tri_ref,
                o_ref, state_ref, *, tm, dk):
    C = GLA_CHUNK

    @pl.when(pl.program_id(2) == 0)
    def _():
        state_ref[...] = jnp.zeros_like(state_ref)

    gate = jnp.dot(glow_ref[...], wgu_ref[...], precision=lax.Precision.HIGHEST,
                   preferred_element_type=F32) + bg_ref[...]
    log_a = (jnp.minimum(gate, 0.0) - jnp.log(1.0 + jnp.exp(-jnp.abs(gate)))) * (1.0 / GLA_TAU)
    row = lax.broadcasted_iota(jnp.int32, (C, C), 0)
    col = lax.broadcasted_iota(jnp.int32, (C, C), 1)
    causal = col <= row
    q_scale = dk ** -0.5
    chunks = [slice(c * C, (c + 1) * C) for c in range(tm // C)]

    tri = tri_ref[...]
    b = [jnp.dot(tri, log_a[sl], precision=lax.Precision.HIGHEST, preferred_element_type=F32)
         for sl in chunks]
    b_last = jnp.concatenate([jnp.broadcast_to(bc[C - 1:C], bc.shape) for bc in b], axis=0)
    b = jnp.concatenate(b, axis=0)
    q = q_ref[...].astype(F32) * q_scale
    k = k_ref[...].astype(F32)
    q_dec = (q * jnp.exp(b)).astype(BF16)
    k_neg = (k * jnp.exp(-b)).astype(BF16)
    k_rem = (k * jnp.exp(b_last - b)).astype(BF16)
    decay = jnp.exp(b_last)
    attn = [lax.dot_general(q_dec[sl], k_neg[sl], _NT, preferred_element_type=F32) for sl in chunks]
    attn = [jnp.where(causal, a, 0.0).astype(BF16) for a in attn]
    o_intra = [jnp.dot(attn[c], v_ref[sl, :], preferred_element_type=F32) for c, sl in enumerate(chunks)]
    u_t = [lax.dot_general(v_ref[sl, :], k_rem[sl], _TN, preferred_element_type=F32) for sl in chunks]

    state_t = state_ref[...]
    outs = []
    for c, sl in enumerate(chunks):
        outs.append(o_intra[c] + lax.dot_general(q_dec[sl], state_t.astype(BF16), _NT,
                                                 preferred_element_type=F32))
        state_t = state_t * decay[c * C:c * C + 1] + u_t[c]
    state_ref[...] = state_t

    o = jnp.concatenate(outs, axis=0)
    y = o * lax.rsqrt(jnp.mean(o * o, axis=-1, keepdims=True) + RMS_EPS) * gout_ref[...]
    r = r_ref[...].astype(F32)
    o_ref[...] = (y * (r * _sigmoid(r))).astype(o_ref.dtype)


def _gla(qkvr, glow, wgu, b_gate, g_out, *, batch, seq):
    T = qkvr.shape[0]
    Hg = GLA_HEADS
    dk_total = wgu.shape[1]
    dk = dk_total // Hg
    dv = (qkvr.shape[1] - 2 * dk_total) // 2 // Hg
    tm = _pick_tile(seq, TOKEN_TILE)
    ns = seq // tm
    kq, kv = dk_total // dk, dk_total // dv
    tri = jnp.asarray(np.tril(np.ones((GLA_CHUNK, GLA_CHUNK), np.float32)))
    rowmap = lambda off: (lambda b, h, s: (b * ns + s, off + h))
    return pl.pallas_call(
        functools.partial(_gla_kernel, tm=tm, dk=dk),
        grid=(batch, Hg, ns),
        in_specs=[
            pl.BlockSpec((tm, dk), rowmap(0)),
            pl.BlockSpec((tm, dk), rowmap(kq)),
            pl.BlockSpec((tm, dv), rowmap(2 * kv)),
            pl.BlockSpec((tm, dv), rowmap(2 * kv + Hg)),
            pl.BlockSpec((tm, LANES), lambda b, h, s: (b * ns + s, 0)),
            pl.BlockSpec((LANES, dk), lambda b, h, s: (0, h)),
            pl.BlockSpec((1, dk), lambda b, h, s: (0, h)),
            pl.BlockSpec((1, dv), lambda b, h, s: (0, 0)),
            pl.BlockSpec((GLA_CHUNK, GLA_CHUNK), lambda b, h, s: (0, 0)),
        ],
        out_specs=pl.BlockSpec((tm, dv), rowmap(0)),
        out_shape=jax.ShapeDtypeStruct((T, Hg * dv), BF16),
        scratch_shapes=[pltpu.VMEM((dv, dk), F32)],
        compiler_params=_cparams("parallel", "parallel", "arbitrary"),
        name="gla",
    )(qkvr, qkvr, qkvr, qkvr, glow, wgu, b_gate.reshape(1, dk_total), g_out.reshape(1, dv), tri)


def _gla_mixer(x, g, w_in, w_gate_up, b_gate, g_out, w_out, *, batch, seq):
    dk_total = w_gate_up.shape[1]
    dv_total = w_out.shape[0]
    lo = 2 * dk_total + dv_total
    w_main = jnp.concatenate([w_in[:, :lo], w_in[:, lo + GLA_GATE_RANK:]], axis=1).astype(BF16)
    w_low = jnp.pad(w_in[:, lo:lo + GLA_GATE_RANK], ((0, 0), (0, LANES - GLA_GATE_RANK))).astype(BF16)
    wgu = jnp.pad(w_gate_up, ((0, LANES - GLA_GATE_RANK), (0, 0)))
    qkvr, glow = _proj(x, g, w_main, seq=seq, w_aux=w_low)
    o = _gla(qkvr, glow, wgu, b_gate, g_out, batch=batch, seq=seq)
    return "plain", [(o, 0)], w_out.astype(BF16)


def _nsa_compress_kernel(kc_ref, vc_ref, pk_ref, pv_ref, wk_ref, wv_ref, gk_ref, ko_ref, vo_ref, *, n16):
    def compress(r_ref, pos_ref, w_ref):
        r = r_ref[...].astype(F32)
        first = jnp.dot((r + pos_ref[0:1, :]).astype(BF16), w_ref[0], preferred_element_type=F32)
        second = jnp.dot((r + pos_ref[1:2, :]).astype(BF16), w_ref[1], preferred_element_type=F32)
        return first + pltpu.roll(second, n16 - 1, 0)

    kc = compress(kc_ref, pk_ref, wk_ref)
    ko_ref[...] = _rms_rows(kc, gk_ref[...]).astype(ko_ref.dtype)
    vo_ref[...] = compress(vc_ref, pv_ref, wv_ref).astype(vo_ref.dtype)


def _nsa_compress(kc_r, vc_r, pos_k, pos_v, w_k, w_v, g_k):
    B, G, n16, width = kc_r.shape
    dh = HEAD_DIM
    half = NSA_CMP_LEN // 2
    dup = lambda w: jnp.concatenate([w, w], axis=-1).reshape(2, half * dh, 2 * dh).astype(BF16)
    spec_r = pl.BlockSpec((None, None, n16, width), lambda b, g: (b, g, 0, 0))
    spec_o = pl.BlockSpec((None, None, n16, 2 * dh), lambda b, g: (b, g, 0, 0))
    const2 = lambda shape: pl.BlockSpec(shape, lambda b, g: (0,) * len(shape))
    out = jax.ShapeDtypeStruct((B, G, n16, 2 * dh), BF16)
    return pl.pallas_call(
        functools.partial(_nsa_compress_kernel, n16=n16),
        grid=(B, G),
        in_specs=[spec_r, spec_r, const2((2, width)), const2((2, width)),
                  const2((2, width, 2 * dh)), const2((2, width, 2 * dh)), const2((1, 2 * dh))],
        out_specs=[spec_o, spec_o],
        out_shape=[out, out],
        compiler_params=_cparams("parallel", "parallel"),
        name="nsa_compress",
    )(kc_r, vc_r, pos_k.reshape(2, width), pos_v.reshape(2, width), dup(w_k), dup(w_v),
      jnp.tile(g_k, 2).reshape(1, 2 * dh))


def _stack_heads(q_ref, extra=None, rows_in=slice(None)):
    lane = lax.broadcasted_iota(jnp.int32, (1, LANES), 1)
    upper = lane >= HEAD_DIM
    rows = []
    for j in range(2):
        slab = q_ref[rows_in, j * LANES:(j + 1) * LANES]
        zero = jnp.zeros_like(slab)
        for i in range(2):
            qh = jnp.where(upper, slab, zero) if i else jnp.where(upper, zero, slab)
            rows.append(qh if extra is None else jnp.concatenate([qh, extra], axis=1))
    return jnp.concatenate(rows, axis=0)


def _unstack_heads(o, tq):
    lane = lax.broadcasted_iota(jnp.int32, (1, LANES), 1)
    upper = lane >= HEAD_DIM
    slabs = []
    for j in range(2):
        lo = o[(2 * j) * tq:(2 * j + 1) * tq]
        hi = pltpu.roll(o[(2 * j + 1) * tq:(2 * j + 2) * tq], HEAD_DIM, 1)
        slabs.append(jnp.where(upper, hi, lo))
    return jnp.concatenate(slabs, axis=1)


def _group_lower(slab_ref, g):
    x = slab_ref[...].astype(F32)
    return jnp.where(g % 2 == 1, pltpu.roll(x, HEAD_DIM, 1), x)


def _both_halves(x):
    lane = lax.broadcasted_iota(jnp.int32, (1, LANES), 1)
    return jnp.where(lane >= HEAD_DIM, pltpu.roll(x, HEAD_DIM, 1), x)


def _with_ones(x):
    lane = lax.broadcasted_iota(jnp.int32, (1, LANES), 1)
    return jnp.where(lane >= HEAD_DIM, 1.0, x)


def _normalise(acc):
    return acc / pltpu.roll(acc, HEAD_DIM, 1)


def _nsa_cmp_attn_kernel(q_ref, kc_ref, vc_ref, cover_ref, oc_ref, sel_ref, *, tq, n16, parts):
    qi = pl.program_id(2)
    tp = tq // parts
    for part in range(parts):
        rows = slice(part * tp, (part + 1) * tp)
        row0 = qi * tq + part * tp
        q4 = _stack_heads(q_ref, rows_in=rows)
        s = lax.dot_general(q4, kc_ref[...], _NT, preferred_element_type=F32)
        qpos = row0 + lax.broadcasted_iota(jnp.int32, (tp, n16), 0)
        cmp_end = lax.broadcasted_iota(jnp.int32, (tp, n16), 1) * NSA_CMP_STRIDE + (NSA_CMP_LEN - 1)
        valid = jnp.concatenate([cmp_end <= qpos] * 4, axis=0)
        s = jnp.where(valid, s, NEG_INF)
        e = jnp.exp(s - jnp.max(s, axis=1, keepdims=True))
        p = jnp.where(valid, e / jnp.sum(e, axis=1, keepdims=True), 0.0)
        o = jnp.dot(p.astype(BF16), vc_ref[...], preferred_element_type=F32)
        oc_ref[rows, :] = _unstack_heads(o, tp).astype(oc_ref.dtype)

        psum = p[0:tp] + p[tp:2 * tp] + p[2 * tp:3 * tp] + p[3 * tp:4 * tp]
        cover = cover_ref[...]
        hi = psum.astype(BF16)
        r1 = psum - hi.astype(F32)
        mid = r1.astype(BF16)
        lo = (r1 - mid.astype(F32)).astype(BF16)
        imp = (jnp.dot(hi, cover, preferred_element_type=F32) + jnp.dot(mid, cover, preferred_element_type=F32)
               + jnp.dot(lo, cover, preferred_element_type=F32))

        blk = lax.broadcasted_iota(jnp.int32, (tp, LANES), 1)
        cur = (row0 + lax.broadcasted_iota(jnp.int32, (tp, LANES), 0)) // NSA_SEL_LEN
        forced = (blk == 0) | (blk == cur) | (blk == cur - 1)
        taken = -3e38
        vals = jnp.where(forced, taken, jnp.where(blk <= cur, imp, -1.0))
        blk_f = blk.astype(F32)
        for _ in range(NSA_N_SEL - 3):
            top = jnp.max(vals, axis=1, keepdims=True)
            first = jnp.min(jnp.where(vals == top, blk_f, float(LANES)), axis=1, keepdims=True)
            vals = jnp.where(blk_f == first, taken, vals)
        sel_ref[rows, :] = jnp.where(vals == taken, 0.0, NEG_INF).astype(sel_ref.dtype)


def _nsa_cmp_attn(qk, kcmp, vcmp, *, batch, seq):
    T = qk.shape[0]
    G = NSA_KV_HEADS
    tq = _pick_tile(seq, ATTN_Q_TILE)
    nq = seq // tq
    n16 = kcmp.shape[2]
    n_sel = seq // NSA_SEL_LEN
    assert n_sel <= LANES and seq % tq == 0
    n = np.arange(n16)[:, None] * NSA_CMP_STRIDE
    j = np.arange(LANES)[None, :] * NSA_SEL_LEN
    cover = (n < j + NSA_SEL_LEN) & (n + NSA_CMP_LEN > j) & (np.arange(LANES)[None, :] < n_sel)
    cover = jnp.asarray(cover.astype(np.float32), BF16)
    spec_c = pl.BlockSpec((None, None, n16, LANES), lambda b, g, q: (b, g, 0, 0))
    return pl.pallas_call(
        functools.partial(_nsa_cmp_attn_kernel, tq=tq, n16=n16, parts=CMP_ROW_GROUPS),
        grid=(batch, G, nq),
        in_specs=[pl.BlockSpec((tq, 2 * LANES), lambda b, g, q: (b * nq + q, g)), spec_c, spec_c,
                  pl.BlockSpec((n16, LANES), lambda b, g, q: (0, 0))],
        out_specs=[pl.BlockSpec((tq, 2 * LANES), lambda b, g, q: (b * nq + q, g)),
                   pl.BlockSpec((tq, LANES), lambda b, g, q: (b * nq + q, g))],
        out_shape=[jax.ShapeDtypeStruct((T, N_HEADS * HEAD_DIM), BF16),
                   jax.ShapeDtypeStruct((T, G * LANES), BF16)],
        compiler_params=_cparams("parallel", "parallel", "parallel"),
        name="nsa_cmp_attn",
    )(qk, kcmp, vcmp, cover)


def _nsa_sel_attn_kernel(q_ref, k_ref, v_ref, sel_ref, o_ref, kaug_ref, vaug_ref, sa_ref, sb_ref, m_ref, acc_ref,
                         *, tq, sub, tk, seq):
    g = pl.program_id(1)
    qi = pl.program_id(2)

    @pl.when(qi == 0)
    def _():
        kaug_ref[:, 0:LANES] = _both_halves(_group_lower(k_ref, g)).astype(BF16)
        blk_of_key = lax.broadcasted_iota(jnp.int32, (seq, LANES), 0) // NSA_SEL_LEN
        blk = lax.broadcasted_iota(jnp.int32, (seq, LANES), 1)
        kaug_ref[:, LANES:2 * LANES] = jnp.where(blk_of_key == blk, 1.0, 0.0).astype(BF16)
        vaug_ref[...] = _with_ones(_group_lower(v_ref, g)).astype(BF16)

    chains = range(tq // sub)
    rows = [slice(c * sub, (c + 1) * sub) for c in chains]
    q4 = [_stack_heads(q_ref, extra=sel_ref[rows[c], :], rows_in=rows[c]) for c in chains]
    n_diag = tq // tk
    assert n_diag == 2

    def head_rows(off):
        return [slice(h * sub + off, (h + 1) * sub) for h in range(4)]

    def take(x, off):
        return x[...] if off == 0 else jnp.concatenate([x[sl] for sl in head_rows(off)], axis=0)

    def put(ref, val, off):
        if off == 0:
            ref[...] = val
        else:
            for h, sl in enumerate(head_rows(off)):
                ref[sl] = val[h * (sub - off):(h + 1) * (sub - off)]

    def produce(s_ref, j, off=0):
        kt = kaug_ref[pl.ds(pl.multiple_of(j * tk, tk), tk), :]
        for c in chains:
            s_ref[c, 0:4 * (sub - off)] = lax.dot_general(take(q4[c], off), kt, _NT,
                                                          preferred_element_type=F32)

    def consume(s_ref, j, diag_offset=None, off=0):
        keys = pl.ds(pl.multiple_of(j * tk, tk), tk)
        live = slice(0, 4 * (sub - off))
        if diag_offset is not None:
            kpos = diag_offset + lax.broadcasted_iota(jnp.int32, (sub - off, tk), 1)
            qpos = off + lax.broadcasted_iota(jnp.int32, (sub - off, tk), 0)
            ok = [jnp.concatenate([kpos <= qpos + c * sub] * 4, axis=0) for c in chains]
            score = lambda c: jnp.where(ok[c], s_ref[c, live], NEG_INF)
        else:
            score = lambda c: s_ref[c, live]
        vt = vaug_ref[keys, :]
        m_old = [take(m_ref.at[c], off) for c in chains]
        m_new = [jnp.maximum(m_old[c], jnp.max(score(c), axis=1, keepdims=True)) for c in chains]
        p = [jnp.exp(score(c) - jnp.tile(m_new[c], (1, tk // LANES))).astype(BF16) for c in chains]
        for c in chains:
            put(m_ref.at[c], m_new[c], off)
            put(acc_ref.at[c], (jnp.exp(m_old[c] - m_new[c]) * take(acc_ref.at[c], off)
                                + jnp.dot(p[c], vt, preferred_element_type=F32)), off)

    m_ref[...] = jnp.full(m_ref.shape, NEG_INF, F32)
    acc_ref[...] = jnp.zeros(acc_ref.shape, F32)
    produce(sa_ref, 0)

    def body(m, _):
        produce(sb_ref, 2 * m + 1)
        consume(sa_ref, 2 * m)
        produce(sa_ref, 2 * m + 2)
        consume(sb_ref, 2 * m + 1)
        return 0

    lax.fori_loop(0, qi, body, 0)
    produce(sb_ref, 2 * qi + 1, off=tk)
    consume(sa_ref, 2 * qi, 0)
    consume(sb_ref, 2 * qi + 1, tk, off=tk)

    for c in chains:
        o_ref[rows[c], :] = _unstack_heads(_normalise(acc_ref[c]), sub).astype(o_ref.dtype)


def _nsa_sel_attn(qk, vv, selbias, *, batch, seq, k_col, v_col):
    T = qk.shape[0]
    G = NSA_KV_HEADS
    tq = _pick_tile(seq, ATTN_Q_TILE)
    sub = tq
    tk = tq // 2
    nq = seq // tq
    return pl.pallas_call(
        functools.partial(_nsa_sel_attn_kernel, tq=tq, sub=sub, tk=tk, seq=seq),
        grid=(batch, G, nq),
        in_specs=[pl.BlockSpec((tq, 2 * LANES), lambda b, g, q: (b * nq + q, g)),
                  pl.BlockSpec((seq, LANES), lambda b, g, q: (b, k_col + g // 2)),
                  pl.BlockSpec((seq, LANES), lambda b, g, q: (b, v_col + g // 2)),
                  pl.BlockSpec((tq, LANES), lambda b, g, q: (b * nq + q, g))],
        out_specs=pl.BlockSpec((tq, 2 * LANES), lambda b, g, q: (b * nq + q, g)),
        out_shape=jax.ShapeDtypeStruct((T, N_HEADS * HEAD_DIM), BF16),
        scratch_shapes=[pltpu.VMEM((seq, 2 * LANES), BF16), pltpu.VMEM((seq, LANES), BF16),
                        pltpu.VMEM((tq // sub, 4 * sub, tk), F32), pltpu.VMEM((tq // sub, 4 * sub, tk), F32),
                        pltpu.VMEM((tq // sub, 4 * sub, LANES), F32),
                        pltpu.VMEM((tq // sub, 4 * sub, LANES), F32)],
        compiler_params=_cparams("parallel", "arbitrary", "arbitrary"),
        name="nsa_sel_attn",
    )(qk, qk, vv, selbias)


def _nsa_win_attn_kernel(q_ref, k_ref, v_ref, o_ref, kb_ref, vaug_ref, *, tq, tp, span):
    g = pl.program_id(1)
    qi = pl.program_id(2)

    @pl.when(qi == 0)
    def _():
        kb_ref[...] = _both_halves(_group_lower(k_ref, g)).astype(BF16)
        vaug_ref[...] = _with_ones(_group_lower(v_ref, g)).astype(BF16)

    for part in range(tq // tp):
        rows = slice(part * tp, (part + 1) * tp)
        q4 = _stack_heads(q_ref, rows_in=rows)
        q0 = qi * tq + part * tp
        k_start = pl.multiple_of(jnp.maximum(q0 - NSA_WINDOW, 0), tp)
        keys = pl.ds(k_start, span)
        dist = (q0 - k_start) + (lax.broadcasted_iota(jnp.int32, (tp, span), 0)
                                 - lax.broadcasted_iota(jnp.int32, (tp, span), 1))
        ok = (dist >= 0) & (dist < NSA_WINDOW)
        ok2 = jnp.concatenate([ok, ok], axis=0)
        pairs = [q4[0:2 * tp], q4[2 * tp:4 * tp]]
        s = [lax.dot_general(qp, kb_ref[keys, :], _NT, preferred_element_type=F32) for qp in pairs]
        s = [jnp.where(ok2, sc, NEG_INF) for sc in s]
        p = [jnp.exp(sc - jnp.max(sc, axis=1, keepdims=True)).astype(BF16) for sc in s]
        acc = [jnp.dot(pc, vaug_ref[keys, :], preferred_element_type=F32) for pc in p]
        acc = jnp.concatenate(acc, axis=0)
        o_ref[rows, :] = _unstack_heads(_normalise(acc), tp).astype(o_ref.dtype)


def _nsa_win_attn(qk, vv, *, batch, seq, k_col, v_col):
    T = qk.shape[0]
    G = NSA_KV_HEADS
    tp = _pick_tile(seq, WIN_Q_GROUP)
    tq = _pick_tile(seq, WIN_GROUPS_PER_STEP * tp)
    nq = seq // tq
    span = NSA_WINDOW + tp
    assert NSA_WINDOW % tp == 0 and seq >= span
    return pl.pallas_call(
        functools.partial(_nsa_win_attn_kernel, tq=tq, tp=tp, span=span),
        grid=(batch, G, nq),
        in_specs=[pl.BlockSpec((tq, 2 * LANES), lambda b, g, q: (b * nq + q, g)),
                  pl.BlockSpec((seq, LANES), lambda b, g, q: (b, k_col + g // 2)),
                  pl.BlockSpec((seq, LANES), lambda b, g, q: (b, v_col + g // 2))],
        out_specs=pl.BlockSpec((tq, 2 * LANES), lambda b, g, q: (b * nq + q, g)),
        out_shape=jax.ShapeDtypeStruct((T, N_HEADS * HEAD_DIM), BF16),
        scratch_shapes=[pltpu.VMEM((seq, LANES), BF16), pltpu.VMEM((seq, LANES), BF16)],
        compiler_params=_cparams("parallel", "arbitrary", "arbitrary"),
        name="nsa_win_attn",
    )(qk, qk, vv)


def _nsa_mixer(x, g, w_in, pos_k, pos_v, w_cmp_k, w_cmp_v, g_q, g_k, w_out, *, batch, seq):
    HD = N_HEADS * HEAD_DIM
    G, dh = NSA_KV_HEADS, HEAD_DIM
    kvw = G * dh
    scale = dh ** -0.5
    col = lambda n: slice(HD + n * kvw, HD + (n + 1) * kvw)
    order = [slice(0, HD), col(0), col(2), col(4), col(1), col(3), col(5)]
    w = jnp.concatenate([w_in[:, c] for c in order], axis=1).astype(BF16)
    n_gate = 3 * N_HEADS
    w_gate = jnp.pad(w_in[:, HD + 6 * kvw:], ((0, 0), (0, LANES - n_gate))).astype(BF16)
    head_gain = jnp.concatenate([jnp.tile(g_q * scale, N_HEADS), jnp.tile(g_k, 3 * G)])

    inv = ROPE_THETA ** (-jnp.arange(0, dh, 2, dtype=F32) / dh)
    ang = jnp.arange(seq, dtype=F32)[:, None] * inv[None, :]
    cos, sin = jnp.cos(ang), jnp.sin(ang)
    cos_t = jnp.tile(cos, (1, PROJ_CHUNK // (dh // 2)))
    sin_t = jnp.tile(jnp.concatenate([-sin, sin], axis=1), (1, PROJ_CHUNK // dh))

    qkv, gate = _proj(x, g, w, seq=seq, head_gain=head_gain, rope=(cos_t, sin_t), w_aux=w_gate)

    def blocks16(t):
        half = NSA_CMP_LEN // 2
        t = t.reshape(batch, seq // half, half, G, dh).transpose(0, 3, 1, 2, 4)
        return t.reshape(batch, G, seq // half, half * dh)

    kc0, vc0 = HD, HD + 3 * kvw
    kcmp, vcmp = _nsa_compress(blocks16(qkv[:, kc0:kc0 + kvw]), blocks16(qkv[:, vc0:vc0 + kvw]),
                               pos_k, pos_v, w_cmp_k, w_cmp_v, g_k)
    o_c, selbias = _nsa_cmp_attn(qkv, kcmp, vcmp, batch=batch, seq=seq)
    o_s = _nsa_sel_attn(qkv, qkv, selbias, batch=batch, seq=seq,
                        k_col=(HD + kvw) // LANES, v_col=(HD + 4 * kvw) // LANES)
    o_w = _nsa_win_attn(qkv, qkv, batch=batch, seq=seq,
                        k_col=(HD + 2 * kvw) // LANES, v_col=(HD + 5 * kvw) // LANES)
    r = np.arange(LANES)[:, None]
    c = np.arange(HD)[None, :] // dh
    expand = np.stack([(r == 3 * c + b) for b in range(3)]).astype(np.float32)
    return _outproj_gated3(x, w_out.astype(BF16), (o_c, o_s, o_w), gate, jnp.asarray(expand, BF16))


def kernel(x, norm_g, ffn1_w_gate, ffn1_w_up, ffn1_w_down, ffn2_w_gate, ffn2_w_up, ffn2_w_down, fox_w_in, fox_b_f, fox_g_q, fox_g_k, fox_w_out, nsa_w_in, nsa_cmp_pos_k, nsa_cmp_pos_v, nsa_w_cmp_k, nsa_w_cmp_v, nsa_g_q, nsa_g_k, nsa_w_out, gla_w_in, gla_w_gate_up, gla_b_gate, gla_g_out, gla_w_out, sb_w_in, sb_w_out):
    B, S, D = x.shape
    depth = norm_g.shape[0]
    n_mixers = 4
    x2 = x.reshape(B * S, D)
    for i in range(depth):
        m, j = i % n_mixers, i // n_mixers
        x2 = _ffn(x2, norm_g[i, 0], ffn1_w_gate[i].astype(BF16), ffn1_w_up[i].astype(BF16),
                  ffn1_w_down[i].astype(BF16))
        g = norm_g[i, 1]
        pending = None
        if m == 0:
            pending = _fox_mixer(x2, g, fox_w_in[j], fox_b_f[j], fox_g_q[j], fox_g_k[j], fox_w_out[j],
                                 batch=B, seq=S)
        elif m == 1:
            x2 = _nsa_mixer(x2, g, nsa_w_in[j], nsa_cmp_pos_k[j], nsa_cmp_pos_v[j], nsa_w_cmp_k[j],
                            nsa_w_cmp_v[j], nsa_g_q[j], nsa_g_k[j], nsa_w_out[j], batch=B, seq=S)
        elif m == 2:
            pending = _gla_mixer(x2, g, gla_w_in[j], gla_w_gate_up[j], gla_b_gate[j], gla_g_out[j],
                                 gla_w_out[j], batch=B, seq=S)
        else:
            pending = _sb_mixer(x2, g, sb_w_in[j], sb_w_out[j], batch=B, seq=S)
        x2 = _ffn(x2, norm_g[i, 2], ffn2_w_gate[i].astype(BF16), ffn2_w_up[i].astype(BF16),
                  ffn2_w_down[i].astype(BF16), pre=pending)
    return x2.reshape(B, S, D)
```
